```python
import jax, jax.numpy as jnp
from jax import lax
import numpy as np

D_MODEL = 1024
BATCH = 32
SEQ = 2048
DEPTH = 4

CHUNK = 64
N_MIXERS = 2
N_SB = (DEPTH + 1) // 2
N_SGU = DEPTH // 2
SB_HEADS = 16
SB_HEAD_DIM = D_MODEL // SB_HEADS
Q_BLOCK = 128
SGU_CHUNK = 2 * CHUNK
SGU_FFN = 2 * D_MODEL
SGU_GROUPS = 8
SGU_GROUP_W = SGU_FFN // SGU_GROUPS
MLP_HIDDEN = 4 * D_MODEL
EPS = 1e-6

kernel_name = "hybrid_stickbreak_sgu_encoder"


def rmsnorm(x, gain):
    x32 = x.astype(jnp.float32)
    y = x32 * lax.rsqrt(jnp.mean(x32 * x32, axis=-1, keepdims=True) + EPS)
    return (y * gain.astype(jnp.float32)).astype(x.dtype)


def stick_breaking_attention(q, k, v):
    seq = q.shape[2]
    scale = SB_HEAD_DIM ** -0.5
    outs = []
    for blk in range(seq // Q_BLOCK):
        t0 = blk * Q_BLOCK
        t1 = t0 + Q_BLOCK
        qb = q[:, :, t0:t1].astype(jnp.float32)
        kb = k[:, :, :t1].astype(jnp.float32)
        vb = v[:, :, :t1].astype(jnp.float32)
        z = jnp.einsum('bhtd,bhsd->bhts', qb, kb) * scale
        t_idx = t0 + jnp.arange(Q_BLOCK)[:, None]
        s_idx = jnp.arange(t1)[None, :]
        past = s_idx < t_idx
        log_beta = jax.nn.log_sigmoid(z)
        log_one_minus = jnp.where(past, log_beta - z, 0.0)
        suffix = lax.cumsum(log_one_minus, axis=3, reverse=True) - log_one_minus
        a = jnp.where(past, jnp.exp(log_beta + suffix), 0.0)
        outs.append(jnp.einsum('bhts,bhsd->bhtd', a, vb))
    return jnp.concatenate(outs, axis=2).astype(q.dtype)


def stick_breaking_mixer(h, w_qkv, w_o):
    b, s, _ = h.shape
    qkv = (h @ w_qkv).reshape(b, s, 3, SB_HEADS, SB_HEAD_DIM)
    q = jnp.transpose(qkv[:, :, 0], (0, 2, 1, 3))
    k = jnp.transpose(qkv[:, :, 1], (0, 2, 1, 3))
    v = jnp.transpose(qkv[:, :, 2], (0, 2, 1, 3))
    o = stick_breaking_attention(q, k, v)
    o = jnp.transpose(o, (0, 2, 1, 3)).reshape(b, s, D_MODEL)
    return o @ w_o


def spatial_gating_mixer(h, w_in, gain, w_s, b_s, w_out):
    b, s, _ = h.shape
    uv = jax.nn.gelu(h @ w_in)
    u, v = uv[..., :SGU_FFN], uv[..., SGU_FFN:]
    v = rmsnorm(v, gain)
    vc = v.reshape(b, s // SGU_CHUNK, SGU_CHUNK, SGU_GROUPS, SGU_GROUP_W)
    causal = jnp.tril(jnp.ones((SGU_CHUNK, SGU_CHUNK), dtype=bool))
    ws = jnp.where(causal[None], w_s, 0.0).astype(v.dtype)
    mixed = jnp.einsum('gts,bnsgc->bntgc', ws, vc) + jnp.transpose(b_s)[None, None, :, :, None]
    y = u * mixed.reshape(b, s, SGU_FFN)
    return y @ w_out


def squared_relu_mlp(h, w1, w2):
    return jnp.square(jax.nn.relu(h @ w1)) @ w2


def _fwd_setup_inputs(seed: int = 0) -> dict:
    key = jax.random.key(seed)
    ks = jax.random.split(key, 14)
    f32 = jnp.float32
    nrm = lambda k, shape, scale: jax.random.normal(k, shape, f32) * scale
    return {
        "x": jax.random.normal(ks[0], (BATCH, SEQ, D_MODEL), f32),
        "norm_mix": 1.0 + nrm(ks[1], (DEPTH, D_MODEL), 0.02),
        "norm_mlp": 1.0 + nrm(ks[2], (DEPTH, D_MODEL), 0.02),
        "sb_wqkv": nrm(ks[3], (N_SB, D_MODEL, 3 * D_MODEL), D_MODEL ** -0.5),
        "sb_wo": nrm(ks[4], (N_SB, D_MODEL, D_MODEL), D_MODEL ** -0.5),
        "sgu_win": nrm(ks[5], (N_SGU, D_MODEL, 2 * SGU_FFN), D_MODEL ** -0.5),
        "sgu_gain": 1.0 + nrm(ks[6], (N_SGU, SGU_FFN), 0.02),
        "sgu_ws": nrm(ks[7], (N_SGU, SGU_GROUPS, SGU_CHUNK, SGU_CHUNK), SGU_CHUNK ** -0.5),
        "sgu_bs": 1.0 + nrm(ks[8], (N_SGU, SGU_GROUPS, SGU_CHUNK), 0.02),
        "sgu_wout": nrm(ks[9], (N_SGU, SGU_FFN, D_MODEL), SGU_FFN ** -0.5),
        "mlp_w1": nrm(ks[10], (DEPTH, D_MODEL, MLP_HIDDEN), D_MODEL ** -0.5),
        "mlp_w2": nrm(ks[11], (DEPTH, MLP_HIDDEN, D_MODEL), 0.5 * MLP_HIDDEN ** -0.5),
        "final_norm": 1.0 + nrm(ks[12], (D_MODEL,), 0.02),
    }


def _fwd_reference(x, norm_mix, norm_mlp, sb_wqkv, sb_wo, sgu_win, sgu_gain, sgu_ws,
              sgu_bs, sgu_wout, mlp_w1, mlp_w2, final_norm):
    for i in range(DEPTH):
        h = rmsnorm(x, norm_mix[i])
        j = i // N_MIXERS
        if i % N_MIXERS == 0:
            x = x + stick_breaking_mixer(h, sb_wqkv[j], sb_wo[j])
        else:
            x = x + spatial_gating_mixer(h, sgu_win[j], sgu_gain[j], sgu_ws[j],
                                         sgu_bs[j], sgu_wout[j])
        x = x + squared_relu_mlp(rmsnorm(x, norm_mlp[i]), mlp_w1[i], mlp_w2[i])
    return rmsnorm(x, final_norm)


import jax as _jax
import jax.numpy as _jnp

TWIN_FORMAT = 'train_step'
FWD_PARAMS = ['x', 'norm_mix', 'norm_mlp', 'sb_wqkv', 'sb_wo', 'sgu_win', 'sgu_gain', 'sgu_ws', 'sgu_bs', 'sgu_wout', 'mlp_w1', 'mlp_w2', 'final_norm']
TWIN_WEIGHTS = ['norm_mix', 'norm_mlp', 'sb_wqkv', 'sb_wo', 'sgu_win', 'sgu_gain', 'sgu_ws', 'sgu_bs', 'sgu_wout', 'mlp_w1', 'mlp_w2', 'final_norm']
TWIN_DIFF_INPUT = 'x'
TWIN_INPUTS = ['x', 'norm_mix', 'norm_mlp', 'sb_wqkv', 'sb_wo', 'sgu_win', 'sgu_gain', 'sgu_ws', 'sgu_bs', 'sgu_wout', 'mlp_w1', 'mlp_w2', 'final_norm', 'loss_target', 'm_norm_mix', 'm_norm_mlp', 'm_sb_wqkv', 'm_sb_wo', 'm_sgu_win', 'm_sgu_gain', 'm_sgu_ws', 'm_sgu_bs', 'm_sgu_wout', 'm_mlp_w1', 'm_mlp_w2', 'm_final_norm', 'v_norm_mix', 'v_norm_mlp', 'v_sb_wqkv', 'v_sb_wo', 'v_sgu_win', 'v_sgu_gain', 'v_sgu_ws', 'v_sgu_bs', 'v_sgu_wout', 'v_mlp_w1', 'v_mlp_w2', 'v_final_norm']
TWIN_OUTPUTS = ['loss', 'grad_x', 'grad_norm_mix', 'grad_norm_mlp', 'grad_sb_wqkv', 'grad_sb_wo', 'grad_sgu_win', 'grad_sgu_gain', 'grad_sgu_ws', 'grad_sgu_bs', 'grad_sgu_wout', 'grad_mlp_w1', 'grad_mlp_w2', 'grad_final_norm', 'delta_norm_mix', 'delta_norm_mlp', 'delta_sb_wqkv', 'delta_sb_wo', 'delta_sgu_win', 'delta_sgu_gain', 'delta_sgu_ws', 'delta_sgu_bs', 'delta_sgu_wout', 'delta_mlp_w1', 'delta_mlp_w2', 'delta_final_norm', 'new_m_norm_mix', 'new_m_norm_mlp', 'new_m_sb_wqkv', 'new_m_sb_wo', 'new_m_sgu_win', 'new_m_sgu_gain', 'new_m_sgu_ws', 'new_m_sgu_bs', 'new_m_sgu_wout', 'new_m_mlp_w1', 'new_m_mlp_w2', 'new_m_final_norm', 'new_v_norm_mix', 'new_v_norm_mlp', 'new_v_sb_wqkv', 'new_v_sb_wo', 'new_v_sgu_win', 'new_v_sgu_gain', 'new_v_sgu_ws', 'new_v_sgu_bs', 'new_v_sgu_wout', 'new_v_mlp_w1', 'new_v_mlp_w2', 'new_v_final_norm']
TWIN_LEAF_KINDS = {'loss': 'loss', 'grad_x': 'grad_x', 'grad_norm_mix': 'grad_w', 'grad_norm_mlp': 'grad_w', 'grad_sb_wqkv': 'grad_w', 'grad_sb_wo': 'grad_w', 'grad_sgu_win': 'grad_w', 'grad_sgu_gain': 'grad_w', 'grad_sgu_ws': 'grad_w', 'grad_sgu_bs': 'grad_w', 'grad_sgu_wout': 'grad_w', 'grad_mlp_w1': 'grad_w', 'grad_mlp_w2': 'grad_w', 'grad_final_norm': 'grad_w', 'delta_norm_mix': 'delta_w', 'delta_norm_mlp': 'delta_w', 'delta_sb_wqkv': 'delta_w', 'delta_sb_wo': 'delta_w', 'delta_sgu_win': 'delta_w', 'delta_sgu_gain': 'delta_w', 'delta_sgu_ws': 'delta_w', 'delta_sgu_bs': 'delta_w', 'delta_sgu_wout': 'delta_w', 'delta_mlp_w1': 'delta_w', 'delta_mlp_w2': 'delta_w', 'delta_final_norm': 'delta_w', 'new_m_norm_mix': 'new_m', 'new_m_norm_mlp': 'new_m', 'new_m_sb_wqkv': 'new_m', 'new_m_sb_wo': 'new_m', 'new_m_sgu_win': 'new_m', 'new_m_sgu_gain': 'new_m', 'new_m_sgu_ws': 'new_m', 'new_m_sgu_bs': 'new_m', 'new_m_sgu_wout': 'new_m', 'new_m_mlp_w1': 'new_m', 'new_m_mlp_w2': 'new_m', 'new_m_final_norm': 'new_m', 'new_v_norm_mix': 'new_v', 'new_v_norm_mlp': 'new_v', 'new_v_sb_wqkv': 'new_v', 'new_v_sb_wo': 'new_v', 'new_v_sgu_win': 'new_v', 'new_v_sgu_gain': 'new_v', 'new_v_sgu_ws': 'new_v', 'new_v_sgu_bs': 'new_v', 'new_v_sgu_wout': 'new_v', 'new_v_mlp_w1': 'new_v', 'new_v_mlp_w2': 'new_v', 'new_v_final_norm': 'new_v'}


def _forward(args):
    return _fwd_reference(*[args[k] for k in FWD_PARAMS])


def _output_shape():
    out = _jax.eval_shape(lambda: _forward(_fwd_setup_inputs(0)))
    return out.shape, out.dtype

N_MICROBATCH = 1
ADAM_LR = 0.001
ADAM_B1 = 0.9
ADAM_B2 = 0.999
ADAM_EPS = 1e-08
ADAM_WD = 0.01
ADAM_STEP = 10
PER_EXAMPLE_BATCH_AXIS = {'x': 0, 'loss_target': 0}
SHARED_INPUTS = []
_WEIGHT_DTYPES = {'norm_mix': _jnp.float32, 'norm_mlp': _jnp.float32, 'sb_wqkv': _jnp.float32, 'sb_wo': _jnp.float32, 'sgu_win': _jnp.float32, 'sgu_gain': _jnp.float32, 'sgu_ws': _jnp.float32, 'sgu_bs': _jnp.float32, 'sgu_wout': _jnp.float32, 'mlp_w1': _jnp.float32, 'mlp_w2': _jnp.float32, 'final_norm': _jnp.float32}
MOMENT_SCALE = {'norm_mix': 1.583398e-01, 'norm_mlp': 1.172890e-01, 'sb_wqkv': 9.646440e-02, 'sb_wo': 1.435789e-01, 'sgu_win': 7.354959e-02, 'sgu_gain': 5.174918e-02, 'sgu_ws': 6.962811e-02, 'sgu_bs': 1.007825e-01, 'sgu_wout': 1.338340e-01, 'mlp_w1': 5.824656e-02, 'mlp_w2': 2.250639e-01, 'final_norm': 6.479949e+01}


def _to_microbatches(a, axis):
    t = _jnp.moveaxis(a, axis, 0)
    t = t.reshape((N_MICROBATCH, t.shape[0] // N_MICROBATCH) + t.shape[1:])
    return _jnp.moveaxis(t, 1, axis + 1)


def setup_inputs(seed: int = 0) -> dict:
    inp = _fwd_setup_inputs(seed)
    key = _jax.random.fold_in(_jax.random.key(seed), 7919)
    shape, _ = _output_shape()
    out = dict(inp)
    out["loss_target"] = _jax.random.normal(_jax.random.fold_in(key, 0), shape, _jnp.float32)
    for i, name in enumerate(TWIN_WEIGHTS):
        w = inp[name].astype(_jnp.float32)
        if MOMENT_SCALE is None:
            s = _jnp.sqrt(_jnp.mean(_jnp.square(w)) + 1e-30)
        else:
            s = MOMENT_SCALE[name]
        km, kv = _jax.random.split(_jax.random.fold_in(key, i + 1))
        out[name] = w
        out["m_" + name] = s * _jax.random.normal(km, w.shape, _jnp.float32)
        out["v_" + name] = (s * s) * _jax.random.uniform(kv, w.shape, _jnp.float32, 0.5, 1.5)
    if N_MICROBATCH > 1:
        for name, axis in PER_EXAMPLE_BATCH_AXIS.items():
            out[name] = _to_microbatches(out[name], axis)
    return {'x': out['x'], 'norm_mix': out['norm_mix'], 'norm_mlp': out['norm_mlp'], 'sb_wqkv': out['sb_wqkv'], 'sb_wo': out['sb_wo'], 'sgu_win': out['sgu_win'], 'sgu_gain': out['sgu_gain'], 'sgu_ws': out['sgu_ws'], 'sgu_bs': out['sgu_bs'], 'sgu_wout': out['sgu_wout'], 'mlp_w1': out['mlp_w1'], 'mlp_w2': out['mlp_w2'], 'final_norm': out['final_norm'], 'loss_target': out['loss_target'], 'm_norm_mix': out['m_norm_mix'], 'm_norm_mlp': out['m_norm_mlp'], 'm_sb_wqkv': out['m_sb_wqkv'], 'm_sb_wo': out['m_sb_wo'], 'm_sgu_win': out['m_sgu_win'], 'm_sgu_gain': out['m_sgu_gain'], 'm_sgu_ws': out['m_sgu_ws'], 'm_sgu_bs': out['m_sgu_bs'], 'm_sgu_wout': out['m_sgu_wout'], 'm_mlp_w1': out['m_mlp_w1'], 'm_mlp_w2': out['m_mlp_w2'], 'm_final_norm': out['m_final_norm'], 'v_norm_mix': out['v_norm_mix'], 'v_norm_mlp': out['v_norm_mlp'], 'v_sb_wqkv': out['v_sb_wqkv'], 'v_sb_wo': out['v_sb_wo'], 'v_sgu_win': out['v_sgu_win'], 'v_sgu_gain': out['v_sgu_gain'], 'v_sgu_ws': out['v_sgu_ws'], 'v_sgu_bs': out['v_sgu_bs'], 'v_sgu_wout': out['v_sgu_wout'], 'v_mlp_w1': out['v_mlp_w1'], 'v_mlp_w2': out['v_mlp_w2'], 'v_final_norm': out['v_final_norm']}


def _loss(weights, diff, rest, loss_target):
    with _jax.named_scope("forward"):
        args = {**rest, TWIN_DIFF_INPUT: diff, **{k: w.astype(_WEIGHT_DTYPES[k]) for k, w in weights.items()}}
        y = _forward(args)
    with _jax.named_scope("loss_head"):
        err = _jnp.square(y.astype(_jnp.float32) - loss_target)
        return 0.5 * _jnp.sum(_jnp.mean(err, axis=-1)) if err.ndim else 0.5 * err


def _adamw(w, g, m, v):
    m = ADAM_B1 * m + (1.0 - ADAM_B1) * g
    v = ADAM_B2 * v + (1.0 - ADAM_B2) * _jnp.square(g)
    m_hat = m / (1.0 - ADAM_B1 ** ADAM_STEP)
    v_hat = v / (1.0 - ADAM_B2 ** ADAM_STEP)
    delta = -ADAM_LR * (m_hat / (_jnp.sqrt(v_hat) + ADAM_EPS) + ADAM_WD * w)
    return delta, m, v


def reference(x, norm_mix, norm_mlp, sb_wqkv, sb_wo, sgu_win, sgu_gain, sgu_ws, sgu_bs, sgu_wout, mlp_w1, mlp_w2, final_norm, loss_target, m_norm_mix, m_norm_mlp, m_sb_wqkv, m_sb_wo, m_sgu_win, m_sgu_gain, m_sgu_ws, m_sgu_bs, m_sgu_wout, m_mlp_w1, m_mlp_w2, m_final_norm, v_norm_mix, v_norm_mlp, v_sb_wqkv, v_sb_wo, v_sgu_win, v_sgu_gain, v_sgu_ws, v_sgu_bs, v_sgu_wout, v_mlp_w1, v_mlp_w2, v_final_norm):
    given = dict(x=x, norm_mix=norm_mix, norm_mlp=norm_mlp, sb_wqkv=sb_wqkv, sb_wo=sb_wo, sgu_win=sgu_win, sgu_gain=sgu_gain, sgu_ws=sgu_ws, sgu_bs=sgu_bs, sgu_wout=sgu_wout, mlp_w1=mlp_w1, mlp_w2=mlp_w2, final_norm=final_norm, loss_target=loss_target, m_norm_mix=m_norm_mix, m_norm_mlp=m_norm_mlp, m_sb_wqkv=m_sb_wqkv, m_sb_wo=m_sb_wo, m_sgu_win=m_sgu_win, m_sgu_gain=m_sgu_gain, m_sgu_ws=m_sgu_ws, m_sgu_bs=m_sgu_bs, m_sgu_wout=m_sgu_wout, m_mlp_w1=m_mlp_w1, m_mlp_w2=m_mlp_w2, m_final_norm=m_final_norm, v_norm_mix=v_norm_mix, v_norm_mlp=v_norm_mlp, v_sb_wqkv=v_sb_wqkv, v_sb_wo=v_sb_wo, v_sgu_win=v_sgu_win, v_sgu_gain=v_sgu_gain, v_sgu_ws=v_sgu_ws, v_sgu_bs=v_sgu_bs, v_sgu_wout=v_sgu_wout, v_mlp_w1=v_mlp_w1, v_mlp_w2=v_mlp_w2, v_final_norm=v_final_norm)
    weights = {n: given[n] for n in TWIN_WEIGHTS}
    shared = {n: given[n] for n in SHARED_INPUTS}
    per_example = {n: given[n] for n in ['x']}
    grad_fn = _jax.value_and_grad(_loss, argnums=(0, 1))

    def one_microbatch(ex, loss_target):
        ex = dict(ex)
        diff = ex.pop(TWIN_DIFF_INPUT)
        return grad_fn(weights, diff, {**shared, **ex}, loss_target)

    if N_MICROBATCH == 1:
        loss, (grad_w, grad_x) = one_microbatch(per_example, given["loss_target"])
    else:
        def body(carry, xs):
            loss_sum, grad_sum = carry
            l_k, (gw_k, gx_k) = one_microbatch(xs[0], xs[1])
            with _jax.named_scope("update"):
                return (loss_sum + l_k, _jax.tree.map(_jnp.add, grad_sum, gw_k)), gx_k

        init = (_jnp.zeros((), _jnp.float32), _jax.tree.map(_jnp.zeros_like, weights))
        (loss, grad_w), grad_x = _jax.lax.scan(body, init, (per_example, given["loss_target"]))
    with _jax.named_scope("update"):
        delta_w, new_m, new_v = {}, {}, {}
        for n in TWIN_WEIGHTS:
            delta_w[n], new_m[n], new_v[n] = _adamw(weights[n], grad_w[n], given["m_" + n], given["v_" + n])
    return (loss, grad_x, *[grad_w[n] for n in TWIN_WEIGHTS], *[delta_w[n] for n in TWIN_WEIGHTS],
            *[new_m[n] for n in TWIN_WEIGHTS], *[new_v[n] for n in TWIN_WEIGHTS])
```

```python
import jax
import jax.numpy as jnp
from jax import lax
from jax.experimental import pallas as pl
from jax.experimental.pallas import tpu as pltpu

F32 = jnp.float32
BF16 = jnp.bfloat16
MESH = pl.DeviceIdType.MESH

EPS = 1e-6
HEAD_DIM = 64
LANES = 128
Q_TILE = 128
SGU_CHUNK = 128
SGU_GROUPS = 8
N_CHIPS = 4
N_DEV = 8
ADAM_LR = 0.001
ADAM_B1 = 0.9
ADAM_B2 = 0.999
ADAM_EPS = 1e-08
ADAM_WD = 0.01
ADAM_STEP = 10
GELU_C0 = 0.7978845608028654
GELU_C1 = 0.044715
VMEM_LIMIT = 48 * 1024 * 1024
ROW_TILE = 1024
NT = (((1,), (1,)), ((), ()))
TN = (((0,), (0,)), ((), ()))


def _params(n_axes):
    return pltpu.CompilerParams(dimension_semantics=("arbitrary",) * n_axes, vmem_limit_bytes=VMEM_LIMIT)


def _rstd(x):
    return lax.rsqrt(jnp.mean(x * x, axis=-1, keepdims=True) + EPS)


def _norm_bwd(dh, x, gain):
    rstd = _rstd(x)
    xh = x * rstd
    dhg = dh * gain
    dx = rstd * (dhg - xh * jnp.mean(dhg * xh, axis=-1, keepdims=True))
    return dx, jnp.sum(dh * xh, axis=0, keepdims=True)


def _gelu(x):
    return 0.5 * x * (1.0 + jnp.tanh(GELU_C0 * (x + GELU_C1 * x * x * x)))


def _gelu_grad(x):
    t = jnp.tanh(GELU_C0 * (x + GELU_C1 * x * x * x))
    return 0.5 * (1.0 + t) + 0.5 * x * (1.0 - t * t) * (GELU_C0 * (1.0 + 3.0 * GELU_C1 * x * x))


def _act(a, act):
    if act == "relu2":
        r = jnp.maximum(a.astype(F32), 0.0)
        return (r * r).astype(BF16)
    return a.astype(BF16)


def _norm_matmul(x, gain, wg, layer, name):
    T, D = x.shape
    nsh, _, _, ns = wg.shape
    tm = min(T, ROW_TILE)

    def body(x_ref, g_ref, w_ref, y_ref, h_ref):
        @pl.when(pl.program_id(1) == 0)
        def _():
            xv = x_ref[...]
            h_ref[...] = (xv * _rstd(xv) * g_ref[...]).astype(BF16)

        y_ref[...] = jnp.dot(h_ref[...], w_ref[...], preferred_element_type=F32).astype(BF16)

    return pl.pallas_call(
        body,
        name=name,
        grid=(T // tm, nsh),
        in_specs=[
            pl.BlockSpec((tm, D), lambda i, j: (i, 0)),
            pl.BlockSpec((1, D), lambda i, j: (0, 0)),
            pl.BlockSpec((None, None, D, ns), lambda i, j: (j, layer, 0, 0)),
        ],
        out_specs=[pl.BlockSpec((tm, ns), lambda i, j: (i, j)), pl.BlockSpec((tm, D), lambda i, j: (i, 0))],
        out_shape=[jax.ShapeDtypeStruct((T, nsh * ns), BF16), jax.ShapeDtypeStruct((T, D), BF16)],
        compiler_params=_params(2),
    )(x, gain.reshape(1, D), wg)


def _act_matmul_res(a, wg, layer, x_in, act, name):
    T, K = a.shape
    nsh, _, kq, D = wg.shape
    tm = min(T, ROW_TILE)

    def body(a_ref, w_ref, x_ref, o_ref):
        @pl.when(pl.program_id(1) == 0)
        def _():
            o_ref[...] = x_ref[...]

        o_ref[...] += jnp.dot(_act(a_ref[...], act), w_ref[...], preferred_element_type=F32)

    return pl.pallas_call(
        body,
        name=name,
        grid=(T // tm, nsh),
        in_specs=[
            pl.BlockSpec((tm, kq), lambda i, k: (i, k)),
            pl.BlockSpec((None, None, kq, D), lambda i, k: (k, layer, 0, 0)),
            pl.BlockSpec((tm, D), lambda i, k: (i, 0)),
        ],
        out_specs=pl.BlockSpec((tm, D), lambda i, k: (i, 0)),
        out_shape=jax.ShapeDtypeStruct((T, D), F32),
        compiler_params=_params(2),
    )(a, wg, x_in)


def _matmul_nt(g, wg, layer, a, name):
    T, D = g.shape
    nsh, _, kq, _ = wg.shape
    tm = min(T, ROW_TILE)

    def body(g_ref, w_ref, *rest):
        r = lax.dot_general(g_ref[...], w_ref[...], NT, preferred_element_type=F32)
        if a is None:
            (o_ref,) = rest
        else:
            a_ref, o_ref = rest
            r = r * (2.0 * jnp.maximum(a_ref[...].astype(F32), 0.0))
        o_ref[...] = r.astype(BF16)

    in_specs = [
        pl.BlockSpec((tm, D), lambda i, k: (i, 0)),
        pl.BlockSpec((None, None, kq, D), lambda i, k: (k, layer, 0, 0)),
    ]
    args = [g, wg]
    if a is not None:
        in_specs.append(pl.BlockSpec((tm, kq), lambda i, k: (i, k)))
        args.append(a)
    return pl.pallas_call(
        body,
        name=name,
        grid=(T // tm, nsh),
        in_specs=in_specs,
        out_specs=pl.BlockSpec((tm, kq), lambda i, k: (i, k)),
        out_shape=jax.ShapeDtypeStruct((T, nsh * kq), BF16),
        compiler_params=_params(2),
    )(*args)


def _matmul_nt_norm_bwd(da, wg, layer, x, gain, dres, name):
    T, D = x.shape
    nsh, _, _, ns = wg.shape
    tm = min(T, ROW_TILE // 2)
    n_i = T // tm

    def body(da_ref, w_ref, x_ref, g_ref, r_ref, dx_ref, dxb_ref, dg_ref, acc_ref):
        i, j = pl.program_id(0), pl.program_id(1)

        @pl.when(j == 0)
        def _():
            acc_ref[...] = jnp.zeros_like(acc_ref)

        acc_ref[...] += lax.dot_general(da_ref[...], w_ref[...], NT, preferred_element_type=F32)

        @pl.when(j == nsh - 1)
        def _():
            dx, dg = _norm_bwd(acc_ref[...], x_ref[...], g_ref[...])
            dx = dx + r_ref[...]
            dx_ref[...] = dx
            dxb_ref[...] = dx.astype(BF16)

            @pl.when(i == 0)
            def _():
                dg_ref[...] = dg

            @pl.when(i > 0)
            def _():
                dg_ref[...] += dg

    del n_i
    return pl.pallas_call(
        body,
        name=name,
        grid=(T // tm, nsh),
        in_specs=[
            pl.BlockSpec((tm, ns), lambda i, j: (i, j)),
            pl.BlockSpec((None, None, D, ns), lambda i, j: (j, layer, 0, 0)),
            pl.BlockSpec((tm, D), lambda i, j: (i, 0)),
            pl.BlockSpec((1, D), lambda i, j: (0, 0)),
            pl.BlockSpec((tm, D), lambda i, j: (i, 0)),
        ],
        out_specs=[
            pl.BlockSpec((tm, D), lambda i, j: (i, 0)),
            pl.BlockSpec((tm, D), lambda i, j: (i, 0)),
            pl.BlockSpec((1, D), lambda i, j: (0, 0)),
        ],
        out_shape=[
            jax.ShapeDtypeStruct((T, D), F32),
            jax.ShapeDtypeStruct((T, D), BF16),
            jax.ShapeDtypeStruct((1, D), F32),
        ],
        scratch_shapes=[pltpu.VMEM((tm, D), F32)],
        compiler_params=_params(2),
    )(da, wg, x, gain.reshape(1, D), dres)


def _matmul_tn(lhs, rhs, bufs, layer, n_layers, shard_lhs, act, name):
    T = lhs.shape[0]
    rows = lhs.shape[1] // N_CHIPS if shard_lhs else lhs.shape[1]
    cols = rhs.shape[1] if shard_lhs else rhs.shape[1] // N_CHIPS
    tt = min(T, ROW_TILE)
    n_t = T // tt

    def body(l_ref, r_ref, *rest):
        o32_ref, o16_ref = rest[-2:]
        t = pl.program_id(1)
        upd = lax.dot_general(_act(l_ref[...], act), r_ref[...].astype(BF16), TN, preferred_element_type=F32)

        @pl.when(t == 0)
        def _():
            o32_ref[...] = upd

        @pl.when(t > 0)
        def _():
            o32_ref[...] += upd

        @pl.when(t == n_t - 1)
        def _():
            o16_ref[...] = o32_ref[...].astype(BF16)

    if shard_lhs:
        in_specs = [pl.BlockSpec((tt, rows), lambda s, t: (t, s)), pl.BlockSpec((tt, cols), lambda s, t: (t, 0))]
    else:
        in_specs = [pl.BlockSpec((tt, rows), lambda s, t: (t, 0)), pl.BlockSpec((tt, cols), lambda s, t: (t, s))]
    args = [lhs, rhs]
    aliases = {}
    if bufs is not None:
        in_specs += [pl.BlockSpec(memory_space=pl.ANY)] * 2
        args += list(bufs)
        aliases = {2: 0, 3: 1}
    shape = (N_CHIPS, n_layers, rows, cols)
    return pl.pallas_call(
        body,
        name=name,
        grid=(N_CHIPS, n_t),
        in_specs=in_specs,
        out_specs=[pl.BlockSpec((None, None, rows, cols), lambda s, t: (s, layer, 0, 0))] * 2,
        out_shape=[jax.ShapeDtypeStruct(shape, F32), jax.ShapeDtypeStruct(shape, BF16)],
        input_output_aliases=aliases,
        compiler_params=_params(2),
    )(*args)


def _split_dot(x, m):
    hi = x.astype(BF16)
    lo = (x - hi.astype(F32)).astype(BF16)
    return jnp.dot(hi, m, preferred_element_type=F32) + jnp.dot(lo, m, preferred_element_type=F32)


def _suffix_matrix(inclusive):
    j = lax.broadcasted_iota(jnp.int32, (Q_TILE, 2 * Q_TILE), 0)
    s = lax.broadcasted_iota(jnp.int32, (Q_TILE, 2 * Q_TILE), 1)
    later = (j >= s) if inclusive else (j > s)
    return jnp.where((s >= Q_TILE) | later, 1.0, 0.0).astype(BF16)


def _log_beta(z):
    return jnp.minimum(z, 0.0) - jnp.log(1.0 + jnp.exp(-jnp.abs(z)))


def _attn_specs(S, n_pairs):
    nq = S // Q_TILE
    q_spec = pl.BlockSpec((Q_TILE, LANES), lambda b, p, i: (b * nq + i, p))
    k_spec = pl.BlockSpec((S, LANES), lambda b, p, i: (b, n_pairs + p))
    v_spec = pl.BlockSpec((S, LANES), lambda b, p, i: (b, 2 * n_pairs + p))
    return nq, q_spec, k_spec, v_spec


def _head_masks():
    lane = lax.broadcasted_iota(jnp.int32, (1, LANES), 1)
    return [lane < HEAD_DIM, lane >= HEAD_DIM]


def _attn_fwd(qkv, n_seq, S, D, name):
    T = n_seq * S
    n_pairs = D // LANES
    nq, q_spec, k_spec, v_spec = _attn_specs(S, n_pairs)
    scale = HEAD_DIM ** -0.5

    def body(q_ref, k_ref, v_ref, o_ref):
        qi = pl.program_id(2)
        t_idx = lax.broadcasted_iota(jnp.int32, (Q_TILE, Q_TILE), 0)
        s_idx = lax.broadcasted_iota(jnp.int32, (Q_TILE, Q_TILE), 1)
        past = s_idx < t_idx
        sfx = _suffix_matrix(False)
        q = q_ref[...]
        acc = jnp.zeros((Q_TILE, LANES), F32)
        for hm in _head_masks():
            qh = jnp.where(hm, q, 0) * scale

            def block(kb, carry, acc, diag, hm=hm, qh=qh):
                off = pl.multiple_of(kb * Q_TILE, Q_TILE)
                kt = k_ref[pl.ds(off, Q_TILE), :]
                vt = jnp.where(hm, v_ref[pl.ds(off, Q_TILE), :], 0)
                z = lax.dot_general(qh, kt, NT, preferred_element_type=F32)
                lb = _log_beta(z)
                l1 = lb - z
                if diag:
                    l1 = jnp.where(past, l1, 0.0)
                r = _split_dot(l1, sfx)
                a = jnp.exp(lb + r[:, :Q_TILE] + carry)
                if diag:
                    a = jnp.where(past, a, 0.0)
                acc = acc + jnp.dot(a.astype(BF16), vt, preferred_element_type=F32)
                return carry + r[:, Q_TILE:], acc

            carry, acc = block(qi, jnp.zeros((Q_TILE, Q_TILE), F32), acc, True)
            carry, acc = lax.fori_loop(
                0, qi, lambda j, ca, block=block: block(qi - 1 - j, ca[0], ca[1], False), (carry, acc)
            )
        o_ref[...] = acc

    return pl.pallas_call(
        body,
        name=name,
        grid=(n_seq, n_pairs, nq),
        in_specs=[q_spec, k_spec, v_spec],
        out_specs=pl.BlockSpec((Q_TILE, LANES), lambda b, p, i: (b * nq + i, p)),
        out_shape=jax.ShapeDtypeStruct((T, D), F32),
        compiler_params=_params(3),
    )(qkv, qkv, qkv)


def _attn_bwd(qkv, o, do, n_seq, S, D, name):
    T = n_seq * S
    n_pairs = D // LANES
    nq, q_spec, k_spec, v_spec = _attn_specs(S, n_pairs)
    scale = HEAD_DIM ** -0.5
    row_spec = pl.BlockSpec((Q_TILE, LANES), lambda b, p, i: (b * nq + i, p))
    seq_spec = pl.BlockSpec((S, LANES), lambda b, p, i: (b, p))

    def body(q_ref, k_ref, v_ref, o_ref, do_ref, dq_ref, dk_ref, dv_ref, dk_acc, dv_acc):
        qi = pl.program_id(2)

        @pl.when(qi == 0)
        def _():
            dk_acc[...] = jnp.zeros_like(dk_acc)
            dv_acc[...] = jnp.zeros_like(dv_acc)

        t_idx = lax.broadcasted_iota(jnp.int32, (Q_TILE, Q_TILE), 0)
        s_idx = lax.broadcasted_iota(jnp.int32, (Q_TILE, Q_TILE), 1)
        past = s_idx < t_idx
        sfx = _suffix_matrix(False)
        sfx_incl = _suffix_matrix(True)
        q = q_ref[...]
        do = do_ref[...]
        prod = do.astype(F32) * o_ref[...]
        dq = jnp.zeros((Q_TILE, LANES), F32)
        for hm in _head_masks():
            qh = jnp.where(hm, q, 0) * scale
            doh = jnp.where(hm, do, 0)
            delta = jnp.sum(jnp.where(hm, prod, 0.0), axis=-1, keepdims=True)

            def block(kb, c1, c2, dq, diag, hm=hm, qh=qh, doh=doh, delta=delta):
                off = pl.multiple_of(kb * Q_TILE, Q_TILE)
                kt = k_ref[pl.ds(off, Q_TILE), :]
                vt = v_ref[pl.ds(off, Q_TILE), :]
                z = lax.dot_general(qh, kt, NT, preferred_element_type=F32)
                lb = _log_beta(z)
                l1 = lb - z
                if diag:
                    l1 = jnp.where(past, l1, 0.0)
                r = _split_dot(l1, sfx)
                a = jnp.exp(lb + r[:, :Q_TILE] + c1)
                if diag:
                    a = jnp.where(past, a, 0.0)
                ab = a.astype(BF16)
                g = ab.astype(F32) * lax.dot_general(doh, vt, NT, preferred_element_type=F32)
                r2 = _split_dot(g, sfx_incl)
                earlier = delta - (r2[:, :Q_TILE] + c2)
                beta = jnp.exp(lb)
                dz = g * (1.0 - beta) - earlier * beta
                if diag:
                    dz = jnp.where(past, dz, 0.0)
                dzb = dz.astype(BF16)
                dq = dq + jnp.dot(dzb, jnp.where(hm, kt, 0), preferred_element_type=F32)
                dk_acc[pl.ds(off, Q_TILE), :] += lax.dot_general(dzb, qh, TN, preferred_element_type=F32)
                dv_acc[pl.ds(off, Q_TILE), :] += lax.dot_general(ab, doh, TN, preferred_element_type=F32)
                return c1 + r[:, Q_TILE:], c2 + r2[:, Q_TILE:], dq

            zero = jnp.zeros((Q_TILE, Q_TILE), F32)
            c1, c2, dq = block(qi, zero, zero, dq, True)
            c1, c2, dq = lax.fori_loop(
                0, qi, lambda j, c, block=block: block(qi - 1 - j, c[0], c[1], c[2], False), (c1, c2, dq)
            )
        dq_ref[...] = (dq * scale).astype(BF16)

        @pl.when(qi == nq - 1)
        def _():
            dk_ref[...] = dk_acc[...].astype(BF16)
            dv_ref[...] = dv_acc[...].astype(BF16)

    return pl.pallas_call(
        body,
        name=name,
        grid=(n_seq, n_pairs, nq),
        in_specs=[q_spec, k_spec, v_spec, row_spec, row_spec],
        out_specs=[row_spec, seq_spec, seq_spec],
        out_shape=[jax.ShapeDtypeStruct((T, D), BF16)] * 3,
        scratch_shapes=[pltpu.VMEM((S, LANES), F32), pltpu.VMEM((S, LANES), F32)],
        compiler_params=_params(3),
    )(qkv, qkv, qkv, o, do)


def _causal_ws(ws_ref, g):
    t = lax.broadcasted_iota(jnp.int32, (SGU_CHUNK, SGU_CHUNK), 0)
    s = lax.broadcasted_iota(jnp.int32, (SGU_CHUNK, SGU_CHUNK), 1)
    return jnp.where(s <= t, ws_ref[g], 0.0)


def _sgu_fwd(a, gain, ws, bsb, name):
    T, F2 = a.shape
    F = F2 // 2
    gw = F // SGU_GROUPS

    def body(a_ref, gain_ref, ws_ref, bsb_ref, y_ref):
        v = _gelu(a_ref[:, F:].astype(F32))
        vn = (v * _rstd(v) * gain_ref[...]).astype(BF16)
        for g in range(SGU_GROUPS):
            cs = slice(g * gw, (g + 1) * gw)
            w = _causal_ws(ws_ref, g).astype(BF16)
            mixed = jnp.dot(w, vn[:, cs], preferred_element_type=F32) + bsb_ref[g]
            y_ref[:, cs] = (_gelu(a_ref[:, cs].astype(F32)) * mixed).astype(BF16)

    return pl.pallas_call(
        body,
        name=name,
        grid=(T // SGU_CHUNK,),
        in_specs=[
            pl.BlockSpec((SGU_CHUNK, F2), lambda i: (i, 0)),
            pl.BlockSpec((1, F), lambda i: (0, 0)),
            pl.BlockSpec((SGU_GROUPS, SGU_CHUNK, SGU_CHUNK), lambda i: (0, 0, 0)),
            pl.BlockSpec((SGU_GROUPS, SGU_CHUNK, gw), lambda i: (0, 0, 0)),
        ],
        out_specs=pl.BlockSpec((SGU_CHUNK, F), lambda i: (i, 0)),
        out_shape=jax.ShapeDtypeStruct((T, F), BF16),
        compiler_params=_params(1),
    )(a, gain.reshape(1, F), ws, bsb)


def _sgu_bwd(a, dy, gain, ws, bsb, name):
    T, F2 = a.shape
    F = F2 // 2
    gw = F // SGU_GROUPS

    def body(a_ref, dy_ref, gain_ref, ws_ref, bsb_ref, da_ref, dws_ref, dbs_ref, dgain_ref, dvn_ref):
        @pl.when(pl.program_id(0) == 0)
        def _():
            dws_ref[...] = jnp.zeros_like(dws_ref)
            dbs_ref[...] = jnp.zeros_like(dbs_ref)
            dgain_ref[...] = jnp.zeros_like(dgain_ref)

        av = a_ref[:, F:].astype(F32)
        v = _gelu(av)
        rstd = _rstd(v)
        vh = v * rstd
        gain = gain_ref[...]
        vn = (vh * gain).astype(BF16)
        ones = jnp.ones((gw, SGU_CHUNK), BF16)
        for g in range(SGU_GROUPS):
            cs = slice(g * gw, (g + 1) * gw)
            w = _causal_ws(ws_ref, g).astype(BF16)
            mixed = jnp.dot(w, vn[:, cs], preferred_element_type=F32) + bsb_ref[g]
            au = a_ref[:, cs].astype(F32)
            dyc = dy_ref[:, cs].astype(F32)
            da_ref[:, cs] = (dyc * mixed * _gelu_grad(au)).astype(BF16)
            dm = (dyc * _gelu(au)).astype(BF16)
            dbs_ref[g] += jnp.dot(dm, ones, preferred_element_type=F32)
            dws_ref[g] += _causal_mask_f32(lax.dot_general(dm, vn[:, cs], NT, preferred_element_type=F32))
            dvn_ref[:, cs] = lax.dot_general(w, dm, TN, preferred_element_type=F32)
        dvn = dvn_ref[...]
        dgain_ref[...] += jnp.sum(dvn * vh, axis=0, keepdims=True)
        dvh = dvn * gain
        dv = rstd * (dvh - vh * jnp.mean(dvh * vh, axis=-1, keepdims=True))
        da_ref[:, F:] = (dv * _gelu_grad(av)).astype(BF16)

    acc_spec = pl.BlockSpec((SGU_GROUPS, SGU_CHUNK, SGU_CHUNK), lambda i: (0, 0, 0))
    acc_shape = jax.ShapeDtypeStruct((SGU_GROUPS, SGU_CHUNK, SGU_CHUNK), F32)
    return pl.pallas_call(
        body,
        name=name,
        grid=(T // SGU_CHUNK,),
        in_specs=[
            pl.BlockSpec((SGU_CHUNK, F2), lambda i: (i, 0)),
            pl.BlockSpec((SGU_CHUNK, F), lambda i: (i, 0)),
            pl.BlockSpec((1, F), lambda i: (0, 0)),
            acc_spec,
            pl.BlockSpec((SGU_GROUPS, SGU_CHUNK, gw), lambda i: (0, 0, 0)),
        ],
        out_specs=[
            pl.BlockSpec((SGU_CHUNK, F2), lambda i: (i, 0)),
            acc_spec,
            acc_spec,
            pl.BlockSpec((1, F), lambda i: (0, 0)),
        ],
        out_shape=[
            jax.ShapeDtypeStruct((T, F2), BF16),
            acc_shape,
            acc_shape,
            jax.ShapeDtypeStruct((1, F), F32),
        ],
        scratch_shapes=[pltpu.VMEM((SGU_CHUNK, F), F32)],
        compiler_params=_params(1),
    )(a, dy, gain.reshape(1, F), ws, bsb)


def _causal_mask_f32(m):
    t = lax.broadcasted_iota(jnp.int32, m.shape, 0)
    s = lax.broadcasted_iota(jnp.int32, m.shape, 1)
    return jnp.where(s <= t, m, 0.0)


def _final_loss(x, gain, target, name):
    T, D = x.shape
    tm = min(T, ROW_TILE // 2)

    def body(x_ref, g_ref, t_ref, sq_ref, dx_ref, dxb_ref, dg_ref):
        xv = x_ref[...]
        gain = g_ref[...]
        err = xv * _rstd(xv) * gain - t_ref[...]
        dx, dg = _norm_bwd(err * (1.0 / D), xv, gain)
        dx_ref[...] = dx
        dxb_ref[...] = dx.astype(BF16)
        sq = jnp.sum(err * err, axis=0, keepdims=True)

        @pl.when(pl.program_id(0) == 0)
        def _():
            sq_ref[...] = sq
            dg_ref[...] = dg

        @pl.when(pl.program_id(0) > 0)
        def _():
            sq_ref[...] += sq
            dg_ref[...] += dg

    row = pl.BlockSpec((tm, D), lambda i: (i, 0))
    vec = pl.BlockSpec((1, D), lambda i: (0, 0))
    return pl.pallas_call(
        body,
        name=name,
        grid=(T // tm,),
        in_specs=[row, vec, row],
        out_specs=[vec, row, row, vec],
        out_shape=[
            jax.ShapeDtypeStruct((1, D), F32),
            jax.ShapeDtypeStruct((T, D), F32),
            jax.ShapeDtypeStruct((T, D), BF16),
            jax.ShapeDtypeStruct((1, D), F32),
        ],
        compiler_params=_params(1),
    )(x, gain.reshape(1, D), target)


def _row_tile(rows, cols, n_arrays):
    budget = VMEM_LIMIT // 2 // (2 * n_arrays * cols * 4)
    tr = rows
    while tr > budget and tr % 16 == 0:
        tr //= 2
    return tr


def _sum_received(own, recv, name):
    R, C = own.shape
    n = recv.shape[0]
    tr = _row_tile(R, C, n + 2)

    def body(own_ref, recv_ref, o_ref):
        s = own_ref[...]
        for k in range(n):
            s = s + recv_ref[k].astype(F32)
        o_ref[...] = s

    return pl.pallas_call(
        body,
        name=name,
        grid=(R // tr,),
        in_specs=[pl.BlockSpec((tr, C), lambda i: (i, 0)), pl.BlockSpec((n, tr, C), lambda i: (0, i, 0))],
        out_specs=pl.BlockSpec((tr, C), lambda i: (i, 0)),
        out_shape=jax.ShapeDtypeStruct((R, C), F32),
        compiler_params=_params(1),
    )(own, recv)


def _adamw(w, m, v, parts, name):
    R, C = w.shape
    n = len(parts)
    tr = _row_tile(R, C, n + 7)

    def body(*refs):
        w_ref, m_ref, v_ref = refs[:3]
        g_ref, d_ref, nm_ref, nv_ref = refs[3 + n :]
        g = refs[3][...]
        for p_ref in refs[4 : 3 + n]:
            g = g + p_ref[...]
        nm = ADAM_B1 * m_ref[...] + (1.0 - ADAM_B1) * g
        nv = ADAM_B2 * v_ref[...] + (1.0 - ADAM_B2) * (g * g)
        m_hat = nm / (1.0 - ADAM_B1**ADAM_STEP)
        v_hat = nv / (1.0 - ADAM_B2**ADAM_STEP)
        g_ref[...] = g
        d_ref[...] = -ADAM_LR * (m_hat / (jnp.sqrt(v_hat) + ADAM_EPS) + ADAM_WD * w_ref[...])
        nm_ref[...] = nm
        nv_ref[...] = nv

    spec = pl.BlockSpec((tr, C), lambda i: (i, 0))
    return pl.pallas_call(
        body,
        name=name,
        grid=(R // tr,),
        in_specs=[spec] * (3 + n),
        out_specs=[spec] * 4,
        out_shape=[jax.ShapeDtypeStruct((R, C), F32)] * 4,
        compiler_params=_params(1),
    )(w, m, v, *parts)


def _chip_peers(x, y):
    return [(1 - x, y), (x, 1 - y), (1 - x, 1 - y)]


def _all_gather_chips(shards, name):
    n = len(shards)
    n_peers = N_CHIPS - 1

    def body(*refs):
        ins, outs = refs[:n], refs[n : 2 * n]
        send_sems, recv_sems, local_sems = refs[2 * n :]
        x, y, c = lax.axis_index("x"), lax.axis_index("y"), lax.axis_index("c")
        me = 2 * x + y
        peers = _chip_peers(x, y)
        started = []
        for a in range(n):
            cp = pltpu.make_async_copy(ins[a], outs[a].at[me], local_sems.at[a])
            cp.start()
            started.append(cp)
        sends = []
        for a in range(n):
            for k, (px, py) in enumerate(peers):
                cp = pltpu.make_async_remote_copy(
                    src_ref=ins[a],
                    dst_ref=outs[a].at[me],
                    send_sem=send_sems.at[a * n_peers + k],
                    recv_sem=recv_sems.at[a * n_peers + k],
                    device_id=(px, py, c),
                    device_id_type=MESH,
                )
                cp.start()
                sends.append(cp)
        for a in range(n):
            for k, (px, py) in enumerate(peers):
                pltpu.make_async_remote_copy(
                    src_ref=ins[a],
                    dst_ref=outs[a].at[2 * px + py],
                    send_sem=send_sems.at[a * n_peers + k],
                    recv_sem=recv_sems.at[a * n_peers + k],
                    device_id=(px, py, c),
                    device_id_type=MESH,
                ).wait_recv()
        for cp in sends:
            cp.wait_send()
        for cp in started:
            cp.wait()

    any_spec = pl.BlockSpec(memory_space=pl.ANY)
    return pl.pallas_call(
        body,
        name=name,
        in_specs=[any_spec] * n,
        out_specs=[any_spec] * n,
        out_shape=[jax.ShapeDtypeStruct((N_CHIPS,) + s.shape, s.dtype) for s in shards],
        scratch_shapes=[
            pltpu.SemaphoreType.DMA((n * n_peers,)),
            pltpu.SemaphoreType.DMA((n * n_peers,)),
            pltpu.SemaphoreType.DMA((n,)),
        ],
        compiler_params=pltpu.CompilerParams(has_side_effects=True),
    )(*shards)


def _scatter_grads(g32, g16, small, name):
    n = len(g32)
    n_peers = N_CHIPS - 1
    n_small = N_DEV - 1

    def body(*refs):
        g32_refs, g16_refs, small_ref = refs[:n], refs[n : 2 * n], refs[2 * n]
        outs = refs[2 * n + 1 : 4 * n + 2]
        own_refs, recv_refs, gathered_ref = outs[:n], outs[n : 2 * n], outs[2 * n]
        send_sems, recv_sems, local_sems = refs[4 * n + 2 :]
        x, y, c = lax.axis_index("x"), lax.axis_index("y"), lax.axis_index("c")
        me = 2 * x + y
        peers = _chip_peers(x, y)
        flips = [(fx, fy, fc) for fx in (0, 1) for fy in (0, 1) for fc in (0, 1)][1:]
        locals_ = []
        for a in range(n):
            cp = pltpu.make_async_copy(g32_refs[a].at[me], own_refs[a], local_sems.at[a])
            cp.start()
            locals_.append(cp)
        cp = pltpu.make_async_copy(small_ref, gathered_ref.at[4 * x + 2 * y + c], local_sems.at[n])
        cp.start()
        locals_.append(cp)
        sends = []
        for a in range(n):
            for k, (px, py) in enumerate(peers):
                cp = pltpu.make_async_remote_copy(
                    src_ref=g16_refs[a].at[2 * px + py],
                    dst_ref=recv_refs[a].at[k],
                    send_sem=send_sems.at[a * n_peers + k],
                    recv_sem=recv_sems.at[a * n_peers + k],
                    device_id=(px, py, c),
                    device_id_type=MESH,
                )
                cp.start()
                sends.append(cp)
        base = n * n_peers
        for k, (fx, fy, fc) in enumerate(flips):
            px, py, pc = x ^ fx, y ^ fy, c ^ fc
            cp = pltpu.make_async_remote_copy(
                src_ref=small_ref,
                dst_ref=gathered_ref.at[4 * x + 2 * y + c],
                send_sem=send_sems.at[base + k],
                recv_sem=recv_sems.at[base + k],
                device_id=(px, py, pc),
                device_id_type=MESH,
            )
            cp.start()
            sends.append(cp)
        for a in range(n):
            for k, (px, py) in enumerate(peers):
                pltpu.make_async_remote_copy(
                    src_ref=g16_refs[a].at[me],
                    dst_ref=recv_refs[a].at[k],
                    send_sem=send_sems.at[a * n_peers + k],
                    recv_sem=recv_sems.at[a * n_peers + k],
                    device_id=(px, py, c),
                    device_id_type=MESH,
                ).wait_recv()
        for k, (fx, fy, fc) in enumerate(flips):
            px, py, pc = x ^ fx, y ^ fy, c ^ fc
            pltpu.make_async_remote_copy(
                src_ref=small_ref,
                dst_ref=gathered_ref.at[4 * px + 2 * py + pc],
                send_sem=send_sems.at[base + k],
                recv_sem=recv_sems.at[base + k],
                device_id=(px, py, pc),
                device_id_type=MESH,
            ).wait_recv()
        for cp in sends:
            cp.wait_send()
        for cp in locals_:
            cp.wait()

    any_spec = pl.BlockSpec(memory_space=pl.ANY)
    out_shape = (
        [jax.ShapeDtypeStruct(g.shape[1:], F32) for g in g32]
        + [jax.ShapeDtypeStruct((n_peers,) + g.shape[1:], BF16) for g in g16]
        + [jax.ShapeDtypeStruct((N_DEV,) + small.shape, F32)]
    )
    n_sems = n * n_peers + n_small
    outs = pl.pallas_call(
        body,
        name=name,
        in_specs=[any_spec] * (2 * n + 1),
        out_specs=[any_spec] * (2 * n + 1),
        out_shape=out_shape,
        scratch_shapes=[
            pltpu.SemaphoreType.DMA((n_sems,)),
            pltpu.SemaphoreType.DMA((n_sems,)),
            pltpu.SemaphoreType.DMA((n + 1,)),
        ],
        compiler_params=pltpu.CompilerParams(has_side_effects=True),
    )(*g32, *g16, small)
    return outs[:n], outs[n : 2 * n], outs[2 * n]


def _swap_with_sibling(parts, name):
    n = len(parts)

    def body(*refs):
        ins, outs = refs[:n], refs[n : 2 * n]
        send_sems, recv_sems = refs[2 * n :]
        sibling = (lax.axis_index("x"), lax.axis_index("y"), 1 - lax.axis_index("c"))
        copies = [
            pltpu.make_async_remote_copy(
                src_ref=ins[a],
                dst_ref=outs[a],
                send_sem=send_sems.at[a],
                recv_sem=recv_sems.at[a],
                device_id=sibling,
                device_id_type=MESH,
            )
            for a in range(n)
        ]
        for cp in copies:
            cp.start()
        for cp in copies:
            cp.wait_recv()
        for cp in copies:
            cp.wait_send()

    any_spec = pl.BlockSpec(memory_space=pl.ANY)
    return pl.pallas_call(
        body,
        name=name,
        in_specs=[any_spec] * n,
        out_specs=[any_spec] * n,
        out_shape=[jax.ShapeDtypeStruct(p.shape, p.dtype) for p in parts],
        scratch_shapes=[pltpu.SemaphoreType.DMA((n,)), pltpu.SemaphoreType.DMA((n,))],
        compiler_params=pltpu.CompilerParams(has_side_effects=True),
    )(*parts)


def _pack(pieces):
    flat = jnp.concatenate([p.reshape(-1) for p in pieces])
    return flat.reshape(-1, LANES)


def _unpack(packed, shapes):
    flat = packed.reshape(-1)
    out, off = [], 0
    for s in shapes:
        size = 1
        for d in s:
            size *= d
        out.append(flat[off : off + size].reshape(s))
        off += size
    return out


def kernel(x, norm_mix, norm_mlp, sb_wqkv, sb_wo, sgu_win, sgu_gain, sgu_ws, sgu_bs, sgu_wout, mlp_w1, mlp_w2, final_norm, loss_target, m_norm_mix, m_norm_mlp, m_sb_wqkv, m_sb_wo, m_sgu_win, m_sgu_gain, m_sgu_ws, m_sgu_bs, m_sgu_wout, m_mlp_w1, m_mlp_w2, m_final_norm, v_norm_mix, v_norm_mlp, v_sb_wqkv, v_sb_wo, v_sgu_win, v_sgu_gain, v_sgu_ws, v_sgu_bs, v_sgu_wout, v_mlp_w1, v_mlp_w2, v_final_norm):
    n_seq, S, D = x.shape
    T = n_seq * S
    depth = norm_mix.shape[0]
    n_sgu = sgu_win.shape[0]
    F = sgu_wout.shape[1] * N_CHIPS
    gw = F // SGU_GROUPS
    chip = 2 * lax.axis_index("x") + lax.axis_index("y")

    big = [sb_wqkv, sb_wo, sgu_win, sgu_wout, mlp_w1, mlp_w2]
    gain_tile = sgu_gain.reshape(-1, LANES)
    gathered = _all_gather_chips([w.astype(BF16) for w in big] + [gain_tile], "gather_weights")
    wg_qkv, wg_wo, wg_win, wg_wout, wg_w1, wg_w2 = gathered[:6]
    gain_full = jnp.transpose(gathered[6].reshape(N_CHIPS, n_sgu, F // N_CHIPS), (1, 0, 2)).reshape(n_sgu, F)
    bsb = [jnp.broadcast_to(sgu_bs[j][:, :, None], (SGU_GROUPS, SGU_CHUNK, gw)) for j in range(n_sgu)]

    xs = x.reshape(T, D)
    saved = []
    for i in range(depth):
        j = i // 2
        if i % 2 == 0:
            qkv, h = _norm_matmul(xs, norm_mix[i], wg_qkv, j, f"qkv_fwd_{i}")
            o = _attn_fwd(qkv, n_seq, S, D, f"attn_fwd_{i}")
            x_mid = _act_matmul_res(o, wg_wo, j, xs, None, f"wo_fwd_{i}")
            mix = (qkv, o)
        else:
            a, h = _norm_matmul(xs, norm_mix[i], wg_win, j, f"win_fwd_{i}")
            yg = _sgu_fwd(a, gain_full[j], sgu_ws[j], bsb[j], f"sgu_fwd_{i}")
            x_mid = _act_matmul_res(yg, wg_wout, j, xs, None, f"wout_fwd_{i}")
            mix = (a, yg)
        a2, h2 = _norm_matmul(x_mid, norm_mlp[i], wg_w1, i, f"w1_fwd_{i}")
        x_out = _act_matmul_res(a2, wg_w2, i, x_mid, "relu2", f"w2_fwd_{i}")
        saved.append((xs, h, mix, x_mid, h2, a2))
        xs = x_out

    sq, dx, dxb, g_final = _final_loss(xs, final_norm, loss_target.reshape(T, D), "loss_head")
    loss = lax.psum(0.5 * jnp.sum(sq) / D, ("x", "y", "c"))

    gb = {k: None for k in ("qkv", "wo", "win", "wout", "w1", "w2")}
    g_mix, g_mlp = [None] * depth, [None] * depth
    g_ws, g_bs, g_gain = [None] * n_sgu, [None] * n_sgu, [None] * n_sgu
    for i in reversed(range(depth)):
        j = i // 2
        x_in, h, mix, x_mid, h2, a2 = saved[i]
        da2 = _matmul_nt(dxb, wg_w2, i, a2, f"w2_bwd_{i}")
        gb["w2"] = _matmul_tn(a2, dxb, gb["w2"], i, depth, True, "relu2", f"w2_grad_{i}")
        gb["w1"] = _matmul_tn(h2, da2, gb["w1"], i, depth, False, None, f"w1_grad_{i}")
        dx, dxb, g_mlp[i] = _matmul_nt_norm_bwd(da2, wg_w1, i, x_mid, norm_mlp[i], dx, f"w1_bwd_{i}")
        if i % 2 == 0:
            qkv, o = mix
            do = _matmul_nt(dxb, wg_wo, j, None, f"wo_bwd_{i}")
            gb["wo"] = _matmul_tn(o, dxb, gb["wo"], j, depth // 2 + depth % 2, True, None, f"wo_grad_{i}")
            dq, dk, dv = _attn_bwd(qkv, o, do, n_seq, S, D, f"attn_bwd_{i}")
            dqkv = jnp.concatenate([dq, dk, dv], axis=1)
            gb["qkv"] = _matmul_tn(h, dqkv, gb["qkv"], j, depth // 2 + depth % 2, False, None, f"qkv_grad_{i}")
            dx, dxb, g_mix[i] = _matmul_nt_norm_bwd(dqkv, wg_qkv, j, x_in, norm_mix[i], dx, f"qkv_bwd_{i}")
        else:
            a, yg = mix
            dyg = _matmul_nt(dxb, wg_wout, j, None, f"wout_bwd_{i}")
            gb["wout"] = _matmul_tn(yg, dxb, gb["wout"], j, n_sgu, True, None, f"wout_grad_{i}")
            da, g_ws[j], dbs, g_gain[j] = _sgu_bwd(a, dyg, gain_full[j], sgu_ws[j], bsb[j], f"sgu_bwd_{i}")
            g_bs[j] = dbs[:, :, 0]
            gb["win"] = _matmul_tn(h, da, gb["win"], j, n_sgu, False, None, f"win_grad_{i}")
            dx, dxb, g_mix[i] = _matmul_nt_norm_bwd(da, wg_win, j, x_in, norm_mix[i], dx, f"win_bwd_{i}")
    grad_x = dx.reshape(n_seq, S, D)

    names = ["qkv", "wo", "win", "wout", "w1", "w2"]
    small_shapes = [norm_mix.shape, norm_mlp.shape, final_norm.shape, sgu_ws.shape, sgu_bs.shape, (n_sgu, F)]
    small = _pack(
        [jnp.stack(g_mix), jnp.stack(g_mlp), g_final, jnp.stack(g_ws), jnp.stack(g_bs), jnp.stack(g_gain)]
    )
    own, recv, small_all = _scatter_grads([gb[k][0] for k in names], [gb[k][1] for k in names], small, "scatter_grads")
    partial = []
    for k, o32, r16 in zip(names, own, recv):
        cols = o32.shape[-1]
        partial.append(_sum_received(o32.reshape(-1, cols), r16.reshape(N_CHIPS - 1, -1, cols), f"sum_{k}"))
    theirs = _swap_with_sibling(partial, "swap_partial_sums")
    small_sum = _sum_received(small_all[0], small_all[1:], "sum_small")

    ms = [m_sb_wqkv, m_sb_wo, m_sgu_win, m_sgu_wout, m_mlp_w1, m_mlp_w2]
    vs = [v_sb_wqkv, v_sb_wo, v_sgu_win, v_sgu_wout, v_mlp_w1, v_mlp_w2]
    res = {}
    keys = ["sb_wqkv", "sb_wo", "sgu_win", "sgu_wout", "mlp_w1", "mlp_w2"]
    for key, k, w, m, v, mine, other in zip(keys, names, big, ms, vs, partial, theirs):
        cols = w.shape[-1]
        outs = _adamw(w.reshape(-1, cols), m.reshape(-1, cols), v.reshape(-1, cols), [mine, other], f"adamw_{k}")
        res[key] = [o.reshape(w.shape) for o in outs]

    g_small = _unpack(small_sum, small_shapes)
    g_small[5] = lax.dynamic_slice_in_dim(g_small[5], chip * (F // N_CHIPS), F // N_CHIPS, axis=1)
    small_keys = ["norm_mix", "norm_mlp", "final_norm", "sgu_ws", "sgu_bs", "sgu_gain"]
    small_w = [norm_mix, norm_mlp, final_norm, sgu_ws, sgu_bs, sgu_gain]
    small_m = [m_norm_mix, m_norm_mlp, m_final_norm, m_sgu_ws, m_sgu_bs, m_sgu_gain]
    small_v = [v_norm_mix, v_norm_mlp, v_final_norm, v_sgu_ws, v_sgu_bs, v_sgu_gain]
    outs = _adamw(_pack(small_w), _pack(small_m), _pack(small_v), [_pack(g_small)], "adamw_small")
    local_shapes = [w.shape for w in small_w]
    for key, parts in zip(small_keys, zip(*[_unpack(o, local_shapes) for o in outs])):
        res[key] = list(parts)

    order = ["norm_mix", "norm_mlp", "sb_wqkv", "sb_wo", "sgu_win", "sgu_gain", "sgu_ws", "sgu_bs", "sgu_wout", "mlp_w1", "mlp_w2", "final_norm"]
    return (loss, grad_x, *[res[k][0] for k in order], *[res[k][1] for k in order], *[res[k][2] for k in order], *[res[k][3] for k in order])
```

```python
import jax
import jax.numpy as jnp
from jax import lax
from jax.experimental import pallas as pl
from jax.experimental.pallas import tpu as pltpu

F32 = jnp.float32
BF16 = jnp.bfloat16
MESH = pl.DeviceIdType.MESH

EPS = 1e-6
HEAD_DIM = 64
LANES = 128
Q_TILE = 128
SGU_CHUNK = 128
SGU_GROUPS = 8
N_CHIPS = 4
N_DEV = 8
ADAM_LR = 0.001
ADAM_B1 = 0.9
ADAM_B2 = 0.999
ADAM_EPS = 1e-08
ADAM_WD = 0.01
ADAM_STEP = 10
GELU_C0 = 0.7978845608028654
GELU_C1 = 0.044715
VMEM_LIMIT = 48 * 1024 * 1024
ROW_TILE = 1024
NT = (((1,), (1,)), ((), ()))
TN = (((0,), (0,)), ((), ()))


def _params(n_axes):
    return pltpu.CompilerParams(dimension_semantics=("arbitrary",) * n_axes, vmem_limit_bytes=VMEM_LIMIT)


def _rstd(x):
    return lax.rsqrt(jnp.mean(x * x, axis=-1, keepdims=True) + EPS)


def _norm_bwd(dh, x, gain):
    rstd = _rstd(x)
    xh = x * rstd
    dhg = dh * gain
    dx = rstd * (dhg - xh * jnp.mean(dhg * xh, axis=-1, keepdims=True))
    return dx, jnp.sum(dh * xh, axis=0, keepdims=True)


def _gelu(x):
    return 0.5 * x * (1.0 + jnp.tanh(GELU_C0 * (x + GELU_C1 * x * x * x)))


def _gelu_grad(x):
    t = jnp.tanh(GELU_C0 * (x + GELU_C1 * x * x * x))
    return 0.5 * (1.0 + t) + 0.5 * x * (1.0 - t * t) * (GELU_C0 * (1.0 + 3.0 * GELU_C1 * x * x))


def _act(a, act):
    if act == "relu2":
        r = jnp.maximum(a.astype(F32), 0.0)
        return (r * r).astype(BF16)
    return a.astype(BF16)


def _norm_matmul(x, gain, wg, layer, name):
    T, D = x.shape
    nsh, _, _, ns = wg.shape
    tm = min(T, ROW_TILE)

    def body(x_ref, g_ref, w_ref, y_ref, h_ref):
        @pl.when(pl.program_id(1) == 0)
        def _():
            xv = x_ref[...]
            h_ref[...] = (xv * _rstd(xv) * g_ref[...]).astype(BF16)

        y_ref[...] = jnp.dot(h_ref[...], w_ref[...], preferred_element_type=F32).astype(BF16)

    return pl.pallas_call(
        body,
        name=name,
        grid=(T // tm, nsh),
        in_specs=[
            pl.BlockSpec((tm, D), lambda i, j: (i, 0)),
            pl.BlockSpec((1, D), lambda i, j: (0, 0)),
            pl.BlockSpec((None, None, D, ns), lambda i, j: (j, layer, 0, 0)),
        ],
        out_specs=[pl.BlockSpec((tm, ns), lambda i, j: (i, j)), pl.BlockSpec((tm, D), lambda i, j: (i, 0))],
        out_shape=[jax.ShapeDtypeStruct((T, nsh * ns), BF16), jax.ShapeDtypeStruct((T, D), BF16)],
        compiler_params=_params(2),
    )(x, gain.reshape(1, D), wg)


def _act_matmul_res(a, wg, layer, x_in, act, name):
    T, K = a.shape
    nsh, _, kq, D = wg.shape
    tm = min(T, ROW_TILE)

    def body(a_ref, w_ref, x_ref, o_ref):
        @pl.when(pl.program_id(1) == 0)
        def _():
            o_ref[...] = x_ref[...]

        o_ref[...] += jnp.dot(_act(a_ref[...], act), w_ref[...], preferred_element_type=F32)

    return pl.pallas_call(
        body,
        name=name,
        grid=(T // tm, nsh),
        in_specs=[
            pl.BlockSpec((tm, kq), lambda i, k: (i, k)),
            pl.BlockSpec((None, None, kq, D), lambda i, k: (k, layer, 0, 0)),
            pl.BlockSpec((tm, D), lambda i, k: (i, 0)),
        ],
        out_specs=pl.BlockSpec((tm, D), lambda i, k: (i, 0)),
        out_shape=jax.ShapeDtypeStruct((T, D), F32),
        compiler_params=_params(2),
    )(a, wg, x_in)


def _matmul_nt(g, wg, layer, a, name):
    T, D = g.shape
    nsh, _, kq, _ = wg.shape
    tm = min(T, ROW_TILE)

    def body(g_ref, w_ref, *rest):
        r = lax.dot_general(g_ref[...], w_ref[...], NT, preferred_element_type=F32)
        if a is None:
            (o_ref,) = rest
        else:
            a_ref, o_ref = rest
            r = r * (2.0 * jnp.maximum(a_ref[...].astype(F32), 0.0))
        o_ref[...] = r.astype(BF16)

    in_specs = [
        pl.BlockSpec((tm, D), lambda i, k: (i, 0)),
        pl.BlockSpec((None, None, kq, D), lambda i, k: (k, layer, 0, 0)),
    ]
    args = [g, wg]
    if a is not None:
        in_specs.append(pl.BlockSpec((tm, kq), lambda i, k: (i, k)))
        args.append(a)
    return pl.pallas_call(
        body,
        name=name,
        grid=(T // tm, nsh),
        in_specs=in_specs,
        out_specs=pl.BlockSpec((tm, kq), lambda i, k: (i, k)),
        out_shape=jax.ShapeDtypeStruct((T, nsh * kq), BF16),
        compiler_params=_params(2),
    )(*args)


def _matmul_nt_norm_bwd(da, wg, layer, x, gain, dres, name):
    T, D = x.shape
    nsh, _, _, ns = wg.shape
    tm = min(T, ROW_TILE // 2)
    n_i = T // tm

    def body(da_ref, w_ref, x_ref, g_ref, r_ref, dx_ref, dxb_ref, dg_ref, acc_ref):
        i, j = pl.program_id(0), pl.program_id(1)

        @pl.when(j == 0)
        def _():
            acc_ref[...] = jnp.zeros_like(acc_ref)

        acc_ref[...] += lax.dot_general(da_ref[...], w_ref[...], NT, preferred_element_type=F32)

        @pl.when(j == nsh - 1)
        def _():
            dx, dg = _norm_bwd(acc_ref[...], x_ref[...], g_ref[...])
            dx = dx + r_ref[...]
            dx_ref[...] = dx
            dxb_ref[...] = dx.astype(BF16)

            @pl.when(i == 0)
            def _():
                dg_ref[...] = dg

            @pl.when(i > 0)
            def _():
                dg_ref[...] += dg

    del n_i
    return pl.pallas_call(
        body,
        name=name,
        grid=(T // tm, nsh),
        in_specs=[
            pl.BlockSpec((tm, ns), lambda i, j: (i, j)),
            pl.BlockSpec((None, None, D, ns), lambda i, j: (j, layer, 0, 0)),
            pl.BlockSpec((tm, D), lambda i, j: (i, 0)),
            pl.BlockSpec((1, D), lambda i, j: (0, 0)),
            pl.BlockSpec((tm, D), lambda i, j: (i, 0)),
        ],
        out_specs=[
            pl.BlockSpec((tm, D), lambda i, j: (i, 0)),
            pl.BlockSpec((tm, D), lambda i, j: (i, 0)),
            pl.BlockSpec((1, D), lambda i, j: (0, 0)),
        ],
        out_shape=[
            jax.ShapeDtypeStruct((T, D), F32),
            jax.ShapeDtypeStruct((T, D), BF16),
            jax.ShapeDtypeStruct((1, D), F32),
        ],
        scratch_shapes=[pltpu.VMEM((tm, D), F32)],
        compiler_params=_params(2),
    )(da, wg, x, gain.reshape(1, D), dres)


def _matmul_tn(lhs, rhs, bufs, layer, n_layers, shard_lhs, act, name):
    T = lhs.shape[0]
    rows = lhs.shape[1] // N_CHIPS if shard_lhs else lhs.shape[1]
    cols = rhs.shape[1] if shard_lhs else rhs.shape[1] // N_CHIPS
    tt = min(T, ROW_TILE)
    n_t = T // tt

    def body(l_ref, r_ref, *rest):
        o32_ref, o16_ref = rest[-2:]
        t = pl.program_id(1)
        upd = lax.dot_general(_act(l_ref[...], act), r_ref[...].astype(BF16), TN, preferred_element_type=F32)

        @pl.when(t == 0)
        def _():
            o32_ref[...] = upd

        @pl.when(t > 0)
        def _():
            o32_ref[...] += upd

        @pl.when(t == n_t - 1)
        def _():
            o16_ref[...] = o32_ref[...].astype(BF16)

    if shard_lhs:
        in_specs = [pl.BlockSpec((tt, rows), lambda s, t: (t, s)), pl.BlockSpec((tt, cols), lambda s, t: (t, 0))]
    else:
        in_specs = [pl.BlockSpec((tt, rows), lambda s, t: (t, 0)), pl.BlockSpec((tt, cols), lambda s, t: (t, s))]
    args = [lhs, rhs]
    aliases = {}
    if bufs is not None:
        in_specs += [pl.BlockSpec(memory_space=pl.ANY)] * 2
        args += list(bufs)
        aliases = {2: 0, 3: 1}
    shape = (N_CHIPS, n_layers, rows, cols)
    return pl.pallas_call(
        body,
        name=name,
        grid=(N_CHIPS, n_t),
        in_specs=in_specs,
        out_specs=[pl.BlockSpec((None, None, rows, cols), lambda s, t: (s, layer, 0, 0))] * 2,
        out_shape=[jax.ShapeDtypeStruct(shape, F32), jax.ShapeDtypeStruct(shape, BF16)],
        input_output_aliases=aliases,
        compiler_params=_params(2),
    )(*args)


ATTN_LANE_TILES = 2


MASKED = -1e30


def _hi_lo(x):
    hi = x.astype(BF16)
    lo = (x - hi.astype(F32)).astype(BF16)
    return jnp.concatenate([hi, lo], axis=1)


def _suffix_matrix(inclusive):
    j = lax.broadcasted_iota(jnp.int32, (2 * Q_TILE, 2 * Q_TILE), 0) & (Q_TILE - 1)
    s = lax.broadcasted_iota(jnp.int32, (2 * Q_TILE, 2 * Q_TILE), 1)
    later = (j >= s) if inclusive else (j > s)
    return jnp.where((s >= Q_TILE) | later, 1.0, 0.0).astype(BF16)


def _log_beta(z):
    return jnp.minimum(z, 0.0) - jnp.log(1.0 + jnp.exp(-jnp.abs(z)))


def _attn_specs(S, n_pairs, width):
    nq = S // Q_TILE
    groups = n_pairs * LANES // width
    q_spec = pl.BlockSpec((Q_TILE, width), lambda b, p, i: (b * nq + i, p))
    k_spec = pl.BlockSpec((S, width), lambda b, p, i: (b, groups + p))
    v_spec = pl.BlockSpec((S, width), lambda b, p, i: (b, 2 * groups + p))
    return nq, groups, q_spec, k_spec, v_spec


def _head_masks(width):
    lane = lax.broadcasted_iota(jnp.int32, (1, width), 1)
    return [(lane >= h * HEAD_DIM) & (lane < (h + 1) * HEAD_DIM) for h in range(width // HEAD_DIM)]


def _per_head_rows(x, masks):
    return jnp.concatenate([jnp.where(hm, x, 0) for hm in masks], axis=0)


def _heads_to_lanes(x, n_heads):
    return jnp.concatenate([x[h * Q_TILE : (h + 1) * Q_TILE] for h in range(n_heads)], axis=1)


def _block_start(kb):
    return kb * Q_TILE if isinstance(kb, int) else pl.multiple_of(kb * Q_TILE, Q_TILE)


def _past_mask(rows):
    t = lax.broadcasted_iota(jnp.int32, (rows, Q_TILE), 0) & (Q_TILE - 1)
    s = lax.broadcasted_iota(jnp.int32, (rows, Q_TILE), 1)
    return s < t


def _attn_fwd(qkv, n_seq, S, D, name):
    T = n_seq * S
    width = min(D, ATTN_LANE_TILES * LANES)
    n_heads = width // HEAD_DIM
    rows = n_heads * Q_TILE
    nq, groups, q_spec, k_spec, v_spec = _attn_specs(S, D // LANES, width)
    scale = HEAD_DIM ** -0.5

    def body(q_ref, k_ref, v_ref, o_ref):
        qi = pl.program_id(2)
        masks = _head_masks(width)
        past = _past_mask(rows)
        sfx = _suffix_matrix(False)
        qh = _per_head_rows(q_ref[...] * scale, masks)

        def scores(kb, diag):
            off = _block_start(kb)
            z = lax.dot_general(qh, k_ref[pl.ds(off, Q_TILE), :], NT, preferred_element_type=F32)
            lb = _log_beta(z)
            l1 = lb - z
            if diag:
                l1 = jnp.where(past, l1, 0.0)
                lb = jnp.where(past, lb, MASKED)
            return lb, _hi_lo(l1)

        def weigh(kb, st, carry, acc):
            lb, l1 = st
            off = _block_start(kb)
            r = jnp.dot(l1, sfx, preferred_element_type=F32)
            a = jnp.exp(lb + r[:, :Q_TILE] + carry)
            vh = _per_head_rows(v_ref[pl.ds(off, Q_TILE), :], masks)
            acc = acc + jnp.dot(_heads_to_lanes(a.astype(BF16), n_heads), vh, preferred_element_type=F32)
            return carry + r[:, Q_TILE:], acc

        def trip(j, c):
            st, carry, acc = c
            nxt = scores(qi - 1 - j, False)
            carry, acc = weigh(qi - j, st, carry, acc)
            return nxt, carry, acc

        init = (scores(qi, True), jnp.zeros((rows, Q_TILE), F32), jnp.zeros((Q_TILE, width), F32))
        st, carry, acc = lax.fori_loop(0, qi, trip, init)
        _, acc = weigh(0, st, carry, acc)
        o_ref[...] = acc

    return pl.pallas_call(
        body,
        name=name,
        grid=(n_seq, groups, nq),
        in_specs=[q_spec, k_spec, v_spec],
        out_specs=pl.BlockSpec((Q_TILE, width), lambda b, p, i: (b * nq + i, p)),
        out_shape=jax.ShapeDtypeStruct((T, D), F32),
        compiler_params=_params(3),
    )(qkv, qkv, qkv)


def _attn_bwd(qkv, o, do, n_seq, S, D, name):
    T = n_seq * S
    width = min(D, ATTN_LANE_TILES * LANES)
    n_heads = width // HEAD_DIM
    rows = n_heads * Q_TILE
    nq, groups, q_spec, k_spec, v_spec = _attn_specs(S, D // LANES, width)
    scale = HEAD_DIM ** -0.5
    row_spec = pl.BlockSpec((Q_TILE, width), lambda b, p, i: (b * nq + i, p))
    seq_spec = pl.BlockSpec((S, width), lambda b, p, i: (b, p))

    def body(q_ref, k_ref, v_ref, o_ref, do_ref, dq_ref, dk_ref, dv_ref, dk_acc, dv_acc):
        qi = pl.program_id(2)

        @pl.when(qi == 0)
        def _():
            dk_acc[...] = jnp.zeros_like(dk_acc)
            dv_acc[...] = jnp.zeros_like(dv_acc)

        masks = _head_masks(width)
        past = _past_mask(rows)
        sfx = _suffix_matrix(False)
        sfx_incl = _suffix_matrix(True)
        qh = _per_head_rows(q_ref[...] * scale, masks)
        do = do_ref[...]
        doh = _per_head_rows(do, masks)
        prod = do.astype(F32) * o_ref[...]
        delta = jnp.concatenate(
            [jnp.sum(jnp.where(hm, prod, 0.0), axis=-1, keepdims=True) for hm in masks], axis=0
        )

        def scores(kb, diag):
            off = _block_start(kb)
            z = lax.dot_general(qh, k_ref[pl.ds(off, Q_TILE), :], NT, preferred_element_type=F32)
            lb = _log_beta(z)
            l1 = lb - z
            if diag:
                l1 = jnp.where(past, l1, 0.0)
                lb = jnp.where(past, lb, MASKED)
            da = lax.dot_general(doh, v_ref[pl.ds(off, Q_TILE), :], NT, preferred_element_type=F32)
            return lb, _hi_lo(l1), da

        def weigh(kb, st, c1, c2, dq):
            lb, l1, da = st
            off = _block_start(kb)
            r = jnp.dot(l1, sfx, preferred_element_type=F32)
            ab = jnp.exp(lb + r[:, :Q_TILE] + c1).astype(BF16)
            g = ab.astype(F32) * da
            r2 = jnp.dot(_hi_lo(g), sfx_incl, preferred_element_type=F32)
            earlier = delta - (r2[:, :Q_TILE] + c2)
            beta = jnp.exp(lb)
            dzb = (g * (1.0 - beta) - earlier * beta).astype(BF16)
            kh = _per_head_rows(k_ref[pl.ds(off, Q_TILE), :], masks)
            dq = dq + jnp.dot(_heads_to_lanes(dzb, n_heads), kh, preferred_element_type=F32)
            dk_acc[pl.ds(off, Q_TILE), :] += lax.dot_general(dzb, qh, TN, preferred_element_type=F32)
            dv_acc[pl.ds(off, Q_TILE), :] += lax.dot_general(ab, doh, TN, preferred_element_type=F32)
            return c1 + r[:, Q_TILE:], c2 + r2[:, Q_TILE:], dq

        def trip(j, c):
            st, c1, c2, dq = c
            nxt = scores(qi - 1 - j, False)
            c1, c2, dq = weigh(qi - j, st, c1, c2, dq)
            return nxt, c1, c2, dq

        zero = jnp.zeros((rows, Q_TILE), F32)
        st, c1, c2, dq = lax.fori_loop(0, qi, trip, (scores(qi, True), zero, zero, jnp.zeros((Q_TILE, width), F32)))
        _, _, dq = weigh(0, st, c1, c2, dq)
        dq_ref[...] = (dq * scale).astype(BF16)

        @pl.when(qi == nq - 1)
        def _():
            dk_ref[...] = dk_acc[...].astype(BF16)
            dv_ref[...] = dv_acc[...].astype(BF16)

    return pl.pallas_call(
        body,
        name=name,
        grid=(n_seq, groups, nq),
        in_specs=[q_spec, k_spec, v_spec, row_spec, row_spec],
        out_specs=[row_spec, seq_spec, seq_spec],
        out_shape=[jax.ShapeDtypeStruct((T, D), BF16)] * 3,
        scratch_shapes=[pltpu.VMEM((S, width), F32), pltpu.VMEM((S, width), F32)],
        compiler_params=_params(3),
    )(qkv, qkv, qkv, o, do)


def _causal_ws(ws_ref, g):
    t = lax.broadcasted_iota(jnp.int32, (SGU_CHUNK, SGU_CHUNK), 0)
    s = lax.broadcasted_iota(jnp.int32, (SGU_CHUNK, SGU_CHUNK), 1)
    return jnp.where(s <= t, ws_ref[g], 0.0)


def _sgu_fwd(a, gain, ws, bsb, name):
    T, F2 = a.shape
    F = F2 // 2
    gw = F // SGU_GROUPS

    def body(a_ref, gain_ref, ws_ref, bsb_ref, y_ref):
        v = _gelu(a_ref[:, F:].astype(F32))
        vn = (v * _rstd(v) * gain_ref[...]).astype(BF16)
        for g in range(SGU_GROUPS):
            cs = slice(g * gw, (g + 1) * gw)
            w = _causal_ws(ws_ref, g).astype(BF16)
            mixed = jnp.dot(w, vn[:, cs], preferred_element_type=F32) + bsb_ref[g]
            y_ref[:, cs] = (_gelu(a_ref[:, cs].astype(F32)) * mixed).astype(BF16)

    return pl.pallas_call(
        body,
        name=name,
        grid=(T // SGU_CHUNK,),
        in_specs=[
            pl.BlockSpec((SGU_CHUNK, F2), lambda i: (i, 0)),
            pl.BlockSpec((1, F), lambda i: (0, 0)),
            pl.BlockSpec((SGU_GROUPS, SGU_CHUNK, SGU_CHUNK), lambda i: (0, 0, 0)),
            pl.BlockSpec((SGU_GROUPS, SGU_CHUNK, gw), lambda i: (0, 0, 0)),
        ],
        out_specs=pl.BlockSpec((SGU_CHUNK, F), lambda i: (i, 0)),
        out_shape=jax.ShapeDtypeStruct((T, F), BF16),
        compiler_params=_params(1),
    )(a, gain.reshape(1, F), ws, bsb)


def _sgu_bwd(a, dy, gain, ws, bsb, name):
    T, F2 = a.shape
    F = F2 // 2
    gw = F // SGU_GROUPS

    def body(a_ref, dy_ref, gain_ref, ws_ref, bsb_ref, da_ref, dws_ref, dbs_ref, dgain_ref, dvn_ref):
        @pl.when(pl.program_id(0) == 0)
        def _():
            dws_ref[...] = jnp.zeros_like(dws_ref)
            dbs_ref[...] = jnp.zeros_like(dbs_ref)
            dgain_ref[...] = jnp.zeros_like(dgain_ref)

        av = a_ref[:, F:].astype(F32)
        v = _gelu(av)
        rstd = _rstd(v)
        vh = v * rstd
        gain = gain_ref[...]
        vn = (vh * gain).astype(BF16)
        ones = jnp.ones((gw, SGU_CHUNK), BF16)
        for g in range(SGU_GROUPS):
            cs = slice(g * gw, (g + 1) * gw)
            w = _causal_ws(ws_ref, g).astype(BF16)
            mixed = jnp.dot(w, vn[:, cs], preferred_element_type=F32) + bsb_ref[g]
            au = a_ref[:, cs].astype(F32)
            dyc = dy_ref[:, cs].astype(F32)
            da_ref[:, cs] = (dyc * mixed * _gelu_grad(au)).astype(BF16)
            dm = (dyc * _gelu(au)).astype(BF16)
            dbs_ref[g] += jnp.dot(dm, ones, preferred_element_type=F32)
            dws_ref[g] += _causal_mask_f32(lax.dot_general(dm, vn[:, cs], NT, preferred_element_type=F32))
            dvn_ref[:, cs] = lax.dot_general(w, dm, TN, preferred_element_type=F32)
        dvn = dvn_ref[...]
        dgain_ref[...] += jnp.sum(dvn * vh, axis=0, keepdims=True)
        dvh = dvn * gain
        dv = rstd * (dvh - vh * jnp.mean(dvh * vh, axis=-1, keepdims=True))
        da_ref[:, F:] = (dv * _gelu_grad(av)).astype(BF16)

    acc_spec = pl.BlockSpec((SGU_GROUPS, SGU_CHUNK, SGU_CHUNK), lambda i: (0, 0, 0))
    acc_shape = jax.ShapeDtypeStruct((SGU_GROUPS, SGU_CHUNK, SGU_CHUNK), F32)
    return pl.pallas_call(
        body,
        name=name,
        grid=(T // SGU_CHUNK,),
        in_specs=[
            pl.BlockSpec((SGU_CHUNK, F2), lambda i: (i, 0)),
            pl.BlockSpec((SGU_CHUNK, F), lambda i: (i, 0)),
            pl.BlockSpec((1, F), lambda i: (0, 0)),
            acc_spec,
            pl.BlockSpec((SGU_GROUPS, SGU_CHUNK, gw), lambda i: (0, 0, 0)),
        ],
        out_specs=[
            pl.BlockSpec((SGU_CHUNK, F2), lambda i: (i, 0)),
            acc_spec,
            acc_spec,
            pl.BlockSpec((1, F), lambda i: (0, 0)),
        ],
        out_shape=[
            jax.ShapeDtypeStruct((T, F2), BF16),
            acc_shape,
            acc_shape,
            jax.ShapeDtypeStruct((1, F), F32),
        ],
        scratch_shapes=[pltpu.VMEM((SGU_CHUNK, F), F32)],
        compiler_params=_params(1),
    )(a, dy, gain.reshape(1, F), ws, bsb)


def _causal_mask_f32(m):
    t = lax.broadcasted_iota(jnp.int32, m.shape, 0)
    s = lax.broadcasted_iota(jnp.int32, m.shape, 1)
    return jnp.where(s <= t, m, 0.0)


def _final_loss(x, gain, target, name):
    T, D = x.shape
    tm = min(T, ROW_TILE // 2)

    def body(x_ref, g_ref, t_ref, sq_ref, dx_ref, dxb_ref, dg_ref):
        xv = x_ref[...]
        gain = g_ref[...]
        err = xv * _rstd(xv) * gain - t_ref[...]
        dx, dg = _norm_bwd(err * (1.0 / D), xv, gain)
        dx_ref[...] = dx
        dxb_ref[...] = dx.astype(BF16)
        sq = jnp.sum(err * err, axis=0, keepdims=True)

        @pl.when(pl.program_id(0) == 0)
        def _():
            sq_ref[...] = sq
            dg_ref[...] = dg

        @pl.when(pl.program_id(0) > 0)
        def _():
            sq_ref[...] += sq
            dg_ref[...] += dg

    row = pl.BlockSpec((tm, D), lambda i: (i, 0))
    vec = pl.BlockSpec((1, D), lambda i: (0, 0))
    return pl.pallas_call(
        body,
        name=name,
        grid=(T // tm,),
        in_specs=[row, vec, row],
        out_specs=[vec, row, row, vec],
        out_shape=[
            jax.ShapeDtypeStruct((1, D), F32),
            jax.ShapeDtypeStruct((T, D), F32),
            jax.ShapeDtypeStruct((T, D), BF16),
            jax.ShapeDtypeStruct((1, D), F32),
        ],
        compiler_params=_params(1),
    )(x, gain.reshape(1, D), target)


def _row_tile(rows, cols, n_arrays):
    budget = VMEM_LIMIT // 2 // (2 * n_arrays * cols * 4)
    tr = rows
    while tr > budget and tr % 16 == 0:
        tr //= 2
    return tr


def _sum_received(own, recv, name):
    R, C = own.shape
    n = recv.shape[0]
    tr = _row_tile(R, C, n + 2)

    def body(own_ref, recv_ref, o_ref):
        s = own_ref[...]
        for k in range(n):
            s = s + recv_ref[k].astype(F32)
        o_ref[...] = s

    return pl.pallas_call(
        body,
        name=name,
        grid=(R // tr,),
        in_specs=[pl.BlockSpec((tr, C), lambda i: (i, 0)), pl.BlockSpec((n, tr, C), lambda i: (0, i, 0))],
        out_specs=pl.BlockSpec((tr, C), lambda i: (i, 0)),
        out_shape=jax.ShapeDtypeStruct((R, C), F32),
        compiler_params=_params(1),
    )(own, recv)


def _adamw(w, m, v, parts, name):
    R, C = w.shape
    n = len(parts)
    tr = _row_tile(R, C, n + 7)

    def body(*refs):
        w_ref, m_ref, v_ref = refs[:3]
        g_ref, d_ref, nm_ref, nv_ref = refs[3 + n :]
        g = refs[3][...]
        for p_ref in refs[4 : 3 + n]:
            g = g + p_ref[...]
        nm = ADAM_B1 * m_ref[...] + (1.0 - ADAM_B1) * g
        nv = ADAM_B2 * v_ref[...] + (1.0 - ADAM_B2) * (g * g)
        m_hat = nm / (1.0 - ADAM_B1**ADAM_STEP)
        v_hat = nv / (1.0 - ADAM_B2**ADAM_STEP)
        g_ref[...] = g
        d_ref[...] = -ADAM_LR * (m_hat / (jnp.sqrt(v_hat) + ADAM_EPS) + ADAM_WD * w_ref[...])
        nm_ref[...] = nm
        nv_ref[...] = nv

    spec = pl.BlockSpec((tr, C), lambda i: (i, 0))
    return pl.pallas_call(
        body,
        name=name,
        grid=(R // tr,),
        in_specs=[spec] * (3 + n),
        out_specs=[spec] * 4,
        out_shape=[jax.ShapeDtypeStruct((R, C), F32)] * 4,
        compiler_params=_params(1),
    )(w, m, v, *parts)


def _chip_peers(x, y):
    return [(1 - x, y), (x, 1 - y), (1 - x, 1 - y)]


def _all_gather_chips(shards, name):
    n = len(shards)
    n_peers = N_CHIPS - 1

    def body(*refs):
        ins, outs = refs[:n], refs[n : 2 * n]
        send_sems, recv_sems, local_sems = refs[2 * n :]
        x, y, c = lax.axis_index("x"), lax.axis_index("y"), lax.axis_index("c")
        me = 2 * x + y
        peers = _chip_peers(x, y)
        started = []
        for a in range(n):
            cp = pltpu.make_async_copy(ins[a], outs[a].at[me], local_sems.at[a])
            cp.start()
            started.append(cp)
        sends = []
        for a in range(n):
            for k, (px, py) in enumerate(peers):
                cp = pltpu.make_async_remote_copy(
                    src_ref=ins[a],
                    dst_ref=outs[a].at[me],
                    send_sem=send_sems.at[a * n_peers + k],
                    recv_sem=recv_sems.at[a * n_peers + k],
                    device_id=(px, py, c),
                    device_id_type=MESH,
                )
                cp.start()
                sends.append(cp)
        for a in range(n):
            for k, (px, py) in enumerate(peers):
                pltpu.make_async_remote_copy(
                    src_ref=ins[a],
                    dst_ref=outs[a].at[2 * px + py],
                    send_sem=send_sems.at[a * n_peers + k],
                    recv_sem=recv_sems.at[a * n_peers + k],
                    device_id=(px, py, c),
                    device_id_type=MESH,
                ).wait_recv()
        for cp in sends:
            cp.wait_send()
        for cp in started:
            cp.wait()

    any_spec = pl.BlockSpec(memory_space=pl.ANY)
    return pl.pallas_call(
        body,
        name=name,
        in_specs=[any_spec] * n,
        out_specs=[any_spec] * n,
        out_shape=[jax.ShapeDtypeStruct((N_CHIPS,) + s.shape, s.dtype) for s in shards],
        scratch_shapes=[
            pltpu.SemaphoreType.DMA((n * n_peers,)),
            pltpu.SemaphoreType.DMA((n * n_peers,)),
            pltpu.SemaphoreType.DMA((n,)),
        ],
        compiler_params=pltpu.CompilerParams(has_side_effects=True),
    )(*shards)


def _scatter_grads(g32, g16, small, name):
    n = len(g32)
    n_peers = N_CHIPS - 1
    n_small = N_DEV - 1

    def body(*refs):
        g32_refs, g16_refs, small_ref = refs[:n], refs[n : 2 * n], refs[2 * n]
        outs = refs[2 * n + 1 : 4 * n + 2]
        own_refs, recv_refs, gathered_ref = outs[:n], outs[n : 2 * n], outs[2 * n]
        send_sems, recv_sems, local_sems = refs[4 * n + 2 :]
        x, y, c = lax.axis_index("x"), lax.axis_index("y"), lax.axis_index("c")
        me = 2 * x + y
        peers = _chip_peers(x, y)
        flips = [(fx, fy, fc) for fx in (0, 1) for fy in (0, 1) for fc in (0, 1)][1:]
        locals_ = []
        for a in range(n):
            cp = pltpu.make_async_copy(g32_refs[a].at[me], own_refs[a], local_sems.at[a])
            cp.start()
            locals_.append(cp)
        cp = pltpu.make_async_copy(small_ref, gathered_ref.at[4 * x + 2 * y + c], local_sems.at[n])
        cp.start()
        locals_.append(cp)
        sends = []
        for a in range(n):
            for k, (px, py) in enumerate(peers):
                cp = pltpu.make_async_remote_copy(
                    src_ref=g16_refs[a].at[2 * px + py],
                    dst_ref=recv_refs[a].at[k],
                    send_sem=send_sems.at[a * n_peers + k],
                    recv_sem=recv_sems.at[a * n_peers + k],
                    device_id=(px, py, c),
                    device_id_type=MESH,
                )
                cp.start()
                sends.append(cp)
        base = n * n_peers
        for k, (fx, fy, fc) in enumerate(flips):
            px, py, pc = x ^ fx, y ^ fy, c ^ fc
            cp = pltpu.make_async_remote_copy(
                src_ref=small_ref,
                dst_ref=gathered_ref.at[4 * x + 2 * y + c],
                send_sem=send_sems.at[base + k],
                recv_sem=recv_sems.at[base + k],
                device_id=(px, py, pc),
                device_id_type=MESH,
            )
            cp.start()
            sends.append(cp)
        for a in range(n):
            for k, (px, py) in enumerate(peers):
                pltpu.make_async_remote_copy(
                    src_ref=g16_refs[a].at[me],
                    dst_ref=recv_refs[a].at[k],
                    send_sem=send_sems.at[a * n_peers + k],
                    recv_sem=recv_sems.at[a * n_peers + k],
                    device_id=(px, py, c),
                    device_id_type=MESH,
                ).wait_recv()
        for k, (fx, fy, fc) in enumerate(flips):
            px, py, pc = x ^ fx, y ^ fy, c ^ fc
            pltpu.make_async_remote_copy(
                src_ref=small_ref,
                dst_ref=gathered_ref.at[4 * px + 2 * py + pc],
                send_sem=send_sems.at[base + k],
                recv_sem=recv_sems.at[base + k],
                device_id=(px, py, pc),
                device_id_type=MESH,
            ).wait_recv()
        for cp in sends:
            cp.wait_send()
        for cp in locals_:
            cp.wait()

    any_spec = pl.BlockSpec(memory_space=pl.ANY)
    out_shape = (
        [jax.ShapeDtypeStruct(g.shape[1:], F32) for g in g32]
        + [jax.ShapeDtypeStruct((n_peers,) + g.shape[1:], BF16) for g in g16]
        + [jax.ShapeDtypeStruct((N_DEV,) + small.shape, F32)]
    )
    n_sems = n * n_peers + n_small
    outs = pl.pallas_call(
        body,
        name=name,
        in_specs=[any_spec] * (2 * n + 1),
        out_specs=[any_spec] * (2 * n + 1),
        out_shape=out_shape,
        scratch_shapes=[
            pltpu.SemaphoreType.DMA((n_sems,)),
            pltpu.SemaphoreType.DMA((n_sems,)),
            pltpu.SemaphoreType.DMA((n + 1,)),
        ],
        compiler_params=pltpu.CompilerParams(has_side_effects=True),
    )(*g32, *g16, small)
    return outs[:n], outs[n : 2 * n], outs[2 * n]


def _swap_with_sibling(parts, name):
    n = len(parts)

    def body(*refs):
        ins, outs = refs[:n], refs[n : 2 * n]
        send_sems, recv_sems = refs[2 * n :]
        sibling = (lax.axis_index("x"), lax.axis_index("y"), 1 - lax.axis_index("c"))
        copies = [
            pltpu.make_async_remote_copy(
                src_ref=ins[a],
                dst_ref=outs[a],
                send_sem=send_sems.at[a],
                recv_sem=recv_sems.at[a],
                device_id=sibling,
                device_id_type=MESH,
            )
            for a in range(n)
        ]
        for cp in copies:
            cp.start()
        for cp in copies:
            cp.wait_recv()
        for cp in copies:
            cp.wait_send()

    any_spec = pl.BlockSpec(memory_space=pl.ANY)
    return pl.pallas_call(
        body,
        name=name,
        in_specs=[any_spec] * n,
        out_specs=[any_spec] * n,
        out_shape=[jax.ShapeDtypeStruct(p.shape, p.dtype) for p in parts],
        scratch_shapes=[pltpu.SemaphoreType.DMA((n,)), pltpu.SemaphoreType.DMA((n,))],
        compiler_params=pltpu.CompilerParams(has_side_effects=True),
    )(*parts)


def _pack(pieces):
    flat = jnp.concatenate([p.reshape(-1) for p in pieces])
    return flat.reshape(-1, LANES)


def _unpack(packed, shapes):
    flat = packed.reshape(-1)
    out, off = [], 0
    for s in shapes:
        size = 1
        for d in s:
            size *= d
        out.append(flat[off : off + size].reshape(s))
        off += size
    return out


def kernel(x, norm_mix, norm_mlp, sb_wqkv, sb_wo, sgu_win, sgu_gain, sgu_ws, sgu_bs, sgu_wout, mlp_w1, mlp_w2, final_norm, loss_target, m_norm_mix, m_norm_mlp, m_sb_wqkv, m_sb_wo, m_sgu_win, m_sgu_gain, m_sgu_ws, m_sgu_bs, m_sgu_wout, m_mlp_w1, m_mlp_w2, m_final_norm, v_norm_mix, v_norm_mlp, v_sb_wqkv, v_sb_wo, v_sgu_win, v_sgu_gain, v_sgu_ws, v_sgu_bs, v_sgu_wout, v_mlp_w1, v_mlp_w2, v_final_norm):
    n_seq, S, D = x.shape
    T = n_seq * S
    depth = norm_mix.shape[0]
    n_sgu = sgu_win.shape[0]
    F = sgu_wout.shape[1] * N_CHIPS
    gw = F // SGU_GROUPS
    chip = 2 * lax.axis_index("x") + lax.axis_index("y")

    big = [sb_wqkv, sb_wo, sgu_win, sgu_wout, mlp_w1, mlp_w2]
    gain_tile = sgu_gain.reshape(-1, LANES)
    gathered = _all_gather_chips([w.astype(BF16) for w in big] + [gain_tile], "gather_weights")
    wg_qkv, wg_wo, wg_win, wg_wout, wg_w1, wg_w2 = gathered[:6]
    gain_full = jnp.transpose(gathered[6].reshape(N_CHIPS, n_sgu, F // N_CHIPS), (1, 0, 2)).reshape(n_sgu, F)
    bsb = [jnp.broadcast_to(sgu_bs[j][:, :, None], (SGU_GROUPS, SGU_CHUNK, gw)) for j in range(n_sgu)]

    xs = x.reshape(T, D)
    saved = []
    for i in range(depth):
        j = i // 2
        if i % 2 == 0:
            qkv, h = _norm_matmul(xs, norm_mix[i], wg_qkv, j, f"qkv_fwd_{i}")
            o = _attn_fwd(qkv, n_seq, S, D, f"attn_fwd_{i}")
            x_mid = _act_matmul_res(o, wg_wo, j, xs, None, f"wo_fwd_{i}")
            mix = (qkv, o)
        else:
            a, h = _norm_matmul(xs, norm_mix[i], wg_win, j, f"win_fwd_{i}")
            yg = _sgu_fwd(a, gain_full[j], sgu_ws[j], bsb[j], f"sgu_fwd_{i}")
            x_mid = _act_matmul_res(yg, wg_wout, j, xs, None, f"wout_fwd_{i}")
            mix = (a, yg)
        a2, h2 = _norm_matmul(x_mid, norm_mlp[i], wg_w1, i, f"w1_fwd_{i}")
        x_out = _act_matmul_res(a2, wg_w2, i, x_mid, "relu2", f"w2_fwd_{i}")
        saved.append((xs, h, mix, x_mid, h2, a2))
        xs = x_out

    sq, dx, dxb, g_final = _final_loss(xs, final_norm, loss_target.reshape(T, D), "loss_head")
    loss = lax.psum(0.5 * jnp.sum(sq) / D, ("x", "y", "c"))

    gb = {k: None for k in ("qkv", "wo", "win", "wout", "w1", "w2")}
    g_mix, g_mlp = [None] * depth, [None] * depth
    g_ws, g_bs, g_gain = [None] * n_sgu, [None] * n_sgu, [None] * n_sgu
    for i in reversed(range(depth)):
        j = i // 2
        x_in, h, mix, x_mid, h2, a2 = saved[i]
        da2 = _matmul_nt(dxb, wg_w2, i, a2, f"w2_bwd_{i}")
        gb["w2"] = _matmul_tn(a2, dxb, gb["w2"], i, depth, True, "relu2", f"w2_grad_{i}")
        gb["w1"] = _matmul_tn(h2, da2, gb["w1"], i, depth, False, None, f"w1_grad_{i}")
        dx, dxb, g_mlp[i] = _matmul_nt_norm_bwd(da2, wg_w1, i, x_mid, norm_mlp[i], dx, f"w1_bwd_{i}")
        if i % 2 == 0:
            qkv, o = mix
            do = _matmul_nt(dxb, wg_wo, j, None, f"wo_bwd_{i}")
            gb["wo"] = _matmul_tn(o, dxb, gb["wo"], j, depth // 2 + depth % 2, True, None, f"wo_grad_{i}")
            dq, dk, dv = _attn_bwd(qkv, o, do, n_seq, S, D, f"attn_bwd_{i}")
            dqkv = jnp.concatenate([dq, dk, dv], axis=1)
            gb["qkv"] = _matmul_tn(h, dqkv, gb["qkv"], j, depth // 2 + depth % 2, False, None, f"qkv_grad_{i}")
            dx, dxb, g_mix[i] = _matmul_nt_norm_bwd(dqkv, wg_qkv, j, x_in, norm_mix[i], dx, f"qkv_bwd_{i}")
        else:
            a, yg = mix
            dyg = _matmul_nt(dxb, wg_wout, j, None, f"wout_bwd_{i}")
            gb["wout"] = _matmul_tn(yg, dxb, gb["wout"], j, n_sgu, True, None, f"wout_grad_{i}")
            da, g_ws[j], dbs, g_gain[j] = _sgu_bwd(a, dyg, gain_full[j], sgu_ws[j], bsb[j], f"sgu_bwd_{i}")
            g_bs[j] = dbs[:, :, 0]
            gb["win"] = _matmul_tn(h, da, gb["win"], j, n_sgu, False, None, f"win_grad_{i}")
            dx, dxb, g_mix[i] = _matmul_nt_norm_bwd(da, wg_win, j, x_in, norm_mix[i], dx, f"win_bwd_{i}")
    grad_x = dx.reshape(n_seq, S, D)

    names = ["qkv", "wo", "win", "wout", "w1", "w2"]
    small_shapes = [norm_mix.shape, norm_mlp.shape, final_norm.shape, sgu_ws.shape, sgu_bs.shape, (n_sgu, F)]
    small = _pack(
        [jnp.stack(g_mix), jnp.stack(g_mlp), g_final, jnp.stack(g_ws), jnp.stack(g_bs), jnp.stack(g_gain)]
    )
    own, recv, small_all = _scatter_grads([gb[k][0] for k in names], [gb[k][1] for k in names], small, "scatter_grads")
    partial = []
    for k, o32, r16 in zip(names, own, recv):
        cols = o32.shape[-1]
        partial.append(_sum_received(o32.reshape(-1, cols), r16.reshape(N_CHIPS - 1, -1, cols), f"sum_{k}"))
    theirs = _swap_with_sibling(partial, "swap_partial_sums")
    small_sum = _sum_received(small_all[0], small_all[1:], "sum_small")

    ms = [m_sb_wqkv, m_sb_wo, m_sgu_win, m_sgu_wout, m_mlp_w1, m_mlp_w2]
    vs = [v_sb_wqkv, v_sb_wo, v_sgu_win, v_sgu_wout, v_mlp_w1, v_mlp_w2]
    res = {}
    keys = ["sb_wqkv", "sb_wo", "sgu_win", "sgu_wout", "mlp_w1", "mlp_w2"]
    for key, k, w, m, v, mine, other in zip(keys, names, big, ms, vs, partial, theirs):
        cols = w.shape[-1]
        outs = _adamw(w.reshape(-1, cols), m.reshape(-1, cols), v.reshape(-1, cols), [mine, other], f"adamw_{k}")
        res[key] = [o.reshape(w.shape) for o in outs]

    g_small = _unpack(small_sum, small_shapes)
    g_small[5] = lax.dynamic_slice_in_dim(g_small[5], chip * (F // N_CHIPS), F // N_CHIPS, axis=1)
    small_keys = ["norm_mix", "norm_mlp", "final_norm", "sgu_ws", "sgu_bs", "sgu_gain"]
    small_w = [norm_mix, norm_mlp, final_norm, sgu_ws, sgu_bs, sgu_gain]
    small_m = [m_norm_mix, m_norm_mlp, m_final_norm, m_sgu_ws, m_sgu_bs, m_sgu_gain]
    small_v = [v_norm_mix, v_norm_mlp, v_final_norm, v_sgu_ws, v_sgu_bs, v_sgu_gain]
    outs = _adamw(_pack(small_w), _pack(small_m), _pack(small_v), [_pack(g_small)], "adamw_small")
    local_shapes = [w.shape for w in small_w]
    for key, parts in zip(small_keys, zip(*[_unpack(o, local_shapes) for o in outs])):
        res[key] = list(parts)

    order = ["norm_mix", "norm_mlp", "sb_wqkv", "sb_wo", "sgu_win", "sgu_gain", "sgu_ws", "sgu_bs", "sgu_wout", "mlp_w1", "mlp_w2", "final_norm"]
    return (loss, grad_x, *[res[k][0] for k in order], *[res[k][1] for k in order], *[res[k][2] for k in order], *[res[k][3] for k in order])
```

```python
import jax
import jax.numpy as jnp
from jax import lax
from jax.experimental import pallas as pl
from jax.experimental.pallas import tpu as pltpu

F32 = jnp.float32
BF16 = jnp.bfloat16
MESH = pl.DeviceIdType.MESH

EPS = 1e-6
HEAD_DIM = 64
LANES = 128
Q_TILE = 128
SGU_CHUNK = 128
SGU_GROUPS = 8
N_CHIPS = 4
N_DEV = 8
ADAM_LR = 0.001
ADAM_B1 = 0.9
ADAM_B2 = 0.999
ADAM_EPS = 1e-08
ADAM_WD = 0.01
ADAM_STEP = 10
GELU_C0 = 0.7978845608028654
GELU_C1 = 0.044715
VMEM_LIMIT = 48 * 1024 * 1024
ROW_TILE = 1024
NT = (((1,), (1,)), ((), ()))
TN = (((0,), (0,)), ((), ()))


def _params(n_axes):
    return pltpu.CompilerParams(dimension_semantics=("arbitrary",) * n_axes, vmem_limit_bytes=VMEM_LIMIT)


def _rstd(x):
    return lax.rsqrt(jnp.mean(x * x, axis=-1, keepdims=True) + EPS)


def _norm_bwd(dh, x, gain):
    rstd = _rstd(x)
    xh = x * rstd
    dhg = dh * gain
    dx = rstd * (dhg - xh * jnp.mean(dhg * xh, axis=-1, keepdims=True))
    return dx, jnp.sum(dh * xh, axis=0, keepdims=True)


def _gelu(x):
    return 0.5 * x * (1.0 + jnp.tanh(GELU_C0 * (x + GELU_C1 * x * x * x)))


def _gelu_grad(x):
    t = jnp.tanh(GELU_C0 * (x + GELU_C1 * x * x * x))
    return 0.5 * (1.0 + t) + 0.5 * x * (1.0 - t * t) * (GELU_C0 * (1.0 + 3.0 * GELU_C1 * x * x))


def _act(a, act):
    if act == "relu2":
        r = jnp.maximum(a.astype(F32), 0.0)
        return (r * r).astype(BF16)
    return a.astype(BF16)


def _norm_matmul(x, gain, wg, layer, name):
    T, D = x.shape
    nsh, _, _, ns = wg.shape
    tm = min(T, ROW_TILE)

    def body(x_ref, g_ref, w_ref, y_ref, h_ref):
        @pl.when(pl.program_id(1) == 0)
        def _():
            xv = x_ref[...]
            h_ref[...] = (xv * _rstd(xv) * g_ref[...]).astype(BF16)

        y_ref[...] = jnp.dot(h_ref[...], w_ref[...], preferred_element_type=F32).astype(BF16)

    return pl.pallas_call(
        body,
        name=name,
        grid=(T // tm, nsh),
        in_specs=[
            pl.BlockSpec((tm, D), lambda i, j: (i, 0)),
            pl.BlockSpec((1, D), lambda i, j: (0, 0)),
            pl.BlockSpec((None, None, D, ns), lambda i, j: (j, layer, 0, 0)),
        ],
        out_specs=[pl.BlockSpec((tm, ns), lambda i, j: (i, j)), pl.BlockSpec((tm, D), lambda i, j: (i, 0))],
        out_shape=[jax.ShapeDtypeStruct((T, nsh * ns), BF16), jax.ShapeDtypeStruct((T, D), BF16)],
        compiler_params=_params(2),
    )(x, gain.reshape(1, D), wg)


def _act_matmul_res(a, wg, layer, x_in, act, name):
    T, K = a.shape
    nsh, _, kq, D = wg.shape
    tm = min(T, ROW_TILE)

    def body(a_ref, w_ref, x_ref, o_ref):
        @pl.when(pl.program_id(1) == 0)
        def _():
            o_ref[...] = x_ref[...]

        o_ref[...] += jnp.dot(_act(a_ref[...], act), w_ref[...], preferred_element_type=F32)

    return pl.pallas_call(
        body,
        name=name,
        grid=(T // tm, nsh),
        in_specs=[
            pl.BlockSpec((tm, kq), lambda i, k: (i, k)),
            pl.BlockSpec((None, None, kq, D), lambda i, k: (k, layer, 0, 0)),
            pl.BlockSpec((tm, D), lambda i, k: (i, 0)),
        ],
        out_specs=pl.BlockSpec((tm, D), lambda i, k: (i, 0)),
        out_shape=jax.ShapeDtypeStruct((T, D), F32),
        compiler_params=_params(2),
    )(a, wg, x_in)


def _matmul_nt(g, wg, layer, a, name):
    T, D = g.shape
    nsh, _, kq, _ = wg.shape
    tm = min(T, ROW_TILE)

    def body(g_ref, w_ref, *rest):
        r = lax.dot_general(g_ref[...], w_ref[...], NT, preferred_element_type=F32)
        if a is None:
            (o_ref,) = rest
        else:
            a_ref, o_ref = rest
            r = r * (2.0 * jnp.maximum(a_ref[...].astype(F32), 0.0))
        o_ref[...] = r.astype(BF16)

    in_specs = [
        pl.BlockSpec((tm, D), lambda i, k: (i, 0)),
        pl.BlockSpec((None, None, kq, D), lambda i, k: (k, layer, 0, 0)),
    ]
    args = [g, wg]
    if a is not None:
        in_specs.append(pl.BlockSpec((tm, kq), lambda i, k: (i, k)))
        args.append(a)
    return pl.pallas_call(
        body,
        name=name,
        grid=(T // tm, nsh),
        in_specs=in_specs,
        out_specs=pl.BlockSpec((tm, kq), lambda i, k: (i, k)),
        out_shape=jax.ShapeDtypeStruct((T, nsh * kq), BF16),
        compiler_params=_params(2),
    )(*args)


def _matmul_nt_norm_bwd(da, wg, layer, x, gain, dres, name):
    T, D = x.shape
    nsh, _, _, ns = wg.shape
    tm = min(T, ROW_TILE // 2)
    n_i = T // tm

    def body(da_ref, w_ref, x_ref, g_ref, r_ref, dx_ref, dxb_ref, dg_ref, acc_ref):
        i, j = pl.program_id(0), pl.program_id(1)

        @pl.when(j == 0)
        def _():
            acc_ref[...] = jnp.zeros_like(acc_ref)

        acc_ref[...] += lax.dot_general(da_ref[...], w_ref[...], NT, preferred_element_type=F32)

        @pl.when(j == nsh - 1)
        def _():
            dx, dg = _norm_bwd(acc_ref[...], x_ref[...], g_ref[...])
            dx = dx + r_ref[...]
            dx_ref[...] = dx
            dxb_ref[...] = dx.astype(BF16)

            @pl.when(i == 0)
            def _():
                dg_ref[...] = dg

            @pl.when(i > 0)
            def _():
                dg_ref[...] += dg

    del n_i
    return pl.pallas_call(
        body,
        name=name,
        grid=(T // tm, nsh),
        in_specs=[
            pl.BlockSpec((tm, ns), lambda i, j: (i, j)),
            pl.BlockSpec((None, None, D, ns), lambda i, j: (j, layer, 0, 0)),
            pl.BlockSpec((tm, D), lambda i, j: (i, 0)),
            pl.BlockSpec((1, D), lambda i, j: (0, 0)),
            pl.BlockSpec((tm, D), lambda i, j: (i, 0)),
        ],
        out_specs=[
            pl.BlockSpec((tm, D), lambda i, j: (i, 0)),
            pl.BlockSpec((tm, D), lambda i, j: (i, 0)),
            pl.BlockSpec((1, D), lambda i, j: (0, 0)),
        ],
        out_shape=[
            jax.ShapeDtypeStruct((T, D), F32),
            jax.ShapeDtypeStruct((T, D), BF16),
            jax.ShapeDtypeStruct((1, D), F32),
        ],
        scratch_shapes=[pltpu.VMEM((tm, D), F32)],
        compiler_params=_params(2),
    )(da, wg, x, gain.reshape(1, D), dres)


def _matmul_tn(lhs, rhs, bufs, layer, n_layers, shard_lhs, act, name):
    T = lhs.shape[0]
    rows = lhs.shape[1] // N_CHIPS if shard_lhs else lhs.shape[1]
    cols = rhs.shape[1] if shard_lhs else rhs.shape[1] // N_CHIPS
    tt = min(T, ROW_TILE)
    n_t = T // tt

    def body(l_ref, r_ref, *rest):
        o32_ref, o16_ref = rest[-2:]
        t = pl.program_id(1)
        upd = lax.dot_general(_act(l_ref[...], act), r_ref[...].astype(BF16), TN, preferred_element_type=F32)

        @pl.when(t == 0)
        def _():
            o32_ref[...] = upd

        @pl.when(t > 0)
        def _():
            o32_ref[...] += upd

        @pl.when(t == n_t - 1)
        def _():
            o16_ref[...] = o32_ref[...].astype(BF16)

    if shard_lhs:
        in_specs = [pl.BlockSpec((tt, rows), lambda s, t: (t, s)), pl.BlockSpec((tt, cols), lambda s, t: (t, 0))]
    else:
        in_specs = [pl.BlockSpec((tt, rows), lambda s, t: (t, 0)), pl.BlockSpec((tt, cols), lambda s, t: (t, s))]
    args = [lhs, rhs]
    aliases = {}
    if bufs is not None:
        in_specs += [pl.BlockSpec(memory_space=pl.ANY)] * 2
        args += list(bufs)
        aliases = {2: 0, 3: 1}
    shape = (N_CHIPS, n_layers, rows, cols)
    return pl.pallas_call(
        body,
        name=name,
        grid=(N_CHIPS, n_t),
        in_specs=in_specs,
        out_specs=[pl.BlockSpec((None, None, rows, cols), lambda s, t: (s, layer, 0, 0))] * 2,
        out_shape=[jax.ShapeDtypeStruct(shape, F32), jax.ShapeDtypeStruct(shape, BF16)],
        input_output_aliases=aliases,
        compiler_params=_params(2),
    )(*args)


def _chip_peers(x, y):
    return [(1 - x, y), (x, 1 - y), (1 - x, 1 - y)]


def _remote(src, dst, sems, s, peer):
    return pltpu.make_async_remote_copy(
        src_ref=src, dst_ref=dst, send_sem=sems[0].at[s], recv_sem=sems[1].at[s], device_id=peer, device_id_type=MESH
    )


class _Exchange:
    def __init__(self, operands, n_alias, new_shapes, n_sems, build):
        self.operands, self.n_alias, self.new_shapes, self.n_sems, self.build = operands, n_alias, new_shapes, n_sems, build

    def out_shapes(self):
        return [jax.ShapeDtypeStruct(a.shape, a.dtype) for a in self.operands[: self.n_alias]] + list(self.new_shapes)

    def scratch(self):
        return [pltpu.SemaphoreType.DMA((n,)) for n in self.n_sems]

    def run(self, ins, outs, sems, first, last):
        starts, recvs, sends, locals_ = self.build(ins, outs, sems)

        def start_all():
            for cp in starts:
                cp.start()

        def wait_all():
            for cp in recvs:
                cp.wait_recv()
            for cp in sends:
                cp.wait_send()
            for cp in locals_:
                cp.wait()

        if first is True:
            start_all()
            return wait_all
        pl.when(first)(start_all)
        return lambda: pl.when(last)(wait_all)


def _gather_exchange(shards, bufs, plan):
    n_arr = len(shards)
    n_cp = len(plan) * (N_CHIPS - 1)

    def build(ins, outs, sems):
        shard_refs = ins[-n_arr:]
        x, y, c = lax.axis_index("x"), lax.axis_index("y"), lax.axis_index("c")
        me = 2 * x + y
        starts, recvs, sends, locals_ = [], [], [], []
        for p, (a, l0, n) in enumerate(plan):
            src = shard_refs[a].at[pl.ds(l0, n)]
            cp = pltpu.make_async_copy(src, outs[a].at[me, pl.ds(l0, n)], sems[2].at[p])
            locals_.append(cp)
            for k, (px, py) in enumerate(_chip_peers(x, y)):
                s = p * (N_CHIPS - 1) + k
                sends.append(_remote(src, outs[a].at[me, pl.ds(l0, n)], sems, s, (px, py, c)))
                recvs.append(_remote(src, outs[a].at[2 * px + py, pl.ds(l0, n)], sems, s, (px, py, c)))
        return locals_ + sends, recvs, sends, locals_

    if bufs is None:
        new = [jax.ShapeDtypeStruct((N_CHIPS,) + s.shape, s.dtype) for s in shards]
        return _Exchange(list(shards), 0, new, [n_cp, n_cp, len(plan)], build)
    return _Exchange(list(bufs) + list(shards), n_arr, [], [n_cp, n_cp, len(plan)], build)


def _scatter_exchange(g16, recv, plan, small=None):
    n_arr = len(g16)
    have = [r for r in recv if r is not None]
    made = [a for a in range(n_arr) if recv[a] is None]
    n_cp = len(plan) * (N_CHIPS - 1) + (N_DEV - 1 if small is not None else 0)

    def build(ins, outs, sems):
        g_refs = ins[:n_arr]
        recv_refs, it_have, it_made = [], iter(outs[n_arr : n_arr + len(have)]), iter(outs[n_arr + len(have) :])
        for a in range(n_arr):
            recv_refs.append(next(it_made) if recv[a] is None else next(it_have))
        x, y, c = lax.axis_index("x"), lax.axis_index("y"), lax.axis_index("c")
        me = 2 * x + y
        starts, recvs, sends, locals_ = [], [], [], []
        for p, (a, l0, n) in enumerate(plan):
            for k, (px, py) in enumerate(_chip_peers(x, y)):
                s = p * (N_CHIPS - 1) + k
                dst = recv_refs[a].at[k, pl.ds(l0, n)]
                sends.append(_remote(g_refs[a].at[2 * px + py, pl.ds(l0, n)], dst, sems, s, (px, py, c)))
                recvs.append(_remote(g_refs[a].at[me, pl.ds(l0, n)], dst, sems, s, (px, py, c)))
        if small is not None:
            small_ref, all_ref = ins[-1], outs[-1]
            slot = 4 * x + 2 * y + c
            locals_.append(pltpu.make_async_copy(small_ref, all_ref.at[slot], sems[2].at[0]))
            flips = [(fx, fy, fc) for fx in (0, 1) for fy in (0, 1) for fc in (0, 1)][1:]
            for k, (fx, fy, fc) in enumerate(flips):
                s = len(plan) * (N_CHIPS - 1) + k
                px, py, pc = x ^ fx, y ^ fy, c ^ fc
                sends.append(_remote(small_ref, all_ref.at[slot], sems, s, (px, py, pc)))
                recvs.append(_remote(small_ref, all_ref.at[4 * px + 2 * py + pc], sems, s, (px, py, pc)))
        return locals_ + sends, recvs, sends, locals_

    operands = list(g16) + have + ([small] if small is not None else [])
    new = [jax.ShapeDtypeStruct((N_CHIPS - 1,) + g16[a].shape[1:], BF16) for a in made]
    if small is not None:
        new.append(jax.ShapeDtypeStruct((N_DEV,) + small.shape, F32))
    return _Exchange(operands, n_arr + len(have), new, [n_cp, n_cp, 1], build)


def _call_with_exchange(body, exch, name, grid, in_specs, out_specs, out_shape, scratch_shapes, args):
    n_in, n_out, n_scr = len(in_specs), len(out_shape), len(scratch_shapes)
    if exch is None:
        outs = pl.pallas_call(
            body, name=name, grid=grid, in_specs=in_specs, out_specs=out_specs, out_shape=out_shape,
            scratch_shapes=scratch_shapes, compiler_params=_params(len(grid)),
        )(*args)
        return outs, []
    e_shapes = exch.out_shapes()
    e_in, e_out = len(exch.operands), len(e_shapes)

    def wrapped(*refs):
        ins, refs = refs[:n_in], refs[n_in:]
        e_ins, refs = refs[:e_in], refs[e_in:]
        outs, refs = refs[:n_out], refs[n_out:]
        e_outs, refs = refs[:e_out], refs[e_out:]
        scr, sems = refs[:n_scr], refs[n_scr:]
        first, last = True, True
        for d, g in enumerate(grid):
            first = (pl.program_id(d) == 0) & first
            last = (pl.program_id(d) == g - 1) & last
        finish = exch.run(e_ins, e_outs, sems, first, last)
        body(*ins, *outs, *scr)
        finish()

    any_spec = pl.BlockSpec(memory_space=pl.ANY)
    outs = pl.pallas_call(
        wrapped,
        name=name,
        grid=grid,
        in_specs=list(in_specs) + [any_spec] * e_in,
        out_specs=list(out_specs) + [any_spec] * e_out,
        out_shape=list(out_shape) + e_shapes,
        input_output_aliases={n_in + i: n_out + i for i in range(exch.n_alias)},
        scratch_shapes=list(scratch_shapes) + exch.scratch(),
        compiler_params=pltpu.CompilerParams(
            dimension_semantics=("arbitrary",) * len(grid), vmem_limit_bytes=VMEM_LIMIT, has_side_effects=True
        ),
    )(*args, *exch.operands)
    return outs[:n_out], outs[n_out:]


def _exchange_only(exch, name):
    n_in = len(exch.operands)
    shapes = exch.out_shapes()

    def body(*refs):
        ins, outs, sems = refs[:n_in], refs[n_in : n_in + len(shapes)], refs[n_in + len(shapes) :]
        exch.run(ins, outs, sems, True, True)()

    any_spec = pl.BlockSpec(memory_space=pl.ANY)
    return pl.pallas_call(
        body,
        name=name,
        in_specs=[any_spec] * n_in,
        out_specs=[any_spec] * len(shapes),
        out_shape=shapes,
        input_output_aliases={i: i for i in range(exch.n_alias)},
        scratch_shapes=exch.scratch(),
        compiler_params=pltpu.CompilerParams(has_side_effects=True),
    )(*exch.operands)


ATTN_LANE_TILES = 2


MASKED = -1e30


def _hi_lo(x):
    hi = x.astype(BF16)
    lo = (x - hi.astype(F32)).astype(BF16)
    return jnp.concatenate([hi, lo], axis=1)


def _suffix_matrix(inclusive):
    j = lax.broadcasted_iota(jnp.int32, (2 * Q_TILE, 2 * Q_TILE), 0) & (Q_TILE - 1)
    s = lax.broadcasted_iota(jnp.int32, (2 * Q_TILE, 2 * Q_TILE), 1)
    later = (j >= s) if inclusive else (j > s)
    return jnp.where((s >= Q_TILE) | later, 1.0, 0.0).astype(BF16)


def _log_beta(z):
    return jnp.minimum(z, 0.0) - jnp.log(1.0 + jnp.exp(-jnp.abs(z)))


def _attn_specs(S, n_pairs, width):
    nq = S // Q_TILE
    groups = n_pairs * LANES // width
    q_spec = pl.BlockSpec((Q_TILE, width), lambda b, p, i: (b * nq + i, p))
    k_spec = pl.BlockSpec((S, width), lambda b, p, i: (b, groups + p))
    v_spec = pl.BlockSpec((S, width), lambda b, p, i: (b, 2 * groups + p))
    return nq, groups, q_spec, k_spec, v_spec


def _head_masks(width):
    lane = lax.broadcasted_iota(jnp.int32, (1, width), 1)
    return [(lane >= h * HEAD_DIM) & (lane < (h + 1) * HEAD_DIM) for h in range(width // HEAD_DIM)]


def _per_head_rows(x, masks):
    return jnp.concatenate([jnp.where(hm, x, 0) for hm in masks], axis=0)


def _heads_to_lanes(x, n_heads):
    return jnp.concatenate([x[h * Q_TILE : (h + 1) * Q_TILE] for h in range(n_heads)], axis=1)


def _block_start(kb):
    return kb * Q_TILE if isinstance(kb, int) else pl.multiple_of(kb * Q_TILE, Q_TILE)


def _past_mask(rows):
    t = lax.broadcasted_iota(jnp.int32, (rows, Q_TILE), 0) & (Q_TILE - 1)
    s = lax.broadcasted_iota(jnp.int32, (rows, Q_TILE), 1)
    return s < t


def _attn_fwd(qkv, n_seq, S, D, exch, name):
    T = n_seq * S
    width = min(D, ATTN_LANE_TILES * LANES)
    n_heads = width // HEAD_DIM
    rows = n_heads * Q_TILE
    nq, groups, q_spec, k_spec, v_spec = _attn_specs(S, D // LANES, width)
    scale = HEAD_DIM ** -0.5

    def body(q_ref, k_ref, v_ref, o_ref):
        qi = pl.program_id(2)
        masks = _head_masks(width)
        past = _past_mask(rows)
        sfx = _suffix_matrix(False)
        qh = _per_head_rows(q_ref[...] * scale, masks)

        def scores(kb, diag):
            off = _block_start(kb)
            z = lax.dot_general(qh, k_ref[pl.ds(off, Q_TILE), :], NT, preferred_element_type=F32)
            lb = _log_beta(z)
            l1 = lb - z
            if diag:
                l1 = jnp.where(past, l1, 0.0)
                lb = jnp.where(past, lb, MASKED)
            return lb, _hi_lo(l1)

        def weigh(kb, st, carry, acc):
            lb, l1 = st
            off = _block_start(kb)
            r = jnp.dot(l1, sfx, preferred_element_type=F32)
            a = jnp.exp(lb + r[:, :Q_TILE] + carry)
            vh = _per_head_rows(v_ref[pl.ds(off, Q_TILE), :], masks)
            acc = acc + jnp.dot(_heads_to_lanes(a.astype(BF16), n_heads), vh, preferred_element_type=F32)
            return carry + r[:, Q_TILE:], acc

        def trip(j, c):
            st, carry, acc = c
            nxt = scores(qi - 1 - j, False)
            carry, acc = weigh(qi - j, st, carry, acc)
            return nxt, carry, acc

        init = (scores(qi, True), jnp.zeros((rows, Q_TILE), F32), jnp.zeros((Q_TILE, width), F32))
        st, carry, acc = lax.fori_loop(0, qi, trip, init)
        _, acc = weigh(0, st, carry, acc)
        o_ref[...] = acc

    (o,), moved = _call_with_exchange(
        body,
        exch,
        name,
        grid=(n_seq, groups, nq),
        in_specs=[q_spec, k_spec, v_spec],
        out_specs=[pl.BlockSpec((Q_TILE, width), lambda b, p, i: (b * nq + i, p))],
        out_shape=[jax.ShapeDtypeStruct((T, D), F32)],
        scratch_shapes=[],
        args=(qkv, qkv, qkv),
    )
    return o, moved


def _attn_bwd(qkv, o, do, n_seq, S, D, exch, name):
    T = n_seq * S
    width = min(D, ATTN_LANE_TILES * LANES)
    n_heads = width // HEAD_DIM
    rows = n_heads * Q_TILE
    nq, groups, q_spec, k_spec, v_spec = _attn_specs(S, D // LANES, width)
    scale = HEAD_DIM ** -0.5
    row_spec = pl.BlockSpec((Q_TILE, width), lambda b, p, i: (b * nq + i, p))
    seq_spec = pl.BlockSpec((S, width), lambda b, p, i: (b, p))

    def body(q_ref, k_ref, v_ref, o_ref, do_ref, dq_ref, dk_ref, dv_ref, dk_acc, dv_acc):
        qi = pl.program_id(2)

        @pl.when(qi == 0)
        def _():
            dk_acc[...] = jnp.zeros_like(dk_acc)
            dv_acc[...] = jnp.zeros_like(dv_acc)

        masks = _head_masks(width)
        past = _past_mask(rows)
        sfx = _suffix_matrix(False)
        sfx_incl = _suffix_matrix(True)
        qh = _per_head_rows(q_ref[...] * scale, masks)
        do = do_ref[...]
        doh = _per_head_rows(do, masks)
        prod = do.astype(F32) * o_ref[...]
        delta = jnp.concatenate(
            [jnp.sum(jnp.where(hm, prod, 0.0), axis=-1, keepdims=True) for hm in masks], axis=0
        )

        def scores(kb, diag):
            off = _block_start(kb)
            z = lax.dot_general(qh, k_ref[pl.ds(off, Q_TILE), :], NT, preferred_element_type=F32)
            lb = _log_beta(z)
            l1 = lb - z
            if diag:
                l1 = jnp.where(past, l1, 0.0)
                lb = jnp.where(past, lb, MASKED)
            da = lax.dot_general(doh, v_ref[pl.ds(off, Q_TILE), :], NT, preferred_element_type=F32)
            return lb, _hi_lo(l1), da

        def weigh(kb, st, c1, c2, dq):
            lb, l1, da = st
            off = _block_start(kb)
            r = jnp.dot(l1, sfx, preferred_element_type=F32)
            ab = jnp.exp(lb + r[:, :Q_TILE] + c1).astype(BF16)
            g = ab.astype(F32) * da
            r2 = jnp.dot(_hi_lo(g), sfx_incl, preferred_element_type=F32)
            earlier = delta - (r2[:, :Q_TILE] + c2)
            beta = jnp.exp(lb)
            dzb = (g * (1.0 - beta) - earlier * beta).astype(BF16)
            kh = _per_head_rows(k_ref[pl.ds(off, Q_TILE), :], masks)
            dq = dq + jnp.dot(_heads_to_lanes(dzb, n_heads), kh, preferred_element_type=F32)
            dk_acc[pl.ds(off, Q_TILE), :] += lax.dot_general(dzb, qh, TN, preferred_element_type=F32)
            dv_acc[pl.ds(off, Q_TILE), :] += lax.dot_general(ab, doh, TN, preferred_element_type=F32)
            return c1 + r[:, Q_TILE:], c2 + r2[:, Q_TILE:], dq

        def trip(j, c):
            st, c1, c2, dq = c
            nxt = scores(qi - 1 - j, False)
            c1, c2, dq = weigh(qi - j, st, c1, c2, dq)
            return nxt, c1, c2, dq

        zero = jnp.zeros((rows, Q_TILE), F32)
        st, c1, c2, dq = lax.fori_loop(0, qi, trip, (scores(qi, True), zero, zero, jnp.zeros((Q_TILE, width), F32)))
        _, _, dq = weigh(0, st, c1, c2, dq)
        dq_ref[...] = (dq * scale).astype(BF16)

        @pl.when(qi == nq - 1)
        def _():
            dk_ref[...] = dk_acc[...].astype(BF16)
            dv_ref[...] = dv_acc[...].astype(BF16)

    return _call_with_exchange(
        body,
        exch,
        name,
        grid=(n_seq, groups, nq),
        in_specs=[q_spec, k_spec, v_spec, row_spec, row_spec],
        out_specs=[row_spec, seq_spec, seq_spec],
        out_shape=[jax.ShapeDtypeStruct((T, D), BF16)] * 3,
        scratch_shapes=[pltpu.VMEM((S, width), F32), pltpu.VMEM((S, width), F32)],
        args=(qkv, qkv, qkv, o, do),
    )


def _causal_ws(ws_ref, g):
    t = lax.broadcasted_iota(jnp.int32, (SGU_CHUNK, SGU_CHUNK), 0)
    s = lax.broadcasted_iota(jnp.int32, (SGU_CHUNK, SGU_CHUNK), 1)
    return jnp.where(s <= t, ws_ref[g], 0.0)


def _sgu_fwd(a, gain, ws, bsb, name):
    T, F2 = a.shape
    F = F2 // 2
    gw = F // SGU_GROUPS

    def body(a_ref, gain_ref, ws_ref, bsb_ref, y_ref):
        v = _gelu(a_ref[:, F:].astype(F32))
        vn = (v * _rstd(v) * gain_ref[...]).astype(BF16)
        for g in range(SGU_GROUPS):
            cs = slice(g * gw, (g + 1) * gw)
            w = _causal_ws(ws_ref, g).astype(BF16)
            mixed = jnp.dot(w, vn[:, cs], preferred_element_type=F32) + bsb_ref[g]
            y_ref[:, cs] = (_gelu(a_ref[:, cs].astype(F32)) * mixed).astype(BF16)

    return pl.pallas_call(
        body,
        name=name,
        grid=(T // SGU_CHUNK,),
        in_specs=[
            pl.BlockSpec((SGU_CHUNK, F2), lambda i: (i, 0)),
            pl.BlockSpec((1, F), lambda i: (0, 0)),
            pl.BlockSpec((SGU_GROUPS, SGU_CHUNK, SGU_CHUNK), lambda i: (0, 0, 0)),
            pl.BlockSpec((SGU_GROUPS, SGU_CHUNK, gw), lambda i: (0, 0, 0)),
        ],
        out_specs=pl.BlockSpec((SGU_CHUNK, F), lambda i: (i, 0)),
        out_shape=jax.ShapeDtypeStruct((T, F), BF16),
        compiler_params=_params(1),
    )(a, gain.reshape(1, F), ws, bsb)


def _sgu_bwd(a, dy, gain, ws, bsb, name):
    T, F2 = a.shape
    F = F2 // 2
    gw = F // SGU_GROUPS

    def body(a_ref, dy_ref, gain_ref, ws_ref, bsb_ref, da_ref, dws_ref, dbs_ref, dgain_ref, dvn_ref):
        @pl.when(pl.program_id(0) == 0)
        def _():
            dws_ref[...] = jnp.zeros_like(dws_ref)
            dbs_ref[...] = jnp.zeros_like(dbs_ref)
            dgain_ref[...] = jnp.zeros_like(dgain_ref)

        av = a_ref[:, F:].astype(F32)
        v = _gelu(av)
        rstd = _rstd(v)
        vh = v * rstd
        gain = gain_ref[...]
        vn = (vh * gain).astype(BF16)
        ones = jnp.ones((gw, SGU_CHUNK), BF16)
        for g in range(SGU_GROUPS):
            cs = slice(g * gw, (g + 1) * gw)
            w = _causal_ws(ws_ref, g).astype(BF16)
            mixed = jnp.dot(w, vn[:, cs], preferred_element_type=F32) + bsb_ref[g]
            au = a_ref[:, cs].astype(F32)
            dyc = dy_ref[:, cs].astype(F32)
            da_ref[:, cs] = (dyc * mixed * _gelu_grad(au)).astype(BF16)
            dm = (dyc * _gelu(au)).astype(BF16)
            dbs_ref[g] += jnp.dot(dm, ones, preferred_element_type=F32)
            dws_ref[g] += _causal_mask_f32(lax.dot_general(dm, vn[:, cs], NT, preferred_element_type=F32))
            dvn_ref[:, cs] = lax.dot_general(w, dm, TN, preferred_element_type=F32)
        dvn = dvn_ref[...]
        dgain_ref[...] += jnp.sum(dvn * vh, axis=0, keepdims=True)
        dvh = dvn * gain
        dv = rstd * (dvh - vh * jnp.mean(dvh * vh, axis=-1, keepdims=True))
        da_ref[:, F:] = (dv * _gelu_grad(av)).astype(BF16)

    acc_spec = pl.BlockSpec((SGU_GROUPS, SGU_CHUNK, SGU_CHUNK), lambda i: (0, 0, 0))
    acc_shape = jax.ShapeDtypeStruct((SGU_GROUPS, SGU_CHUNK, SGU_CHUNK), F32)
    return pl.pallas_call(
        body,
        name=name,
        grid=(T // SGU_CHUNK,),
        in_specs=[
            pl.BlockSpec((SGU_CHUNK, F2), lambda i: (i, 0)),
            pl.BlockSpec((SGU_CHUNK, F), lambda i: (i, 0)),
            pl.BlockSpec((1, F), lambda i: (0, 0)),
            acc_spec,
            pl.BlockSpec((SGU_GROUPS, SGU_CHUNK, gw), lambda i: (0, 0, 0)),
        ],
        out_specs=[
            pl.BlockSpec((SGU_CHUNK, F2), lambda i: (i, 0)),
            acc_spec,
            acc_spec,
            pl.BlockSpec((1, F), lambda i: (0, 0)),
        ],
        out_shape=[
            jax.ShapeDtypeStruct((T, F2), BF16),
            acc_shape,
            acc_shape,
            jax.ShapeDtypeStruct((1, F), F32),
        ],
        scratch_shapes=[pltpu.VMEM((SGU_CHUNK, F), F32)],
        compiler_params=_params(1),
    )(a, dy, gain.reshape(1, F), ws, bsb)


def _causal_mask_f32(m):
    t = lax.broadcasted_iota(jnp.int32, m.shape, 0)
    s = lax.broadcasted_iota(jnp.int32, m.shape, 1)
    return jnp.where(s <= t, m, 0.0)


def _final_loss(x, gain, target, name):
    T, D = x.shape
    tm = min(T, ROW_TILE // 2)

    def body(x_ref, g_ref, t_ref, sq_ref, dx_ref, dxb_ref, dg_ref):
        xv = x_ref[...]
        gain = g_ref[...]
        err = xv * _rstd(xv) * gain - t_ref[...]
        dx, dg = _norm_bwd(err * (1.0 / D), xv, gain)
        dx_ref[...] = dx
        dxb_ref[...] = dx.astype(BF16)
        sq = jnp.sum(err * err, axis=0, keepdims=True)

        @pl.when(pl.program_id(0) == 0)
        def _():
            sq_ref[...] = sq
            dg_ref[...] = dg

        @pl.when(pl.program_id(0) > 0)
        def _():
            sq_ref[...] += sq
            dg_ref[...] += dg

    row = pl.BlockSpec((tm, D), lambda i: (i, 0))
    vec = pl.BlockSpec((1, D), lambda i: (0, 0))
    return pl.pallas_call(
        body,
        name=name,
        grid=(T // tm,),
        in_specs=[row, vec, row],
        out_specs=[vec, row, row, vec],
        out_shape=[
            jax.ShapeDtypeStruct((1, D), F32),
            jax.ShapeDtypeStruct((T, D), F32),
            jax.ShapeDtypeStruct((T, D), BF16),
            jax.ShapeDtypeStruct((1, D), F32),
        ],
        compiler_params=_params(1),
    )(x, gain.reshape(1, D), target)


def _row_tile(rows, cols, n_arrays):
    budget = VMEM_LIMIT // 2 // (2 * n_arrays * cols * 4)
    tr = rows
    while tr > budget and tr % 16 == 0:
        tr //= 2
    return tr


def _sum_received(own, recv, name):
    R, C = own.shape
    n = recv.shape[0]
    tr = _row_tile(R, C, n + 2)

    def body(own_ref, recv_ref, o_ref):
        s = own_ref[...]
        for k in range(n):
            s = s + recv_ref[k].astype(F32)
        o_ref[...] = s

    return pl.pallas_call(
        body,
        name=name,
        grid=(R // tr,),
        in_specs=[pl.BlockSpec((tr, C), lambda i: (i, 0)), pl.BlockSpec((n, tr, C), lambda i: (0, i, 0))],
        out_specs=pl.BlockSpec((tr, C), lambda i: (i, 0)),
        out_shape=jax.ShapeDtypeStruct((R, C), F32),
        compiler_params=_params(1),
    )(own, recv)


def _sum_chip_shard(g32, recv, chip, name):
    _, L, r, c = g32.shape
    n = recv.shape[0]
    tr = _row_tile(r, c, n + 2)

    def body(chip_ref, own_ref, recv_ref, o_ref):
        s = own_ref[...]
        for k in range(n):
            s = s + recv_ref[k].astype(F32)
        o_ref[...] = s

    return pl.pallas_call(
        body,
        name=name,
        grid_spec=pltpu.PrefetchScalarGridSpec(
            num_scalar_prefetch=1,
            grid=(L, r // tr),
            in_specs=[
                pl.BlockSpec((None, None, tr, c), lambda l, i, chip_ref: (chip_ref[0], l, i, 0)),
                pl.BlockSpec((n, None, tr, c), lambda l, i, chip_ref: (0, l, i, 0)),
            ],
            out_specs=pl.BlockSpec((None, tr, c), lambda l, i, chip_ref: (l, i, 0)),
        ),
        out_shape=jax.ShapeDtypeStruct((L, r, c), F32),
        compiler_params=_params(2),
    )(chip.reshape(1).astype(jnp.int32), g32, recv)


def _adamw(w, m, v, parts, name):
    R, C = w.shape
    n = len(parts)
    tr = _row_tile(R, C, n + 7)

    def body(*refs):
        w_ref, m_ref, v_ref = refs[:3]
        g_ref, d_ref, nm_ref, nv_ref = refs[3 + n :]
        g = refs[3][...]
        for p_ref in refs[4 : 3 + n]:
            g = g + p_ref[...]
        nm = ADAM_B1 * m_ref[...] + (1.0 - ADAM_B1) * g
        nv = ADAM_B2 * v_ref[...] + (1.0 - ADAM_B2) * (g * g)
        m_hat = nm / (1.0 - ADAM_B1**ADAM_STEP)
        v_hat = nv / (1.0 - ADAM_B2**ADAM_STEP)
        g_ref[...] = g
        d_ref[...] = -ADAM_LR * (m_hat / (jnp.sqrt(v_hat) + ADAM_EPS) + ADAM_WD * w_ref[...])
        nm_ref[...] = nm
        nv_ref[...] = nv

    spec = pl.BlockSpec((tr, C), lambda i: (i, 0))
    return pl.pallas_call(
        body,
        name=name,
        grid=(R // tr,),
        in_specs=[spec] * (3 + n),
        out_specs=[spec] * 4,
        out_shape=[jax.ShapeDtypeStruct((R, C), F32)] * 4,
        compiler_params=_params(1),
    )(w, m, v, *parts)


def _swap_with_sibling(parts, name):
    n = len(parts)

    def body(*refs):
        ins, outs = refs[:n], refs[n : 2 * n]
        send_sems, recv_sems = refs[2 * n :]
        sibling = (lax.axis_index("x"), lax.axis_index("y"), 1 - lax.axis_index("c"))
        copies = [
            pltpu.make_async_remote_copy(
                src_ref=ins[a],
                dst_ref=outs[a],
                send_sem=send_sems.at[a],
                recv_sem=recv_sems.at[a],
                device_id=sibling,
                device_id_type=MESH,
            )
            for a in range(n)
        ]
        for cp in copies:
            cp.start()
        for cp in copies:
            cp.wait_recv()
        for cp in copies:
            cp.wait_send()

    any_spec = pl.BlockSpec(memory_space=pl.ANY)
    return pl.pallas_call(
        body,
        name=name,
        in_specs=[any_spec] * n,
        out_specs=[any_spec] * n,
        out_shape=[jax.ShapeDtypeStruct(p.shape, p.dtype) for p in parts],
        scratch_shapes=[pltpu.SemaphoreType.DMA((n,)), pltpu.SemaphoreType.DMA((n,))],
        compiler_params=pltpu.CompilerParams(has_side_effects=True),
    )(*parts)


def _pack(pieces):
    flat = jnp.concatenate([p.reshape(-1) for p in pieces])
    return flat.reshape(-1, LANES)


def _unpack(packed, shapes):
    flat = packed.reshape(-1)
    out, off = [], 0
    for s in shapes:
        size = 1
        for d in s:
            size *= d
        out.append(flat[off : off + size].reshape(s))
        off += size
    return out


def kernel(x, norm_mix, norm_mlp, sb_wqkv, sb_wo, sgu_win, sgu_gain, sgu_ws, sgu_bs, sgu_wout, mlp_w1, mlp_w2, final_norm, loss_target, m_norm_mix, m_norm_mlp, m_sb_wqkv, m_sb_wo, m_sgu_win, m_sgu_gain, m_sgu_ws, m_sgu_bs, m_sgu_wout, m_mlp_w1, m_mlp_w2, m_final_norm, v_norm_mix, v_norm_mlp, v_sb_wqkv, v_sb_wo, v_sgu_win, v_sgu_gain, v_sgu_ws, v_sgu_bs, v_sgu_wout, v_mlp_w1, v_mlp_w2, v_final_norm):
    n_seq, S, D = x.shape
    T = n_seq * S
    depth = norm_mix.shape[0]
    n_sgu = sgu_win.shape[0]
    F = sgu_wout.shape[1] * N_CHIPS
    gw = F // SGU_GROUPS
    chip = 2 * lax.axis_index("x") + lax.axis_index("y")

    QKV, WO, WIN, WOUT, W1, W2, GAIN = range(7)
    big = [sb_wqkv, sb_wo, sgu_win, sgu_wout, mlp_w1, mlp_w2]
    n_sb = sb_wqkv.shape[0]
    shards = [w.astype(BF16) for w in big] + [sgu_gain.reshape(1, -1, LANES)]
    first_plan = [(QKV, 0, 1), (WO, 0, 1), (W1, 0, 1), (W2, 0, 1), (GAIN, 0, 1)]
    rest_plan = [(QKV, 1, n_sb - 1), (WO, 1, n_sb - 1), (WIN, 0, n_sgu), (WOUT, 0, n_sgu), (W1, 1, depth - 1), (W2, 1, depth - 1)]
    rest_plan = [p for p in rest_plan if p[2] > 0]
    wg = _exchange_only(_gather_exchange(shards, None, first_plan), "gather_first_layer")
    gain_full = jnp.transpose(wg[GAIN].reshape(N_CHIPS, n_sgu, F // N_CHIPS), (1, 0, 2)).reshape(n_sgu, F)
    bsb = [jnp.broadcast_to(sgu_bs[j][:, :, None], (SGU_GROUPS, SGU_CHUNK, gw)) for j in range(n_sgu)]

    xs = x.reshape(T, D)
    saved = []
    for i in range(depth):
        j = i // 2
        if i % 2 == 0:
            qkv, h = _norm_matmul(xs, norm_mix[i], wg[QKV], j, f"qkv_fwd_{i}")
            exch = _gather_exchange(shards, wg, rest_plan) if i == 0 and rest_plan else None
            o, moved = _attn_fwd(qkv, n_seq, S, D, exch, f"attn_fwd_{i}")
            if exch is not None:
                wg = moved
            wg_qkv, wg_wo, wg_win, wg_wout, wg_w1, wg_w2 = wg[:6]
            x_mid = _act_matmul_res(o, wg_wo, j, xs, None, f"wo_fwd_{i}")
            mix = (qkv, o)
        else:
            a, h = _norm_matmul(xs, norm_mix[i], wg_win, j, f"win_fwd_{i}")
            yg = _sgu_fwd(a, gain_full[j], sgu_ws[j], bsb[j], f"sgu_fwd_{i}")
            x_mid = _act_matmul_res(yg, wg_wout, j, xs, None, f"wout_fwd_{i}")
            mix = (a, yg)
        a2, h2 = _norm_matmul(x_mid, norm_mlp[i], wg_w1, i, f"w1_fwd_{i}")
        x_out = _act_matmul_res(a2, wg_w2, i, x_mid, "relu2", f"w2_fwd_{i}")
        saved.append((xs, h, mix, x_mid, h2, a2))
        xs = x_out

    sq, dx, dxb, g_final = _final_loss(xs, final_norm, loss_target.reshape(T, D), "loss_head")
    loss = lax.psum(0.5 * jnp.sum(sq) / D, ("x", "y", "c"))

    n_layers = [n_sb, n_sb, n_sgu, n_sgu, depth, depth]
    g32, g16, recv = [None] * 6, [None] * 6, [None] * 6
    done_from, sent_from = list(n_layers), list(n_layers)

    def grad(a, layer, lhs, rhs, shard_lhs, act, name):
        bufs = None if g32[a] is None else (g32[a], g16[a])
        g32[a], g16[a] = _matmul_tn(lhs, rhs, bufs, layer, n_layers[a], shard_lhs, act, name)
        done_from[a] = layer

    def unsent_plan():
        plan = [(a, done_from[a], sent_from[a] - done_from[a]) for a in range(6) if sent_from[a] > done_from[a]]
        for a, l0, _ in plan:
            sent_from[a] = l0
        return plan

    def scatter(plan, small):
        arrays = sorted({a for a, _, _ in plan})
        have = [a for a in arrays if recv[a] is not None]
        made = [a for a in arrays if recv[a] is None]
        exch = _scatter_exchange(
            [g16[a] for a in arrays], [recv[a] for a in arrays], [(arrays.index(a), l0, n) for a, l0, n in plan], small
        )

        def take(moved):
            for a, buf in zip(arrays, moved):
                g16[a] = buf
            for a, buf in zip(have + made, moved[len(arrays) :]):
                recv[a] = buf
            return moved[-1]

        return exch, take

    g_mix, g_mlp = [None] * depth, [None] * depth
    g_ws, g_bs, g_gain = [None] * n_sgu, [None] * n_sgu, [None] * n_sgu
    for i in reversed(range(depth)):
        j = i // 2
        x_in, h, mix, x_mid, h2, a2 = saved[i]
        da2 = _matmul_nt(dxb, wg_w2, i, a2, f"w2_bwd_{i}")
        grad(W2, i, a2, dxb, True, "relu2", f"w2_grad_{i}")
        grad(W1, i, h2, da2, False, None, f"w1_grad_{i}")
        dx, dxb, g_mlp[i] = _matmul_nt_norm_bwd(da2, wg_w1, i, x_mid, norm_mlp[i], dx, f"w1_bwd_{i}")
        if i % 2 == 0:
            qkv, o = mix
            do = _matmul_nt(dxb, wg_wo, j, None, f"wo_bwd_{i}")
            grad(WO, j, o, dxb, True, None, f"wo_grad_{i}")
            exch, take = scatter(unsent_plan(), None)
            (dq, dk, dv), moved = _attn_bwd(qkv, o, do, n_seq, S, D, exch, f"attn_bwd_{i}")
            take(moved)
            dqkv = jnp.concatenate([dq, dk, dv], axis=1)
            grad(QKV, j, h, dqkv, False, None, f"qkv_grad_{i}")
            dx, dxb, g_mix[i] = _matmul_nt_norm_bwd(dqkv, wg_qkv, j, x_in, norm_mix[i], dx, f"qkv_bwd_{i}")
        else:
            a, yg = mix
            dyg = _matmul_nt(dxb, wg_wout, j, None, f"wout_bwd_{i}")
            grad(WOUT, j, yg, dxb, True, None, f"wout_grad_{i}")
            da, g_ws[j], dbs, g_gain[j] = _sgu_bwd(a, dyg, gain_full[j], sgu_ws[j], bsb[j], f"sgu_bwd_{i}")
            g_bs[j] = dbs[:, :, 0]
            grad(WIN, j, h, da, False, None, f"win_grad_{i}")
            dx, dxb, g_mix[i] = _matmul_nt_norm_bwd(da, wg_win, j, x_in, norm_mix[i], dx, f"win_bwd_{i}")
    grad_x = dx.reshape(n_seq, S, D)

    names = ["qkv", "wo", "win", "wout", "w1", "w2"]
    small_shapes = [norm_mix.shape, norm_mlp.shape, final_norm.shape, sgu_ws.shape, sgu_bs.shape, (n_sgu, F)]
    small = _pack(
        [jnp.stack(g_mix), jnp.stack(g_mlp), g_final, jnp.stack(g_ws), jnp.stack(g_bs), jnp.stack(g_gain)]
    )
    exch, take = scatter(unsent_plan(), small)
    small_all = take(_exchange_only(exch, "scatter_last_grads"))
    partial = [_sum_chip_shard(g32[a], recv[a], chip, f"sum_{names[a]}") for a in range(6)]
    partial = [p.reshape(-1, p.shape[-1]) for p in partial]
    theirs = _swap_with_sibling(partial, "swap_partial_sums")
    small_sum = _sum_received(small_all[0], small_all[1:], "sum_small")

    ms = [m_sb_wqkv, m_sb_wo, m_sgu_win, m_sgu_wout, m_mlp_w1, m_mlp_w2]
    vs = [v_sb_wqkv, v_sb_wo, v_sgu_win, v_sgu_wout, v_mlp_w1, v_mlp_w2]
    res = {}
    keys = ["sb_wqkv", "sb_wo", "sgu_win", "sgu_wout", "mlp_w1", "mlp_w2"]
    for key, k, w, m, v, mine, other in zip(keys, names, big, ms, vs, partial, theirs):
        cols = w.shape[-1]
        outs = _adamw(w.reshape(-1, cols), m.reshape(-1, cols), v.reshape(-1, cols), [mine, other], f"adamw_{k}")
        res[key] = [o.reshape(w.shape) for o in outs]

    g_small = _unpack(small_sum, small_shapes)
    g_small[5] = lax.dynamic_slice_in_dim(g_small[5], chip * (F // N_CHIPS), F // N_CHIPS, axis=1)
    small_keys = ["norm_mix", "norm_mlp", "final_norm", "sgu_ws", "sgu_bs", "sgu_gain"]
    small_w = [norm_mix, norm_mlp, final_norm, sgu_ws, sgu_bs, sgu_gain]
    small_m = [m_norm_mix, m_norm_mlp, m_final_norm, m_sgu_ws, m_sgu_bs, m_sgu_gain]
    small_v = [v_norm_mix, v_norm_mlp, v_final_norm, v_sgu_ws, v_sgu_bs, v_sgu_gain]
    outs = _adamw(_pack(small_w), _pack(small_m), _pack(small_v), [_pack(g_small)], "adamw_small")
    local_shapes = [w.shape for w in small_w]
    for key, parts in zip(small_keys, zip(*[_unpack(o, local_shapes) for o in outs])):
        res[key] = list(parts)

    order = ["norm_mix", "norm_mlp", "sb_wqkv", "sb_wo", "sgu_win", "sgu_gain", "sgu_ws", "sgu_bs", "sgu_wout", "mlp_w1", "mlp_w2", "final_norm"]
    return (loss, grad_x, *[res[k][0] for k in order], *[res[k][1] for k in order], *[res[k][2] for k in order], *[res[k][3] for k in order])
```

```python
import jax
import jax.numpy as jnp
from jax import lax
from jax.experimental import pallas as pl
from jax.experimental.pallas import tpu as pltpu

F32 = jnp.float32
BF16 = jnp.bfloat16
MESH = pl.DeviceIdType.MESH

EPS = 1e-6
HEAD_DIM = 64
LANES = 128
Q_TILE = 128
SGU_CHUNK = 128
SGU_GROUPS = 8
N_CHIPS = 4
N_DEV = 8
ADAM_LR = 0.001
ADAM_B1 = 0.9
ADAM_B2 = 0.999
ADAM_EPS = 1e-08
ADAM_WD = 0.01
ADAM_STEP = 10
GELU_C0 = 0.7978845608028654
GELU_C1 = 0.044715
VMEM_LIMIT = 48 * 1024 * 1024
ROW_TILE = 1024
NT = (((1,), (1,)), ((), ()))
TN = (((0,), (0,)), ((), ()))


def _params(n_axes):
    return pltpu.CompilerParams(dimension_semantics=("arbitrary",) * n_axes, vmem_limit_bytes=VMEM_LIMIT)


def _rstd(x):
    return lax.rsqrt(jnp.mean(x * x, axis=-1, keepdims=True) + EPS)


def _norm_bwd(dh, x, gain):
    rstd = _rstd(x)
    xh = x * rstd
    dhg = dh * gain
    dx = rstd * (dhg - xh * jnp.mean(dhg * xh, axis=-1, keepdims=True))
    return dx, jnp.sum(dh * xh, axis=0, keepdims=True)


def _gelu(x):
    return 0.5 * x * (1.0 + jnp.tanh(GELU_C0 * (x + GELU_C1 * x * x * x)))


def _gelu_grad(x):
    t = jnp.tanh(GELU_C0 * (x + GELU_C1 * x * x * x))
    return 0.5 * (1.0 + t) + 0.5 * x * (1.0 - t * t) * (GELU_C0 * (1.0 + 3.0 * GELU_C1 * x * x))


def _act(a, act):
    if act == "relu2":
        r = jnp.maximum(a.astype(F32), 0.0)
        return (r * r).astype(BF16)
    return a.astype(BF16)


def _norm_matmul(x, gain, wg, layer, name):
    T, D = x.shape
    nsh, _, _, ns = wg.shape
    tm = min(T, ROW_TILE)

    def body(x_ref, g_ref, w_ref, y_ref, h_ref):
        @pl.when(pl.program_id(1) == 0)
        def _():
            xv = x_ref[...]
            h_ref[...] = (xv * _rstd(xv) * g_ref[...]).astype(BF16)

        y_ref[...] = jnp.dot(h_ref[...], w_ref[...], preferred_element_type=F32).astype(BF16)

    return pl.pallas_call(
        body,
        name=name,
        grid=(T // tm, nsh),
        in_specs=[
            pl.BlockSpec((tm, D), lambda i, j: (i, 0)),
            pl.BlockSpec((1, D), lambda i, j: (0, 0)),
            pl.BlockSpec((None, None, D, ns), lambda i, j: (j, layer, 0, 0)),
        ],
        out_specs=[pl.BlockSpec((tm, ns), lambda i, j: (i, j)), pl.BlockSpec((tm, D), lambda i, j: (i, 0))],
        out_shape=[jax.ShapeDtypeStruct((T, nsh * ns), BF16), jax.ShapeDtypeStruct((T, D), BF16)],
        compiler_params=_params(2),
    )(x, gain.reshape(1, D), wg)


def _act_matmul_res(a, wg, layer, x_in, act, name):
    T, K = a.shape
    nsh, _, kq, D = wg.shape
    tm = min(T, ROW_TILE)

    def body(a_ref, w_ref, x_ref, o_ref):
        @pl.when(pl.program_id(1) == 0)
        def _():
            o_ref[...] = x_ref[...]

        o_ref[...] += jnp.dot(_act(a_ref[...], act), w_ref[...], preferred_element_type=F32)

    return pl.pallas_call(
        body,
        name=name,
        grid=(T // tm, nsh),
        in_specs=[
            pl.BlockSpec((tm, kq), lambda i, k: (i, k)),
            pl.BlockSpec((None, None, kq, D), lambda i, k: (k, layer, 0, 0)),
            pl.BlockSpec((tm, D), lambda i, k: (i, 0)),
        ],
        out_specs=pl.BlockSpec((tm, D), lambda i, k: (i, 0)),
        out_shape=jax.ShapeDtypeStruct((T, D), F32),
        compiler_params=_params(2),
    )(a, wg, x_in)


def _matmul_nt(g, wg, layer, a, name):
    T, D = g.shape
    nsh, _, kq, _ = wg.shape
    tm = min(T, ROW_TILE)

    def body(g_ref, w_ref, *rest):
        r = lax.dot_general(g_ref[...], w_ref[...], NT, preferred_element_type=F32)
        if a is None:
            (o_ref,) = rest
        else:
            a_ref, o_ref = rest
            r = r * (2.0 * jnp.maximum(a_ref[...].astype(F32), 0.0))
        o_ref[...] = r.astype(BF16)

    in_specs = [
        pl.BlockSpec((tm, D), lambda i, k: (i, 0)),
        pl.BlockSpec((None, None, kq, D), lambda i, k: (k, layer, 0, 0)),
    ]
    args = [g, wg]
    if a is not None:
        in_specs.append(pl.BlockSpec((tm, kq), lambda i, k: (i, k)))
        args.append(a)
    return pl.pallas_call(
        body,
        name=name,
        grid=(T // tm, nsh),
        in_specs=in_specs,
        out_specs=pl.BlockSpec((tm, kq), lambda i, k: (i, k)),
        out_shape=jax.ShapeDtypeStruct((T, nsh * kq), BF16),
        compiler_params=_params(2),
    )(*args)


def _matmul_nt_norm_bwd(da, wg, layer, x, gain, dres, name):
    T, D = x.shape
    nsh, _, _, ns = wg.shape
    tm = min(T, ROW_TILE // 2)
    n_i = T // tm

    def body(da_ref, w_ref, x_ref, g_ref, r_ref, dx_ref, dxb_ref, dg_ref, acc_ref):
        i, j = pl.program_id(0), pl.program_id(1)

        @pl.when(j == 0)
        def _():
            acc_ref[...] = jnp.zeros_like(acc_ref)

        acc_ref[...] += lax.dot_general(da_ref[...], w_ref[...], NT, preferred_element_type=F32)

        @pl.when(j == nsh - 1)
        def _():
            dx, dg = _norm_bwd(acc_ref[...], x_ref[...], g_ref[...])
            dx = dx + r_ref[...]
            dx_ref[...] = dx
            dxb_ref[...] = dx.astype(BF16)

            @pl.when(i == 0)
            def _():
                dg_ref[...] = dg

            @pl.when(i > 0)
            def _():
                dg_ref[...] += dg

    del n_i
    return pl.pallas_call(
        body,
        name=name,
        grid=(T // tm, nsh),
        in_specs=[
            pl.BlockSpec((tm, ns), lambda i, j: (i, j)),
            pl.BlockSpec((None, None, D, ns), lambda i, j: (j, layer, 0, 0)),
            pl.BlockSpec((tm, D), lambda i, j: (i, 0)),
            pl.BlockSpec((1, D), lambda i, j: (0, 0)),
            pl.BlockSpec((tm, D), lambda i, j: (i, 0)),
        ],
        out_specs=[
            pl.BlockSpec((tm, D), lambda i, j: (i, 0)),
            pl.BlockSpec((tm, D), lambda i, j: (i, 0)),
            pl.BlockSpec((1, D), lambda i, j: (0, 0)),
        ],
        out_shape=[
            jax.ShapeDtypeStruct((T, D), F32),
            jax.ShapeDtypeStruct((T, D), BF16),
            jax.ShapeDtypeStruct((1, D), F32),
        ],
        scratch_shapes=[pltpu.VMEM((tm, D), F32)],
        compiler_params=_params(2),
    )(da, wg, x, gain.reshape(1, D), dres)


def _matmul_tn(lhs, rhs, bufs, layer, n_layers, shard_lhs, act, name):
    T = lhs.shape[0]
    rows = lhs.shape[1] // N_CHIPS if shard_lhs else lhs.shape[1]
    cols = rhs.shape[1] if shard_lhs else rhs.shape[1] // N_CHIPS
    tt = min(T, ROW_TILE)
    n_t = T // tt

    def body(l_ref, r_ref, *rest):
        o32_ref, o16_ref = rest[-2:]
        t = pl.program_id(1)
        upd = lax.dot_general(_act(l_ref[...], act), r_ref[...].astype(BF16), TN, preferred_element_type=F32)

        @pl.when(t == 0)
        def _():
            o32_ref[...] = upd

        @pl.when(t > 0)
        def _():
            o32_ref[...] += upd

        @pl.when(t == n_t - 1)
        def _():
            o16_ref[...] = o32_ref[...].astype(BF16)

    if shard_lhs:
        in_specs = [pl.BlockSpec((tt, rows), lambda s, t: (t, s)), pl.BlockSpec((tt, cols), lambda s, t: (t, 0))]
    else:
        in_specs = [pl.BlockSpec((tt, rows), lambda s, t: (t, 0)), pl.BlockSpec((tt, cols), lambda s, t: (t, s))]
    args = [lhs, rhs]
    aliases = {}
    if bufs is not None:
        in_specs += [pl.BlockSpec(memory_space=pl.ANY)] * 2
        args += list(bufs)
        aliases = {2: 0, 3: 1}
    shape = (N_CHIPS, n_layers, rows, cols)
    return pl.pallas_call(
        body,
        name=name,
        grid=(N_CHIPS, n_t),
        in_specs=in_specs,
        out_specs=[pl.BlockSpec((None, None, rows, cols), lambda s, t: (s, layer, 0, 0))] * 2,
        out_shape=[jax.ShapeDtypeStruct(shape, F32), jax.ShapeDtypeStruct(shape, BF16)],
        input_output_aliases=aliases,
        compiler_params=_params(2),
    )(*args)


def _chip_peers(x, y):
    return [(1 - x, y), (x, 1 - y), (1 - x, 1 - y)]


def _remote(src, dst, sems, s, peer):
    return pltpu.make_async_remote_copy(
        src_ref=src, dst_ref=dst, send_sem=sems[0].at[s], recv_sem=sems[1].at[s], device_id=peer, device_id_type=MESH
    )


class _Exchange:
    def __init__(self, operands, n_alias, new_shapes, n_sems, build):
        self.operands, self.n_alias, self.new_shapes, self.n_sems, self.build = operands, n_alias, new_shapes, n_sems, build

    def out_shapes(self):
        return [jax.ShapeDtypeStruct(a.shape, a.dtype) for a in self.operands[: self.n_alias]] + list(self.new_shapes)

    def scratch(self):
        return [pltpu.SemaphoreType.DMA((n,)) for n in self.n_sems]

    def run(self, ins, outs, sems, first, last):
        starts, recvs, sends, locals_ = self.build(ins, outs, sems)

        def start_all():
            for cp in starts:
                cp.start()

        def wait_all():
            for cp in recvs:
                cp.wait_recv()
            for cp in sends:
                cp.wait_send()
            for cp in locals_:
                cp.wait()

        if first is True:
            start_all()
            return wait_all
        pl.when(first)(start_all)
        return lambda: pl.when(last)(wait_all)


def _gather_exchange(shards, bufs, plan):
    n_arr = len(shards)
    n_cp = len(plan) * (N_CHIPS - 1)

    def build(ins, outs, sems):
        shard_refs = ins[-n_arr:]
        x, y, c = lax.axis_index("x"), lax.axis_index("y"), lax.axis_index("c")
        me = 2 * x + y
        starts, recvs, sends, locals_ = [], [], [], []
        for p, (a, l0, n) in enumerate(plan):
            src = shard_refs[a].at[pl.ds(l0, n)]
            cp = pltpu.make_async_copy(src, outs[a].at[me, pl.ds(l0, n)], sems[2].at[p])
            locals_.append(cp)
            for k, (px, py) in enumerate(_chip_peers(x, y)):
                s = p * (N_CHIPS - 1) + k
                sends.append(_remote(src, outs[a].at[me, pl.ds(l0, n)], sems, s, (px, py, c)))
                recvs.append(_remote(src, outs[a].at[2 * px + py, pl.ds(l0, n)], sems, s, (px, py, c)))
        return locals_ + sends, recvs, sends, locals_

    if bufs is None:
        new = [jax.ShapeDtypeStruct((N_CHIPS,) + s.shape, s.dtype) for s in shards]
        return _Exchange(list(shards), 0, new, [n_cp, n_cp, len(plan)], build)
    return _Exchange(list(bufs) + list(shards), n_arr, [], [n_cp, n_cp, len(plan)], build)


def _scatter_exchange(g16, recv, plan, small=None):
    n_arr = len(g16)
    have = [r for r in recv if r is not None]
    made = [a for a in range(n_arr) if recv[a] is None]
    n_cp = len(plan) * (N_CHIPS - 1) + (N_DEV - 1 if small is not None else 0)

    def build(ins, outs, sems):
        g_refs = ins[:n_arr]
        recv_refs, it_have, it_made = [], iter(outs[n_arr : n_arr + len(have)]), iter(outs[n_arr + len(have) :])
        for a in range(n_arr):
            recv_refs.append(next(it_made) if recv[a] is None else next(it_have))
        x, y, c = lax.axis_index("x"), lax.axis_index("y"), lax.axis_index("c")
        me = 2 * x + y
        starts, recvs, sends, locals_ = [], [], [], []
        for p, (a, l0, n) in enumerate(plan):
            for k, (px, py) in enumerate(_chip_peers(x, y)):
                s = p * (N_CHIPS - 1) + k
                dst = recv_refs[a].at[k, pl.ds(l0, n)]
                sends.append(_remote(g_refs[a].at[2 * px + py, pl.ds(l0, n)], dst, sems, s, (px, py, c)))
                recvs.append(_remote(g_refs[a].at[me, pl.ds(l0, n)], dst, sems, s, (px, py, c)))
        if small is not None:
            small_ref, all_ref = ins[-1], outs[-1]
            slot = 4 * x + 2 * y + c
            locals_.append(pltpu.make_async_copy(small_ref, all_ref.at[slot], sems[2].at[0]))
            flips = [(fx, fy, fc) for fx in (0, 1) for fy in (0, 1) for fc in (0, 1)][1:]
            for k, (fx, fy, fc) in enumerate(flips):
                s = len(plan) * (N_CHIPS - 1) + k
                px, py, pc = x ^ fx, y ^ fy, c ^ fc
                sends.append(_remote(small_ref, all_ref.at[slot], sems, s, (px, py, pc)))
                recvs.append(_remote(small_ref, all_ref.at[4 * px + 2 * py + pc], sems, s, (px, py, pc)))
        return locals_ + sends, recvs, sends, locals_

    operands = list(g16) + have + ([small] if small is not None else [])
    new = [jax.ShapeDtypeStruct((N_CHIPS - 1,) + g16[a].shape[1:], BF16) for a in made]
    if small is not None:
        new.append(jax.ShapeDtypeStruct((N_DEV,) + small.shape, F32))
    return _Exchange(operands, n_arr + len(have), new, [n_cp, n_cp, 1], build)


def _call_with_exchange(body, exch, name, grid, in_specs, out_specs, out_shape, scratch_shapes, args):
    n_in, n_out, n_scr = len(in_specs), len(out_shape), len(scratch_shapes)
    if exch is None:
        outs = pl.pallas_call(
            body, name=name, grid=grid, in_specs=in_specs, out_specs=out_specs, out_shape=out_shape,
            scratch_shapes=scratch_shapes, compiler_params=_params(len(grid)),
        )(*args)
        return outs, []
    e_shapes = exch.out_shapes()
    e_in, e_out = len(exch.operands), len(e_shapes)

    def wrapped(*refs):
        ins, refs = refs[:n_in], refs[n_in:]
        e_ins, refs = refs[:e_in], refs[e_in:]
        outs, refs = refs[:n_out], refs[n_out:]
        e_outs, refs = refs[:e_out], refs[e_out:]
        scr, sems = refs[:n_scr], refs[n_scr:]
        first, last = True, True
        for d, g in enumerate(grid):
            first = (pl.program_id(d) == 0) & first
            last = (pl.program_id(d) == g - 1) & last
        finish = exch.run(e_ins, e_outs, sems, first, last)
        body(*ins, *outs, *scr)
        finish()

    any_spec = pl.BlockSpec(memory_space=pl.ANY)
    outs = pl.pallas_call(
        wrapped,
        name=name,
        grid=grid,
        in_specs=list(in_specs) + [any_spec] * e_in,
        out_specs=list(out_specs) + [any_spec] * e_out,
        out_shape=list(out_shape) + e_shapes,
        input_output_aliases={n_in + i: n_out + i for i in range(exch.n_alias)},
        scratch_shapes=list(scratch_shapes) + exch.scratch(),
        compiler_params=pltpu.CompilerParams(
            dimension_semantics=("arbitrary",) * len(grid), vmem_limit_bytes=VMEM_LIMIT, has_side_effects=True
        ),
    )(*args, *exch.operands)
    return outs[:n_out], outs[n_out:]


def _exchange_only(exch, name):
    n_in = len(exch.operands)
    shapes = exch.out_shapes()

    def body(*refs):
        ins, outs, sems = refs[:n_in], refs[n_in : n_in + len(shapes)], refs[n_in + len(shapes) :]
        exch.run(ins, outs, sems, True, True)()

    any_spec = pl.BlockSpec(memory_space=pl.ANY)
    return pl.pallas_call(
        body,
        name=name,
        in_specs=[any_spec] * n_in,
        out_specs=[any_spec] * len(shapes),
        out_shape=shapes,
        input_output_aliases={i: i for i in range(exch.n_alias)},
        scratch_shapes=exch.scratch(),
        compiler_params=pltpu.CompilerParams(has_side_effects=True),
    )(*exch.operands)


ATTN_LANE_TILES = 2
ATTN_UNROLL = 8


MASKED = -1e30


def _hi_lo(x):
    hi = x.astype(BF16)
    lo = (x - hi.astype(F32)).astype(BF16)
    return jnp.concatenate([hi, lo], axis=1)


def _suffix_matrix(inclusive):
    j = lax.broadcasted_iota(jnp.int32, (2 * Q_TILE, 2 * Q_TILE), 0) & (Q_TILE - 1)
    s = lax.broadcasted_iota(jnp.int32, (2 * Q_TILE, 2 * Q_TILE), 1)
    later = (j >= s) if inclusive else (j > s)
    return jnp.where((s >= Q_TILE) | later, 1.0, 0.0).astype(BF16)


def _log_beta(z):
    return jnp.minimum(z, 0.0) - jnp.log(1.0 + jnp.exp(-jnp.abs(z)))


def _head_masks(width):
    lane = lax.broadcasted_iota(jnp.int32, (1, width), 1)
    return [(lane >= h * HEAD_DIM) & (lane < (h + 1) * HEAD_DIM) for h in range(width // HEAD_DIM)]


def _per_head_rows(x, masks):
    return jnp.concatenate([jnp.where(hm, x, 0) for hm in masks], axis=0)


def _heads_to_lanes(x, n_heads):
    return jnp.concatenate([x[h * Q_TILE : (h + 1) * Q_TILE] for h in range(n_heads)], axis=1)


def _block_start(kb):
    return kb * Q_TILE if isinstance(kb, int) else pl.multiple_of(kb * Q_TILE, Q_TILE)


def _clamp(i, n):
    return jnp.minimum(i, n - 1)


def _next_block(pos):
    qi, kb = pos
    row_done = kb == 0
    nqi = jnp.where(row_done, qi + 1, qi)
    return nqi, jnp.where(row_done, nqi, kb - 1)


def _stream_unroll(n_blocks):
    return next(u for u in (ATTN_UNROLL, 2, 1) if n_blocks % u == 0)


def _past_mask(rows):
    t = lax.broadcasted_iota(jnp.int32, (rows, Q_TILE), 0) & (Q_TILE - 1)
    s = lax.broadcasted_iota(jnp.int32, (rows, Q_TILE), 1)
    return s < t


def _attn_fwd(qkv, n_seq, S, D, exch, name):
    T = n_seq * S
    width = min(D, ATTN_LANE_TILES * LANES)
    n_heads = width // HEAD_DIM
    rows = n_heads * Q_TILE
    nq = S // Q_TILE
    groups = D // width
    n_blocks = nq * (nq + 1) // 2
    unroll = _stream_unroll(n_blocks)
    scale = HEAD_DIM ** -0.5

    def body(q_ref, k_ref, v_ref, o_ref, qh_scr, vh_scr, bias_scr):
        masks = _head_masks(width)
        sfx = _suffix_matrix(False)

        def per_head_tables(i, c):
            blk = pl.ds(_block_start(i), Q_TILE)
            qh_scr[i] = _per_head_rows(q_ref[blk, :] * scale, masks)
            vh_scr[i] = _per_head_rows(v_ref[blk, :], masks)
            return c

        lax.fori_loop(0, nq, per_head_tables, 0)
        bias_scr[0] = jnp.zeros((rows, Q_TILE), F32)
        bias_scr[1] = jnp.where(_past_mask(rows), 0.0, MASKED)

        def scores(pos):
            qi, kb = pos
            kt = k_ref[pl.ds(_block_start(_clamp(kb, nq)), Q_TILE), :]
            z = lax.dot_general(qh_scr[_clamp(qi, nq)], kt, NT, preferred_element_type=F32)
            z = z + bias_scr[(kb == qi).astype(jnp.int32)]
            lb = _log_beta(z)
            return lb, _hi_lo(lb - z)

        def weigh(pos, st, carry, acc):
            qi, kb = pos
            lb, l1 = st
            r = jnp.dot(l1, sfx, preferred_element_type=F32)
            carry = jnp.where(kb == qi, 0.0, carry)
            a = _heads_to_lanes(jnp.exp(lb + r[:, :Q_TILE] + carry).astype(BF16), n_heads)
            acc = jnp.where(kb == qi, 0.0, acc) + jnp.dot(a, vh_scr[_clamp(kb, nq)], preferred_element_type=F32)
            o_ref[pl.ds(_block_start(_clamp(qi, nq)), Q_TILE), :] = acc
            return carry + r[:, Q_TILE:], acc

        def trip(n, c):
            pos, st, carry, acc = c
            for _ in range(unroll):
                nxt = _next_block(pos)
                st_nxt = scores(nxt)
                carry, acc = weigh(pos, st, carry, acc)
                pos, st = nxt, st_nxt
            return pos, st, carry, acc

        first = (jnp.int32(0), jnp.int32(0))
        zero = bias_scr[0]
        init = (first, scores(first), zero, jnp.concatenate([zero[:Q_TILE]] * (width // Q_TILE), axis=1))
        lax.fori_loop(0, n_blocks // unroll, trip, init)

    seq = lambda col0: pl.BlockSpec((S, width), lambda b, p: (b, col0 + p))
    (o,), moved = _call_with_exchange(
        body,
        exch,
        name,
        grid=(n_seq, groups),
        in_specs=[seq(0), seq(groups), seq(2 * groups)],
        out_specs=[seq(0)],
        out_shape=[jax.ShapeDtypeStruct((T, D), F32)],
        scratch_shapes=[
            pltpu.VMEM((nq, rows, width), BF16),
            pltpu.VMEM((nq, rows, width), BF16),
            pltpu.VMEM((2, rows, Q_TILE), F32),
        ],
        args=(qkv, qkv, qkv),
    )
    return o, moved


def _attn_bwd(qkv, o, do, n_seq, S, D, exch, name):
    T = n_seq * S
    width = min(D, ATTN_LANE_TILES * LANES)
    n_heads = width // HEAD_DIM
    rows = n_heads * Q_TILE
    nq = S // Q_TILE
    groups = D // width
    n_blocks = nq * (nq + 1) // 2
    unroll = _stream_unroll(n_blocks)
    scale = HEAD_DIM ** -0.5

    def body(q_ref, k_ref, v_ref, o_ref, do_ref, dq_ref, dk_ref, dv_ref, dk_acc, dv_acc, qh_scr, doh_scr, delta_scr, bias_scr):
        masks = _head_masks(width)
        sfx = _suffix_matrix(False)
        sfx_incl = _suffix_matrix(True)
        dk_acc[...] = jnp.zeros_like(dk_acc)
        dv_acc[...] = jnp.zeros_like(dv_acc)

        def per_head_tables(i, c):
            blk = pl.ds(_block_start(i), Q_TILE)
            do = do_ref[blk, :]
            qh_scr[i] = _per_head_rows(q_ref[blk, :] * scale, masks)
            doh_scr[i] = _per_head_rows(do, masks)
            prod = do.astype(F32) * o_ref[blk, :]
            delta = jnp.concatenate(
                [jnp.sum(jnp.where(hm, prod, 0.0), axis=-1, keepdims=True) for hm in masks], axis=0
            )
            delta_scr[i] = jnp.broadcast_to(delta, (rows, Q_TILE))
            return c

        lax.fori_loop(0, nq, per_head_tables, 0)
        bias_scr[0] = jnp.zeros((rows, Q_TILE), F32)
        bias_scr[1] = jnp.where(_past_mask(rows), 0.0, MASKED)

        def scores(pos):
            qi, kb = pos
            blk = pl.ds(_block_start(_clamp(kb, nq)), Q_TILE)
            z = lax.dot_general(qh_scr[_clamp(qi, nq)], k_ref[blk, :], NT, preferred_element_type=F32)
            z = z + bias_scr[(kb == qi).astype(jnp.int32)]
            lb = _log_beta(z)
            da = lax.dot_general(doh_scr[_clamp(qi, nq)], v_ref[blk, :], NT, preferred_element_type=F32)
            return lb, _hi_lo(lb - z), da

        def weigh(pos, st, c1, c2, dq):
            qi, kb = pos
            lb, l1, da = st
            first = kb == qi
            blk = pl.ds(_block_start(kb), Q_TILE)
            r = jnp.dot(l1, sfx, preferred_element_type=F32)
            c1 = jnp.where(first, 0.0, c1)
            ab = jnp.exp(lb + r[:, :Q_TILE] + c1).astype(BF16)
            g = ab.astype(F32) * da
            r2 = jnp.dot(_hi_lo(g), sfx_incl, preferred_element_type=F32)
            c2 = jnp.where(first, 0.0, c2)
            earlier = delta_scr[qi] - (r2[:, :Q_TILE] + c2)
            beta = jnp.exp(lb)
            dzb = (g * (1.0 - beta) - earlier * beta).astype(BF16)
            kh = _per_head_rows(k_ref[blk, :], masks)
            dq = jnp.where(first, 0.0, dq) + jnp.dot(_heads_to_lanes(dzb, n_heads), kh, preferred_element_type=F32)
            dq_ref[pl.ds(_block_start(qi), Q_TILE), :] = (dq * scale).astype(BF16)
            dk_acc[blk, :] += lax.dot_general(dzb, qh_scr[qi], TN, preferred_element_type=F32)
            dv_acc[blk, :] += lax.dot_general(ab, doh_scr[qi], TN, preferred_element_type=F32)
            return c1 + r[:, Q_TILE:], c2 + r2[:, Q_TILE:], dq

        def trip(n, c):
            pos, st, c1, c2, dq = c
            for _ in range(unroll):
                nxt = _next_block(pos)
                st_nxt = scores(nxt)
                c1, c2, dq = weigh(pos, st, c1, c2, dq)
                pos, st = nxt, st_nxt
            return pos, st, c1, c2, dq

        first = (jnp.int32(0), jnp.int32(0))
        zero = bias_scr[0]
        init = (first, scores(first), zero, zero, jnp.concatenate([zero[:Q_TILE]] * (width // Q_TILE), axis=1))
        lax.fori_loop(0, n_blocks // unroll, trip, init)
        dk_ref[...] = dk_acc[...].astype(BF16)
        dv_ref[...] = dv_acc[...].astype(BF16)

    seq = lambda col0: pl.BlockSpec((S, width), lambda b, p: (b, col0 + p))
    return _call_with_exchange(
        body,
        exch,
        name,
        grid=(n_seq, groups),
        in_specs=[seq(0), seq(groups), seq(2 * groups), seq(0), seq(0)],
        out_specs=[seq(0)] * 3,
        out_shape=[jax.ShapeDtypeStruct((T, D), BF16)] * 3,
        scratch_shapes=[
            pltpu.VMEM((S, width), F32),
            pltpu.VMEM((S, width), F32),
            pltpu.VMEM((nq, rows, width), BF16),
            pltpu.VMEM((nq, rows, width), BF16),
            pltpu.VMEM((nq, rows, Q_TILE), F32),
            pltpu.VMEM((2, rows, Q_TILE), F32),
        ],
        args=(qkv, qkv, qkv, o, do),
    )


def _causal_ws(ws_ref, g):
    t = lax.broadcasted_iota(jnp.int32, (SGU_CHUNK, SGU_CHUNK), 0)
    s = lax.broadcasted_iota(jnp.int32, (SGU_CHUNK, SGU_CHUNK), 1)
    return jnp.where(s <= t, ws_ref[g], 0.0)


def _sgu_fwd(a, gain, ws, bsb, name):
    T, F2 = a.shape
    F = F2 // 2
    gw = F // SGU_GROUPS

    def body(a_ref, gain_ref, ws_ref, bsb_ref, y_ref):
        v = _gelu(a_ref[:, F:].astype(F32))
        vn = (v * _rstd(v) * gain_ref[...]).astype(BF16)
        for g in range(SGU_GROUPS):
            cs = slice(g * gw, (g + 1) * gw)
            w = _causal_ws(ws_ref, g).astype(BF16)
            mixed = jnp.dot(w, vn[:, cs], preferred_element_type=F32) + bsb_ref[g]
            y_ref[:, cs] = (_gelu(a_ref[:, cs].astype(F32)) * mixed).astype(BF16)

    return pl.pallas_call(
        body,
        name=name,
        grid=(T // SGU_CHUNK,),
        in_specs=[
            pl.BlockSpec((SGU_CHUNK, F2), lambda i: (i, 0)),
            pl.BlockSpec((1, F), lambda i: (0, 0)),
            pl.BlockSpec((SGU_GROUPS, SGU_CHUNK, SGU_CHUNK), lambda i: (0, 0, 0)),
            pl.BlockSpec((SGU_GROUPS, SGU_CHUNK, gw), lambda i: (0, 0, 0)),
        ],
        out_specs=pl.BlockSpec((SGU_CHUNK, F), lambda i: (i, 0)),
        out_shape=jax.ShapeDtypeStruct((T, F), BF16),
        compiler_params=_params(1),
    )(a, gain.reshape(1, F), ws, bsb)


def _sgu_bwd(a, dy, gain, ws, bsb, name):
    T, F2 = a.shape
    F = F2 // 2
    gw = F // SGU_GROUPS

    def body(a_ref, dy_ref, gain_ref, ws_ref, bsb_ref, da_ref, dws_ref, dbs_ref, dgain_ref, dvn_ref):
        @pl.when(pl.program_id(0) == 0)
        def _():
            dws_ref[...] = jnp.zeros_like(dws_ref)
            dbs_ref[...] = jnp.zeros_like(dbs_ref)
            dgain_ref[...] = jnp.zeros_like(dgain_ref)

        av = a_ref[:, F:].astype(F32)
        v = _gelu(av)
        rstd = _rstd(v)
        vh = v * rstd
        gain = gain_ref[...]
        vn = (vh * gain).astype(BF16)
        ones = jnp.ones((gw, SGU_CHUNK), BF16)
        for g in range(SGU_GROUPS):
            cs = slice(g * gw, (g + 1) * gw)
            w = _causal_ws(ws_ref, g).astype(BF16)
            mixed = jnp.dot(w, vn[:, cs], preferred_element_type=F32) + bsb_ref[g]
            au = a_ref[:, cs].astype(F32)
            dyc = dy_ref[:, cs].astype(F32)
            da_ref[:, cs] = (dyc * mixed * _gelu_grad(au)).astype(BF16)
            dm = (dyc * _gelu(au)).astype(BF16)
            dbs_ref[g] += jnp.dot(dm, ones, preferred_element_type=F32)
            dws_ref[g] += _causal_mask_f32(lax.dot_general(dm, vn[:, cs], NT, preferred_element_type=F32))
            dvn_ref[:, cs] = lax.dot_general(w, dm, TN, preferred_element_type=F32)
        dvn = dvn_ref[...]
        dgain_ref[...] += jnp.sum(dvn * vh, axis=0, keepdims=True)
        dvh = dvn * gain
        dv = rstd * (dvh - vh * jnp.mean(dvh * vh, axis=-1, keepdims=True))
        da_ref[:, F:] = (dv * _gelu_grad(av)).astype(BF16)

    acc_spec = pl.BlockSpec((SGU_GROUPS, SGU_CHUNK, SGU_CHUNK), lambda i: (0, 0, 0))
    acc_shape = jax.ShapeDtypeStruct((SGU_GROUPS, SGU_CHUNK, SGU_CHUNK), F32)
    return pl.pallas_call(
        body,
        name=name,
        grid=(T // SGU_CHUNK,),
        in_specs=[
            pl.BlockSpec((SGU_CHUNK, F2), lambda i: (i, 0)),
            pl.BlockSpec((SGU_CHUNK, F), lambda i: (i, 0)),
            pl.BlockSpec((1, F), lambda i: (0, 0)),
            acc_spec,
            pl.BlockSpec((SGU_GROUPS, SGU_CHUNK, gw), lambda i: (0, 0, 0)),
        ],
        out_specs=[
            pl.BlockSpec((SGU_CHUNK, F2), lambda i: (i, 0)),
            acc_spec,
            acc_spec,
            pl.BlockSpec((1, F), lambda i: (0, 0)),
        ],
        out_shape=[
            jax.ShapeDtypeStruct((T, F2), BF16),
            acc_shape,
            acc_shape,
            jax.ShapeDtypeStruct((1, F), F32),
        ],
        scratch_shapes=[pltpu.VMEM((SGU_CHUNK, F), F32)],
        compiler_params=_params(1),
    )(a, dy, gain.reshape(1, F), ws, bsb)


def _causal_mask_f32(m):
    t = lax.broadcasted_iota(jnp.int32, m.shape, 0)
    s = lax.broadcasted_iota(jnp.int32, m.shape, 1)
    return jnp.where(s <= t, m, 0.0)


def _final_loss(x, gain, target, name):
    T, D = x.shape
    tm = min(T, ROW_TILE // 2)

    def body(x_ref, g_ref, t_ref, sq_ref, dx_ref, dxb_ref, dg_ref):
        xv = x_ref[...]
        gain = g_ref[...]
        err = xv * _rstd(xv) * gain - t_ref[...]
        dx, dg = _norm_bwd(err * (1.0 / D), xv, gain)
        dx_ref[...] = dx
        dxb_ref[...] = dx.astype(BF16)
        sq = jnp.sum(err * err, axis=0, keepdims=True)

        @pl.when(pl.program_id(0) == 0)
        def _():
            sq_ref[...] = sq
            dg_ref[...] = dg

        @pl.when(pl.program_id(0) > 0)
        def _():
            sq_ref[...] += sq
            dg_ref[...] += dg

    row = pl.BlockSpec((tm, D), lambda i: (i, 0))
    vec = pl.BlockSpec((1, D), lambda i: (0, 0))
    return pl.pallas_call(
        body,
        name=name,
        grid=(T // tm,),
        in_specs=[row, vec, row],
        out_specs=[vec, row, row, vec],
        out_shape=[
            jax.ShapeDtypeStruct((1, D), F32),
            jax.ShapeDtypeStruct((T, D), F32),
            jax.ShapeDtypeStruct((T, D), BF16),
            jax.ShapeDtypeStruct((1, D), F32),
        ],
        compiler_params=_params(1),
    )(x, gain.reshape(1, D), target)


def _row_tile(rows, cols, n_arrays):
    budget = VMEM_LIMIT // 2 // (2 * n_arrays * cols * 4)
    tr = rows
    while tr > budget and tr % 16 == 0:
        tr //= 2
    return tr


def _sum_received(own, recv, name):
    R, C = own.shape
    n = recv.shape[0]
    tr = _row_tile(R, C, n + 2)

    def body(own_ref, recv_ref, o_ref):
        s = own_ref[...]
        for k in range(n):
            s = s + recv_ref[k].astype(F32)
        o_ref[...] = s

    return pl.pallas_call(
        body,
        name=name,
        grid=(R // tr,),
        in_specs=[pl.BlockSpec((tr, C), lambda i: (i, 0)), pl.BlockSpec((n, tr, C), lambda i: (0, i, 0))],
        out_specs=pl.BlockSpec((tr, C), lambda i: (i, 0)),
        out_shape=jax.ShapeDtypeStruct((R, C), F32),
        compiler_params=_params(1),
    )(own, recv)


def _sum_chip_shard(g32, recv, chip, name):
    _, L, r, c = g32.shape
    n = recv.shape[0]
    tr = _row_tile(r, c, n + 2)

    def body(chip_ref, own_ref, recv_ref, o_ref):
        s = own_ref[...]
        for k in range(n):
            s = s + recv_ref[k].astype(F32)
        o_ref[...] = s

    return pl.pallas_call(
        body,
        name=name,
        grid_spec=pltpu.PrefetchScalarGridSpec(
            num_scalar_prefetch=1,
            grid=(L, r // tr),
            in_specs=[
                pl.BlockSpec((None, None, tr, c), lambda l, i, chip_ref: (chip_ref[0], l, i, 0)),
                pl.BlockSpec((n, None, tr, c), lambda l, i, chip_ref: (0, l, i, 0)),
            ],
            out_specs=pl.BlockSpec((None, tr, c), lambda l, i, chip_ref: (l, i, 0)),
        ),
        out_shape=jax.ShapeDtypeStruct((L, r, c), F32),
        compiler_params=_params(2),
    )(chip.reshape(1).astype(jnp.int32), g32, recv)


def _adamw(w, m, v, parts, name):
    R, C = w.shape
    n = len(parts)
    tr = _row_tile(R, C, n + 7)

    def body(*refs):
        w_ref, m_ref, v_ref = refs[:3]
        g_ref, d_ref, nm_ref, nv_ref = refs[3 + n :]
        g = refs[3][...]
        for p_ref in refs[4 : 3 + n]:
            g = g + p_ref[...]
        nm = ADAM_B1 * m_ref[...] + (1.0 - ADAM_B1) * g
        nv = ADAM_B2 * v_ref[...] + (1.0 - ADAM_B2) * (g * g)
        m_hat = nm / (1.0 - ADAM_B1**ADAM_STEP)
        v_hat = nv / (1.0 - ADAM_B2**ADAM_STEP)
        g_ref[...] = g
        d_ref[...] = -ADAM_LR * (m_hat / (jnp.sqrt(v_hat) + ADAM_EPS) + ADAM_WD * w_ref[...])
        nm_ref[...] = nm
        nv_ref[...] = nv

    spec = pl.BlockSpec((tr, C), lambda i: (i, 0))
    return pl.pallas_call(
        body,
        name=name,
        grid=(R // tr,),
        in_specs=[spec] * (3 + n),
        out_specs=[spec] * 4,
        out_shape=[jax.ShapeDtypeStruct((R, C), F32)] * 4,
        compiler_params=_params(1),
    )(w, m, v, *parts)


def _swap_with_sibling(parts, name):
    n = len(parts)

    def body(*refs):
        ins, outs = refs[:n], refs[n : 2 * n]
        send_sems, recv_sems = refs[2 * n :]
        sibling = (lax.axis_index("x"), lax.axis_index("y"), 1 - lax.axis_index("c"))
        copies = [
            pltpu.make_async_remote_copy(
                src_ref=ins[a],
                dst_ref=outs[a],
                send_sem=send_sems.at[a],
                recv_sem=recv_sems.at[a],
                device_id=sibling,
                device_id_type=MESH,
            )
            for a in range(n)
        ]
        for cp in copies:
            cp.start()
        for cp in copies:
            cp.wait_recv()
        for cp in copies:
            cp.wait_send()

    any_spec = pl.BlockSpec(memory_space=pl.ANY)
    return pl.pallas_call(
        body,
        name=name,
        in_specs=[any_spec] * n,
        out_specs=[any_spec] * n,
        out_shape=[jax.ShapeDtypeStruct(p.shape, p.dtype) for p in parts],
        scratch_shapes=[pltpu.SemaphoreType.DMA((n,)), pltpu.SemaphoreType.DMA((n,))],
        compiler_params=pltpu.CompilerParams(has_side_effects=True),
    )(*parts)


def _pack(pieces):
    flat = jnp.concatenate([p.reshape(-1) for p in pieces])
    return flat.reshape(-1, LANES)


def _unpack(packed, shapes):
    flat = packed.reshape(-1)
    out, off = [], 0
    for s in shapes:
        size = 1
        for d in s:
            size *= d
        out.append(flat[off : off + size].reshape(s))
        off += size
    return out


def kernel(x, norm_mix, norm_mlp, sb_wqkv, sb_wo, sgu_win, sgu_gain, sgu_ws, sgu_bs, sgu_wout, mlp_w1, mlp_w2, final_norm, loss_target, m_norm_mix, m_norm_mlp, m_sb_wqkv, m_sb_wo, m_sgu_win, m_sgu_gain, m_sgu_ws, m_sgu_bs, m_sgu_wout, m_mlp_w1, m_mlp_w2, m_final_norm, v_norm_mix, v_norm_mlp, v_sb_wqkv, v_sb_wo, v_sgu_win, v_sgu_gain, v_sgu_ws, v_sgu_bs, v_sgu_wout, v_mlp_w1, v_mlp_w2, v_final_norm):
    n_seq, S, D = x.shape
    T = n_seq * S
    depth = norm_mix.shape[0]
    n_sgu = sgu_win.shape[0]
    F = sgu_wout.shape[1] * N_CHIPS
    gw = F // SGU_GROUPS
    chip = 2 * lax.axis_index("x") + lax.axis_index("y")

    QKV, WO, WIN, WOUT, W1, W2, GAIN = range(7)
    big = [sb_wqkv, sb_wo, sgu_win, sgu_wout, mlp_w1, mlp_w2]
    n_sb = sb_wqkv.shape[0]
    shards = [w.astype(BF16) for w in big] + [sgu_gain.reshape(1, -1, LANES)]
    first_plan = [(QKV, 0, 1), (WO, 0, 1), (W1, 0, 1), (W2, 0, 1), (GAIN, 0, 1)]
    rest_plan = [(QKV, 1, n_sb - 1), (WO, 1, n_sb - 1), (WIN, 0, n_sgu), (WOUT, 0, n_sgu), (W1, 1, depth - 1), (W2, 1, depth - 1)]
    rest_plan = [p for p in rest_plan if p[2] > 0]
    wg = _exchange_only(_gather_exchange(shards, None, first_plan), "gather_first_layer")
    gain_full = jnp.transpose(wg[GAIN].reshape(N_CHIPS, n_sgu, F // N_CHIPS), (1, 0, 2)).reshape(n_sgu, F)
    bsb = [jnp.broadcast_to(sgu_bs[j][:, :, None], (SGU_GROUPS, SGU_CHUNK, gw)) for j in range(n_sgu)]

    xs = x.reshape(T, D)
    saved = []
    for i in range(depth):
        j = i // 2
        if i % 2 == 0:
            qkv, h = _norm_matmul(xs, norm_mix[i], wg[QKV], j, f"qkv_fwd_{i}")
            exch = _gather_exchange(shards, wg, rest_plan) if i == 0 and rest_plan else None
            o, moved = _attn_fwd(qkv, n_seq, S, D, exch, f"attn_fwd_{i}")
            if exch is not None:
                wg = moved
            wg_qkv, wg_wo, wg_win, wg_wout, wg_w1, wg_w2 = wg[:6]
            x_mid = _act_matmul_res(o, wg_wo, j, xs, None, f"wo_fwd_{i}")
            mix = (qkv, o)
        else:
            a, h = _norm_matmul(xs, norm_mix[i], wg_win, j, f"win_fwd_{i}")
            yg = _sgu_fwd(a, gain_full[j], sgu_ws[j], bsb[j], f"sgu_fwd_{i}")
            x_mid = _act_matmul_res(yg, wg_wout, j, xs, None, f"wout_fwd_{i}")
            mix = (a, yg)
        a2, h2 = _norm_matmul(x_mid, norm_mlp[i], wg_w1, i, f"w1_fwd_{i}")
        x_out = _act_matmul_res(a2, wg_w2, i, x_mid, "relu2", f"w2_fwd_{i}")
        saved.append((xs, h, mix, x_mid, h2, a2))
        xs = x_out

    sq, dx, dxb, g_final = _final_loss(xs, final_norm, loss_target.reshape(T, D), "loss_head")
    loss = lax.psum(0.5 * jnp.sum(sq) / D, ("x", "y", "c"))

    n_layers = [n_sb, n_sb, n_sgu, n_sgu, depth, depth]
    g32, g16, recv = [None] * 6, [None] * 6, [None] * 6
    done_from, sent_from = list(n_layers), list(n_layers)

    def grad(a, layer, lhs, rhs, shard_lhs, act, name):
        bufs = None if g32[a] is None else (g32[a], g16[a])
        g32[a], g16[a] = _matmul_tn(lhs, rhs, bufs, layer, n_layers[a], shard_lhs, act, name)
        done_from[a] = layer

    def unsent_plan():
        plan = [(a, done_from[a], sent_from[a] - done_from[a]) for a in range(6) if sent_from[a] > done_from[a]]
        for a, l0, _ in plan:
            sent_from[a] = l0
        return plan

    def scatter(plan, small):
        arrays = sorted({a for a, _, _ in plan})
        have = [a for a in arrays if recv[a] is not None]
        made = [a for a in arrays if recv[a] is None]
        exch = _scatter_exchange(
            [g16[a] for a in arrays], [recv[a] for a in arrays], [(arrays.index(a), l0, n) for a, l0, n in plan], small
        )

        def take(moved):
            for a, buf in zip(arrays, moved):
                g16[a] = buf
            for a, buf in zip(have + made, moved[len(arrays) :]):
                recv[a] = buf
            return moved[-1]

        return exch, take

    g_mix, g_mlp = [None] * depth, [None] * depth
    g_ws, g_bs, g_gain = [None] * n_sgu, [None] * n_sgu, [None] * n_sgu
    for i in reversed(range(depth)):
        j = i // 2
        x_in, h, mix, x_mid, h2, a2 = saved[i]
        da2 = _matmul_nt(dxb, wg_w2, i, a2, f"w2_bwd_{i}")
        grad(W2, i, a2, dxb, True, "relu2", f"w2_grad_{i}")
        grad(W1, i, h2, da2, False, None, f"w1_grad_{i}")
        dx, dxb, g_mlp[i] = _matmul_nt_norm_bwd(da2, wg_w1, i, x_mid, norm_mlp[i], dx, f"w1_bwd_{i}")
        if i % 2 == 0:
            qkv, o = mix
            do = _matmul_nt(dxb, wg_wo, j, None, f"wo_bwd_{i}")
            grad(WO, j, o, dxb, True, None, f"wo_grad_{i}")
            exch, take = scatter(unsent_plan(), None)
            (dq, dk, dv), moved = _attn_bwd(qkv, o, do, n_seq, S, D, exch, f"attn_bwd_{i}")
            take(moved)
            dqkv = jnp.concatenate([dq, dk, dv], axis=1)
            grad(QKV, j, h, dqkv, False, None, f"qkv_grad_{i}")
            dx, dxb, g_mix[i] = _matmul_nt_norm_bwd(dqkv, wg_qkv, j, x_in, norm_mix[i], dx, f"qkv_bwd_{i}")
        else:
            a, yg = mix
            dyg = _matmul_nt(dxb, wg_wout, j, None, f"wout_bwd_{i}")
            grad(WOUT, j, yg, dxb, True, None, f"wout_grad_{i}")
            da, g_ws[j], dbs, g_gain[j] = _sgu_bwd(a, dyg, gain_full[j], sgu_ws[j], bsb[j], f"sgu_bwd_{i}")
            g_bs[j] = dbs[:, :, 0]
            grad(WIN, j, h, da, False, None, f"win_grad_{i}")
            dx, dxb, g_mix[i] = _matmul_nt_norm_bwd(da, wg_win, j, x_in, norm_mix[i], dx, f"win_bwd_{i}")
    grad_x = dx.reshape(n_seq, S, D)

    names = ["qkv", "wo", "win", "wout", "w1", "w2"]
    small_shapes = [norm_mix.shape, norm_mlp.shape, final_norm.shape, sgu_ws.shape, sgu_bs.shape, (n_sgu, F)]
    small = _pack(
        [jnp.stack(g_mix), jnp.stack(g_mlp), g_final, jnp.stack(g_ws), jnp.stack(g_bs), jnp.stack(g_gain)]
    )
    exch, take = scatter(unsent_plan(), small)
    small_all = take(_exchange_only(exch, "scatter_last_grads"))
    partial = [_sum_chip_shard(g32[a], recv[a], chip, f"sum_{names[a]}") for a in range(6)]
    partial = [p.reshape(-1, p.shape[-1]) for p in partial]
    theirs = _swap_with_sibling(partial, "swap_partial_sums")
    small_sum = _sum_received(small_all[0], small_all[1:], "sum_small")

    ms = [m_sb_wqkv, m_sb_wo, m_sgu_win, m_sgu_wout, m_mlp_w1, m_mlp_w2]
    vs = [v_sb_wqkv, v_sb_wo, v_sgu_win, v_sgu_wout, v_mlp_w1, v_mlp_w2]
    res = {}
    keys = ["sb_wqkv", "sb_wo", "sgu_win", "sgu_wout", "mlp_w1", "mlp_w2"]
    for key, k, w, m, v, mine, other in zip(keys, names, big, ms, vs, partial, theirs):
        cols = w.shape[-1]
        outs = _adamw(w.reshape(-1, cols), m.reshape(-1, cols), v.reshape(-1, cols), [mine, other], f"adamw_{k}")
        res[key] = [o.reshape(w.shape) for o in outs]

    g_small = _unpack(small_sum, small_shapes)
    g_small[5] = lax.dynamic_slice_in_dim(g_small[5], chip * (F // N_CHIPS), F // N_CHIPS, axis=1)
    small_keys = ["norm_mix", "norm_mlp", "final_norm", "sgu_ws", "sgu_bs", "sgu_gain"]
    small_w = [norm_mix, norm_mlp, final_norm, sgu_ws, sgu_bs, sgu_gain]
    small_m = [m_norm_mix, m_norm_mlp, m_final_norm, m_sgu_ws, m_sgu_bs, m_sgu_gain]
    small_v = [v_norm_mix, v_norm_mlp, v_final_norm, v_sgu_ws, v_sgu_bs, v_sgu_gain]
    outs = _adamw(_pack(small_w), _pack(small_m), _pack(small_v), [_pack(g_small)], "adamw_small")
    local_shapes = [w.shape for w in small_w]
    for key, parts in zip(small_keys, zip(*[_unpack(o, local_shapes) for o in outs])):
        res[key] = list(parts)

    order = ["norm_mix", "norm_mlp", "sb_wqkv", "sb_wo", "sgu_win", "sgu_gain", "sgu_ws", "sgu_bs", "sgu_wout", "mlp_w1", "mlp_w2", "final_norm"]
    return (loss, grad_x, *[res[k][0] for k in order], *[res[k][1] for k in order], *[res[k][2] for k in order], *[res[k][3] for k in order])
```

```python
import jax
import jax.numpy as jnp
from jax import lax
from jax.experimental import pallas as pl
from jax.experimental.pallas import tpu as pltpu

F32 = jnp.float32
BF16 = jnp.bfloat16
MESH = pl.DeviceIdType.MESH

EPS = 1e-6
HEAD_DIM = 64
LANES = 128
Q_TILE = 128
SGU_CHUNK = 128
SGU_GROUPS = 8
N_CHIPS = 4
N_DEV = 8
ADAM_LR = 0.001
ADAM_B1 = 0.9
ADAM_B2 = 0.999
ADAM_EPS = 1e-08
ADAM_WD = 0.01
ADAM_STEP = 10
GELU_C0 = 0.7978845608028654
GELU_C1 = 0.044715
VMEM_LIMIT = 48 * 1024 * 1024
ROW_TILE = 1024
NT = (((1,), (1,)), ((), ()))
TN = (((0,), (0,)), ((), ()))


def _params(n_axes):
    return pltpu.CompilerParams(dimension_semantics=("arbitrary",) * n_axes, vmem_limit_bytes=VMEM_LIMIT)


def _rstd(x):
    return lax.rsqrt(jnp.mean(x * x, axis=-1, keepdims=True) + EPS)


def _norm_bwd(dh, x, gain):
    rstd = _rstd(x)
    xh = x * rstd
    dhg = dh * gain
    dx = rstd * (dhg - xh * jnp.mean(dhg * xh, axis=-1, keepdims=True))
    return dx, jnp.sum(dh * xh, axis=0, keepdims=True)


def _gelu(x):
    return 0.5 * x * (1.0 + jnp.tanh(GELU_C0 * (x + GELU_C1 * x * x * x)))


def _gelu_grad(x):
    t = jnp.tanh(GELU_C0 * (x + GELU_C1 * x * x * x))
    return 0.5 * (1.0 + t) + 0.5 * x * (1.0 - t * t) * (GELU_C0 * (1.0 + 3.0 * GELU_C1 * x * x))


def _act(a, act):
    if act == "relu2":
        r = jnp.maximum(a.astype(F32), 0.0)
        return (r * r).astype(BF16)
    return a.astype(BF16)


def _layer_spec(wg, layer):
    nsh, _, r, c = wg.shape
    return pl.BlockSpec((nsh, None, r, c), lambda i: (0, layer, 0, 0), pipeline_mode=pl.Buffered(1))


def _norm_matmul(x, gain, wg, layer, name):
    T, D = x.shape
    nsh, _, _, ns = wg.shape
    tm = min(T, ROW_TILE // 2)

    def body(x_ref, g_ref, w_ref, y_ref, h_ref):
        xv = x_ref[...]
        h = (xv * _rstd(xv) * g_ref[...]).astype(BF16)
        h_ref[...] = h
        for j in range(nsh):
            y_ref[:, j * ns : (j + 1) * ns] = jnp.dot(h, w_ref[j], preferred_element_type=F32).astype(BF16)

    return pl.pallas_call(
        body,
        name=name,
        grid=(T // tm,),
        in_specs=[pl.BlockSpec((tm, D), lambda i: (i, 0)), pl.BlockSpec((1, D), lambda i: (0, 0)), _layer_spec(wg, layer)],
        out_specs=[pl.BlockSpec((tm, nsh * ns), lambda i: (i, 0)), pl.BlockSpec((tm, D), lambda i: (i, 0))],
        out_shape=[jax.ShapeDtypeStruct((T, nsh * ns), BF16), jax.ShapeDtypeStruct((T, D), BF16)],
        compiler_params=_params(1),
    )(x, gain.reshape(1, D), wg)


def _act_matmul_res(a, wg, layer, x_in, act, name):
    T, K = a.shape
    nsh, _, kq, D = wg.shape
    tm = min(T, ROW_TILE // 2)

    def body(a_ref, w_ref, x_ref, o_ref):
        w = w_ref[...].reshape(nsh * kq, D)
        o_ref[...] = x_ref[...] + jnp.dot(_act(a_ref[...], act), w, preferred_element_type=F32)

    return pl.pallas_call(
        body,
        name=name,
        grid=(T // tm,),
        in_specs=[pl.BlockSpec((tm, K), lambda i: (i, 0)), _layer_spec(wg, layer), pl.BlockSpec((tm, D), lambda i: (i, 0))],
        out_specs=pl.BlockSpec((tm, D), lambda i: (i, 0)),
        out_shape=jax.ShapeDtypeStruct((T, D), F32),
        compiler_params=_params(1),
    )(a, wg, x_in)


def _matmul_nt(g, wg, layer, a, name):
    T, D = g.shape
    nsh, _, kq, _ = wg.shape
    tm = min(T, ROW_TILE)

    def body(g_ref, w_ref, *rest):
        r = lax.dot_general(g_ref[...], w_ref[...], NT, preferred_element_type=F32)
        if a is None:
            (o_ref,) = rest
        else:
            a_ref, o_ref = rest
            r = r * (2.0 * jnp.maximum(a_ref[...].astype(F32), 0.0))
        o_ref[...] = r.astype(BF16)

    in_specs = [
        pl.BlockSpec((tm, D), lambda i, k: (i, 0)),
        pl.BlockSpec((None, None, kq, D), lambda i, k: (k, layer, 0, 0)),
    ]
    args = [g, wg]
    if a is not None:
        in_specs.append(pl.BlockSpec((tm, kq), lambda i, k: (i, k)))
        args.append(a)
    return pl.pallas_call(
        body,
        name=name,
        grid=(T // tm, nsh),
        in_specs=in_specs,
        out_specs=pl.BlockSpec((tm, kq), lambda i, k: (i, k)),
        out_shape=jax.ShapeDtypeStruct((T, nsh * kq), BF16),
        compiler_params=_params(2),
    )(*args)


def _matmul_nt_norm_bwd(da, wg, layer, x, gain, dres, name):
    T, D = x.shape
    nsh, _, _, ns = wg.shape
    tm = min(T, ROW_TILE // 2)

    def body(da_ref, w_ref, x_ref, g_ref, r_ref, dx_ref, dxb_ref, dg_ref):
        dh = lax.dot_general(da_ref[:, :ns], w_ref[0], NT, preferred_element_type=F32)
        for j in range(1, nsh):
            dh = dh + lax.dot_general(da_ref[:, j * ns : (j + 1) * ns], w_ref[j], NT, preferred_element_type=F32)
        dx, dg = _norm_bwd(dh, x_ref[...], g_ref[...])
        dx = dx + r_ref[...]
        dx_ref[...] = dx
        dxb_ref[...] = dx.astype(BF16)

        @pl.when(pl.program_id(0) == 0)
        def _():
            dg_ref[...] = dg

        @pl.when(pl.program_id(0) > 0)
        def _():
            dg_ref[...] += dg

    row = pl.BlockSpec((tm, D), lambda i: (i, 0))
    vec = pl.BlockSpec((1, D), lambda i: (0, 0))
    return pl.pallas_call(
        body,
        name=name,
        grid=(T // tm,),
        in_specs=[pl.BlockSpec((tm, nsh * ns), lambda i: (i, 0)), _layer_spec(wg, layer), row, vec, row],
        out_specs=[row, row, vec],
        out_shape=[
            jax.ShapeDtypeStruct((T, D), F32),
            jax.ShapeDtypeStruct((T, D), BF16),
            jax.ShapeDtypeStruct((1, D), F32),
        ],
        compiler_params=_params(1),
    )(da, wg, x, gain.reshape(1, D), dres)


def _matmul_tn(lhs, rhs, bufs, layer, n_layers, shard_lhs, act, name):
    T = lhs.shape[0]
    rows = lhs.shape[1] // N_CHIPS if shard_lhs else lhs.shape[1]
    cols = rhs.shape[1] if shard_lhs else rhs.shape[1] // N_CHIPS
    tt = min(T, ROW_TILE)
    n_t = T // tt

    def body(l_ref, r_ref, *rest):
        o32_ref, o16_ref = rest[-2:]
        t = pl.program_id(1)
        upd = lax.dot_general(_act(l_ref[...], act), r_ref[...].astype(BF16), TN, preferred_element_type=F32)

        @pl.when(t == 0)
        def _():
            o32_ref[...] = upd

        @pl.when(t > 0)
        def _():
            o32_ref[...] += upd

        @pl.when(t == n_t - 1)
        def _():
            o16_ref[...] = o32_ref[...].astype(BF16)

    if shard_lhs:
        in_specs = [pl.BlockSpec((tt, rows), lambda s, t: (t, s)), pl.BlockSpec((tt, cols), lambda s, t: (t, 0))]
    else:
        in_specs = [pl.BlockSpec((tt, rows), lambda s, t: (t, 0)), pl.BlockSpec((tt, cols), lambda s, t: (t, s))]
    args = [lhs, rhs]
    aliases = {}
    if bufs is not None:
        in_specs += [pl.BlockSpec(memory_space=pl.ANY)] * 2
        args += list(bufs)
        aliases = {2: 0, 3: 1}
    shape = (N_CHIPS, n_layers, rows, cols)
    return pl.pallas_call(
        body,
        name=name,
        grid=(N_CHIPS, n_t),
        in_specs=in_specs,
        out_specs=[pl.BlockSpec((None, None, rows, cols), lambda s, t: (s, layer, 0, 0))] * 2,
        out_shape=[jax.ShapeDtypeStruct(shape, F32), jax.ShapeDtypeStruct(shape, BF16)],
        input_output_aliases=aliases,
        compiler_params=_params(2),
    )(*args)


def _chip_peers(x, y):
    return [(1 - x, y), (x, 1 - y), (1 - x, 1 - y)]


def _remote(src, dst, sems, s, peer):
    return pltpu.make_async_remote_copy(
        src_ref=src, dst_ref=dst, send_sem=sems[0].at[s], recv_sem=sems[1].at[s], device_id=peer, device_id_type=MESH
    )


class _Exchange:
    def __init__(self, operands, n_alias, new_shapes, n_sems, build):
        self.operands, self.n_alias, self.new_shapes, self.n_sems, self.build = operands, n_alias, new_shapes, n_sems, build

    def out_shapes(self):
        return [jax.ShapeDtypeStruct(a.shape, a.dtype) for a in self.operands[: self.n_alias]] + list(self.new_shapes)

    def scratch(self):
        return [pltpu.SemaphoreType.DMA((n,)) for n in self.n_sems]

    def run(self, ins, outs, sems, first, last):
        starts, recvs, sends, locals_ = self.build(ins, outs, sems)

        def start_all():
            for cp in starts:
                cp.start()

        def wait_all():
            for cp in recvs:
                cp.wait_recv()
            for cp in sends:
                cp.wait_send()
            for cp in locals_:
                cp.wait()

        if first is True:
            start_all()
            return wait_all
        pl.when(first)(start_all)
        return lambda: pl.when(last)(wait_all)


def _gather_exchange(shards, bufs, plan):
    n_arr = len(shards)
    n_cp = len(plan) * (N_CHIPS - 1)

    def build(ins, outs, sems):
        shard_refs = ins[-n_arr:]
        x, y, c = lax.axis_index("x"), lax.axis_index("y"), lax.axis_index("c")
        me = 2 * x + y
        starts, recvs, sends, locals_ = [], [], [], []
        for p, (a, l0, n) in enumerate(plan):
            src = shard_refs[a].at[pl.ds(l0, n)]
            cp = pltpu.make_async_copy(src, outs[a].at[me, pl.ds(l0, n)], sems[2].at[p])
            locals_.append(cp)
            for k, (px, py) in enumerate(_chip_peers(x, y)):
                s = p * (N_CHIPS - 1) + k
                sends.append(_remote(src, outs[a].at[me, pl.ds(l0, n)], sems, s, (px, py, c)))
                recvs.append(_remote(src, outs[a].at[2 * px + py, pl.ds(l0, n)], sems, s, (px, py, c)))
        return locals_ + sends, recvs, sends, locals_

    if bufs is None:
        new = [jax.ShapeDtypeStruct((N_CHIPS,) + s.shape, s.dtype) for s in shards]
        return _Exchange(list(shards), 0, new, [n_cp, n_cp, len(plan)], build)
    return _Exchange(list(bufs) + list(shards), n_arr, [], [n_cp, n_cp, len(plan)], build)


def _scatter_exchange(g16, recv, plan, small=None):
    n_arr = len(g16)
    have = [r for r in recv if r is not None]
    made = [a for a in range(n_arr) if recv[a] is None]
    n_cp = len(plan) * (N_CHIPS - 1) + (N_DEV - 1 if small is not None else 0)

    def build(ins, outs, sems):
        g_refs = ins[:n_arr]
        recv_refs, it_have, it_made = [], iter(outs[n_arr : n_arr + len(have)]), iter(outs[n_arr + len(have) :])
        for a in range(n_arr):
            recv_refs.append(next(it_made) if recv[a] is None else next(it_have))
        x, y, c = lax.axis_index("x"), lax.axis_index("y"), lax.axis_index("c")
        me = 2 * x + y
        starts, recvs, sends, locals_ = [], [], [], []
        for p, (a, l0, n) in enumerate(plan):
            for k, (px, py) in enumerate(_chip_peers(x, y)):
                s = p * (N_CHIPS - 1) + k
                dst = recv_refs[a].at[k, pl.ds(l0, n)]
                sends.append(_remote(g_refs[a].at[2 * px + py, pl.ds(l0, n)], dst, sems, s, (px, py, c)))
                recvs.append(_remote(g_refs[a].at[me, pl.ds(l0, n)], dst, sems, s, (px, py, c)))
        if small is not None:
            small_ref, all_ref = ins[-1], outs[-1]
            slot = 4 * x + 2 * y + c
            locals_.append(pltpu.make_async_copy(small_ref, all_ref.at[slot], sems[2].at[0]))
            flips = [(fx, fy, fc) for fx in (0, 1) for fy in (0, 1) for fc in (0, 1)][1:]
            for k, (fx, fy, fc) in enumerate(flips):
                s = len(plan) * (N_CHIPS - 1) + k
                px, py, pc = x ^ fx, y ^ fy, c ^ fc
                sends.append(_remote(small_ref, all_ref.at[slot], sems, s, (px, py, pc)))
                recvs.append(_remote(small_ref, all_ref.at[4 * px + 2 * py + pc], sems, s, (px, py, pc)))
        return locals_ + sends, recvs, sends, locals_

    operands = list(g16) + have + ([small] if small is not None else [])
    new = [jax.ShapeDtypeStruct((N_CHIPS - 1,) + g16[a].shape[1:], BF16) for a in made]
    if small is not None:
        new.append(jax.ShapeDtypeStruct((N_DEV,) + small.shape, F32))
    return _Exchange(operands, n_arr + len(have), new, [n_cp, n_cp, 1], build)


def _call_with_exchange(body, exch, name, grid, in_specs, out_specs, out_shape, scratch_shapes, args):
    n_in, n_out, n_scr = len(in_specs), len(out_shape), len(scratch_shapes)
    if exch is None:
        outs = pl.pallas_call(
            body, name=name, grid=grid, in_specs=in_specs, out_specs=out_specs, out_shape=out_shape,
            scratch_shapes=scratch_shapes, compiler_params=_params(len(grid)),
        )(*args)
        return outs, []
    e_shapes = exch.out_shapes()
    e_in, e_out = len(exch.operands), len(e_shapes)

    def wrapped(*refs):
        ins, refs = refs[:n_in], refs[n_in:]
        e_ins, refs = refs[:e_in], refs[e_in:]
        outs, refs = refs[:n_out], refs[n_out:]
        e_outs, refs = refs[:e_out], refs[e_out:]
        scr, sems = refs[:n_scr], refs[n_scr:]
        first, last = True, True
        for d, g in enumerate(grid):
            first = (pl.program_id(d) == 0) & first
            last = (pl.program_id(d) == g - 1) & last
        finish = exch.run(e_ins, e_outs, sems, first, last)
        body(*ins, *outs, *scr)
        finish()

    any_spec = pl.BlockSpec(memory_space=pl.ANY)
    outs = pl.pallas_call(
        wrapped,
        name=name,
        grid=grid,
        in_specs=list(in_specs) + [any_spec] * e_in,
        out_specs=list(out_specs) + [any_spec] * e_out,
        out_shape=list(out_shape) + e_shapes,
        input_output_aliases={n_in + i: n_out + i for i in range(exch.n_alias)},
        scratch_shapes=list(scratch_shapes) + exch.scratch(),
        compiler_params=pltpu.CompilerParams(
            dimension_semantics=("arbitrary",) * len(grid), vmem_limit_bytes=VMEM_LIMIT, has_side_effects=True
        ),
    )(*args, *exch.operands)
    return outs[:n_out], outs[n_out:]


def _exchange_only(exch, name):
    n_in = len(exch.operands)
    shapes = exch.out_shapes()

    def body(*refs):
        ins, outs, sems = refs[:n_in], refs[n_in : n_in + len(shapes)], refs[n_in + len(shapes) :]
        exch.run(ins, outs, sems, True, True)()

    any_spec = pl.BlockSpec(memory_space=pl.ANY)
    return pl.pallas_call(
        body,
        name=name,
        in_specs=[any_spec] * n_in,
        out_specs=[any_spec] * len(shapes),
        out_shape=shapes,
        input_output_aliases={i: i for i in range(exch.n_alias)},
        scratch_shapes=exch.scratch(),
        compiler_params=pltpu.CompilerParams(has_side_effects=True),
    )(*exch.operands)


ATTN_LANE_TILES = 2
ATTN_UNROLL = 8


MASKED = -1e30


def _hi_lo(x):
    hi = x.astype(BF16)
    lo = (x - hi.astype(F32)).astype(BF16)
    return jnp.concatenate([hi, lo], axis=1)


def _suffix_matrix(inclusive):
    j = lax.broadcasted_iota(jnp.int32, (2 * Q_TILE, 2 * Q_TILE), 0) & (Q_TILE - 1)
    s = lax.broadcasted_iota(jnp.int32, (2 * Q_TILE, 2 * Q_TILE), 1)
    later = (j >= s) if inclusive else (j > s)
    return jnp.where((s >= Q_TILE) | later, 1.0, 0.0).astype(BF16)


def _log_beta(z):
    return jnp.minimum(z, 0.0) - jnp.log(1.0 + jnp.exp(-jnp.abs(z)))


def _head_masks(width):
    lane = lax.broadcasted_iota(jnp.int32, (1, width), 1)
    return [(lane >= h * HEAD_DIM) & (lane < (h + 1) * HEAD_DIM) for h in range(width // HEAD_DIM)]


def _per_head_rows(x, masks):
    return jnp.concatenate([jnp.where(hm, x, 0) for hm in masks], axis=0)


def _heads_to_lanes(x, n_heads):
    return jnp.concatenate([x[h * Q_TILE : (h + 1) * Q_TILE] for h in range(n_heads)], axis=1)


def _block_start(kb):
    return kb * Q_TILE if isinstance(kb, int) else pl.multiple_of(kb * Q_TILE, Q_TILE)


def _clamp(i, n):
    return jnp.minimum(i, n - 1)


def _next_block(pos):
    qi, kb = pos
    row_done = kb == 0
    nqi = jnp.where(row_done, qi + 1, qi)
    return nqi, jnp.where(row_done, nqi, kb - 1)


def _stream_unroll(n_blocks):
    return next(u for u in (ATTN_UNROLL, 2, 1) if n_blocks % u == 0)


def _past_mask(rows):
    t = lax.broadcasted_iota(jnp.int32, (rows, Q_TILE), 0) & (Q_TILE - 1)
    s = lax.broadcasted_iota(jnp.int32, (rows, Q_TILE), 1)
    return s < t


def _attn_fwd(qkv, n_seq, S, D, exch, name):
    T = n_seq * S
    width = min(D, ATTN_LANE_TILES * LANES)
    n_heads = width // HEAD_DIM
    rows = n_heads * Q_TILE
    nq = S // Q_TILE
    groups = D // width
    n_blocks = nq * (nq + 1) // 2
    unroll = _stream_unroll(n_blocks)
    scale = HEAD_DIM ** -0.5

    def body(q_ref, k_ref, v_ref, o_ref, qh_scr, vh_scr, bias_scr):
        masks = _head_masks(width)
        sfx = _suffix_matrix(False)

        def per_head_tables(i, c):
            blk = pl.ds(_block_start(i), Q_TILE)
            qh_scr[i] = _per_head_rows(q_ref[blk, :] * scale, masks)
            vh_scr[i] = _per_head_rows(v_ref[blk, :], masks)
            return c

        lax.fori_loop(0, nq, per_head_tables, 0)
        bias_scr[0] = jnp.zeros((rows, Q_TILE), F32)
        bias_scr[1] = jnp.where(_past_mask(rows), 0.0, MASKED)

        def scores(pos):
            qi, kb = pos
            kt = k_ref[pl.ds(_block_start(_clamp(kb, nq)), Q_TILE), :]
            z = lax.dot_general(qh_scr[_clamp(qi, nq)], kt, NT, preferred_element_type=F32)
            z = z + bias_scr[(kb == qi).astype(jnp.int32)]
            lb = _log_beta(z)
            return lb, _hi_lo(lb - z)

        def weigh(pos, st, carry, acc):
            qi, kb = pos
            lb, l1 = st
            r = jnp.dot(l1, sfx, preferred_element_type=F32)
            carry = jnp.where(kb == qi, 0.0, carry)
            a = _heads_to_lanes(jnp.exp(lb + r[:, :Q_TILE] + carry).astype(BF16), n_heads)
            acc = jnp.where(kb == qi, 0.0, acc) + jnp.dot(a, vh_scr[_clamp(kb, nq)], preferred_element_type=F32)
            o_ref[pl.ds(_block_start(_clamp(qi, nq)), Q_TILE), :] = acc
            return carry + r[:, Q_TILE:], acc

        def trip(n, c):
            pos, st, carry, acc = c
            for _ in range(unroll):
                nxt = _next_block(pos)
                st_nxt = scores(nxt)
                carry, acc = weigh(pos, st, carry, acc)
                pos, st = nxt, st_nxt
            return pos, st, carry, acc

        first = (jnp.int32(0), jnp.int32(0))
        zero = bias_scr[0]
        init = (first, scores(first), zero, jnp.concatenate([zero[:Q_TILE]] * (width // Q_TILE), axis=1))
        lax.fori_loop(0, n_blocks // unroll, trip, init)

    seq = lambda col0: pl.BlockSpec((S, width), lambda b, p: (b, col0 + p))
    (o,), moved = _call_with_exchange(
        body,
        exch,
        name,
        grid=(n_seq, groups),
        in_specs=[seq(0), seq(groups), seq(2 * groups)],
        out_specs=[seq(0)],
        out_shape=[jax.ShapeDtypeStruct((T, D), F32)],
        scratch_shapes=[
            pltpu.VMEM((nq, rows, width), BF16),
            pltpu.VMEM((nq, rows, width), BF16),
            pltpu.VMEM((2, rows, Q_TILE), F32),
        ],
        args=(qkv, qkv, qkv),
    )
    return o, moved


def _attn_bwd(qkv, o, do, n_seq, S, D, exch, name):
    T = n_seq * S
    width = min(D, ATTN_LANE_TILES * LANES)
    n_heads = width // HEAD_DIM
    rows = n_heads * Q_TILE
    nq = S // Q_TILE
    groups = D // width
    n_blocks = nq * (nq + 1) // 2
    unroll = _stream_unroll(n_blocks)
    scale = HEAD_DIM ** -0.5

    def body(q_ref, k_ref, v_ref, o_ref, do_ref, dq_ref, dk_ref, dv_ref, dk_acc, dv_acc, qh_scr, doh_scr, delta_scr, bias_scr):
        masks = _head_masks(width)
        sfx = _suffix_matrix(False)
        sfx_incl = _suffix_matrix(True)
        dk_acc[...] = jnp.zeros_like(dk_acc)
        dv_acc[...] = jnp.zeros_like(dv_acc)

        def per_head_tables(i, c):
            blk = pl.ds(_block_start(i), Q_TILE)
            do = do_ref[blk, :]
            qh_scr[i] = _per_head_rows(q_ref[blk, :] * scale, masks)
            doh_scr[i] = _per_head_rows(do, masks)
            prod = do.astype(F32) * o_ref[blk, :]
            delta = jnp.concatenate(
                [jnp.sum(jnp.where(hm, prod, 0.0), axis=-1, keepdims=True) for hm in masks], axis=0
            )
            delta_scr[i] = jnp.broadcast_to(delta, (rows, Q_TILE))
            return c

        lax.fori_loop(0, nq, per_head_tables, 0)
        bias_scr[0] = jnp.zeros((rows, Q_TILE), F32)
        bias_scr[1] = jnp.where(_past_mask(rows), 0.0, MASKED)

        def scores(pos):
            qi, kb = pos
            blk = pl.ds(_block_start(_clamp(kb, nq)), Q_TILE)
            z = lax.dot_general(qh_scr[_clamp(qi, nq)], k_ref[blk, :], NT, preferred_element_type=F32)
            z = z + bias_scr[(kb == qi).astype(jnp.int32)]
            lb = _log_beta(z)
            da = lax.dot_general(doh_scr[_clamp(qi, nq)], v_ref[blk, :], NT, preferred_element_type=F32)
            return lb, _hi_lo(lb - z), da

        def weigh(pos, st, c1, c2, dq):
            qi, kb = pos
            lb, l1, da = st
            first = kb == qi
            blk = pl.ds(_block_start(kb), Q_TILE)
            r = jnp.dot(l1, sfx, preferred_element_type=F32)
            c1 = jnp.where(first, 0.0, c1)
            ab = jnp.exp(lb + r[:, :Q_TILE] + c1).astype(BF16)
            g = ab.astype(F32) * da
            r2 = jnp.dot(_hi_lo(g), sfx_incl, preferred_element_type=F32)
            c2 = jnp.where(first, 0.0, c2)
            earlier = delta_scr[qi] - (r2[:, :Q_TILE] + c2)
            beta = jnp.exp(lb)
            dzb = (g * (1.0 - beta) - earlier * beta).astype(BF16)
            kh = _per_head_rows(k_ref[blk, :], masks)
            dq = jnp.where(first, 0.0, dq) + jnp.dot(_heads_to_lanes(dzb, n_heads), kh, preferred_element_type=F32)
            dq_ref[pl.ds(_block_start(qi), Q_TILE), :] = (dq * scale).astype(BF16)
            dk_acc[blk, :] += lax.dot_general(dzb, qh_scr[qi], TN, preferred_element_type=F32)
            dv_acc[blk, :] += lax.dot_general(ab, doh_scr[qi], TN, preferred_element_type=F32)
            return c1 + r[:, Q_TILE:], c2 + r2[:, Q_TILE:], dq

        def trip(n, c):
            pos, st, c1, c2, dq = c
            for _ in range(unroll):
                nxt = _next_block(pos)
                st_nxt = scores(nxt)
                c1, c2, dq = weigh(pos, st, c1, c2, dq)
                pos, st = nxt, st_nxt
            return pos, st, c1, c2, dq

        first = (jnp.int32(0), jnp.int32(0))
        zero = bias_scr[0]
        init = (first, scores(first), zero, zero, jnp.concatenate([zero[:Q_TILE]] * (width // Q_TILE), axis=1))
        lax.fori_loop(0, n_blocks // unroll, trip, init)
        dk_ref[...] = dk_acc[...].astype(BF16)
        dv_ref[...] = dv_acc[...].astype(BF16)

    seq = lambda col0: pl.BlockSpec((S, width), lambda b, p: (b, col0 + p))
    return _call_with_exchange(
        body,
        exch,
        name,
        grid=(n_seq, groups),
        in_specs=[seq(0), seq(groups), seq(2 * groups), seq(0), seq(0)],
        out_specs=[seq(0)] * 3,
        out_shape=[jax.ShapeDtypeStruct((T, D), BF16)] * 3,
        scratch_shapes=[
            pltpu.VMEM((S, width), F32),
            pltpu.VMEM((S, width), F32),
            pltpu.VMEM((nq, rows, width), BF16),
            pltpu.VMEM((nq, rows, width), BF16),
            pltpu.VMEM((nq, rows, Q_TILE), F32),
            pltpu.VMEM((2, rows, Q_TILE), F32),
        ],
        args=(qkv, qkv, qkv, o, do),
    )


def _causal_ws(ws_ref, g):
    t = lax.broadcasted_iota(jnp.int32, (SGU_CHUNK, SGU_CHUNK), 0)
    s = lax.broadcasted_iota(jnp.int32, (SGU_CHUNK, SGU_CHUNK), 1)
    return jnp.where(s <= t, ws_ref[g], 0.0)


def _sgu_fwd(a, gain, ws, bsb, name):
    T, F2 = a.shape
    F = F2 // 2
    gw = F // SGU_GROUPS

    def body(a_ref, gain_ref, ws_ref, bsb_ref, y_ref):
        v = _gelu(a_ref[:, F:].astype(F32))
        vn = (v * _rstd(v) * gain_ref[...]).astype(BF16)
        for g in range(SGU_GROUPS):
            cs = slice(g * gw, (g + 1) * gw)
            w = _causal_ws(ws_ref, g).astype(BF16)
            mixed = jnp.dot(w, vn[:, cs], preferred_element_type=F32) + bsb_ref[g]
            y_ref[:, cs] = (_gelu(a_ref[:, cs].astype(F32)) * mixed).astype(BF16)

    return pl.pallas_call(
        body,
        name=name,
        grid=(T // SGU_CHUNK,),
        in_specs=[
            pl.BlockSpec((SGU_CHUNK, F2), lambda i: (i, 0)),
            pl.BlockSpec((1, F), lambda i: (0, 0)),
            pl.BlockSpec((SGU_GROUPS, SGU_CHUNK, SGU_CHUNK), lambda i: (0, 0, 0)),
            pl.BlockSpec((SGU_GROUPS, SGU_CHUNK, gw), lambda i: (0, 0, 0)),
        ],
        out_specs=pl.BlockSpec((SGU_CHUNK, F), lambda i: (i, 0)),
        out_shape=jax.ShapeDtypeStruct((T, F), BF16),
        compiler_params=_params(1),
    )(a, gain.reshape(1, F), ws, bsb)


def _sgu_bwd(a, dy, gain, ws, bsb, name):
    T, F2 = a.shape
    F = F2 // 2
    gw = F // SGU_GROUPS

    def body(a_ref, dy_ref, gain_ref, ws_ref, bsb_ref, da_ref, dws_ref, dbs_ref, dgain_ref, dvn_ref):
        @pl.when(pl.program_id(0) == 0)
        def _():
            dws_ref[...] = jnp.zeros_like(dws_ref)
            dbs_ref[...] = jnp.zeros_like(dbs_ref)
            dgain_ref[...] = jnp.zeros_like(dgain_ref)

        av = a_ref[:, F:].astype(F32)
        v = _gelu(av)
        rstd = _rstd(v)
        vh = v * rstd
        gain = gain_ref[...]
        vn = (vh * gain).astype(BF16)
        ones = jnp.ones((gw, SGU_CHUNK), BF16)
        for g in range(SGU_GROUPS):
            cs = slice(g * gw, (g + 1) * gw)
            w = _causal_ws(ws_ref, g).astype(BF16)
            mixed = jnp.dot(w, vn[:, cs], preferred_element_type=F32) + bsb_ref[g]
            au = a_ref[:, cs].astype(F32)
            dyc = dy_ref[:, cs].astype(F32)
            da_ref[:, cs] = (dyc * mixed * _gelu_grad(au)).astype(BF16)
            dm = (dyc * _gelu(au)).astype(BF16)
            dbs_ref[g] += jnp.dot(dm, ones, preferred_element_type=F32)
            dws_ref[g] += _causal_mask_f32(lax.dot_general(dm, vn[:, cs], NT, preferred_element_type=F32))
            dvn_ref[:, cs] = lax.dot_general(w, dm, TN, preferred_element_type=F32)
        dvn = dvn_ref[...]
        dgain_ref[...] += jnp.sum(dvn * vh, axis=0, keepdims=True)
        dvh = dvn * gain
        dv = rstd * (dvh - vh * jnp.mean(dvh * vh, axis=-1, keepdims=True))
        da_ref[:, F:] = (dv * _gelu_grad(av)).astype(BF16)

    acc_spec = pl.BlockSpec((SGU_GROUPS, SGU_CHUNK, SGU_CHUNK), lambda i: (0, 0, 0))
    acc_shape = jax.ShapeDtypeStruct((SGU_GROUPS, SGU_CHUNK, SGU_CHUNK), F32)
    return pl.pallas_call(
        body,
        name=name,
        grid=(T // SGU_CHUNK,),
        in_specs=[
            pl.BlockSpec((SGU_CHUNK, F2), lambda i: (i, 0)),
            pl.BlockSpec((SGU_CHUNK, F), lambda i: (i, 0)),
            pl.BlockSpec((1, F), lambda i: (0, 0)),
            acc_spec,
            pl.BlockSpec((SGU_GROUPS, SGU_CHUNK, gw), lambda i: (0, 0, 0)),
        ],
        out_specs=[
            pl.BlockSpec((SGU_CHUNK, F2), lambda i: (i, 0)),
            acc_spec,
            acc_spec,
            pl.BlockSpec((1, F), lambda i: (0, 0)),
        ],
        out_shape=[
            jax.ShapeDtypeStruct((T, F2), BF16),
            acc_shape,
            acc_shape,
            jax.ShapeDtypeStruct((1, F), F32),
        ],
        scratch_shapes=[pltpu.VMEM((SGU_CHUNK, F), F32)],
        compiler_params=_params(1),
    )(a, dy, gain.reshape(1, F), ws, bsb)


def _causal_mask_f32(m):
    t = lax.broadcasted_iota(jnp.int32, m.shape, 0)
    s = lax.broadcasted_iota(jnp.int32, m.shape, 1)
    return jnp.where(s <= t, m, 0.0)


def _final_loss(x, gain, target, name):
    T, D = x.shape
    tm = min(T, ROW_TILE // 2)

    def body(x_ref, g_ref, t_ref, sq_ref, dx_ref, dxb_ref, dg_ref):
        xv = x_ref[...]
        gain = g_ref[...]
        err = xv * _rstd(xv) * gain - t_ref[...]
        dx, dg = _norm_bwd(err * (1.0 / D), xv, gain)
        dx_ref[...] = dx
        dxb_ref[...] = dx.astype(BF16)
        sq = jnp.sum(err * err, axis=0, keepdims=True)

        @pl.when(pl.program_id(0) == 0)
        def _():
            sq_ref[...] = sq
            dg_ref[...] = dg

        @pl.when(pl.program_id(0) > 0)
        def _():
            sq_ref[...] += sq
            dg_ref[...] += dg

    row = pl.BlockSpec((tm, D), lambda i: (i, 0))
    vec = pl.BlockSpec((1, D), lambda i: (0, 0))
    return pl.pallas_call(
        body,
        name=name,
        grid=(T // tm,),
        in_specs=[row, vec, row],
        out_specs=[vec, row, row, vec],
        out_shape=[
            jax.ShapeDtypeStruct((1, D), F32),
            jax.ShapeDtypeStruct((T, D), F32),
            jax.ShapeDtypeStruct((T, D), BF16),
            jax.ShapeDtypeStruct((1, D), F32),
        ],
        compiler_params=_params(1),
    )(x, gain.reshape(1, D), target)


def _row_tile(rows, cols, n_arrays):
    budget = VMEM_LIMIT // 2 // (2 * n_arrays * cols * 4)
    tr = rows
    while tr > budget and tr % 16 == 0:
        tr //= 2
    return tr


def _sum_received(own, recv, name):
    R, C = own.shape
    n = recv.shape[0]
    tr = _row_tile(R, C, n + 2)

    def body(own_ref, recv_ref, o_ref):
        s = own_ref[...]
        for k in range(n):
            s = s + recv_ref[k].astype(F32)
        o_ref[...] = s

    return pl.pallas_call(
        body,
        name=name,
        grid=(R // tr,),
        in_specs=[pl.BlockSpec((tr, C), lambda i: (i, 0)), pl.BlockSpec((n, tr, C), lambda i: (0, i, 0))],
        out_specs=pl.BlockSpec((tr, C), lambda i: (i, 0)),
        out_shape=jax.ShapeDtypeStruct((R, C), F32),
        compiler_params=_params(1),
    )(own, recv)


def _sum_chip_shard(g32, recv, chip, name):
    _, L, r, c = g32.shape
    n = recv.shape[0]
    tr = _row_tile(r, c, n + 2)

    def body(chip_ref, own_ref, recv_ref, o_ref):
        s = own_ref[...]
        for k in range(n):
            s = s + recv_ref[k].astype(F32)
        o_ref[...] = s

    return pl.pallas_call(
        body,
        name=name,
        grid_spec=pltpu.PrefetchScalarGridSpec(
            num_scalar_prefetch=1,
            grid=(L, r // tr),
            in_specs=[
                pl.BlockSpec((None, None, tr, c), lambda l, i, chip_ref: (chip_ref[0], l, i, 0)),
                pl.BlockSpec((n, None, tr, c), lambda l, i, chip_ref: (0, l, i, 0)),
            ],
            out_specs=pl.BlockSpec((None, tr, c), lambda l, i, chip_ref: (l, i, 0)),
        ),
        out_shape=jax.ShapeDtypeStruct((L, r, c), F32),
        compiler_params=_params(2),
    )(chip.reshape(1).astype(jnp.int32), g32, recv)


def _adamw(w, m, v, parts, name):
    R, C = w.shape
    n = len(parts)
    tr = _row_tile(R, C, n + 7)

    def body(*refs):
        w_ref, m_ref, v_ref = refs[:3]
        g_ref, d_ref, nm_ref, nv_ref = refs[3 + n :]
        g = refs[3][...]
        for p_ref in refs[4 : 3 + n]:
            g = g + p_ref[...]
        nm = ADAM_B1 * m_ref[...] + (1.0 - ADAM_B1) * g
        nv = ADAM_B2 * v_ref[...] + (1.0 - ADAM_B2) * (g * g)
        m_hat = nm / (1.0 - ADAM_B1**ADAM_STEP)
        v_hat = nv / (1.0 - ADAM_B2**ADAM_STEP)
        g_ref[...] = g
        d_ref[...] = -ADAM_LR * (m_hat / (jnp.sqrt(v_hat) + ADAM_EPS) + ADAM_WD * w_ref[...])
        nm_ref[...] = nm
        nv_ref[...] = nv

    spec = pl.BlockSpec((tr, C), lambda i: (i, 0))
    return pl.pallas_call(
        body,
        name=name,
        grid=(R // tr,),
        in_specs=[spec] * (3 + n),
        out_specs=[spec] * 4,
        out_shape=[jax.ShapeDtypeStruct((R, C), F32)] * 4,
        compiler_params=_params(1),
    )(w, m, v, *parts)


def _swap_with_sibling(parts, name):
    n = len(parts)

    def body(*refs):
        ins, outs = refs[:n], refs[n : 2 * n]
        send_sems, recv_sems = refs[2 * n :]
        sibling = (lax.axis_index("x"), lax.axis_index("y"), 1 - lax.axis_index("c"))
        copies = [
            pltpu.make_async_remote_copy(
                src_ref=ins[a],
                dst_ref=outs[a],
                send_sem=send_sems.at[a],
                recv_sem=recv_sems.at[a],
                device_id=sibling,
                device_id_type=MESH,
            )
            for a in range(n)
        ]
        for cp in copies:
            cp.start()
        for cp in copies:
            cp.wait_recv()
        for cp in copies:
            cp.wait_send()

    any_spec = pl.BlockSpec(memory_space=pl.ANY)
    return pl.pallas_call(
        body,
        name=name,
        in_specs=[any_spec] * n,
        out_specs=[any_spec] * n,
        out_shape=[jax.ShapeDtypeStruct(p.shape, p.dtype) for p in parts],
        scratch_shapes=[pltpu.SemaphoreType.DMA((n,)), pltpu.SemaphoreType.DMA((n,))],
        compiler_params=pltpu.CompilerParams(has_side_effects=True),
    )(*parts)


def _pack(pieces):
    flat = jnp.concatenate([p.reshape(-1) for p in pieces])
    return flat.reshape(-1, LANES)


def _unpack(packed, shapes):
    flat = packed.reshape(-1)
    out, off = [], 0
    for s in shapes:
        size = 1
        for d in s:
            size *= d
        out.append(flat[off : off + size].reshape(s))
        off += size
    return out


def kernel(x, norm_mix, norm_mlp, sb_wqkv, sb_wo, sgu_win, sgu_gain, sgu_ws, sgu_bs, sgu_wout, mlp_w1, mlp_w2, final_norm, loss_target, m_norm_mix, m_norm_mlp, m_sb_wqkv, m_sb_wo, m_sgu_win, m_sgu_gain, m_sgu_ws, m_sgu_bs, m_sgu_wout, m_mlp_w1, m_mlp_w2, m_final_norm, v_norm_mix, v_norm_mlp, v_sb_wqkv, v_sb_wo, v_sgu_win, v_sgu_gain, v_sgu_ws, v_sgu_bs, v_sgu_wout, v_mlp_w1, v_mlp_w2, v_final_norm):
    n_seq, S, D = x.shape
    T = n_seq * S
    depth = norm_mix.shape[0]
    n_sgu = sgu_win.shape[0]
    F = sgu_wout.shape[1] * N_CHIPS
    gw = F // SGU_GROUPS
    chip = 2 * lax.axis_index("x") + lax.axis_index("y")

    QKV, WO, WIN, WOUT, W1, W2, GAIN = range(7)
    big = [sb_wqkv, sb_wo, sgu_win, sgu_wout, mlp_w1, mlp_w2]
    n_sb = sb_wqkv.shape[0]
    shards = [w.astype(BF16) for w in big] + [sgu_gain.reshape(1, -1, LANES)]
    def gather_plan(i):
        j, mlp = i // 2, min(2, depth - i)
        plan = [(WO, j, 1), (W1, i, mlp), (W2, i, mlp)]
        if i + 1 < depth:
            plan += [(WIN, (i + 1) // 2, 1), (WOUT, (i + 1) // 2, 1)]
        if j + 1 < n_sb:
            plan += [(QKV, j + 1, 1)]
        return plan

    wg = _exchange_only(_gather_exchange(shards, None, [(QKV, 0, 1), (GAIN, 0, 1)]), "gather_first_weights")
    gain_full = jnp.transpose(wg[GAIN].reshape(N_CHIPS, n_sgu, F // N_CHIPS), (1, 0, 2)).reshape(n_sgu, F)
    bsb = [jnp.broadcast_to(sgu_bs[j][:, :, None], (SGU_GROUPS, SGU_CHUNK, gw)) for j in range(n_sgu)]

    xs = x.reshape(T, D)
    saved = []
    for i in range(depth):
        j = i // 2
        if i % 2 == 0:
            qkv, h = _norm_matmul(xs, norm_mix[i], wg[QKV], j, f"qkv_fwd_{i}")
            o, wg = _attn_fwd(qkv, n_seq, S, D, _gather_exchange(shards, wg, gather_plan(i)), f"attn_fwd_{i}")
            wg_qkv, wg_wo, wg_win, wg_wout, wg_w1, wg_w2 = wg[:6]
            x_mid = _act_matmul_res(o, wg_wo, j, xs, None, f"wo_fwd_{i}")
            mix = (qkv, o)
        else:
            a, h = _norm_matmul(xs, norm_mix[i], wg_win, j, f"win_fwd_{i}")
            yg = _sgu_fwd(a, gain_full[j], sgu_ws[j], bsb[j], f"sgu_fwd_{i}")
            x_mid = _act_matmul_res(yg, wg_wout, j, xs, None, f"wout_fwd_{i}")
            mix = (a, yg)
        a2, h2 = _norm_matmul(x_mid, norm_mlp[i], wg_w1, i, f"w1_fwd_{i}")
        x_out = _act_matmul_res(a2, wg_w2, i, x_mid, "relu2", f"w2_fwd_{i}")
        saved.append((xs, h, mix, x_mid, h2, a2))
        xs = x_out

    sq, dx, dxb, g_final = _final_loss(xs, final_norm, loss_target.reshape(T, D), "loss_head")
    loss = lax.psum(0.5 * jnp.sum(sq) / D, ("x", "y", "c"))

    n_layers = [n_sb, n_sb, n_sgu, n_sgu, depth, depth]
    g32, g16, recv = [None] * 6, [None] * 6, [None] * 6
    done_from, sent_from = list(n_layers), list(n_layers)

    def grad(a, layer, lhs, rhs, shard_lhs, act, name):
        bufs = None if g32[a] is None else (g32[a], g16[a])
        g32[a], g16[a] = _matmul_tn(lhs, rhs, bufs, layer, n_layers[a], shard_lhs, act, name)
        done_from[a] = layer

    def unsent_plan():
        plan = [(a, done_from[a], sent_from[a] - done_from[a]) for a in range(6) if sent_from[a] > done_from[a]]
        for a, l0, _ in plan:
            sent_from[a] = l0
        return plan

    def scatter(plan, small):
        arrays = sorted({a for a, _, _ in plan})
        have = [a for a in arrays if recv[a] is not None]
        made = [a for a in arrays if recv[a] is None]
        exch = _scatter_exchange(
            [g16[a] for a in arrays], [recv[a] for a in arrays], [(arrays.index(a), l0, n) for a, l0, n in plan], small
        )

        def take(moved):
            for a, buf in zip(arrays, moved):
                g16[a] = buf
            for a, buf in zip(have + made, moved[len(arrays) :]):
                recv[a] = buf
            return moved[-1]

        return exch, take

    g_mix, g_mlp = [None] * depth, [None] * depth
    g_ws, g_bs, g_gain = [None] * n_sgu, [None] * n_sgu, [None] * n_sgu
    for i in reversed(range(depth)):
        j = i // 2
        x_in, h, mix, x_mid, h2, a2 = saved[i]
        da2 = _matmul_nt(dxb, wg_w2, i, a2, f"w2_bwd_{i}")
        grad(W2, i, a2, dxb, True, "relu2", f"w2_grad_{i}")
        grad(W1, i, h2, da2, False, None, f"w1_grad_{i}")
        dx, dxb, g_mlp[i] = _matmul_nt_norm_bwd(da2, wg_w1, i, x_mid, norm_mlp[i], dx, f"w1_bwd_{i}")
        if i % 2 == 0:
            qkv, o = mix
            do = _matmul_nt(dxb, wg_wo, j, None, f"wo_bwd_{i}")
            grad(WO, j, o, dxb, True, None, f"wo_grad_{i}")
            exch, take = scatter(unsent_plan(), None)
            (dq, dk, dv), moved = _attn_bwd(qkv, o, do, n_seq, S, D, exch, f"attn_bwd_{i}")
            take(moved)
            dqkv = jnp.concatenate([dq, dk, dv], axis=1)
            grad(QKV, j, h, dqkv, False, None, f"qkv_grad_{i}")
            dx, dxb, g_mix[i] = _matmul_nt_norm_bwd(dqkv, wg_qkv, j, x_in, norm_mix[i], dx, f"qkv_bwd_{i}")
        else:
            a, yg = mix
            dyg = _matmul_nt(dxb, wg_wout, j, None, f"wout_bwd_{i}")
            grad(WOUT, j, yg, dxb, True, None, f"wout_grad_{i}")
            da, g_ws[j], dbs, g_gain[j] = _sgu_bwd(a, dyg, gain_full[j], sgu_ws[j], bsb[j], f"sgu_bwd_{i}")
            g_bs[j] = dbs[:, :, 0]
            grad(WIN, j, h, da, False, None, f"win_grad_{i}")
            dx, dxb, g_mix[i] = _matmul_nt_norm_bwd(da, wg_win, j, x_in, norm_mix[i], dx, f"win_bwd_{i}")
    grad_x = dx.reshape(n_seq, S, D)

    names = ["qkv", "wo", "win", "wout", "w1", "w2"]
    small_shapes = [norm_mix.shape, norm_mlp.shape, final_norm.shape, sgu_ws.shape, sgu_bs.shape, (n_sgu, F)]
    small = _pack(
        [jnp.stack(g_mix), jnp.stack(g_mlp), g_final, jnp.stack(g_ws), jnp.stack(g_bs), jnp.stack(g_gain)]
    )
    exch, take = scatter(unsent_plan(), small)
    small_all = take(_exchange_only(exch, "scatter_last_grads"))
    partial = [_sum_chip_shard(g32[a], recv[a], chip, f"sum_{names[a]}") for a in range(6)]
    partial = [p.reshape(-1, p.shape[-1]) for p in partial]
    theirs = _swap_with_sibling(partial, "swap_partial_sums")
    small_sum = _sum_received(small_all[0], small_all[1:], "sum_small")

    ms = [m_sb_wqkv, m_sb_wo, m_sgu_win, m_sgu_wout, m_mlp_w1, m_mlp_w2]
    vs = [v_sb_wqkv, v_sb_wo, v_sgu_win, v_sgu_wout, v_mlp_w1, v_mlp_w2]
    res = {}
    keys = ["sb_wqkv", "sb_wo", "sgu_win", "sgu_wout", "mlp_w1", "mlp_w2"]
    for key, k, w, m, v, mine, other in zip(keys, names, big, ms, vs, partial, theirs):
        cols = w.shape[-1]
        outs = _adamw(w.reshape(-1, cols), m.reshape(-1, cols), v.reshape(-1, cols), [mine, other], f"adamw_{k}")
        res[key] = [o.reshape(w.shape) for o in outs]

    g_small = _unpack(small_sum, small_shapes)
    g_small[5] = lax.dynamic_slice_in_dim(g_small[5], chip * (F // N_CHIPS), F // N_CHIPS, axis=1)
    small_keys = ["norm_mix", "norm_mlp", "final_norm", "sgu_ws", "sgu_bs", "sgu_gain"]
    small_w = [norm_mix, norm_mlp, final_norm, sgu_ws, sgu_bs, sgu_gain]
    small_m = [m_norm_mix, m_norm_mlp, m_final_norm, m_sgu_ws, m_sgu_bs, m_sgu_gain]
    small_v = [v_norm_mix, v_norm_mlp, v_final_norm, v_sgu_ws, v_sgu_bs, v_sgu_gain]
    outs = _adamw(_pack(small_w), _pack(small_m), _pack(small_v), [_pack(g_small)], "adamw_small")
    local_shapes = [w.shape for w in small_w]
    for key, parts in zip(small_keys, zip(*[_unpack(o, local_shapes) for o in outs])):
        res[key] = list(parts)

    order = ["norm_mix", "norm_mlp", "sb_wqkv", "sb_wo", "sgu_win", "sgu_gain", "sgu_ws", "sgu_bs", "sgu_wout", "mlp_w1", "mlp_w2", "final_norm"]
    return (loss, grad_x, *[res[k][0] for k in order], *[res[k][1] for k in order], *[res[k][2] for k in order], *[res[k][3] for k in order])
```

```python
import jax
import jax.numpy as jnp
from jax import lax
from jax.experimental import pallas as pl
from jax.experimental.pallas import tpu as pltpu

F32 = jnp.float32
BF16 = jnp.bfloat16
MESH = pl.DeviceIdType.MESH

EPS = 1e-6
HEAD_DIM = 64
LANES = 128
Q_TILE = 128
SGU_CHUNK = 128
SGU_GROUPS = 8
N_CHIPS = 4
N_DEV = 8
ADAM_LR = 0.001
ADAM_B1 = 0.9
ADAM_B2 = 0.999
ADAM_EPS = 1e-08
ADAM_WD = 0.01
ADAM_STEP = 10
GELU_C0 = 0.7978845608028654
GELU_C1 = 0.044715
VMEM_LIMIT = 48 * 1024 * 1024
ROW_TILE = 1024
NT = (((1,), (1,)), ((), ()))
TN = (((0,), (0,)), ((), ()))


def _params(n_axes):
    return pltpu.CompilerParams(dimension_semantics=("arbitrary",) * n_axes, vmem_limit_bytes=VMEM_LIMIT)


def _rstd(x):
    return lax.rsqrt(jnp.mean(x * x, axis=-1, keepdims=True) + EPS)


def _norm_bwd(dh, x, gain):
    rstd = _rstd(x)
    xh = x * rstd
    dhg = dh * gain
    dx = rstd * (dhg - xh * jnp.mean(dhg * xh, axis=-1, keepdims=True))
    return dx, jnp.sum(dh * xh, axis=0, keepdims=True)


def _gelu(x):
    return 0.5 * x * (1.0 + jnp.tanh(GELU_C0 * (x + GELU_C1 * x * x * x)))


def _gelu_grad(x):
    t = jnp.tanh(GELU_C0 * (x + GELU_C1 * x * x * x))
    return 0.5 * (1.0 + t) + 0.5 * x * (1.0 - t * t) * (GELU_C0 * (1.0 + 3.0 * GELU_C1 * x * x))


def _act(a, act):
    if act == "relu2":
        r = jnp.maximum(a.astype(F32), 0.0)
        return (r * r).astype(BF16)
    return a.astype(BF16)


def _layer_spec(wg, layer):
    nsh, _, r, c = wg.shape
    return pl.BlockSpec((nsh, None, r, c), lambda i: (0, layer, 0, 0), pipeline_mode=pl.Buffered(1))


def _norm_matmul(x, gain, wg, layer, name):
    T, D = x.shape
    nsh, _, _, ns = wg.shape
    tm = min(T, ROW_TILE // 2)

    def body(x_ref, g_ref, w_ref, y_ref, h_ref):
        xv = x_ref[...]
        h = (xv * _rstd(xv) * g_ref[...]).astype(BF16)
        h_ref[...] = h
        for j in range(nsh):
            y_ref[:, j * ns : (j + 1) * ns] = jnp.dot(h, w_ref[j], preferred_element_type=F32).astype(BF16)

    return pl.pallas_call(
        body,
        name=name,
        grid=(T // tm,),
        in_specs=[pl.BlockSpec((tm, D), lambda i: (i, 0)), pl.BlockSpec((1, D), lambda i: (0, 0)), _layer_spec(wg, layer)],
        out_specs=[pl.BlockSpec((tm, nsh * ns), lambda i: (i, 0)), pl.BlockSpec((tm, D), lambda i: (i, 0))],
        out_shape=[jax.ShapeDtypeStruct((T, nsh * ns), BF16), jax.ShapeDtypeStruct((T, D), BF16)],
        compiler_params=_params(1),
    )(x, gain.reshape(1, D), wg)


def _act_matmul_res(a, wg, layer, x_in, act, name):
    T, K = a.shape
    nsh, _, kq, D = wg.shape
    tm = min(T, ROW_TILE // 2)

    def body(a_ref, w_ref, x_ref, o_ref):
        w = w_ref[...].reshape(nsh * kq, D)
        o_ref[...] = x_ref[...] + jnp.dot(_act(a_ref[...], act), w, preferred_element_type=F32)

    return pl.pallas_call(
        body,
        name=name,
        grid=(T // tm,),
        in_specs=[pl.BlockSpec((tm, K), lambda i: (i, 0)), _layer_spec(wg, layer), pl.BlockSpec((tm, D), lambda i: (i, 0))],
        out_specs=pl.BlockSpec((tm, D), lambda i: (i, 0)),
        out_shape=jax.ShapeDtypeStruct((T, D), F32),
        compiler_params=_params(1),
    )(a, wg, x_in)


def _matmul_nt(g, wg, layer, a, name):
    T, D = g.shape
    nsh, _, kq, _ = wg.shape
    tm = min(T, ROW_TILE // 2)

    def body(g_ref, w_ref, *rest):
        gv = g_ref[...]
        for k in range(nsh):
            cols = slice(k * kq, (k + 1) * kq)
            r = lax.dot_general(gv, w_ref[k], NT, preferred_element_type=F32)
            if a is not None:
                r = r * (2.0 * jnp.maximum(rest[0][:, cols].astype(F32), 0.0))
            rest[-1][:, cols] = r.astype(BF16)

    row = pl.BlockSpec((tm, nsh * kq), lambda i: (i, 0))
    in_specs = [pl.BlockSpec((tm, D), lambda i: (i, 0)), _layer_spec(wg, layer)]
    args = [g, wg]
    if a is not None:
        in_specs.append(row)
        args.append(a)
    return pl.pallas_call(
        body,
        name=name,
        grid=(T // tm,),
        in_specs=in_specs,
        out_specs=row,
        out_shape=jax.ShapeDtypeStruct((T, nsh * kq), BF16),
        compiler_params=_params(1),
    )(*args)


def _matmul_nt_norm_bwd(da, wg, layer, x, gain, dres, name):
    T, D = x.shape
    nsh, _, _, ns = wg.shape
    tm = min(T, ROW_TILE // 2)

    def body(da_ref, w_ref, x_ref, g_ref, r_ref, dx_ref, dxb_ref, dg_ref):
        dh = lax.dot_general(da_ref[:, :ns], w_ref[0], NT, preferred_element_type=F32)
        for j in range(1, nsh):
            dh = dh + lax.dot_general(da_ref[:, j * ns : (j + 1) * ns], w_ref[j], NT, preferred_element_type=F32)
        dx, dg = _norm_bwd(dh, x_ref[...], g_ref[...])
        dx = dx + r_ref[...]
        dx_ref[...] = dx
        dxb_ref[...] = dx.astype(BF16)

        @pl.when(pl.program_id(0) == 0)
        def _():
            dg_ref[...] = dg

        @pl.when(pl.program_id(0) > 0)
        def _():
            dg_ref[...] += dg

    row = pl.BlockSpec((tm, D), lambda i: (i, 0))
    vec = pl.BlockSpec((1, D), lambda i: (0, 0))
    return pl.pallas_call(
        body,
        name=name,
        grid=(T // tm,),
        in_specs=[pl.BlockSpec((tm, nsh * ns), lambda i: (i, 0)), _layer_spec(wg, layer), row, vec, row],
        out_specs=[row, row, vec],
        out_shape=[
            jax.ShapeDtypeStruct((T, D), F32),
            jax.ShapeDtypeStruct((T, D), BF16),
            jax.ShapeDtypeStruct((1, D), F32),
        ],
        compiler_params=_params(1),
    )(da, wg, x, gain.reshape(1, D), dres)


def _matmul_tn(lhs, rhs, bufs, layer, n_layers, shard_lhs, act, name):
    T = lhs.shape[0]
    rows = lhs.shape[1] // N_CHIPS if shard_lhs else lhs.shape[1]
    cols = rhs.shape[1] if shard_lhs else rhs.shape[1] // N_CHIPS
    tt = min(T, 2 * ROW_TILE)
    n_t = T // tt

    def body(l_ref, r_ref, *rest):
        o32_ref, o16_ref = rest[-2:]
        t = pl.program_id(1)
        upd = lax.dot_general(_act(l_ref[...], act), r_ref[...].astype(BF16), TN, preferred_element_type=F32)

        @pl.when(t == 0)
        def _():
            o32_ref[...] = upd

        @pl.when(t > 0)
        def _():
            o32_ref[...] += upd

        @pl.when(t == n_t - 1)
        def _():
            o16_ref[...] = o32_ref[...].astype(BF16)

    if shard_lhs:
        in_specs = [pl.BlockSpec((tt, rows), lambda s, t: (t, s)), pl.BlockSpec((tt, cols), lambda s, t: (t, 0))]
    else:
        in_specs = [pl.BlockSpec((tt, rows), lambda s, t: (t, 0)), pl.BlockSpec((tt, cols), lambda s, t: (t, s))]
    args = [lhs, rhs]
    aliases = {}
    if bufs is not None:
        in_specs += [pl.BlockSpec(memory_space=pl.ANY)] * 2
        args += list(bufs)
        aliases = {2: 0, 3: 1}
    shape = (N_CHIPS, n_layers, rows, cols)
    return pl.pallas_call(
        body,
        name=name,
        grid=(N_CHIPS, n_t),
        in_specs=in_specs,
        out_specs=[pl.BlockSpec((None, None, rows, cols), lambda s, t: (s, layer, 0, 0))] * 2,
        out_shape=[jax.ShapeDtypeStruct(shape, F32), jax.ShapeDtypeStruct(shape, BF16)],
        input_output_aliases=aliases,
        compiler_params=_params(2),
    )(*args)


def _chip_peers(x, y):
    return [(1 - x, y), (x, 1 - y), (1 - x, 1 - y)]


def _remote(src, dst, sems, s, peer):
    return pltpu.make_async_remote_copy(
        src_ref=src, dst_ref=dst, send_sem=sems[0].at[s], recv_sem=sems[1].at[s], device_id=peer, device_id_type=MESH
    )


class _Exchange:
    def __init__(self, operands, n_alias, new_shapes, n_sems, build):
        self.operands, self.n_alias, self.new_shapes, self.n_sems, self.build = operands, n_alias, new_shapes, n_sems, build

    def out_shapes(self):
        return [jax.ShapeDtypeStruct(a.shape, a.dtype) for a in self.operands[: self.n_alias]] + list(self.new_shapes)

    def scratch(self):
        return [pltpu.SemaphoreType.DMA((n,)) for n in self.n_sems]

    def run(self, ins, outs, sems, first, last):
        starts, recvs, sends, locals_ = self.build(ins, outs, sems)

        def start_all():
            for cp in starts:
                cp.start()

        def wait_all():
            for cp in recvs:
                cp.wait_recv()
            for cp in sends:
                cp.wait_send()
            for cp in locals_:
                cp.wait()

        if first is True:
            start_all()
            return wait_all
        pl.when(first)(start_all)
        return lambda: pl.when(last)(wait_all)


def _gather_exchange(shards, bufs, plan):
    n_arr = len(shards)
    n_cp = len(plan) * (N_CHIPS - 1)

    def build(ins, outs, sems):
        shard_refs = ins[-n_arr:]
        x, y, c = lax.axis_index("x"), lax.axis_index("y"), lax.axis_index("c")
        me = 2 * x + y
        starts, recvs, sends, locals_ = [], [], [], []
        for p, (a, l0, n) in enumerate(plan):
            src = shard_refs[a].at[pl.ds(l0, n)]
            cp = pltpu.make_async_copy(src, outs[a].at[me, pl.ds(l0, n)], sems[2].at[p])
            locals_.append(cp)
            for k, (px, py) in enumerate(_chip_peers(x, y)):
                s = p * (N_CHIPS - 1) + k
                sends.append(_remote(src, outs[a].at[me, pl.ds(l0, n)], sems, s, (px, py, c)))
                recvs.append(_remote(src, outs[a].at[2 * px + py, pl.ds(l0, n)], sems, s, (px, py, c)))
        return locals_ + sends, recvs, sends, locals_

    if bufs is None:
        new = [jax.ShapeDtypeStruct((N_CHIPS,) + s.shape, s.dtype) for s in shards]
        return _Exchange(list(shards), 0, new, [n_cp, n_cp, len(plan)], build)
    return _Exchange(list(bufs) + list(shards), n_arr, [], [n_cp, n_cp, len(plan)], build)


def _scatter_exchange(g16, recv, plan, small=None):
    n_arr = len(g16)
    have = [r for r in recv if r is not None]
    made = [a for a in range(n_arr) if recv[a] is None]
    n_cp = len(plan) * (N_CHIPS - 1) + (N_DEV - 1 if small is not None else 0)

    def build(ins, outs, sems):
        g_refs = ins[:n_arr]
        recv_refs, it_have, it_made = [], iter(outs[n_arr : n_arr + len(have)]), iter(outs[n_arr + len(have) :])
        for a in range(n_arr):
            recv_refs.append(next(it_made) if recv[a] is None else next(it_have))
        x, y, c = lax.axis_index("x"), lax.axis_index("y"), lax.axis_index("c")
        me = 2 * x + y
        starts, recvs, sends, locals_ = [], [], [], []
        for p, (a, l0, n) in enumerate(plan):
            for k, (px, py) in enumerate(_chip_peers(x, y)):
                s = p * (N_CHIPS - 1) + k
                dst = recv_refs[a].at[k, pl.ds(l0, n)]
                sends.append(_remote(g_refs[a].at[2 * px + py, pl.ds(l0, n)], dst, sems, s, (px, py, c)))
                recvs.append(_remote(g_refs[a].at[me, pl.ds(l0, n)], dst, sems, s, (px, py, c)))
        if small is not None:
            small_ref, all_ref = ins[-1], outs[-1]
            slot = 4 * x + 2 * y + c
            locals_.append(pltpu.make_async_copy(small_ref, all_ref.at[slot], sems[2].at[0]))
            flips = [(fx, fy, fc) for fx in (0, 1) for fy in (0, 1) for fc in (0, 1)][1:]
            for k, (fx, fy, fc) in enumerate(flips):
                s = len(plan) * (N_CHIPS - 1) + k
                px, py, pc = x ^ fx, y ^ fy, c ^ fc
                sends.append(_remote(small_ref, all_ref.at[slot], sems, s, (px, py, pc)))
                recvs.append(_remote(small_ref, all_ref.at[4 * px + 2 * py + pc], sems, s, (px, py, pc)))
        return locals_ + sends, recvs, sends, locals_

    operands = list(g16) + have + ([small] if small is not None else [])
    new = [jax.ShapeDtypeStruct((N_CHIPS - 1,) + g16[a].shape[1:], BF16) for a in made]
    if small is not None:
        new.append(jax.ShapeDtypeStruct((N_DEV,) + small.shape, F32))
    return _Exchange(operands, n_arr + len(have), new, [n_cp, n_cp, 1], build)


def _call_with_exchange(body, exch, name, grid, in_specs, out_specs, out_shape, scratch_shapes, args):
    n_in, n_out, n_scr = len(in_specs), len(out_shape), len(scratch_shapes)
    if exch is None:
        outs = pl.pallas_call(
            body, name=name, grid=grid, in_specs=in_specs, out_specs=out_specs, out_shape=out_shape,
            scratch_shapes=scratch_shapes, compiler_params=_params(len(grid)),
        )(*args)
        return outs, []
    e_shapes = exch.out_shapes()
    e_in, e_out = len(exch.operands), len(e_shapes)

    def wrapped(*refs):
        ins, refs = refs[:n_in], refs[n_in:]
        e_ins, refs = refs[:e_in], refs[e_in:]
        outs, refs = refs[:n_out], refs[n_out:]
        e_outs, refs = refs[:e_out], refs[e_out:]
        scr, sems = refs[:n_scr], refs[n_scr:]
        first, last = True, True
        for d, g in enumerate(grid):
            first = (pl.program_id(d) == 0) & first
            last = (pl.program_id(d) == g - 1) & last
        finish = exch.run(e_ins, e_outs, sems, first, last)
        body(*ins, *outs, *scr)
        finish()

    any_spec = pl.BlockSpec(memory_space=pl.ANY)
    outs = pl.pallas_call(
        wrapped,
        name=name,
        grid=grid,
        in_specs=list(in_specs) + [any_spec] * e_in,
        out_specs=list(out_specs) + [any_spec] * e_out,
        out_shape=list(out_shape) + e_shapes,
        input_output_aliases={n_in + i: n_out + i for i in range(exch.n_alias)},
        scratch_shapes=list(scratch_shapes) + exch.scratch(),
        compiler_params=pltpu.CompilerParams(
            dimension_semantics=("arbitrary",) * len(grid), vmem_limit_bytes=VMEM_LIMIT, has_side_effects=True
        ),
    )(*args, *exch.operands)
    return outs[:n_out], outs[n_out:]


def _exchange_only(exch, name):
    n_in = len(exch.operands)
    shapes = exch.out_shapes()

    def body(*refs):
        ins, outs, sems = refs[:n_in], refs[n_in : n_in + len(shapes)], refs[n_in + len(shapes) :]
        exch.run(ins, outs, sems, True, True)()

    any_spec = pl.BlockSpec(memory_space=pl.ANY)
    return pl.pallas_call(
        body,
        name=name,
        in_specs=[any_spec] * n_in,
        out_specs=[any_spec] * len(shapes),
        out_shape=shapes,
        input_output_aliases={i: i for i in range(exch.n_alias)},
        scratch_shapes=exch.scratch(),
        compiler_params=pltpu.CompilerParams(has_side_effects=True),
    )(*exch.operands)


ATTN_LANE_TILES = 2
ATTN_UNROLL = 8


MASKED = -1e30


def _hi_lo(x):
    hi = x.astype(BF16)
    lo = (x - hi.astype(F32)).astype(BF16)
    return jnp.concatenate([hi, lo], axis=1)


def _suffix_matrix(inclusive):
    j = lax.broadcasted_iota(jnp.int32, (2 * Q_TILE, 2 * Q_TILE), 0) & (Q_TILE - 1)
    s = lax.broadcasted_iota(jnp.int32, (2 * Q_TILE, 2 * Q_TILE), 1)
    later = (j >= s) if inclusive else (j > s)
    return jnp.where((s >= Q_TILE) | later, 1.0, 0.0).astype(BF16)


def _log_beta(z):
    return jnp.minimum(z, 0.0) - jnp.log(1.0 + jnp.exp(-jnp.abs(z)))


def _head_masks(width):
    lane = lax.broadcasted_iota(jnp.int32, (1, width), 1)
    return [(lane >= h * HEAD_DIM) & (lane < (h + 1) * HEAD_DIM) for h in range(width // HEAD_DIM)]


def _per_head_rows(x, masks):
    return jnp.concatenate([jnp.where(hm, x, 0) for hm in masks], axis=0)


def _heads_to_lanes(x, n_heads):
    return jnp.concatenate([x[h * Q_TILE : (h + 1) * Q_TILE] for h in range(n_heads)], axis=1)


def _block_start(kb):
    return kb * Q_TILE if isinstance(kb, int) else pl.multiple_of(kb * Q_TILE, Q_TILE)


def _clamp(i, n):
    return jnp.minimum(i, n - 1)


def _next_block(pos):
    qi, kb = pos
    row_done = kb == 0
    nqi = jnp.where(row_done, qi + 1, qi)
    return nqi, jnp.where(row_done, nqi, kb - 1)


def _stream_unroll(n_blocks):
    return next(u for u in (ATTN_UNROLL, 2, 1) if n_blocks % u == 0)


def _past_mask(rows):
    t = lax.broadcasted_iota(jnp.int32, (rows, Q_TILE), 0) & (Q_TILE - 1)
    s = lax.broadcasted_iota(jnp.int32, (rows, Q_TILE), 1)
    return s < t


def _attn_fwd(qkv, n_seq, S, D, exch, name):
    T = n_seq * S
    width = min(D, ATTN_LANE_TILES * LANES)
    n_heads = width // HEAD_DIM
    rows = n_heads * Q_TILE
    nq = S // Q_TILE
    groups = D // width
    n_blocks = nq * (nq + 1) // 2
    unroll = _stream_unroll(n_blocks)
    scale = HEAD_DIM ** -0.5

    def body(q_ref, k_ref, v_ref, o_ref, qh_scr, vh_scr, bias_scr):
        masks = _head_masks(width)
        sfx = _suffix_matrix(False)

        def per_head_tables(i, c):
            blk = pl.ds(_block_start(i), Q_TILE)
            qh_scr[i] = _per_head_rows(q_ref[blk, :] * scale, masks)
            vh_scr[i] = _per_head_rows(v_ref[blk, :], masks)
            return c

        lax.fori_loop(0, nq, per_head_tables, 0)
        bias_scr[0] = jnp.zeros((rows, Q_TILE), F32)
        bias_scr[1] = jnp.where(_past_mask(rows), 0.0, MASKED)

        def scores(pos):
            qi, kb = pos
            kt = k_ref[pl.ds(_block_start(_clamp(kb, nq)), Q_TILE), :]
            z = lax.dot_general(qh_scr[_clamp(qi, nq)], kt, NT, preferred_element_type=F32)
            z = z + bias_scr[(kb == qi).astype(jnp.int32)]
            lb = _log_beta(z)
            return lb, _hi_lo(lb - z)

        def weigh(pos, st, carry, acc):
            qi, kb = pos
            lb, l1 = st
            r = jnp.dot(l1, sfx, preferred_element_type=F32)
            carry = jnp.where(kb == qi, 0.0, carry)
            a = _heads_to_lanes(jnp.exp(lb + r[:, :Q_TILE] + carry).astype(BF16), n_heads)
            acc = jnp.where(kb == qi, 0.0, acc) + jnp.dot(a, vh_scr[_clamp(kb, nq)], preferred_element_type=F32)
            o_ref[pl.ds(_block_start(_clamp(qi, nq)), Q_TILE), :] = acc
            return carry + r[:, Q_TILE:], acc

        def trip(n, c):
            pos, st, carry, acc = c
            for _ in range(unroll):
                nxt = _next_block(pos)
                st_nxt = scores(nxt)
                carry, acc = weigh(pos, st, carry, acc)
                pos, st = nxt, st_nxt
            return pos, st, carry, acc

        first = (jnp.int32(0), jnp.int32(0))
        zero = bias_scr[0]
        init = (first, scores(first), zero, jnp.concatenate([zero[:Q_TILE]] * (width // Q_TILE), axis=1))
        lax.fori_loop(0, n_blocks // unroll, trip, init)

    seq = lambda col0: pl.BlockSpec((S, width), lambda b, p: (b, col0 + p))
    (o,), moved = _call_with_exchange(
        body,
        exch,
        name,
        grid=(n_seq, groups),
        in_specs=[seq(0), seq(groups), seq(2 * groups)],
        out_specs=[seq(0)],
        out_shape=[jax.ShapeDtypeStruct((T, D), F32)],
        scratch_shapes=[
            pltpu.VMEM((nq, rows, width), BF16),
            pltpu.VMEM((nq, rows, width), BF16),
            pltpu.VMEM((2, rows, Q_TILE), F32),
        ],
        args=(qkv, qkv, qkv),
    )
    return o, moved


def _attn_bwd(qkv, o, do, n_seq, S, D, exch, name):
    T = n_seq * S
    width = min(D, ATTN_LANE_TILES * LANES)
    n_heads = width // HEAD_DIM
    rows = n_heads * Q_TILE
    nq = S // Q_TILE
    groups = D // width
    n_blocks = nq * (nq + 1) // 2
    unroll = _stream_unroll(n_blocks)
    scale = HEAD_DIM ** -0.5

    def body(q_ref, k_ref, v_ref, o_ref, do_ref, dq_ref, dk_ref, dv_ref, dk_acc, dv_acc, qh_scr, doh_scr, delta_scr, bias_scr):
        masks = _head_masks(width)
        sfx = _suffix_matrix(False)
        sfx_incl = _suffix_matrix(True)
        dk_acc[...] = jnp.zeros_like(dk_acc)
        dv_acc[...] = jnp.zeros_like(dv_acc)

        def per_head_tables(i, c):
            blk = pl.ds(_block_start(i), Q_TILE)
            do = do_ref[blk, :]
            qh_scr[i] = _per_head_rows(q_ref[blk, :] * scale, masks)
            doh_scr[i] = _per_head_rows(do, masks)
            prod = do.astype(F32) * o_ref[blk, :]
            delta = jnp.concatenate(
                [jnp.sum(jnp.where(hm, prod, 0.0), axis=-1, keepdims=True) for hm in masks], axis=0
            )
            delta_scr[i] = jnp.broadcast_to(delta, (rows, Q_TILE))
            return c

        lax.fori_loop(0, nq, per_head_tables, 0)
        bias_scr[0] = jnp.zeros((rows, Q_TILE), F32)
        bias_scr[1] = jnp.where(_past_mask(rows), 0.0, MASKED)

        def scores(pos):
            qi, kb = pos
            blk = pl.ds(_block_start(_clamp(kb, nq)), Q_TILE)
            z = lax.dot_general(qh_scr[_clamp(qi, nq)], k_ref[blk, :], NT, preferred_element_type=F32)
            z = z + bias_scr[(kb == qi).astype(jnp.int32)]
            lb = _log_beta(z)
            return lb, _hi_lo(lb - z)

        def weigh(pos, st, c1, c2, dq):
            qi, kb = pos
            lb, l1 = st
            first = kb == qi
            blk = pl.ds(_block_start(kb), Q_TILE)
            da = lax.dot_general(doh_scr[qi], v_ref[blk, :], NT, preferred_element_type=F32)
            r = jnp.dot(l1, sfx, preferred_element_type=F32)
            c1 = jnp.where(first, 0.0, c1)
            ab = jnp.exp(lb + r[:, :Q_TILE] + c1).astype(BF16)
            g = ab.astype(F32) * da
            r2 = jnp.dot(_hi_lo(g), sfx_incl, preferred_element_type=F32)
            c2 = jnp.where(first, 0.0, c2)
            earlier = delta_scr[qi] - (r2[:, :Q_TILE] + c2)
            beta = jnp.exp(lb)
            dzb = (g * (1.0 - beta) - earlier * beta).astype(BF16)
            kh = _per_head_rows(k_ref[blk, :], masks)
            dq = jnp.where(first, 0.0, dq) + jnp.dot(_heads_to_lanes(dzb, n_heads), kh, preferred_element_type=F32)
            dq_ref[pl.ds(_block_start(qi), Q_TILE), :] = (dq * scale).astype(BF16)
            dk_acc[blk, :] += lax.dot_general(dzb, qh_scr[qi], TN, preferred_element_type=F32)
            dv_acc[blk, :] += lax.dot_general(ab, doh_scr[qi], TN, preferred_element_type=F32)
            return c1 + r[:, Q_TILE:], c2 + r2[:, Q_TILE:], dq

        def trip(n, c):
            pos, st, c1, c2, dq = c
            for _ in range(unroll):
                nxt = _next_block(pos)
                st_nxt = scores(nxt)
                c1, c2, dq = weigh(pos, st, c1, c2, dq)
                pos, st = nxt, st_nxt
            return pos, st, c1, c2, dq

        first = (jnp.int32(0), jnp.int32(0))
        zero = bias_scr[0]
        init = (first, scores(first), zero, zero, jnp.concatenate([zero[:Q_TILE]] * (width // Q_TILE), axis=1))
        lax.fori_loop(0, n_blocks // unroll, trip, init)
        dk_ref[...] = dk_acc[...].astype(BF16)
        dv_ref[...] = dv_acc[...].astype(BF16)

    seq = lambda col0: pl.BlockSpec((S, width), lambda b, p: (b, col0 + p))
    return _call_with_exchange(
        body,
        exch,
        name,
        grid=(n_seq, groups),
        in_specs=[seq(0), seq(groups), seq(2 * groups), seq(0), seq(0)],
        out_specs=[seq(0)] * 3,
        out_shape=[jax.ShapeDtypeStruct((T, D), BF16)] * 3,
        scratch_shapes=[
            pltpu.VMEM((S, width), F32),
            pltpu.VMEM((S, width), F32),
            pltpu.VMEM((nq, rows, width), BF16),
            pltpu.VMEM((nq, rows, width), BF16),
            pltpu.VMEM((nq, rows, Q_TILE), F32),
            pltpu.VMEM((2, rows, Q_TILE), F32),
        ],
        args=(qkv, qkv, qkv, o, do),
    )


def _causal_ws(ws_ref, g):
    t = lax.broadcasted_iota(jnp.int32, (SGU_CHUNK, SGU_CHUNK), 0)
    s = lax.broadcasted_iota(jnp.int32, (SGU_CHUNK, SGU_CHUNK), 1)
    return jnp.where(s <= t, ws_ref[g], 0.0)


def _sgu_fwd(a, gain, ws, bsb, name):
    T, F2 = a.shape
    F = F2 // 2
    gw = F // SGU_GROUPS

    def body(a_ref, gain_ref, ws_ref, bsb_ref, y_ref):
        v = _gelu(a_ref[:, F:].astype(F32))
        vn = (v * _rstd(v) * gain_ref[...]).astype(BF16)
        for g in range(SGU_GROUPS):
            cs = slice(g * gw, (g + 1) * gw)
            w = _causal_ws(ws_ref, g).astype(BF16)
            mixed = jnp.dot(w, vn[:, cs], preferred_element_type=F32) + bsb_ref[g]
            y_ref[:, cs] = (_gelu(a_ref[:, cs].astype(F32)) * mixed).astype(BF16)

    return pl.pallas_call(
        body,
        name=name,
        grid=(T // SGU_CHUNK,),
        in_specs=[
            pl.BlockSpec((SGU_CHUNK, F2), lambda i: (i, 0)),
            pl.BlockSpec((1, F), lambda i: (0, 0)),
            pl.BlockSpec((SGU_GROUPS, SGU_CHUNK, SGU_CHUNK), lambda i: (0, 0, 0)),
            pl.BlockSpec((SGU_GROUPS, SGU_CHUNK, gw), lambda i: (0, 0, 0)),
        ],
        out_specs=pl.BlockSpec((SGU_CHUNK, F), lambda i: (i, 0)),
        out_shape=jax.ShapeDtypeStruct((T, F), BF16),
        compiler_params=_params(1),
    )(a, gain.reshape(1, F), ws, bsb)


def _sgu_bwd(a, dy, gain, ws, bsb, name):
    T, F2 = a.shape
    F = F2 // 2
    gw = F // SGU_GROUPS

    def body(a_ref, dy_ref, gain_ref, ws_ref, bsb_ref, da_ref, dws_ref, dbs_ref, dgain_ref, dvn_ref):
        @pl.when(pl.program_id(0) == 0)
        def _():
            dws_ref[...] = jnp.zeros_like(dws_ref)
            dbs_ref[...] = jnp.zeros_like(dbs_ref)
            dgain_ref[...] = jnp.zeros_like(dgain_ref)

        av = a_ref[:, F:].astype(F32)
        v = _gelu(av)
        rstd = _rstd(v)
        vh = v * rstd
        gain = gain_ref[...]
        vn = (vh * gain).astype(BF16)
        ones = jnp.ones((gw, SGU_CHUNK), BF16)
        for g in range(SGU_GROUPS):
            cs = slice(g * gw, (g + 1) * gw)
            w = _causal_ws(ws_ref, g).astype(BF16)
            mixed = jnp.dot(w, vn[:, cs], preferred_element_type=F32) + bsb_ref[g]
            au = a_ref[:, cs].astype(F32)
            dyc = dy_ref[:, cs].astype(F32)
            da_ref[:, cs] = (dyc * mixed * _gelu_grad(au)).astype(BF16)
            dm = (dyc * _gelu(au)).astype(BF16)
            dbs_ref[g] += jnp.dot(dm, ones, preferred_element_type=F32)
            dws_ref[g] += _causal_mask_f32(lax.dot_general(dm, vn[:, cs], NT, preferred_element_type=F32))
            dvn_ref[:, cs] = lax.dot_general(w, dm, TN, preferred_element_type=F32)
        dvn = dvn_ref[...]
        dgain_ref[...] += jnp.sum(dvn * vh, axis=0, keepdims=True)
        dvh = dvn * gain
        dv = rstd * (dvh - vh * jnp.mean(dvh * vh, axis=-1, keepdims=True))
        da_ref[:, F:] = (dv * _gelu_grad(av)).astype(BF16)

    acc_spec = pl.BlockSpec((SGU_GROUPS, SGU_CHUNK, SGU_CHUNK), lambda i: (0, 0, 0))
    acc_shape = jax.ShapeDtypeStruct((SGU_GROUPS, SGU_CHUNK, SGU_CHUNK), F32)
    return pl.pallas_call(
        body,
        name=name,
        grid=(T // SGU_CHUNK,),
        in_specs=[
            pl.BlockSpec((SGU_CHUNK, F2), lambda i: (i, 0)),
            pl.BlockSpec((SGU_CHUNK, F), lambda i: (i, 0)),
            pl.BlockSpec((1, F), lambda i: (0, 0)),
            acc_spec,
            pl.BlockSpec((SGU_GROUPS, SGU_CHUNK, gw), lambda i: (0, 0, 0)),
        ],
        out_specs=[
            pl.BlockSpec((SGU_CHUNK, F2), lambda i: (i, 0)),
            acc_spec,
            acc_spec,
            pl.BlockSpec((1, F), lambda i: (0, 0)),
        ],
        out_shape=[
            jax.ShapeDtypeStruct((T, F2), BF16),
            acc_shape,
            acc_shape,
            jax.ShapeDtypeStruct((1, F), F32),
        ],
        scratch_shapes=[pltpu.VMEM((SGU_CHUNK, F), F32)],
        compiler_params=_params(1),
    )(a, dy, gain.reshape(1, F), ws, bsb)


def _causal_mask_f32(m):
    t = lax.broadcasted_iota(jnp.int32, m.shape, 0)
    s = lax.broadcasted_iota(jnp.int32, m.shape, 1)
    return jnp.where(s <= t, m, 0.0)


def _final_loss(x, gain, target, name):
    T, D = x.shape
    tm = min(T, ROW_TILE // 2)

    def body(x_ref, g_ref, t_ref, sq_ref, dx_ref, dxb_ref, dg_ref):
        xv = x_ref[...]
        gain = g_ref[...]
        err = xv * _rstd(xv) * gain - t_ref[...]
        dx, dg = _norm_bwd(err * (1.0 / D), xv, gain)
        dx_ref[...] = dx
        dxb_ref[...] = dx.astype(BF16)
        sq = jnp.sum(err * err, axis=0, keepdims=True)

        @pl.when(pl.program_id(0) == 0)
        def _():
            sq_ref[...] = sq
            dg_ref[...] = dg

        @pl.when(pl.program_id(0) > 0)
        def _():
            sq_ref[...] += sq
            dg_ref[...] += dg

    row = pl.BlockSpec((tm, D), lambda i: (i, 0))
    vec = pl.BlockSpec((1, D), lambda i: (0, 0))
    return pl.pallas_call(
        body,
        name=name,
        grid=(T // tm,),
        in_specs=[row, vec, row],
        out_specs=[vec, row, row, vec],
        out_shape=[
            jax.ShapeDtypeStruct((1, D), F32),
            jax.ShapeDtypeStruct((T, D), F32),
            jax.ShapeDtypeStruct((T, D), BF16),
            jax.ShapeDtypeStruct((1, D), F32),
        ],
        compiler_params=_params(1),
    )(x, gain.reshape(1, D), target)


def _row_tile(rows, cols, n_arrays):
    budget = VMEM_LIMIT // 2 // (2 * n_arrays * cols * 4)
    tr = rows
    while tr > budget and tr % 16 == 0:
        tr //= 2
    return tr


def _sum_received(own, recv, name):
    R, C = own.shape
    n = recv.shape[0]
    tr = _row_tile(R, C, n + 2)

    def body(own_ref, recv_ref, o_ref):
        s = own_ref[...]
        for k in range(n):
            s = s + recv_ref[k].astype(F32)
        o_ref[...] = s

    return pl.pallas_call(
        body,
        name=name,
        grid=(R // tr,),
        in_specs=[pl.BlockSpec((tr, C), lambda i: (i, 0)), pl.BlockSpec((n, tr, C), lambda i: (0, i, 0))],
        out_specs=pl.BlockSpec((tr, C), lambda i: (i, 0)),
        out_shape=jax.ShapeDtypeStruct((R, C), F32),
        compiler_params=_params(1),
    )(own, recv)


def _sum_chip_shard(g32, recv, chip, name):
    _, L, r, c = g32.shape
    n = recv.shape[0]
    tr = _row_tile(r, c, n + 2)

    def body(chip_ref, own_ref, recv_ref, o_ref):
        s = own_ref[...]
        for k in range(n):
            s = s + recv_ref[k].astype(F32)
        o_ref[...] = s

    return pl.pallas_call(
        body,
        name=name,
        grid_spec=pltpu.PrefetchScalarGridSpec(
            num_scalar_prefetch=1,
            grid=(L, r // tr),
            in_specs=[
                pl.BlockSpec((None, None, tr, c), lambda l, i, chip_ref: (chip_ref[0], l, i, 0)),
                pl.BlockSpec((n, None, tr, c), lambda l, i, chip_ref: (0, l, i, 0)),
            ],
            out_specs=pl.BlockSpec((None, tr, c), lambda l, i, chip_ref: (l, i, 0)),
        ),
        out_shape=jax.ShapeDtypeStruct((L, r, c), F32),
        compiler_params=_params(2),
    )(chip.reshape(1).astype(jnp.int32), g32, recv)


def _adamw(w, m, v, parts, name):
    R, C = w.shape
    n = len(parts)
    tr = _row_tile(R, C, n + 7)

    def body(*refs):
        w_ref, m_ref, v_ref = refs[:3]
        g_ref, d_ref, nm_ref, nv_ref = refs[3 + n :]
        g = refs[3][...]
        for p_ref in refs[4 : 3 + n]:
            g = g + p_ref[...]
        nm = ADAM_B1 * m_ref[...] + (1.0 - ADAM_B1) * g
        nv = ADAM_B2 * v_ref[...] + (1.0 - ADAM_B2) * (g * g)
        m_hat = nm / (1.0 - ADAM_B1**ADAM_STEP)
        v_hat = nv / (1.0 - ADAM_B2**ADAM_STEP)
        g_ref[...] = g
        d_ref[...] = -ADAM_LR * (m_hat / (jnp.sqrt(v_hat) + ADAM_EPS) + ADAM_WD * w_ref[...])
        nm_ref[...] = nm
        nv_ref[...] = nv

    spec = pl.BlockSpec((tr, C), lambda i: (i, 0))
    return pl.pallas_call(
        body,
        name=name,
        grid=(R // tr,),
        in_specs=[spec] * (3 + n),
        out_specs=[spec] * 4,
        out_shape=[jax.ShapeDtypeStruct((R, C), F32)] * 4,
        compiler_params=_params(1),
    )(w, m, v, *parts)


def _swap_with_sibling(parts, name):
    n = len(parts)

    def body(*refs):
        ins, outs = refs[:n], refs[n : 2 * n]
        send_sems, recv_sems = refs[2 * n :]
        sibling = (lax.axis_index("x"), lax.axis_index("y"), 1 - lax.axis_index("c"))
        copies = [
            pltpu.make_async_remote_copy(
                src_ref=ins[a],
                dst_ref=outs[a],
                send_sem=send_sems.at[a],
                recv_sem=recv_sems.at[a],
                device_id=sibling,
                device_id_type=MESH,
            )
            for a in range(n)
        ]
        for cp in copies:
            cp.start()
        for cp in copies:
            cp.wait_recv()
        for cp in copies:
            cp.wait_send()

    any_spec = pl.BlockSpec(memory_space=pl.ANY)
    return pl.pallas_call(
        body,
        name=name,
        in_specs=[any_spec] * n,
        out_specs=[any_spec] * n,
        out_shape=[jax.ShapeDtypeStruct(p.shape, p.dtype) for p in parts],
        scratch_shapes=[pltpu.SemaphoreType.DMA((n,)), pltpu.SemaphoreType.DMA((n,))],
        compiler_params=pltpu.CompilerParams(has_side_effects=True),
    )(*parts)


def _pack(pieces):
    flat = jnp.concatenate([p.reshape(-1) for p in pieces])
    return flat.reshape(-1, LANES)


def _unpack(packed, shapes):
    flat = packed.reshape(-1)
    out, off = [], 0
    for s in shapes:
        size = 1
        for d in s:
            size *= d
        out.append(flat[off : off + size].reshape(s))
        off += size
    return out


def kernel(x, norm_mix, norm_mlp, sb_wqkv, sb_wo, sgu_win, sgu_gain, sgu_ws, sgu_bs, sgu_wout, mlp_w1, mlp_w2, final_norm, loss_target, m_norm_mix, m_norm_mlp, m_sb_wqkv, m_sb_wo, m_sgu_win, m_sgu_gain, m_sgu_ws, m_sgu_bs, m_sgu_wout, m_mlp_w1, m_mlp_w2, m_final_norm, v_norm_mix, v_norm_mlp, v_sb_wqkv, v_sb_wo, v_sgu_win, v_sgu_gain, v_sgu_ws, v_sgu_bs, v_sgu_wout, v_mlp_w1, v_mlp_w2, v_final_norm):
    n_seq, S, D = x.shape
    T = n_seq * S
    depth = norm_mix.shape[0]
    n_sgu = sgu_win.shape[0]
    F = sgu_wout.shape[1] * N_CHIPS
    gw = F // SGU_GROUPS
    chip = 2 * lax.axis_index("x") + lax.axis_index("y")

    QKV, WO, WIN, WOUT, W1, W2, GAIN = range(7)
    big = [sb_wqkv, sb_wo, sgu_win, sgu_wout, mlp_w1, mlp_w2]
    n_sb = sb_wqkv.shape[0]
    shards = [w.astype(BF16) for w in big] + [sgu_gain.reshape(1, -1, LANES)]
    def gather_plan(i):
        j, mlp = i // 2, min(2, depth - i)
        plan = [(WO, j, 1), (W1, i, mlp), (W2, i, mlp)]
        if i + 1 < depth:
            plan += [(WIN, (i + 1) // 2, 1), (WOUT, (i + 1) // 2, 1)]
        if j + 1 < n_sb:
            plan += [(QKV, j + 1, 1)]
        return plan

    wg = _exchange_only(_gather_exchange(shards, None, [(QKV, 0, 1), (GAIN, 0, 1)]), "gather_first_weights")
    gain_full = jnp.transpose(wg[GAIN].reshape(N_CHIPS, n_sgu, F // N_CHIPS), (1, 0, 2)).reshape(n_sgu, F)
    bsb = [jnp.broadcast_to(sgu_bs[j][:, :, None], (SGU_GROUPS, SGU_CHUNK, gw)) for j in range(n_sgu)]

    xs = x.reshape(T, D)
    saved = []
    for i in range(depth):
        j = i // 2
        if i % 2 == 0:
            qkv, h = _norm_matmul(xs, norm_mix[i], wg[QKV], j, f"qkv_fwd_{i}")
            o, wg = _attn_fwd(qkv, n_seq, S, D, _gather_exchange(shards, wg, gather_plan(i)), f"attn_fwd_{i}")
            wg_qkv, wg_wo, wg_win, wg_wout, wg_w1, wg_w2 = wg[:6]
            x_mid = _act_matmul_res(o, wg_wo, j, xs, None, f"wo_fwd_{i}")
            mix = (qkv, o)
        else:
            a, h = _norm_matmul(xs, norm_mix[i], wg_win, j, f"win_fwd_{i}")
            yg = _sgu_fwd(a, gain_full[j], sgu_ws[j], bsb[j], f"sgu_fwd_{i}")
            x_mid = _act_matmul_res(yg, wg_wout, j, xs, None, f"wout_fwd_{i}")
            mix = (a, yg)
        a2, h2 = _norm_matmul(x_mid, norm_mlp[i], wg_w1, i, f"w1_fwd_{i}")
        x_out = _act_matmul_res(a2, wg_w2, i, x_mid, "relu2", f"w2_fwd_{i}")
        saved.append((xs, h, mix, x_mid, h2, a2))
        xs = x_out

    sq, dx, dxb, g_final = _final_loss(xs, final_norm, loss_target.reshape(T, D), "loss_head")
    loss = lax.psum(0.5 * jnp.sum(sq) / D, ("x", "y", "c"))

    n_layers = [n_sb, n_sb, n_sgu, n_sgu, depth, depth]
    g32, g16, recv = [None] * 6, [None] * 6, [None] * 6
    done_from, sent_from = list(n_layers), list(n_layers)

    def grad(a, layer, lhs, rhs, shard_lhs, act, name):
        bufs = None if g32[a] is None else (g32[a], g16[a])
        g32[a], g16[a] = _matmul_tn(lhs, rhs, bufs, layer, n_layers[a], shard_lhs, act, name)
        done_from[a] = layer

    def unsent_plan():
        plan = [(a, done_from[a], sent_from[a] - done_from[a]) for a in range(6) if sent_from[a] > done_from[a]]
        for a, l0, _ in plan:
            sent_from[a] = l0
        return plan

    def scatter(plan, small):
        arrays = sorted({a for a, _, _ in plan})
        have = [a for a in arrays if recv[a] is not None]
        made = [a for a in arrays if recv[a] is None]
        exch = _scatter_exchange(
            [g16[a] for a in arrays], [recv[a] for a in arrays], [(arrays.index(a), l0, n) for a, l0, n in plan], small
        )

        def take(moved):
            for a, buf in zip(arrays, moved):
                g16[a] = buf
            for a, buf in zip(have + made, moved[len(arrays) :]):
                recv[a] = buf
            return moved[-1]

        return exch, take

    g_mix, g_mlp = [None] * depth, [None] * depth
    g_ws, g_bs, g_gain = [None] * n_sgu, [None] * n_sgu, [None] * n_sgu
    for i in reversed(range(depth)):
        j = i // 2
        x_in, h, mix, x_mid, h2, a2 = saved[i]
        da2 = _matmul_nt(dxb, wg_w2, i, a2, f"w2_bwd_{i}")
        grad(W2, i, a2, dxb, True, "relu2", f"w2_grad_{i}")
        grad(W1, i, h2, da2, False, None, f"w1_grad_{i}")
        dx, dxb, g_mlp[i] = _matmul_nt_norm_bwd(da2, wg_w1, i, x_mid, norm_mlp[i], dx, f"w1_bwd_{i}")
        if i % 2 == 0:
            qkv, o = mix
            do = _matmul_nt(dxb, wg_wo, j, None, f"wo_bwd_{i}")
            grad(WO, j, o, dxb, True, None, f"wo_grad_{i}")
            exch, take = scatter(unsent_plan(), None)
            (dq, dk, dv), moved = _attn_bwd(qkv, o, do, n_seq, S, D, exch, f"attn_bwd_{i}")
            take(moved)
            dqkv = jnp.concatenate([dq, dk, dv], axis=1)
            grad(QKV, j, h, dqkv, False, None, f"qkv_grad_{i}")
            dx, dxb, g_mix[i] = _matmul_nt_norm_bwd(dqkv, wg_qkv, j, x_in, norm_mix[i], dx, f"qkv_bwd_{i}")
        else:
            a, yg = mix
            dyg = _matmul_nt(dxb, wg_wout, j, None, f"wout_bwd_{i}")
            grad(WOUT, j, yg, dxb, True, None, f"wout_grad_{i}")
            da, g_ws[j], dbs, g_gain[j] = _sgu_bwd(a, dyg, gain_full[j], sgu_ws[j], bsb[j], f"sgu_bwd_{i}")
            g_bs[j] = dbs[:, :, 0]
            grad(WIN, j, h, da, False, None, f"win_grad_{i}")
            dx, dxb, g_mix[i] = _matmul_nt_norm_bwd(da, wg_win, j, x_in, norm_mix[i], dx, f"win_bwd_{i}")
    grad_x = dx.reshape(n_seq, S, D)

    names = ["qkv", "wo", "win", "wout", "w1", "w2"]
    small_shapes = [norm_mix.shape, norm_mlp.shape, final_norm.shape, sgu_ws.shape, sgu_bs.shape, (n_sgu, F)]
    small = _pack(
        [jnp.stack(g_mix), jnp.stack(g_mlp), g_final, jnp.stack(g_ws), jnp.stack(g_bs), jnp.stack(g_gain)]
    )
    exch, take = scatter(unsent_plan(), small)
    small_all = take(_exchange_only(exch, "scatter_last_grads"))
    partial = [_sum_chip_shard(g32[a], recv[a], chip, f"sum_{names[a]}") for a in range(6)]
    partial = [p.reshape(-1, p.shape[-1]) for p in partial]
    theirs = _swap_with_sibling(partial, "swap_partial_sums")
    small_sum = _sum_received(small_all[0], small_all[1:], "sum_small")

    ms = [m_sb_wqkv, m_sb_wo, m_sgu_win, m_sgu_wout, m_mlp_w1, m_mlp_w2]
    vs = [v_sb_wqkv, v_sb_wo, v_sgu_win, v_sgu_wout, v_mlp_w1, v_mlp_w2]
    res = {}
    keys = ["sb_wqkv", "sb_wo", "sgu_win", "sgu_wout", "mlp_w1", "mlp_w2"]
    for key, k, w, m, v, mine, other in zip(keys, names, big, ms, vs, partial, theirs):
        cols = w.shape[-1]
        outs = _adamw(w.reshape(-1, cols), m.reshape(-1, cols), v.reshape(-1, cols), [mine, other], f"adamw_{k}")
        res[key] = [o.reshape(w.shape) for o in outs]

    g_small = _unpack(small_sum, small_shapes)
    g_small[5] = lax.dynamic_slice_in_dim(g_small[5], chip * (F // N_CHIPS), F // N_CHIPS, axis=1)
    small_keys = ["norm_mix", "norm_mlp", "final_norm", "sgu_ws", "sgu_bs", "sgu_gain"]
    small_w = [norm_mix, norm_mlp, final_norm, sgu_ws, sgu_bs, sgu_gain]
    small_m = [m_norm_mix, m_norm_mlp, m_final_norm, m_sgu_ws, m_sgu_bs, m_sgu_gain]
    small_v = [v_norm_mix, v_norm_mlp, v_final_norm, v_sgu_ws, v_sgu_bs, v_sgu_gain]
    outs = _adamw(_pack(small_w), _pack(small_m), _pack(small_v), [_pack(g_small)], "adamw_small")
    local_shapes = [w.shape for w in small_w]
    for key, parts in zip(small_keys, zip(*[_unpack(o, local_shapes) for o in outs])):
        res[key] = list(parts)

    order = ["norm_mix", "norm_mlp", "sb_wqkv", "sb_wo", "sgu_win", "sgu_gain", "sgu_ws", "sgu_bs", "sgu_wout", "mlp_w1", "mlp_w2", "final_norm"]
    return (loss, grad_x, *[res[k][0] for k in order], *[res[k][1] for k in order], *[res[k][2] for k in order], *[res[k][3] for k in order])
```

```python
import jax
import jax.numpy as jnp
from jax import lax
from jax.experimental import pallas as pl
from jax.experimental.pallas import tpu as pltpu

F32 = jnp.float32
BF16 = jnp.bfloat16
MESH = pl.DeviceIdType.MESH

EPS = 1e-6
HEAD_DIM = 64
LANES = 128
Q_TILE = 128
SGU_CHUNK = 128
SGU_GROUPS = 8
N_CHIPS = 4
N_DEV = 8
ADAM_LR = 0.001
ADAM_B1 = 0.9
ADAM_B2 = 0.999
ADAM_EPS = 1e-08
ADAM_WD = 0.01
ADAM_STEP = 10
GELU_C0 = 0.7978845608028654
GELU_C1 = 0.044715
VMEM_LIMIT = 48 * 1024 * 1024
ROW_TILE = 1024
NT = (((1,), (1,)), ((), ()))
TN = (((0,), (0,)), ((), ()))


def _params(n_axes):
    return pltpu.CompilerParams(dimension_semantics=("arbitrary",) * n_axes, vmem_limit_bytes=VMEM_LIMIT)


def _rstd(x):
    return lax.rsqrt(jnp.mean(x * x, axis=-1, keepdims=True) + EPS)


def _norm_bwd(dh, x, gain):
    rstd = _rstd(x)
    xh = x * rstd
    dhg = dh * gain
    dx = rstd * (dhg - xh * jnp.mean(dhg * xh, axis=-1, keepdims=True))
    return dx, jnp.sum(dh * xh, axis=0, keepdims=True)


def _gelu(x):
    return 0.5 * x * (1.0 + jnp.tanh(GELU_C0 * (x + GELU_C1 * x * x * x)))


def _gelu_grad(x):
    t = jnp.tanh(GELU_C0 * (x + GELU_C1 * x * x * x))
    return 0.5 * (1.0 + t) + 0.5 * x * (1.0 - t * t) * (GELU_C0 * (1.0 + 3.0 * GELU_C1 * x * x))


def _act(a, act):
    if act == "relu2":
        r = jnp.maximum(a.astype(F32), 0.0)
        return (r * r).astype(BF16)
    return a.astype(BF16)


def _layer_spec(wg, layer):
    nsh, _, r, c = wg.shape
    return pl.BlockSpec((nsh, None, r, c), lambda i: (0, layer, 0, 0), pipeline_mode=pl.Buffered(1))


def _norm_matmul(x, gain, wg, layer, name):
    T, D = x.shape
    nsh, _, _, ns = wg.shape
    tm = min(T, ROW_TILE // 2)

    def body(x_ref, g_ref, w_ref, y_ref, h_ref):
        xv = x_ref[...]
        h = (xv * _rstd(xv) * g_ref[...]).astype(BF16)
        h_ref[...] = h
        for j in range(nsh):
            y_ref[:, j * ns : (j + 1) * ns] = jnp.dot(h, w_ref[j], preferred_element_type=F32).astype(BF16)

    return pl.pallas_call(
        body,
        name=name,
        grid=(T // tm,),
        in_specs=[pl.BlockSpec((tm, D), lambda i: (i, 0)), pl.BlockSpec((1, D), lambda i: (0, 0)), _layer_spec(wg, layer)],
        out_specs=[pl.BlockSpec((tm, nsh * ns), lambda i: (i, 0)), pl.BlockSpec((tm, D), lambda i: (i, 0))],
        out_shape=[jax.ShapeDtypeStruct((T, nsh * ns), BF16), jax.ShapeDtypeStruct((T, D), BF16)],
        compiler_params=_params(1),
    )(x, gain.reshape(1, D), wg)


def _act_matmul_res(a, wg, layer, x_in, act, name):
    T, K = a.shape
    nsh, _, kq, D = wg.shape
    tm = min(T, ROW_TILE // 2)

    def body(a_ref, w_ref, x_ref, o_ref):
        w = w_ref[...].reshape(nsh * kq, D)
        o_ref[...] = x_ref[...] + jnp.dot(_act(a_ref[...], act), w, preferred_element_type=F32)

    return pl.pallas_call(
        body,
        name=name,
        grid=(T // tm,),
        in_specs=[pl.BlockSpec((tm, K), lambda i: (i, 0)), _layer_spec(wg, layer), pl.BlockSpec((tm, D), lambda i: (i, 0))],
        out_specs=pl.BlockSpec((tm, D), lambda i: (i, 0)),
        out_shape=jax.ShapeDtypeStruct((T, D), F32),
        compiler_params=_params(1),
    )(a, wg, x_in)


def _matmul_nt(g, wg, layer, a, name):
    T, D = g.shape
    nsh, _, kq, _ = wg.shape
    tm = min(T, ROW_TILE // 2)

    def body(g_ref, w_ref, *rest):
        gv = g_ref[...]
        for k in range(nsh):
            cols = slice(k * kq, (k + 1) * kq)
            r = lax.dot_general(gv, w_ref[k], NT, preferred_element_type=F32)
            if a is not None:
                r = r * (2.0 * jnp.maximum(rest[0][:, cols].astype(F32), 0.0))
            rest[-1][:, cols] = r.astype(BF16)

    row = pl.BlockSpec((tm, nsh * kq), lambda i: (i, 0))
    in_specs = [pl.BlockSpec((tm, D), lambda i: (i, 0)), _layer_spec(wg, layer)]
    args = [g, wg]
    if a is not None:
        in_specs.append(row)
        args.append(a)
    return pl.pallas_call(
        body,
        name=name,
        grid=(T // tm,),
        in_specs=in_specs,
        out_specs=row,
        out_shape=jax.ShapeDtypeStruct((T, nsh * kq), BF16),
        compiler_params=_params(1),
    )(*args)


def _matmul_nt_norm_bwd(da, wg, layer, x, gain, dres, name):
    T, D = x.shape
    nsh, _, _, ns = wg.shape
    tm = min(T, ROW_TILE // 2)

    def body(da_ref, w_ref, x_ref, g_ref, r_ref, dx_ref, dxb_ref, dg_ref):
        dh = lax.dot_general(da_ref[:, :ns], w_ref[0], NT, preferred_element_type=F32)
        for j in range(1, nsh):
            dh = dh + lax.dot_general(da_ref[:, j * ns : (j + 1) * ns], w_ref[j], NT, preferred_element_type=F32)
        dx, dg = _norm_bwd(dh, x_ref[...], g_ref[...])
        dx = dx + r_ref[...]
        dx_ref[...] = dx
        dxb_ref[...] = dx.astype(BF16)

        @pl.when(pl.program_id(0) == 0)
        def _():
            dg_ref[...] = dg

        @pl.when(pl.program_id(0) > 0)
        def _():
            dg_ref[...] += dg

    row = pl.BlockSpec((tm, D), lambda i: (i, 0))
    vec = pl.BlockSpec((1, D), lambda i: (0, 0))
    return pl.pallas_call(
        body,
        name=name,
        grid=(T // tm,),
        in_specs=[pl.BlockSpec((tm, nsh * ns), lambda i: (i, 0)), _layer_spec(wg, layer), row, vec, row],
        out_specs=[row, row, vec],
        out_shape=[
            jax.ShapeDtypeStruct((T, D), F32),
            jax.ShapeDtypeStruct((T, D), BF16),
            jax.ShapeDtypeStruct((1, D), F32),
        ],
        compiler_params=_params(1),
    )(da, wg, x, gain.reshape(1, D), dres)


def _matmul_tn(lhs, rhs, bufs, layer, n_layers, shard_lhs, act, name):
    T = lhs.shape[0]
    rows = lhs.shape[1] // N_CHIPS if shard_lhs else lhs.shape[1]
    cols = rhs.shape[1] if shard_lhs else rhs.shape[1] // N_CHIPS
    tt = min(T, 2 * ROW_TILE)
    n_t = T // tt

    def body(l_ref, r_ref, *rest):
        o32_ref, o16_ref = rest[-2:]
        t = pl.program_id(1)
        upd = lax.dot_general(_act(l_ref[...], act), r_ref[...].astype(BF16), TN, preferred_element_type=F32)

        @pl.when(t == 0)
        def _():
            o32_ref[...] = upd

        @pl.when(t > 0)
        def _():
            o32_ref[...] += upd

        @pl.when(t == n_t - 1)
        def _():
            o16_ref[...] = o32_ref[...].astype(BF16)

    if shard_lhs:
        in_specs = [pl.BlockSpec((tt, rows), lambda s, t: (t, s)), pl.BlockSpec((tt, cols), lambda s, t: (t, 0))]
    else:
        in_specs = [pl.BlockSpec((tt, rows), lambda s, t: (t, 0)), pl.BlockSpec((tt, cols), lambda s, t: (t, s))]
    args = [lhs, rhs]
    aliases = {}
    if bufs is not None:
        in_specs += [pl.BlockSpec(memory_space=pl.ANY)] * 2
        args += list(bufs)
        aliases = {2: 0, 3: 1}
    shape = (N_CHIPS, n_layers, rows, cols)
    return pl.pallas_call(
        body,
        name=name,
        grid=(N_CHIPS, n_t),
        in_specs=in_specs,
        out_specs=[pl.BlockSpec((None, None, rows, cols), lambda s, t: (s, layer, 0, 0))] * 2,
        out_shape=[jax.ShapeDtypeStruct(shape, F32), jax.ShapeDtypeStruct(shape, BF16)],
        input_output_aliases=aliases,
        compiler_params=_params(2),
    )(*args)


def _chip_peers(x, y):
    return [(1 - x, y), (x, 1 - y), (1 - x, 1 - y)]


def _remote(src, dst, sems, s, peer):
    return pltpu.make_async_remote_copy(
        src_ref=src, dst_ref=dst, send_sem=sems[0].at[s], recv_sem=sems[1].at[s], device_id=peer, device_id_type=MESH
    )


class _Exchange:
    def __init__(self, operands, n_alias, new_shapes, n_sems, build):
        self.operands, self.n_alias, self.new_shapes, self.n_sems, self.build = operands, n_alias, new_shapes, n_sems, build

    def out_shapes(self):
        return [jax.ShapeDtypeStruct(a.shape, a.dtype) for a in self.operands[: self.n_alias]] + list(self.new_shapes)

    def scratch(self):
        return [pltpu.SemaphoreType.DMA((n,)) for n in self.n_sems]

    def run(self, ins, outs, sems, first, last):
        starts, recvs, sends, locals_ = self.build(ins, outs, sems)

        def start_all():
            for cp in starts:
                cp.start()

        def wait_all():
            for cp in recvs:
                cp.wait_recv()
            for cp in sends:
                cp.wait_send()
            for cp in locals_:
                cp.wait()

        if first is True:
            start_all()
            return wait_all
        pl.when(first)(start_all)
        return lambda: pl.when(last)(wait_all)


def _gather_exchange(shards, bufs, plan):
    n_arr = len(shards)
    n_cp = len(plan) * (N_CHIPS - 1)

    def build(ins, outs, sems):
        shard_refs = ins[-n_arr:]
        x, y, c = lax.axis_index("x"), lax.axis_index("y"), lax.axis_index("c")
        me = 2 * x + y
        starts, recvs, sends, locals_ = [], [], [], []
        for p, (a, l0, n) in enumerate(plan):
            src = shard_refs[a].at[pl.ds(l0, n)]
            cp = pltpu.make_async_copy(src, outs[a].at[me, pl.ds(l0, n)], sems[2].at[p])
            locals_.append(cp)
            for k, (px, py) in enumerate(_chip_peers(x, y)):
                s = p * (N_CHIPS - 1) + k
                sends.append(_remote(src, outs[a].at[me, pl.ds(l0, n)], sems, s, (px, py, c)))
                recvs.append(_remote(src, outs[a].at[2 * px + py, pl.ds(l0, n)], sems, s, (px, py, c)))
        return locals_ + sends, recvs, sends, locals_

    if bufs is None:
        new = [jax.ShapeDtypeStruct((N_CHIPS,) + s.shape, s.dtype) for s in shards]
        return _Exchange(list(shards), 0, new, [n_cp, n_cp, len(plan)], build)
    return _Exchange(list(bufs) + list(shards), n_arr, [], [n_cp, n_cp, len(plan)], build)


def _scatter_exchange(g16, recv, plan, small=None):
    n_arr = len(g16)
    have = [r for r in recv if r is not None]
    made = [a for a in range(n_arr) if recv[a] is None]
    n_cp = len(plan) * (N_CHIPS - 1) + (N_DEV - 1 if small is not None else 0)

    def build(ins, outs, sems):
        g_refs = ins[:n_arr]
        recv_refs, it_have, it_made = [], iter(outs[n_arr : n_arr + len(have)]), iter(outs[n_arr + len(have) :])
        for a in range(n_arr):
            recv_refs.append(next(it_made) if recv[a] is None else next(it_have))
        x, y, c = lax.axis_index("x"), lax.axis_index("y"), lax.axis_index("c")
        me = 2 * x + y
        starts, recvs, sends, locals_ = [], [], [], []
        for p, (a, l0, n) in enumerate(plan):
            for k, (px, py) in enumerate(_chip_peers(x, y)):
                s = p * (N_CHIPS - 1) + k
                dst = recv_refs[a].at[k, pl.ds(l0, n)]
                sends.append(_remote(g_refs[a].at[2 * px + py, pl.ds(l0, n)], dst, sems, s, (px, py, c)))
                recvs.append(_remote(g_refs[a].at[me, pl.ds(l0, n)], dst, sems, s, (px, py, c)))
        if small is not None:
            small_ref, all_ref = ins[-1], outs[-1]
            slot = 4 * x + 2 * y + c
            locals_.append(pltpu.make_async_copy(small_ref, all_ref.at[slot], sems[2].at[0]))
            flips = [(fx, fy, fc) for fx in (0, 1) for fy in (0, 1) for fc in (0, 1)][1:]
            for k, (fx, fy, fc) in enumerate(flips):
                s = len(plan) * (N_CHIPS - 1) + k
                px, py, pc = x ^ fx, y ^ fy, c ^ fc
                sends.append(_remote(small_ref, all_ref.at[slot], sems, s, (px, py, pc)))
                recvs.append(_remote(small_ref, all_ref.at[4 * px + 2 * py + pc], sems, s, (px, py, pc)))
        return locals_ + sends, recvs, sends, locals_

    operands = list(g16) + have + ([small] if small is not None else [])
    new = [jax.ShapeDtypeStruct((N_CHIPS - 1,) + g16[a].shape[1:], BF16) for a in made]
    if small is not None:
        new.append(jax.ShapeDtypeStruct((N_DEV,) + small.shape, F32))
    return _Exchange(operands, n_arr + len(have), new, [n_cp, n_cp, 1], build)


def _call_with_exchange(body, exch, name, grid, in_specs, out_specs, out_shape, scratch_shapes, args):
    n_in, n_out, n_scr = len(in_specs), len(out_shape), len(scratch_shapes)
    if exch is None:
        outs = pl.pallas_call(
            body, name=name, grid=grid, in_specs=in_specs, out_specs=out_specs, out_shape=out_shape,
            scratch_shapes=scratch_shapes, compiler_params=_params(len(grid)),
        )(*args)
        return outs, []
    e_shapes = exch.out_shapes()
    e_in, e_out = len(exch.operands), len(e_shapes)

    def wrapped(*refs):
        ins, refs = refs[:n_in], refs[n_in:]
        e_ins, refs = refs[:e_in], refs[e_in:]
        outs, refs = refs[:n_out], refs[n_out:]
        e_outs, refs = refs[:e_out], refs[e_out:]
        scr, sems = refs[:n_scr], refs[n_scr:]
        first, last = True, True
        for d, g in enumerate(grid):
            first = (pl.program_id(d) == 0) & first
            last = (pl.program_id(d) == g - 1) & last
        finish = exch.run(e_ins, e_outs, sems, first, last)
        body(*ins, *outs, *scr)
        finish()

    any_spec = pl.BlockSpec(memory_space=pl.ANY)
    outs = pl.pallas_call(
        wrapped,
        name=name,
        grid=grid,
        in_specs=list(in_specs) + [any_spec] * e_in,
        out_specs=list(out_specs) + [any_spec] * e_out,
        out_shape=list(out_shape) + e_shapes,
        input_output_aliases={n_in + i: n_out + i for i in range(exch.n_alias)},
        scratch_shapes=list(scratch_shapes) + exch.scratch(),
        compiler_params=pltpu.CompilerParams(
            dimension_semantics=("arbitrary",) * len(grid), vmem_limit_bytes=VMEM_LIMIT, has_side_effects=True
        ),
    )(*args, *exch.operands)
    return outs[:n_out], outs[n_out:]


def _exchange_only(exch, name):
    n_in = len(exch.operands)
    shapes = exch.out_shapes()

    def body(*refs):
        ins, outs, sems = refs[:n_in], refs[n_in : n_in + len(shapes)], refs[n_in + len(shapes) :]
        exch.run(ins, outs, sems, True, True)()

    any_spec = pl.BlockSpec(memory_space=pl.ANY)
    return pl.pallas_call(
        body,
        name=name,
        in_specs=[any_spec] * n_in,
        out_specs=[any_spec] * len(shapes),
        out_shape=shapes,
        input_output_aliases={i: i for i in range(exch.n_alias)},
        scratch_shapes=exch.scratch(),
        compiler_params=pltpu.CompilerParams(has_side_effects=True),
    )(*exch.operands)


ATTN_LANE_TILES = 2
ATTN_UNROLL = 8


MASKED = -1e30


def _hi_lo(x):
    hi = x.astype(BF16)
    lo = (x - hi.astype(F32)).astype(BF16)
    return jnp.concatenate([hi, lo], axis=1)


def _suffix_matrix(inclusive):
    j = lax.broadcasted_iota(jnp.int32, (2 * Q_TILE, 2 * Q_TILE), 0) & (Q_TILE - 1)
    s = lax.broadcasted_iota(jnp.int32, (2 * Q_TILE, 2 * Q_TILE), 1)
    later = (j >= s) if inclusive else (j > s)
    return jnp.where((s >= Q_TILE) | later, 1.0, 0.0).astype(BF16)


def _log_beta(z):
    return jnp.minimum(z, 0.0) - jnp.log(1.0 + jnp.exp(-jnp.abs(z)))


def _head_masks(width):
    lane = lax.broadcasted_iota(jnp.int32, (1, width), 1)
    return [(lane >= h * HEAD_DIM) & (lane < (h + 1) * HEAD_DIM) for h in range(width // HEAD_DIM)]


def _per_head_rows(x, masks):
    return jnp.concatenate([jnp.where(hm, x, 0) for hm in masks], axis=0)


def _heads_to_lanes(x, n_heads):
    return jnp.concatenate([x[h * Q_TILE : (h + 1) * Q_TILE] for h in range(n_heads)], axis=1)


def _block_start(kb):
    return kb * Q_TILE if isinstance(kb, int) else pl.multiple_of(kb * Q_TILE, Q_TILE)


def _clamp(i, n):
    return jnp.minimum(i, n - 1)


def _next_block(pos):
    qi, kb = pos
    row_done = kb == 0
    nqi = jnp.where(row_done, qi + 1, qi)
    return nqi, jnp.where(row_done, nqi, kb - 1)


def _stream_unroll(n_blocks):
    return next(u for u in (ATTN_UNROLL, 2, 1) if n_blocks % u == 0)


def _past_mask(rows):
    t = lax.broadcasted_iota(jnp.int32, (rows, Q_TILE), 0) & (Q_TILE - 1)
    s = lax.broadcasted_iota(jnp.int32, (rows, Q_TILE), 1)
    return s < t


def _attn_fwd(qkv, n_seq, S, D, exch, name):
    T = n_seq * S
    width = min(D, ATTN_LANE_TILES * LANES)
    n_heads = width // HEAD_DIM
    rows = n_heads * Q_TILE
    nq = S // Q_TILE
    groups = D // width
    n_blocks = nq * (nq + 1) // 2
    unroll = _stream_unroll(n_blocks)
    scale = HEAD_DIM ** -0.5

    n_trips = n_blocks // unroll

    def body(q_ref, k_ref, v_ref, o_ref, a_out, b_out, qh_scr, vh_scr, bias_scr, a_stage, b_stage, sems):
        masks = _head_masks(width)
        sfx = _suffix_matrix(False)
        stream = pl.program_id(0) * groups + pl.program_id(1)

        def per_head_tables(i, c):
            blk = pl.ds(_block_start(i), Q_TILE)
            qh_scr[i] = _per_head_rows(q_ref[blk, :] * scale, masks)
            vh_scr[i] = _per_head_rows(v_ref[blk, :], masks)
            return c

        lax.fori_loop(0, nq, per_head_tables, 0)
        bias_scr[0] = jnp.zeros((rows, Q_TILE), F32)
        bias_scr[1] = jnp.where(_past_mask(rows), 0.0, MASKED)

        def save(n, slot):
            blocks = pl.ds(n * unroll, unroll)
            return [
                pltpu.make_async_copy(a_stage.at[slot], a_out.at[stream, blocks], sems.at[slot]),
                pltpu.make_async_copy(b_stage.at[slot], b_out.at[stream, blocks], sems.at[2 + slot]),
            ]

        def scores(pos):
            qi, kb = pos
            kt = k_ref[pl.ds(_block_start(_clamp(kb, nq)), Q_TILE), :]
            z = lax.dot_general(qh_scr[_clamp(qi, nq)], kt, NT, preferred_element_type=F32)
            z = z + bias_scr[(kb == qi).astype(jnp.int32)]
            lb = _log_beta(z)
            return lb, _hi_lo(lb - z)

        def weigh(pos, st, carry, acc, slot, u):
            qi, kb = pos
            lb, l1 = st
            r = jnp.dot(l1, sfx, preferred_element_type=F32)
            carry = jnp.where(kb == qi, 0.0, carry)
            a = jnp.exp(lb + r[:, :Q_TILE] + carry).astype(BF16)
            a_stage[slot, u] = a
            b_stage[slot, u] = jnp.exp(lb).astype(BF16)
            acc = jnp.where(kb == qi, 0.0, acc) + jnp.dot(
                _heads_to_lanes(a, n_heads), vh_scr[_clamp(kb, nq)], preferred_element_type=F32
            )
            o_ref[pl.ds(_block_start(_clamp(qi, nq)), Q_TILE), :] = acc
            return carry + r[:, Q_TILE:], acc

        def trip(n, c):
            pos, st, carry, acc = c
            slot = n % 2

            @pl.when(n >= 2)
            def _():
                for cp in save(n - 2, slot):
                    cp.wait()

            for u in range(unroll):
                nxt = _next_block(pos)
                st_nxt = scores(nxt)
                carry, acc = weigh(pos, st, carry, acc, slot, u)
                pos, st = nxt, st_nxt
            for cp in save(n, slot):
                cp.start()
            return pos, st, carry, acc

        first = (jnp.int32(0), jnp.int32(0))
        zero = bias_scr[0]
        init = (first, scores(first), zero, jnp.concatenate([zero[:Q_TILE]] * (width // Q_TILE), axis=1))
        lax.fori_loop(0, n_trips, trip, init)
        for n in range(max(n_trips - 2, 0), n_trips):
            for cp in save(n, n % 2):
                cp.wait()

    seq = lambda col0: pl.BlockSpec((S, width), lambda b, p: (b, col0 + p))
    saved = jax.ShapeDtypeStruct((n_seq * groups, n_blocks, rows, Q_TILE), BF16)
    stage = pltpu.VMEM((2, unroll, rows, Q_TILE), BF16)
    (o, a_w, beta), moved = _call_with_exchange(
        body,
        exch,
        name,
        grid=(n_seq, groups),
        in_specs=[seq(0), seq(groups), seq(2 * groups)],
        out_specs=[seq(0), pl.BlockSpec(memory_space=pl.ANY), pl.BlockSpec(memory_space=pl.ANY)],
        out_shape=[jax.ShapeDtypeStruct((T, D), F32), saved, saved],
        scratch_shapes=[
            pltpu.VMEM((nq, rows, width), BF16),
            pltpu.VMEM((nq, rows, width), BF16),
            pltpu.VMEM((2, rows, Q_TILE), F32),
            stage,
            stage,
            pltpu.SemaphoreType.DMA((4,)),
        ],
        args=(qkv, qkv, qkv),
    )
    return (o, a_w, beta), moved


def _attn_bwd(qkv, fwd, do, n_seq, S, D, exch, name):
    o, a_w, beta = fwd
    T = n_seq * S
    width = min(D, ATTN_LANE_TILES * LANES)
    n_heads = width // HEAD_DIM
    rows = n_heads * Q_TILE
    nq = S // Q_TILE
    groups = D // width
    n_blocks = nq * (nq + 1) // 2
    unroll = _stream_unroll(n_blocks)
    scale = HEAD_DIM ** -0.5

    n_trips = n_blocks // unroll

    def body(q_ref, k_ref, v_ref, o_ref, do_ref, a_in, b_in, dq_ref, dk_ref, dv_ref, dk_acc, dv_acc, qh_scr, doh_scr, delta_scr, a_stage, b_stage, sems):
        masks = _head_masks(width)
        sfx_incl = _suffix_matrix(True)
        stream = pl.program_id(0) * groups + pl.program_id(1)

        def fetch(n, slot):
            blocks = pl.ds(n * unroll, unroll)
            return [
                pltpu.make_async_copy(a_in.at[stream, blocks], a_stage.at[slot], sems.at[slot]),
                pltpu.make_async_copy(b_in.at[stream, blocks], b_stage.at[slot], sems.at[2 + slot]),
            ]

        for cp in fetch(0, 0):
            cp.start()
        dk_acc[...] = jnp.zeros_like(dk_acc)
        dv_acc[...] = jnp.zeros_like(dv_acc)

        def per_head_tables(i, c):
            blk = pl.ds(_block_start(i), Q_TILE)
            do = do_ref[blk, :]
            qh_scr[i] = _per_head_rows(q_ref[blk, :] * scale, masks)
            doh_scr[i] = _per_head_rows(do, masks)
            prod = do.astype(F32) * o_ref[blk, :]
            delta = jnp.concatenate(
                [jnp.sum(jnp.where(hm, prod, 0.0), axis=-1, keepdims=True) for hm in masks], axis=0
            )
            delta_scr[i] = jnp.broadcast_to(delta, (rows, Q_TILE))
            return c

        lax.fori_loop(0, nq, per_head_tables, 0)

        def weigh(pos, ab, beta, c2, dq):
            qi, kb = pos
            first = kb == qi
            blk = pl.ds(_block_start(kb), Q_TILE)
            g = ab.astype(F32) * lax.dot_general(doh_scr[qi], v_ref[blk, :], NT, preferred_element_type=F32)
            r2 = jnp.dot(_hi_lo(g), sfx_incl, preferred_element_type=F32)
            c2 = jnp.where(first, 0.0, c2)
            earlier = delta_scr[qi] - (r2[:, :Q_TILE] + c2)
            beta = beta.astype(F32)
            dzb = (g * (1.0 - beta) - earlier * beta).astype(BF16)
            kh = _per_head_rows(k_ref[blk, :], masks)
            dq = jnp.where(first, 0.0, dq) + jnp.dot(_heads_to_lanes(dzb, n_heads), kh, preferred_element_type=F32)
            dq_ref[pl.ds(_block_start(qi), Q_TILE), :] = (dq * scale).astype(BF16)
            dk_acc[blk, :] += lax.dot_general(dzb, qh_scr[qi], TN, preferred_element_type=F32)
            dv_acc[blk, :] += lax.dot_general(ab, doh_scr[qi], TN, preferred_element_type=F32)
            return c2 + r2[:, Q_TILE:], dq

        def trip(n, c):
            pos, c2, dq = c
            slot = n % 2
            for cp in fetch(n, slot):
                cp.wait()

            @pl.when(n + 1 < n_trips)
            def _():
                for cp in fetch(n + 1, 1 - slot):
                    cp.start()

            for u in range(unroll):
                c2, dq = weigh(pos, a_stage[slot, u], b_stage[slot, u], c2, dq)
                pos = _next_block(pos)
            return pos, c2, dq

        zero = dk_acc[pl.ds(0, Q_TILE), :]
        init = ((jnp.int32(0), jnp.int32(0)), jnp.concatenate([zero[:, :Q_TILE]] * n_heads, axis=0), zero)
        lax.fori_loop(0, n_trips, trip, init)
        dk_ref[...] = dk_acc[...].astype(BF16)
        dv_ref[...] = dv_acc[...].astype(BF16)

    seq = lambda col0: pl.BlockSpec((S, width), lambda b, p: (b, col0 + p))
    return _call_with_exchange(
        body,
        exch,
        name,
        grid=(n_seq, groups),
        in_specs=[seq(0), seq(groups), seq(2 * groups), seq(0), seq(0)] + [pl.BlockSpec(memory_space=pl.ANY)] * 2,
        out_specs=[seq(0)] * 3,
        out_shape=[jax.ShapeDtypeStruct((T, D), BF16)] * 3,
        scratch_shapes=[
            pltpu.VMEM((S, width), F32),
            pltpu.VMEM((S, width), F32),
            pltpu.VMEM((nq, rows, width), BF16),
            pltpu.VMEM((nq, rows, width), BF16),
            pltpu.VMEM((nq, rows, Q_TILE), F32),
            pltpu.VMEM((2, unroll, rows, Q_TILE), BF16),
            pltpu.VMEM((2, unroll, rows, Q_TILE), BF16),
            pltpu.SemaphoreType.DMA((4,)),
        ],
        args=(qkv, qkv, qkv, o, do, a_w, beta),
    )


def _causal_ws(ws_ref, g):
    t = lax.broadcasted_iota(jnp.int32, (SGU_CHUNK, SGU_CHUNK), 0)
    s = lax.broadcasted_iota(jnp.int32, (SGU_CHUNK, SGU_CHUNK), 1)
    return jnp.where(s <= t, ws_ref[g], 0.0)


def _sgu_fwd(a, gain, ws, bsb, name):
    T, F2 = a.shape
    F = F2 // 2
    gw = F // SGU_GROUPS

    def body(a_ref, gain_ref, ws_ref, bsb_ref, y_ref):
        v = _gelu(a_ref[:, F:].astype(F32))
        vn = (v * _rstd(v) * gain_ref[...]).astype(BF16)
        for g in range(SGU_GROUPS):
            cs = slice(g * gw, (g + 1) * gw)
            w = _causal_ws(ws_ref, g).astype(BF16)
            mixed = jnp.dot(w, vn[:, cs], preferred_element_type=F32) + bsb_ref[g]
            y_ref[:, cs] = (_gelu(a_ref[:, cs].astype(F32)) * mixed).astype(BF16)

    return pl.pallas_call(
        body,
        name=name,
        grid=(T // SGU_CHUNK,),
        in_specs=[
            pl.BlockSpec((SGU_CHUNK, F2), lambda i: (i, 0)),
            pl.BlockSpec((1, F), lambda i: (0, 0)),
            pl.BlockSpec((SGU_GROUPS, SGU_CHUNK, SGU_CHUNK), lambda i: (0, 0, 0)),
            pl.BlockSpec((SGU_GROUPS, SGU_CHUNK, gw), lambda i: (0, 0, 0)),
        ],
        out_specs=pl.BlockSpec((SGU_CHUNK, F), lambda i: (i, 0)),
        out_shape=jax.ShapeDtypeStruct((T, F), BF16),
        compiler_params=_params(1),
    )(a, gain.reshape(1, F), ws, bsb)


def _sgu_bwd(a, dy, gain, ws, bsb, name):
    T, F2 = a.shape
    F = F2 // 2
    gw = F // SGU_GROUPS

    def body(a_ref, dy_ref, gain_ref, ws_ref, bsb_ref, da_ref, dws_ref, dbs_ref, dgain_ref, dvn_ref):
        @pl.when(pl.program_id(0) == 0)
        def _():
            dws_ref[...] = jnp.zeros_like(dws_ref)
            dbs_ref[...] = jnp.zeros_like(dbs_ref)
            dgain_ref[...] = jnp.zeros_like(dgain_ref)

        av = a_ref[:, F:].astype(F32)
        v = _gelu(av)
        rstd = _rstd(v)
        vh = v * rstd
        gain = gain_ref[...]
        vn = (vh * gain).astype(BF16)
        ones = jnp.ones((gw, SGU_CHUNK), BF16)
        for g in range(SGU_GROUPS):
            cs = slice(g * gw, (g + 1) * gw)
            w = _causal_ws(ws_ref, g).astype(BF16)
            mixed = jnp.dot(w, vn[:, cs], preferred_element_type=F32) + bsb_ref[g]
            au = a_ref[:, cs].astype(F32)
            dyc = dy_ref[:, cs].astype(F32)
            da_ref[:, cs] = (dyc * mixed * _gelu_grad(au)).astype(BF16)
            dm = (dyc * _gelu(au)).astype(BF16)
            dbs_ref[g] += jnp.dot(dm, ones, preferred_element_type=F32)
            dws_ref[g] += _causal_mask_f32(lax.dot_general(dm, vn[:, cs], NT, preferred_element_type=F32))
            dvn_ref[:, cs] = lax.dot_general(w, dm, TN, preferred_element_type=F32)
        dvn = dvn_ref[...]
        dgain_ref[...] += jnp.sum(dvn * vh, axis=0, keepdims=True)
        dvh = dvn * gain
        dv = rstd * (dvh - vh * jnp.mean(dvh * vh, axis=-1, keepdims=True))
        da_ref[:, F:] = (dv * _gelu_grad(av)).astype(BF16)

    acc_spec = pl.BlockSpec((SGU_GROUPS, SGU_CHUNK, SGU_CHUNK), lambda i: (0, 0, 0))
    acc_shape = jax.ShapeDtypeStruct((SGU_GROUPS, SGU_CHUNK, SGU_CHUNK), F32)
    return pl.pallas_call(
        body,
        name=name,
        grid=(T // SGU_CHUNK,),
        in_specs=[
            pl.BlockSpec((SGU_CHUNK, F2), lambda i: (i, 0)),
            pl.BlockSpec((SGU_CHUNK, F), lambda i: (i, 0)),
            pl.BlockSpec((1, F), lambda i: (0, 0)),
            acc_spec,
            pl.BlockSpec((SGU_GROUPS, SGU_CHUNK, gw), lambda i: (0, 0, 0)),
        ],
        out_specs=[
            pl.BlockSpec((SGU_CHUNK, F2), lambda i: (i, 0)),
            acc_spec,
            acc_spec,
            pl.BlockSpec((1, F), lambda i: (0, 0)),
        ],
        out_shape=[
            jax.ShapeDtypeStruct((T, F2), BF16),
            acc_shape,
            acc_shape,
            jax.ShapeDtypeStruct((1, F), F32),
        ],
        scratch_shapes=[pltpu.VMEM((SGU_CHUNK, F), F32)],
        compiler_params=_params(1),
    )(a, dy, gain.reshape(1, F), ws, bsb)


def _causal_mask_f32(m):
    t = lax.broadcasted_iota(jnp.int32, m.shape, 0)
    s = lax.broadcasted_iota(jnp.int32, m.shape, 1)
    return jnp.where(s <= t, m, 0.0)


def _final_loss(x, gain, target, name):
    T, D = x.shape
    tm = min(T, ROW_TILE // 2)

    def body(x_ref, g_ref, t_ref, sq_ref, dx_ref, dxb_ref, dg_ref):
        xv = x_ref[...]
        gain = g_ref[...]
        err = xv * _rstd(xv) * gain - t_ref[...]
        dx, dg = _norm_bwd(err * (1.0 / D), xv, gain)
        dx_ref[...] = dx
        dxb_ref[...] = dx.astype(BF16)
        sq = jnp.sum(err * err, axis=0, keepdims=True)

        @pl.when(pl.program_id(0) == 0)
        def _():
            sq_ref[...] = sq
            dg_ref[...] = dg

        @pl.when(pl.program_id(0) > 0)
        def _():
            sq_ref[...] += sq
            dg_ref[...] += dg

    row = pl.BlockSpec((tm, D), lambda i: (i, 0))
    vec = pl.BlockSpec((1, D), lambda i: (0, 0))
    return pl.pallas_call(
        body,
        name=name,
        grid=(T // tm,),
        in_specs=[row, vec, row],
        out_specs=[vec, row, row, vec],
        out_shape=[
            jax.ShapeDtypeStruct((1, D), F32),
            jax.ShapeDtypeStruct((T, D), F32),
            jax.ShapeDtypeStruct((T, D), BF16),
            jax.ShapeDtypeStruct((1, D), F32),
        ],
        compiler_params=_params(1),
    )(x, gain.reshape(1, D), target)


def _row_tile(rows, cols, n_arrays):
    budget = VMEM_LIMIT // 2 // (2 * n_arrays * cols * 4)
    tr = rows
    while tr > budget and tr % 16 == 0:
        tr //= 2
    return tr


def _sum_received(own, recv, name):
    R, C = own.shape
    n = recv.shape[0]
    tr = _row_tile(R, C, n + 2)

    def body(own_ref, recv_ref, o_ref):
        s = own_ref[...]
        for k in range(n):
            s = s + recv_ref[k].astype(F32)
        o_ref[...] = s

    return pl.pallas_call(
        body,
        name=name,
        grid=(R // tr,),
        in_specs=[pl.BlockSpec((tr, C), lambda i: (i, 0)), pl.BlockSpec((n, tr, C), lambda i: (0, i, 0))],
        out_specs=pl.BlockSpec((tr, C), lambda i: (i, 0)),
        out_shape=jax.ShapeDtypeStruct((R, C), F32),
        compiler_params=_params(1),
    )(own, recv)


def _sum_chip_shard(g32, recv, chip, name):
    _, L, r, c = g32.shape
    n = recv.shape[0]
    tr = _row_tile(r, c, n + 2)

    def body(chip_ref, own_ref, recv_ref, o_ref):
        s = own_ref[...]
        for k in range(n):
            s = s + recv_ref[k].astype(F32)
        o_ref[...] = s

    return pl.pallas_call(
        body,
        name=name,
        grid_spec=pltpu.PrefetchScalarGridSpec(
            num_scalar_prefetch=1,
            grid=(L, r // tr),
            in_specs=[
                pl.BlockSpec((None, None, tr, c), lambda l, i, chip_ref: (chip_ref[0], l, i, 0)),
                pl.BlockSpec((n, None, tr, c), lambda l, i, chip_ref: (0, l, i, 0)),
            ],
            out_specs=pl.BlockSpec((None, tr, c), lambda l, i, chip_ref: (l, i, 0)),
        ),
        out_shape=jax.ShapeDtypeStruct((L, r, c), F32),
        compiler_params=_params(2),
    )(chip.reshape(1).astype(jnp.int32), g32, recv)


def _adamw(w, m, v, parts, name):
    R, C = w.shape
    n = len(parts)
    tr = _row_tile(R, C, n + 7)

    def body(*refs):
        w_ref, m_ref, v_ref = refs[:3]
        g_ref, d_ref, nm_ref, nv_ref = refs[3 + n :]
        g = refs[3][...]
        for p_ref in refs[4 : 3 + n]:
            g = g + p_ref[...]
        nm = ADAM_B1 * m_ref[...] + (1.0 - ADAM_B1) * g
        nv = ADAM_B2 * v_ref[...] + (1.0 - ADAM_B2) * (g * g)
        m_hat = nm / (1.0 - ADAM_B1**ADAM_STEP)
        v_hat = nv / (1.0 - ADAM_B2**ADAM_STEP)
        g_ref[...] = g
        d_ref[...] = -ADAM_LR * (m_hat / (jnp.sqrt(v_hat) + ADAM_EPS) + ADAM_WD * w_ref[...])
        nm_ref[...] = nm
        nv_ref[...] = nv

    spec = pl.BlockSpec((tr, C), lambda i: (i, 0))
    return pl.pallas_call(
        body,
        name=name,
        grid=(R // tr,),
        in_specs=[spec] * (3 + n),
        out_specs=[spec] * 4,
        out_shape=[jax.ShapeDtypeStruct((R, C), F32)] * 4,
        compiler_params=_params(1),
    )(w, m, v, *parts)


def _swap_with_sibling(parts, name):
    n = len(parts)

    def body(*refs):
        ins, outs = refs[:n], refs[n : 2 * n]
        send_sems, recv_sems = refs[2 * n :]
        sibling = (lax.axis_index("x"), lax.axis_index("y"), 1 - lax.axis_index("c"))
        copies = [
            pltpu.make_async_remote_copy(
                src_ref=ins[a],
                dst_ref=outs[a],
                send_sem=send_sems.at[a],
                recv_sem=recv_sems.at[a],
                device_id=sibling,
                device_id_type=MESH,
            )
            for a in range(n)
        ]
        for cp in copies:
            cp.start()
        for cp in copies:
            cp.wait_recv()
        for cp in copies:
            cp.wait_send()

    any_spec = pl.BlockSpec(memory_space=pl.ANY)
    return pl.pallas_call(
        body,
        name=name,
        in_specs=[any_spec] * n,
        out_specs=[any_spec] * n,
        out_shape=[jax.ShapeDtypeStruct(p.shape, p.dtype) for p in parts],
        scratch_shapes=[pltpu.SemaphoreType.DMA((n,)), pltpu.SemaphoreType.DMA((n,))],
        compiler_params=pltpu.CompilerParams(has_side_effects=True),
    )(*parts)


def _pack(pieces):
    flat = jnp.concatenate([p.reshape(-1) for p in pieces])
    return flat.reshape(-1, LANES)


def _unpack(packed, shapes):
    flat = packed.reshape(-1)
    out, off = [], 0
    for s in shapes:
        size = 1
        for d in s:
            size *= d
        out.append(flat[off : off + size].reshape(s))
        off += size
    return out


def kernel(x, norm_mix, norm_mlp, sb_wqkv, sb_wo, sgu_win, sgu_gain, sgu_ws, sgu_bs, sgu_wout, mlp_w1, mlp_w2, final_norm, loss_target, m_norm_mix, m_norm_mlp, m_sb_wqkv, m_sb_wo, m_sgu_win, m_sgu_gain, m_sgu_ws, m_sgu_bs, m_sgu_wout, m_mlp_w1, m_mlp_w2, m_final_norm, v_norm_mix, v_norm_mlp, v_sb_wqkv, v_sb_wo, v_sgu_win, v_sgu_gain, v_sgu_ws, v_sgu_bs, v_sgu_wout, v_mlp_w1, v_mlp_w2, v_final_norm):
    n_seq, S, D = x.shape
    T = n_seq * S
    depth = norm_mix.shape[0]
    n_sgu = sgu_win.shape[0]
    F = sgu_wout.shape[1] * N_CHIPS
    gw = F // SGU_GROUPS
    chip = 2 * lax.axis_index("x") + lax.axis_index("y")

    QKV, WO, WIN, WOUT, W1, W2, GAIN = range(7)
    big = [sb_wqkv, sb_wo, sgu_win, sgu_wout, mlp_w1, mlp_w2]
    n_sb = sb_wqkv.shape[0]
    shards = [w.astype(BF16) for w in big] + [sgu_gain.reshape(1, -1, LANES)]
    def gather_plan(i):
        j, mlp = i // 2, min(2, depth - i)
        plan = [(WO, j, 1), (W1, i, mlp), (W2, i, mlp)]
        if i + 1 < depth:
            plan += [(WIN, (i + 1) // 2, 1), (WOUT, (i + 1) // 2, 1)]
        if j + 1 < n_sb:
            plan += [(QKV, j + 1, 1)]
        return plan

    wg = _exchange_only(_gather_exchange(shards, None, [(QKV, 0, 1), (GAIN, 0, 1)]), "gather_first_weights")
    gain_full = jnp.transpose(wg[GAIN].reshape(N_CHIPS, n_sgu, F // N_CHIPS), (1, 0, 2)).reshape(n_sgu, F)
    bsb = [jnp.broadcast_to(sgu_bs[j][:, :, None], (SGU_GROUPS, SGU_CHUNK, gw)) for j in range(n_sgu)]

    xs = x.reshape(T, D)
    saved = []
    for i in range(depth):
        j = i // 2
        if i % 2 == 0:
            qkv, h = _norm_matmul(xs, norm_mix[i], wg[QKV], j, f"qkv_fwd_{i}")
            attn, wg = _attn_fwd(qkv, n_seq, S, D, _gather_exchange(shards, wg, gather_plan(i)), f"attn_fwd_{i}")
            wg_qkv, wg_wo, wg_win, wg_wout, wg_w1, wg_w2 = wg[:6]
            x_mid = _act_matmul_res(attn[0], wg_wo, j, xs, None, f"wo_fwd_{i}")
            mix = (qkv, attn)
        else:
            a, h = _norm_matmul(xs, norm_mix[i], wg_win, j, f"win_fwd_{i}")
            yg = _sgu_fwd(a, gain_full[j], sgu_ws[j], bsb[j], f"sgu_fwd_{i}")
            x_mid = _act_matmul_res(yg, wg_wout, j, xs, None, f"wout_fwd_{i}")
            mix = (a, yg)
        a2, h2 = _norm_matmul(x_mid, norm_mlp[i], wg_w1, i, f"w1_fwd_{i}")
        x_out = _act_matmul_res(a2, wg_w2, i, x_mid, "relu2", f"w2_fwd_{i}")
        saved.append((xs, h, mix, x_mid, h2, a2))
        xs = x_out

    sq, dx, dxb, g_final = _final_loss(xs, final_norm, loss_target.reshape(T, D), "loss_head")
    loss = lax.psum(0.5 * jnp.sum(sq) / D, ("x", "y", "c"))

    n_layers = [n_sb, n_sb, n_sgu, n_sgu, depth, depth]
    g32, g16, recv = [None] * 6, [None] * 6, [None] * 6
    done_from, sent_from = list(n_layers), list(n_layers)

    def grad(a, layer, lhs, rhs, shard_lhs, act, name):
        bufs = None if g32[a] is None else (g32[a], g16[a])
        g32[a], g16[a] = _matmul_tn(lhs, rhs, bufs, layer, n_layers[a], shard_lhs, act, name)
        done_from[a] = layer

    def unsent_plan():
        plan = [(a, done_from[a], sent_from[a] - done_from[a]) for a in range(6) if sent_from[a] > done_from[a]]
        for a, l0, _ in plan:
            sent_from[a] = l0
        return plan

    def scatter(plan, small):
        arrays = sorted({a for a, _, _ in plan})
        have = [a for a in arrays if recv[a] is not None]
        made = [a for a in arrays if recv[a] is None]
        exch = _scatter_exchange(
            [g16[a] for a in arrays], [recv[a] for a in arrays], [(arrays.index(a), l0, n) for a, l0, n in plan], small
        )

        def take(moved):
            for a, buf in zip(arrays, moved):
                g16[a] = buf
            for a, buf in zip(have + made, moved[len(arrays) :]):
                recv[a] = buf
            return moved[-1]

        return exch, take

    g_mix, g_mlp = [None] * depth, [None] * depth
    g_ws, g_bs, g_gain = [None] * n_sgu, [None] * n_sgu, [None] * n_sgu
    for i in reversed(range(depth)):
        j = i // 2
        x_in, h, mix, x_mid, h2, a2 = saved[i]
        da2 = _matmul_nt(dxb, wg_w2, i, a2, f"w2_bwd_{i}")
        grad(W2, i, a2, dxb, True, "relu2", f"w2_grad_{i}")
        grad(W1, i, h2, da2, False, None, f"w1_grad_{i}")
        dx, dxb, g_mlp[i] = _matmul_nt_norm_bwd(da2, wg_w1, i, x_mid, norm_mlp[i], dx, f"w1_bwd_{i}")
        if i % 2 == 0:
            qkv, attn = mix
            do = _matmul_nt(dxb, wg_wo, j, None, f"wo_bwd_{i}")
            grad(WO, j, attn[0], dxb, True, None, f"wo_grad_{i}")
            exch, take = scatter(unsent_plan(), None)
            (dq, dk, dv), moved = _attn_bwd(qkv, attn, do, n_seq, S, D, exch, f"attn_bwd_{i}")
            take(moved)
            dqkv = jnp.concatenate([dq, dk, dv], axis=1)
            grad(QKV, j, h, dqkv, False, None, f"qkv_grad_{i}")
            dx, dxb, g_mix[i] = _matmul_nt_norm_bwd(dqkv, wg_qkv, j, x_in, norm_mix[i], dx, f"qkv_bwd_{i}")
        else:
            a, yg = mix
            dyg = _matmul_nt(dxb, wg_wout, j, None, f"wout_bwd_{i}")
            grad(WOUT, j, yg, dxb, True, None, f"wout_grad_{i}")
            da, g_ws[j], dbs, g_gain[j] = _sgu_bwd(a, dyg, gain_full[j], sgu_ws[j], bsb[j], f"sgu_bwd_{i}")
            g_bs[j] = dbs[:, :, 0]
            grad(WIN, j, h, da, False, None, f"win_grad_{i}")
            dx, dxb, g_mix[i] = _matmul_nt_norm_bwd(da, wg_win, j, x_in, norm_mix[i], dx, f"win_bwd_{i}")
    grad_x = dx.reshape(n_seq, S, D)

    names = ["qkv", "wo", "win", "wout", "w1", "w2"]
    small_shapes = [norm_mix.shape, norm_mlp.shape, final_norm.shape, sgu_ws.shape, sgu_bs.shape, (n_sgu, F)]
    small = _pack(
        [jnp.stack(g_mix), jnp.stack(g_mlp), g_final, jnp.stack(g_ws), jnp.stack(g_bs), jnp.stack(g_gain)]
    )
    exch, take = scatter(unsent_plan(), small)
    small_all = take(_exchange_only(exch, "scatter_last_grads"))
    partial = [_sum_chip_shard(g32[a], recv[a], chip, f"sum_{names[a]}") for a in range(6)]
    partial = [p.reshape(-1, p.shape[-1]) for p in partial]
    theirs = _swap_with_sibling(partial, "swap_partial_sums")
    small_sum = _sum_received(small_all[0], small_all[1:], "sum_small")

    ms = [m_sb_wqkv, m_sb_wo, m_sgu_win, m_sgu_wout, m_mlp_w1, m_mlp_w2]
    vs = [v_sb_wqkv, v_sb_wo, v_sgu_win, v_sgu_wout, v_mlp_w1, v_mlp_w2]
    res = {}
    keys = ["sb_wqkv", "sb_wo", "sgu_win", "sgu_wout", "mlp_w1", "mlp_w2"]
    for key, k, w, m, v, mine, other in zip(keys, names, big, ms, vs, partial, theirs):
        cols = w.shape[-1]
        outs = _adamw(w.reshape(-1, cols), m.reshape(-1, cols), v.reshape(-1, cols), [mine, other], f"adamw_{k}")
        res[key] = [o.reshape(w.shape) for o in outs]

    g_small = _unpack(small_sum, small_shapes)
    g_small[5] = lax.dynamic_slice_in_dim(g_small[5], chip * (F // N_CHIPS), F // N_CHIPS, axis=1)
    small_keys = ["norm_mix", "norm_mlp", "final_norm", "sgu_ws", "sgu_bs", "sgu_gain"]
    small_w = [norm_mix, norm_mlp, final_norm, sgu_ws, sgu_bs, sgu_gain]
    small_m = [m_norm_mix, m_norm_mlp, m_final_norm, m_sgu_ws, m_sgu_bs, m_sgu_gain]
    small_v = [v_norm_mix, v_norm_mlp, v_final_norm, v_sgu_ws, v_sgu_bs, v_sgu_gain]
    outs = _adamw(_pack(small_w), _pack(small_m), _pack(small_v), [_pack(g_small)], "adamw_small")
    local_shapes = [w.shape for w in small_w]
    for key, parts in zip(small_keys, zip(*[_unpack(o, local_shapes) for o in outs])):
        res[key] = list(parts)

    order = ["norm_mix", "norm_mlp", "sb_wqkv", "sb_wo", "sgu_win", "sgu_gain", "sgu_ws", "sgu_bs", "sgu_wout", "mlp_w1", "mlp_w2", "final_norm"]
    return (loss, grad_x, *[res[k][0] for k in order], *[res[k][1] for k in order], *[res[k][2] for k in order], *[res[k][3] for k in order])
```

```python
import jax
import jax.numpy as jnp
from jax import lax
from jax.experimental import pallas as pl
from jax.experimental.pallas import tpu as pltpu

F32 = jnp.float32
BF16 = jnp.bfloat16
MESH = pl.DeviceIdType.MESH

EPS = 1e-6
HEAD_DIM = 64
LANES = 128
Q_TILE = 128
SGU_CHUNK = 128
SGU_GROUPS = 8
N_CHIPS = 4
N_DEV = 8
ADAM_LR = 0.001
ADAM_B1 = 0.9
ADAM_B2 = 0.999
ADAM_EPS = 1e-08
ADAM_WD = 0.01
ADAM_STEP = 10
GELU_C0 = 0.7978845608028654
GELU_C1 = 0.044715
VMEM_LIMIT = 48 * 1024 * 1024
ROW_TILE = 1024
NT = (((1,), (1,)), ((), ()))
TN = (((0,), (0,)), ((), ()))


def _params(n_axes):
    return pltpu.CompilerParams(dimension_semantics=("arbitrary",) * n_axes, vmem_limit_bytes=VMEM_LIMIT)


def _rstd(x):
    return lax.rsqrt(jnp.mean(x * x, axis=-1, keepdims=True) + EPS)


def _norm_bwd(dh, x, gain):
    rstd = _rstd(x)
    xh = x * rstd
    dhg = dh * gain
    dx = rstd * (dhg - xh * jnp.mean(dhg * xh, axis=-1, keepdims=True))
    return dx, jnp.sum(dh * xh, axis=0, keepdims=True)


def _gelu(x):
    return 0.5 * x * (1.0 + jnp.tanh(GELU_C0 * (x + GELU_C1 * x * x * x)))


def _gelu_grad(x):
    t = jnp.tanh(GELU_C0 * (x + GELU_C1 * x * x * x))
    return 0.5 * (1.0 + t) + 0.5 * x * (1.0 - t * t) * (GELU_C0 * (1.0 + 3.0 * GELU_C1 * x * x))


def _act(a, act):
    if act == "relu2":
        r = jnp.maximum(a.astype(F32), 0.0)
        return (r * r).astype(BF16)
    return a.astype(BF16)


def _layer_spec(wg, layer):
    nsh, _, r, c = wg.shape
    return pl.BlockSpec((nsh, None, r, c), lambda i: (0, layer, 0, 0), pipeline_mode=pl.Buffered(1))


def _norm_matmul(x, gain, wg, layer, name):
    T, D = x.shape
    nsh, _, _, ns = wg.shape
    tm = min(T, ROW_TILE // 2)

    def body(x_ref, g_ref, w_ref, y_ref, h_ref):
        xv = x_ref[...]
        h = (xv * _rstd(xv) * g_ref[...]).astype(BF16)
        h_ref[...] = h
        for j in range(nsh):
            y_ref[:, j * ns : (j + 1) * ns] = jnp.dot(h, w_ref[j], preferred_element_type=F32).astype(BF16)

    return pl.pallas_call(
        body,
        name=name,
        grid=(T // tm,),
        in_specs=[pl.BlockSpec((tm, D), lambda i: (i, 0)), pl.BlockSpec((1, D), lambda i: (0, 0)), _layer_spec(wg, layer)],
        out_specs=[pl.BlockSpec((tm, nsh * ns), lambda i: (i, 0)), pl.BlockSpec((tm, D), lambda i: (i, 0))],
        out_shape=[jax.ShapeDtypeStruct((T, nsh * ns), BF16), jax.ShapeDtypeStruct((T, D), BF16)],
        compiler_params=_params(1),
    )(x, gain.reshape(1, D), wg)


def _act_matmul_res(a, wg, layer, x_in, act, name):
    T, K = a.shape
    nsh, _, kq, D = wg.shape
    tm = min(T, ROW_TILE // 2)

    def body(a_ref, w_ref, x_ref, o_ref):
        w = w_ref[...].reshape(nsh * kq, D)
        o_ref[...] = x_ref[...] + jnp.dot(_act(a_ref[...], act), w, preferred_element_type=F32)

    return pl.pallas_call(
        body,
        name=name,
        grid=(T // tm,),
        in_specs=[pl.BlockSpec((tm, K), lambda i: (i, 0)), _layer_spec(wg, layer), pl.BlockSpec((tm, D), lambda i: (i, 0))],
        out_specs=pl.BlockSpec((tm, D), lambda i: (i, 0)),
        out_shape=jax.ShapeDtypeStruct((T, D), F32),
        compiler_params=_params(1),
    )(a, wg, x_in)


def _matmul_nt(g, wg, layer, a, name):
    T, D = g.shape
    nsh, _, kq, _ = wg.shape
    tm = min(T, ROW_TILE // 2)

    def body(g_ref, w_ref, *rest):
        gv = g_ref[...]
        for k in range(nsh):
            cols = slice(k * kq, (k + 1) * kq)
            r = lax.dot_general(gv, w_ref[k], NT, preferred_element_type=F32)
            if a is not None:
                r = r * (2.0 * jnp.maximum(rest[0][:, cols].astype(F32), 0.0))
            rest[-1][:, cols] = r.astype(BF16)

    row = pl.BlockSpec((tm, nsh * kq), lambda i: (i, 0))
    in_specs = [pl.BlockSpec((tm, D), lambda i: (i, 0)), _layer_spec(wg, layer)]
    args = [g, wg]
    if a is not None:
        in_specs.append(row)
        args.append(a)
    return pl.pallas_call(
        body,
        name=name,
        grid=(T // tm,),
        in_specs=in_specs,
        out_specs=row,
        out_shape=jax.ShapeDtypeStruct((T, nsh * kq), BF16),
        compiler_params=_params(1),
    )(*args)


def _matmul_nt_norm_bwd(da, wg, layer, x, gain, dres, exch, name):
    T, D = x.shape
    nsh, _, _, ns = wg.shape
    tm = min(T, ROW_TILE // 2)

    def body(da_ref, w_ref, x_ref, g_ref, r_ref, dx_ref, dxb_ref, dg_ref):
        dh = lax.dot_general(da_ref[:, :ns], w_ref[0], NT, preferred_element_type=F32)
        for j in range(1, nsh):
            dh = dh + lax.dot_general(da_ref[:, j * ns : (j + 1) * ns], w_ref[j], NT, preferred_element_type=F32)
        dx, dg = _norm_bwd(dh, x_ref[...], g_ref[...])
        dx = dx + r_ref[...]
        dx_ref[...] = dx
        dxb_ref[...] = dx.astype(BF16)

        @pl.when(pl.program_id(0) == 0)
        def _():
            dg_ref[...] = dg

        @pl.when(pl.program_id(0) > 0)
        def _():
            dg_ref[...] += dg

    row = pl.BlockSpec((tm, D), lambda i: (i, 0))
    vec = pl.BlockSpec((1, D), lambda i: (0, 0))
    return _call_with_exchange(
        body,
        exch,
        name,
        grid=(T // tm,),
        in_specs=[pl.BlockSpec((tm, nsh * ns), lambda i: (i, 0)), _layer_spec(wg, layer), row, vec, row],
        out_specs=[row, row, vec],
        out_shape=[
            jax.ShapeDtypeStruct((T, D), F32),
            jax.ShapeDtypeStruct((T, D), BF16),
            jax.ShapeDtypeStruct((1, D), F32),
        ],
        scratch_shapes=[],
        args=(da, wg, x, gain.reshape(1, D), dres),
    )


def _matmul_tn(lhs, rhs, bufs, layer, n_layers, shard_lhs, act, name):
    T = lhs.shape[0]
    rows = lhs.shape[1] // N_CHIPS if shard_lhs else lhs.shape[1]
    cols = rhs.shape[1] if shard_lhs else rhs.shape[1] // N_CHIPS
    tt = min(T, 2 * ROW_TILE)
    n_t = T // tt

    def body(l_ref, r_ref, *rest):
        o32_ref, o16_ref = rest[-2:]
        t = pl.program_id(1)
        upd = lax.dot_general(_act(l_ref[...], act), r_ref[...].astype(BF16), TN, preferred_element_type=F32)

        @pl.when(t == 0)
        def _():
            o32_ref[...] = upd

        @pl.when(t > 0)
        def _():
            o32_ref[...] += upd

        @pl.when(t == n_t - 1)
        def _():
            o16_ref[...] = o32_ref[...].astype(BF16)

    if shard_lhs:
        in_specs = [pl.BlockSpec((tt, rows), lambda s, t: (t, s)), pl.BlockSpec((tt, cols), lambda s, t: (t, 0))]
    else:
        in_specs = [pl.BlockSpec((tt, rows), lambda s, t: (t, 0)), pl.BlockSpec((tt, cols), lambda s, t: (t, s))]
    args = [lhs, rhs]
    aliases = {}
    if bufs is not None:
        in_specs += [pl.BlockSpec(memory_space=pl.ANY)] * 2
        args += list(bufs)
        aliases = {2: 0, 3: 1}
    shape = (N_CHIPS, n_layers, rows, cols)
    return pl.pallas_call(
        body,
        name=name,
        grid=(N_CHIPS, n_t),
        in_specs=in_specs,
        out_specs=[pl.BlockSpec((None, None, rows, cols), lambda s, t: (s, layer, 0, 0))] * 2,
        out_shape=[jax.ShapeDtypeStruct(shape, F32), jax.ShapeDtypeStruct(shape, BF16)],
        input_output_aliases=aliases,
        compiler_params=_params(2),
    )(*args)


def _chip_peers(x, y):
    return [(1 - x, y), (x, 1 - y), (1 - x, 1 - y)]


def _remote(src, dst, sems, s, peer):
    return pltpu.make_async_remote_copy(
        src_ref=src, dst_ref=dst, send_sem=sems[0].at[s], recv_sem=sems[1].at[s], device_id=peer, device_id_type=MESH
    )


class _Exchange:
    def __init__(self, operands, n_alias, new_shapes, n_sems, build):
        self.operands, self.n_alias, self.new_shapes, self.n_sems, self.build = operands, n_alias, new_shapes, n_sems, build

    def out_shapes(self):
        return [jax.ShapeDtypeStruct(a.shape, a.dtype) for a in self.operands[: self.n_alias]] + list(self.new_shapes)

    def scratch(self):
        return [pltpu.SemaphoreType.DMA((n,)) for n in self.n_sems]

    def run(self, ins, outs, sems, first, last):
        starts, recvs, sends, locals_ = self.build(ins, outs, sems)

        def start_all():
            for cp in starts:
                cp.start()

        def wait_all():
            for cp in recvs:
                cp.wait_recv()
            for cp in sends:
                cp.wait_send()
            for cp in locals_:
                cp.wait()

        if first is True:
            start_all()
            return wait_all
        pl.when(first)(start_all)
        return lambda: pl.when(last)(wait_all)


def _gather_exchange(shards, bufs, plan):
    n_arr = len(shards)
    n_cp = len(plan) * (N_CHIPS - 1)

    def build(ins, outs, sems):
        shard_refs = ins[-n_arr:]
        x, y, c = lax.axis_index("x"), lax.axis_index("y"), lax.axis_index("c")
        me = 2 * x + y
        starts, recvs, sends, locals_ = [], [], [], []
        for p, (a, l0, n) in enumerate(plan):
            src = shard_refs[a].at[pl.ds(l0, n)]
            cp = pltpu.make_async_copy(src, outs[a].at[me, pl.ds(l0, n)], sems[2].at[p])
            locals_.append(cp)
            for k, (px, py) in enumerate(_chip_peers(x, y)):
                s = p * (N_CHIPS - 1) + k
                sends.append(_remote(src, outs[a].at[me, pl.ds(l0, n)], sems, s, (px, py, c)))
                recvs.append(_remote(src, outs[a].at[2 * px + py, pl.ds(l0, n)], sems, s, (px, py, c)))
        return locals_ + sends, recvs, sends, locals_

    if bufs is None:
        new = [jax.ShapeDtypeStruct((N_CHIPS,) + s.shape, s.dtype) for s in shards]
        return _Exchange(list(shards), 0, new, [n_cp, n_cp, len(plan)], build)
    return _Exchange(list(bufs) + list(shards), n_arr, [], [n_cp, n_cp, len(plan)], build)


def _scatter_exchange(g16, recv, plan, small=None):
    n_arr = len(g16)
    have = [r for r in recv if r is not None]
    made = [a for a in range(n_arr) if recv[a] is None]
    n_cp = len(plan) * (N_CHIPS - 1) + (N_DEV - 1 if small is not None else 0)

    def build(ins, outs, sems):
        g_refs = ins[:n_arr]
        recv_refs, it_have, it_made = [], iter(outs[n_arr : n_arr + len(have)]), iter(outs[n_arr + len(have) :])
        for a in range(n_arr):
            recv_refs.append(next(it_made) if recv[a] is None else next(it_have))
        x, y, c = lax.axis_index("x"), lax.axis_index("y"), lax.axis_index("c")
        me = 2 * x + y
        starts, recvs, sends, locals_ = [], [], [], []
        for p, (a, l0, n) in enumerate(plan):
            for k, (px, py) in enumerate(_chip_peers(x, y)):
                s = p * (N_CHIPS - 1) + k
                dst = recv_refs[a].at[k, pl.ds(l0, n)]
                sends.append(_remote(g_refs[a].at[2 * px + py, pl.ds(l0, n)], dst, sems, s, (px, py, c)))
                recvs.append(_remote(g_refs[a].at[me, pl.ds(l0, n)], dst, sems, s, (px, py, c)))
        if small is not None:
            small_ref, all_ref = ins[-1], outs[-1]
            slot = 4 * x + 2 * y + c
            locals_.append(pltpu.make_async_copy(small_ref, all_ref.at[slot], sems[2].at[0]))
            flips = [(fx, fy, fc) for fx in (0, 1) for fy in (0, 1) for fc in (0, 1)][1:]
            for k, (fx, fy, fc) in enumerate(flips):
                s = len(plan) * (N_CHIPS - 1) + k
                px, py, pc = x ^ fx, y ^ fy, c ^ fc
                sends.append(_remote(small_ref, all_ref.at[slot], sems, s, (px, py, pc)))
                recvs.append(_remote(small_ref, all_ref.at[4 * px + 2 * py + pc], sems, s, (px, py, pc)))
        return locals_ + sends, recvs, sends, locals_

    operands = list(g16) + have + ([small] if small is not None else [])
    new = [jax.ShapeDtypeStruct((N_CHIPS - 1,) + g16[a].shape[1:], BF16) for a in made]
    if small is not None:
        new.append(jax.ShapeDtypeStruct((N_DEV,) + small.shape, F32))
    return _Exchange(operands, n_arr + len(have), new, [n_cp, n_cp, 1], build)


def _call_with_exchange(body, exch, name, grid, in_specs, out_specs, out_shape, scratch_shapes, args):
    n_in, n_out, n_scr = len(in_specs), len(out_shape), len(scratch_shapes)
    if exch is None:
        outs = pl.pallas_call(
            body, name=name, grid=grid, in_specs=in_specs, out_specs=out_specs, out_shape=out_shape,
            scratch_shapes=scratch_shapes, compiler_params=_params(len(grid)),
        )(*args)
        return outs, []
    e_shapes = exch.out_shapes()
    e_in, e_out = len(exch.operands), len(e_shapes)

    def wrapped(*refs):
        ins, refs = refs[:n_in], refs[n_in:]
        e_ins, refs = refs[:e_in], refs[e_in:]
        outs, refs = refs[:n_out], refs[n_out:]
        e_outs, refs = refs[:e_out], refs[e_out:]
        scr, sems = refs[:n_scr], refs[n_scr:]
        first, last = True, True
        for d, g in enumerate(grid):
            first = (pl.program_id(d) == 0) & first
            last = (pl.program_id(d) == g - 1) & last
        finish = exch.run(e_ins, e_outs, sems, first, last)
        body(*ins, *outs, *scr)
        finish()

    any_spec = pl.BlockSpec(memory_space=pl.ANY)
    outs = pl.pallas_call(
        wrapped,
        name=name,
        grid=grid,
        in_specs=list(in_specs) + [any_spec] * e_in,
        out_specs=list(out_specs) + [any_spec] * e_out,
        out_shape=list(out_shape) + e_shapes,
        input_output_aliases={n_in + i: n_out + i for i in range(exch.n_alias)},
        scratch_shapes=list(scratch_shapes) + exch.scratch(),
        compiler_params=pltpu.CompilerParams(
            dimension_semantics=("arbitrary",) * len(grid), vmem_limit_bytes=VMEM_LIMIT, has_side_effects=True
        ),
    )(*args, *exch.operands)
    return outs[:n_out], outs[n_out:]


def _exchange_only(exch, name):
    n_in = len(exch.operands)
    shapes = exch.out_shapes()

    def body(*refs):
        ins, outs, sems = refs[:n_in], refs[n_in : n_in + len(shapes)], refs[n_in + len(shapes) :]
        exch.run(ins, outs, sems, True, True)()

    any_spec = pl.BlockSpec(memory_space=pl.ANY)
    return pl.pallas_call(
        body,
        name=name,
        in_specs=[any_spec] * n_in,
        out_specs=[any_spec] * len(shapes),
        out_shape=shapes,
        input_output_aliases={i: i for i in range(exch.n_alias)},
        scratch_shapes=exch.scratch(),
        compiler_params=pltpu.CompilerParams(has_side_effects=True),
    )(*exch.operands)


ATTN_LANE_TILES = 2
ATTN_UNROLL = 8


MASKED = -1e30


def _hi_lo(x):
    hi = x.astype(BF16)
    lo = (x - hi.astype(F32)).astype(BF16)
    return jnp.concatenate([hi, lo], axis=1)


def _suffix_matrix(inclusive):
    j = lax.broadcasted_iota(jnp.int32, (2 * Q_TILE, 2 * Q_TILE), 0) & (Q_TILE - 1)
    s = lax.broadcasted_iota(jnp.int32, (2 * Q_TILE, 2 * Q_TILE), 1)
    later = (j >= s) if inclusive else (j > s)
    return jnp.where((s >= Q_TILE) | later, 1.0, 0.0).astype(BF16)


def _log_beta(z):
    return jnp.minimum(z, 0.0) - jnp.log(1.0 + jnp.exp(-jnp.abs(z)))


def _head_masks(width):
    lane = lax.broadcasted_iota(jnp.int32, (1, width), 1)
    return [(lane >= h * HEAD_DIM) & (lane < (h + 1) * HEAD_DIM) for h in range(width // HEAD_DIM)]


def _per_head_rows(x, masks):
    return jnp.concatenate([jnp.where(hm, x, 0) for hm in masks], axis=0)


def _heads_to_lanes(x, n_heads):
    return jnp.concatenate([x[h * Q_TILE : (h + 1) * Q_TILE] for h in range(n_heads)], axis=1)


def _block_start(kb):
    return kb * Q_TILE if isinstance(kb, int) else pl.multiple_of(kb * Q_TILE, Q_TILE)


def _clamp(i, n):
    return jnp.minimum(i, n - 1)


def _next_block(pos):
    qi, kb = pos
    row_done = kb == 0
    nqi = jnp.where(row_done, qi + 1, qi)
    return nqi, jnp.where(row_done, nqi, kb - 1)


def _stream_unroll(n_blocks):
    return next(u for u in (ATTN_UNROLL, 2, 1) if n_blocks % u == 0)


def _past_mask(rows):
    t = lax.broadcasted_iota(jnp.int32, (rows, Q_TILE), 0) & (Q_TILE - 1)
    s = lax.broadcasted_iota(jnp.int32, (rows, Q_TILE), 1)
    return s < t


def _attn_fwd(qkv, n_seq, S, D, exch, name):
    T = n_seq * S
    width = min(D, ATTN_LANE_TILES * LANES)
    n_heads = width // HEAD_DIM
    rows = n_heads * Q_TILE
    nq = S // Q_TILE
    groups = D // width
    n_blocks = nq * (nq + 1) // 2
    unroll = _stream_unroll(n_blocks)
    scale = HEAD_DIM ** -0.5

    n_trips = n_blocks // unroll

    def body(q_ref, k_ref, v_ref, o_ref, a_out, b_out, qh_scr, vh_scr, bias_scr, a_stage, b_stage, sems):
        masks = _head_masks(width)
        sfx = _suffix_matrix(False)
        stream = pl.program_id(0) * groups + pl.program_id(1)

        def per_head_tables(i, c):
            blk = pl.ds(_block_start(i), Q_TILE)
            qh_scr[i] = _per_head_rows(q_ref[blk, :] * scale, masks)
            vh_scr[i] = _per_head_rows(v_ref[blk, :], masks)
            return c

        lax.fori_loop(0, nq, per_head_tables, 0)
        bias_scr[0] = jnp.zeros((rows, Q_TILE), F32)
        bias_scr[1] = jnp.where(_past_mask(rows), 0.0, MASKED)

        def save(n, slot):
            blocks = pl.ds(n * unroll, unroll)
            return [
                pltpu.make_async_copy(a_stage.at[slot], a_out.at[stream, blocks], sems.at[slot]),
                pltpu.make_async_copy(b_stage.at[slot], b_out.at[stream, blocks], sems.at[2 + slot]),
            ]

        def scores(pos):
            qi, kb = pos
            kt = k_ref[pl.ds(_block_start(_clamp(kb, nq)), Q_TILE), :]
            z = lax.dot_general(qh_scr[_clamp(qi, nq)], kt, NT, preferred_element_type=F32)
            z = z + bias_scr[(kb == qi).astype(jnp.int32)]
            lb = _log_beta(z)
            return lb, _hi_lo(lb - z)

        def weigh(pos, st, carry, acc, slot, u):
            qi, kb = pos
            lb, l1 = st
            r = jnp.dot(l1, sfx, preferred_element_type=F32)
            carry = jnp.where(kb == qi, 0.0, carry)
            a = jnp.exp(lb + r[:, :Q_TILE] + carry).astype(BF16)
            a_stage[slot, u] = a
            b_stage[slot, u] = jnp.exp(lb).astype(BF16)
            acc = jnp.where(kb == qi, 0.0, acc) + jnp.dot(
                _heads_to_lanes(a, n_heads), vh_scr[_clamp(kb, nq)], preferred_element_type=F32
            )
            o_ref[pl.ds(_block_start(_clamp(qi, nq)), Q_TILE), :] = acc
            return carry + r[:, Q_TILE:], acc

        def trip(n, c):
            pos, st, carry, acc = c
            slot = n % 2

            @pl.when(n >= 2)
            def _():
                for cp in save(n - 2, slot):
                    cp.wait()

            for u in range(unroll):
                nxt = _next_block(pos)
                st_nxt = scores(nxt)
                carry, acc = weigh(pos, st, carry, acc, slot, u)
                pos, st = nxt, st_nxt
            for cp in save(n, slot):
                cp.start()
            return pos, st, carry, acc

        first = (jnp.int32(0), jnp.int32(0))
        zero = bias_scr[0]
        init = (first, scores(first), zero, jnp.concatenate([zero[:Q_TILE]] * (width // Q_TILE), axis=1))
        lax.fori_loop(0, n_trips, trip, init)
        for n in range(max(n_trips - 2, 0), n_trips):
            for cp in save(n, n % 2):
                cp.wait()

    seq = lambda col0: pl.BlockSpec((S, width), lambda b, p: (b, col0 + p))
    saved = jax.ShapeDtypeStruct((n_seq * groups, n_blocks, rows, Q_TILE), BF16)
    stage = pltpu.VMEM((2, unroll, rows, Q_TILE), BF16)
    (o, a_w, beta), moved = _call_with_exchange(
        body,
        exch,
        name,
        grid=(n_seq, groups),
        in_specs=[seq(0), seq(groups), seq(2 * groups)],
        out_specs=[seq(0), pl.BlockSpec(memory_space=pl.ANY), pl.BlockSpec(memory_space=pl.ANY)],
        out_shape=[jax.ShapeDtypeStruct((T, D), F32), saved, saved],
        scratch_shapes=[
            pltpu.VMEM((nq, rows, width), BF16),
            pltpu.VMEM((nq, rows, width), BF16),
            pltpu.VMEM((2, rows, Q_TILE), F32),
            stage,
            stage,
            pltpu.SemaphoreType.DMA((4,)),
        ],
        args=(qkv, qkv, qkv),
    )
    return (o, a_w, beta), moved


def _attn_bwd(qkv, fwd, do, n_seq, S, D, exch, name):
    o, a_w, beta = fwd
    T = n_seq * S
    width = min(D, ATTN_LANE_TILES * LANES)
    n_heads = width // HEAD_DIM
    rows = n_heads * Q_TILE
    nq = S // Q_TILE
    groups = D // width
    n_blocks = nq * (nq + 1) // 2
    unroll = _stream_unroll(n_blocks)
    scale = HEAD_DIM ** -0.5

    n_trips = n_blocks // unroll

    def body(q_ref, k_ref, v_ref, o_ref, do_ref, a_in, b_in, dq_ref, dk_ref, dv_ref, dk_acc, dv_acc, qh_scr, doh_scr, delta_scr, a_stage, b_stage, sems):
        masks = _head_masks(width)
        sfx_incl = _suffix_matrix(True)
        stream = pl.program_id(0) * groups + pl.program_id(1)

        def fetch(n, slot):
            blocks = pl.ds(n * unroll, unroll)
            return [
                pltpu.make_async_copy(a_in.at[stream, blocks], a_stage.at[slot], sems.at[slot]),
                pltpu.make_async_copy(b_in.at[stream, blocks], b_stage.at[slot], sems.at[2 + slot]),
            ]

        for cp in fetch(0, 0):
            cp.start()
        dk_acc[...] = jnp.zeros_like(dk_acc)
        dv_acc[...] = jnp.zeros_like(dv_acc)

        def per_head_tables(i, c):
            blk = pl.ds(_block_start(i), Q_TILE)
            do = do_ref[blk, :]
            qh_scr[i] = _per_head_rows(q_ref[blk, :] * scale, masks)
            doh_scr[i] = _per_head_rows(do, masks)
            prod = do.astype(F32) * o_ref[blk, :]
            delta = jnp.concatenate(
                [jnp.sum(jnp.where(hm, prod, 0.0), axis=-1, keepdims=True) for hm in masks], axis=0
            )
            delta_scr[i] = jnp.broadcast_to(delta, (rows, Q_TILE))
            return c

        lax.fori_loop(0, nq, per_head_tables, 0)

        def weigh(pos, ab, beta, c2, dq):
            qi, kb = pos
            first = kb == qi
            blk = pl.ds(_block_start(kb), Q_TILE)
            g = ab.astype(F32) * lax.dot_general(doh_scr[qi], v_ref[blk, :], NT, preferred_element_type=F32)
            r2 = jnp.dot(_hi_lo(g), sfx_incl, preferred_element_type=F32)
            c2 = jnp.where(first, 0.0, c2)
            earlier = delta_scr[qi] - (r2[:, :Q_TILE] + c2)
            beta = beta.astype(F32)
            dzb = (g * (1.0 - beta) - earlier * beta).astype(BF16)
            kh = _per_head_rows(k_ref[blk, :], masks)
            dq = jnp.where(first, 0.0, dq) + jnp.dot(_heads_to_lanes(dzb, n_heads), kh, preferred_element_type=F32)
            dq_ref[pl.ds(_block_start(qi), Q_TILE), :] = (dq * scale).astype(BF16)
            dk_acc[blk, :] += lax.dot_general(dzb, qh_scr[qi], TN, preferred_element_type=F32)
            dv_acc[blk, :] += lax.dot_general(ab, doh_scr[qi], TN, preferred_element_type=F32)
            return c2 + r2[:, Q_TILE:], dq

        def trip(n, c):
            pos, c2, dq = c
            slot = n % 2
            for cp in fetch(n, slot):
                cp.wait()

            @pl.when(n + 1 < n_trips)
            def _():
                for cp in fetch(n + 1, 1 - slot):
                    cp.start()

            for u in range(unroll):
                c2, dq = weigh(pos, a_stage[slot, u], b_stage[slot, u], c2, dq)
                pos = _next_block(pos)
            return pos, c2, dq

        zero = dk_acc[pl.ds(0, Q_TILE), :]
        init = ((jnp.int32(0), jnp.int32(0)), jnp.concatenate([zero[:, :Q_TILE]] * n_heads, axis=0), zero)
        lax.fori_loop(0, n_trips, trip, init)
        dk_ref[...] = dk_acc[...].astype(BF16)
        dv_ref[...] = dv_acc[...].astype(BF16)

    seq = lambda col0: pl.BlockSpec((S, width), lambda b, p: (b, col0 + p))
    return _call_with_exchange(
        body,
        exch,
        name,
        grid=(n_seq, groups),
        in_specs=[seq(0), seq(groups), seq(2 * groups), seq(0), seq(0)] + [pl.BlockSpec(memory_space=pl.ANY)] * 2,
        out_specs=[seq(0)] * 3,
        out_shape=[jax.ShapeDtypeStruct((T, D), BF16)] * 3,
        scratch_shapes=[
            pltpu.VMEM((S, width), F32),
            pltpu.VMEM((S, width), F32),
            pltpu.VMEM((nq, rows, width), BF16),
            pltpu.VMEM((nq, rows, width), BF16),
            pltpu.VMEM((nq, rows, Q_TILE), F32),
            pltpu.VMEM((2, unroll, rows, Q_TILE), BF16),
            pltpu.VMEM((2, unroll, rows, Q_TILE), BF16),
            pltpu.SemaphoreType.DMA((4,)),
        ],
        args=(qkv, qkv, qkv, o, do, a_w, beta),
    )


def _causal_ws(ws_ref, g):
    t = lax.broadcasted_iota(jnp.int32, (SGU_CHUNK, SGU_CHUNK), 0)
    s = lax.broadcasted_iota(jnp.int32, (SGU_CHUNK, SGU_CHUNK), 1)
    return jnp.where(s <= t, ws_ref[g], 0.0)


def _sgu_fwd(a, gain, ws, bsb, name):
    T, F2 = a.shape
    F = F2 // 2
    gw = F // SGU_GROUPS

    def body(a_ref, gain_ref, ws_ref, bsb_ref, y_ref):
        v = _gelu(a_ref[:, F:].astype(F32))
        vn = (v * _rstd(v) * gain_ref[...]).astype(BF16)
        for g in range(SGU_GROUPS):
            cs = slice(g * gw, (g + 1) * gw)
            w = _causal_ws(ws_ref, g).astype(BF16)
            mixed = jnp.dot(w, vn[:, cs], preferred_element_type=F32) + bsb_ref[g]
            y_ref[:, cs] = (_gelu(a_ref[:, cs].astype(F32)) * mixed).astype(BF16)

    return pl.pallas_call(
        body,
        name=name,
        grid=(T // SGU_CHUNK,),
        in_specs=[
            pl.BlockSpec((SGU_CHUNK, F2), lambda i: (i, 0)),
            pl.BlockSpec((1, F), lambda i: (0, 0)),
            pl.BlockSpec((SGU_GROUPS, SGU_CHUNK, SGU_CHUNK), lambda i: (0, 0, 0)),
            pl.BlockSpec((SGU_GROUPS, SGU_CHUNK, gw), lambda i: (0, 0, 0)),
        ],
        out_specs=pl.BlockSpec((SGU_CHUNK, F), lambda i: (i, 0)),
        out_shape=jax.ShapeDtypeStruct((T, F), BF16),
        compiler_params=_params(1),
    )(a, gain.reshape(1, F), ws, bsb)


def _sgu_bwd(a, dy, gain, ws, bsb, name):
    T, F2 = a.shape
    F = F2 // 2
    gw = F // SGU_GROUPS

    def body(a_ref, dy_ref, gain_ref, ws_ref, bsb_ref, da_ref, dws_ref, dbs_ref, dgain_ref, dvn_ref):
        @pl.when(pl.program_id(0) == 0)
        def _():
            dws_ref[...] = jnp.zeros_like(dws_ref)
            dbs_ref[...] = jnp.zeros_like(dbs_ref)
            dgain_ref[...] = jnp.zeros_like(dgain_ref)

        av = a_ref[:, F:].astype(F32)
        v = _gelu(av)
        rstd = _rstd(v)
        vh = v * rstd
        gain = gain_ref[...]
        vn = (vh * gain).astype(BF16)
        ones = jnp.ones((gw, SGU_CHUNK), BF16)
        for g in range(SGU_GROUPS):
            cs = slice(g * gw, (g + 1) * gw)
            w = _causal_ws(ws_ref, g).astype(BF16)
            mixed = jnp.dot(w, vn[:, cs], preferred_element_type=F32) + bsb_ref[g]
            au = a_ref[:, cs].astype(F32)
            dyc = dy_ref[:, cs].astype(F32)
            da_ref[:, cs] = (dyc * mixed * _gelu_grad(au)).astype(BF16)
            dm = (dyc * _gelu(au)).astype(BF16)
            dbs_ref[g] += jnp.dot(dm, ones, preferred_element_type=F32)
            dws_ref[g] += _causal_mask_f32(lax.dot_general(dm, vn[:, cs], NT, preferred_element_type=F32))
            dvn_ref[:, cs] = lax.dot_general(w, dm, TN, preferred_element_type=F32)
        dvn = dvn_ref[...]
        dgain_ref[...] += jnp.sum(dvn * vh, axis=0, keepdims=True)
        dvh = dvn * gain
        dv = rstd * (dvh - vh * jnp.mean(dvh * vh, axis=-1, keepdims=True))
        da_ref[:, F:] = (dv * _gelu_grad(av)).astype(BF16)

    acc_spec = pl.BlockSpec((SGU_GROUPS, SGU_CHUNK, SGU_CHUNK), lambda i: (0, 0, 0))
    acc_shape = jax.ShapeDtypeStruct((SGU_GROUPS, SGU_CHUNK, SGU_CHUNK), F32)
    return pl.pallas_call(
        body,
        name=name,
        grid=(T // SGU_CHUNK,),
        in_specs=[
            pl.BlockSpec((SGU_CHUNK, F2), lambda i: (i, 0)),
            pl.BlockSpec((SGU_CHUNK, F), lambda i: (i, 0)),
            pl.BlockSpec((1, F), lambda i: (0, 0)),
            acc_spec,
            pl.BlockSpec((SGU_GROUPS, SGU_CHUNK, gw), lambda i: (0, 0, 0)),
        ],
        out_specs=[
            pl.BlockSpec((SGU_CHUNK, F2), lambda i: (i, 0)),
            acc_spec,
            acc_spec,
            pl.BlockSpec((1, F), lambda i: (0, 0)),
        ],
        out_shape=[
            jax.ShapeDtypeStruct((T, F2), BF16),
            acc_shape,
            acc_shape,
            jax.ShapeDtypeStruct((1, F), F32),
        ],
        scratch_shapes=[pltpu.VMEM((SGU_CHUNK, F), F32)],
        compiler_params=_params(1),
    )(a, dy, gain.reshape(1, F), ws, bsb)


def _causal_mask_f32(m):
    t = lax.broadcasted_iota(jnp.int32, m.shape, 0)
    s = lax.broadcasted_iota(jnp.int32, m.shape, 1)
    return jnp.where(s <= t, m, 0.0)


def _final_loss(x, gain, target, name):
    T, D = x.shape
    tm = min(T, ROW_TILE // 2)

    def body(x_ref, g_ref, t_ref, sq_ref, dx_ref, dxb_ref, dg_ref):
        xv = x_ref[...]
        gain = g_ref[...]
        err = xv * _rstd(xv) * gain - t_ref[...]
        dx, dg = _norm_bwd(err * (1.0 / D), xv, gain)
        dx_ref[...] = dx
        dxb_ref[...] = dx.astype(BF16)
        sq = jnp.sum(err * err, axis=0, keepdims=True)

        @pl.when(pl.program_id(0) == 0)
        def _():
            sq_ref[...] = sq
            dg_ref[...] = dg

        @pl.when(pl.program_id(0) > 0)
        def _():
            sq_ref[...] += sq
            dg_ref[...] += dg

    row = pl.BlockSpec((tm, D), lambda i: (i, 0))
    vec = pl.BlockSpec((1, D), lambda i: (0, 0))
    return pl.pallas_call(
        body,
        name=name,
        grid=(T // tm,),
        in_specs=[row, vec, row],
        out_specs=[vec, row, row, vec],
        out_shape=[
            jax.ShapeDtypeStruct((1, D), F32),
            jax.ShapeDtypeStruct((T, D), F32),
            jax.ShapeDtypeStruct((T, D), BF16),
            jax.ShapeDtypeStruct((1, D), F32),
        ],
        compiler_params=_params(1),
    )(x, gain.reshape(1, D), target)


def _row_tile(rows, cols, n_arrays):
    budget = VMEM_LIMIT // 2 // (2 * n_arrays * cols * 4)
    tr = rows
    while tr > budget and tr % 16 == 0:
        tr //= 2
    return tr


def _sum_received(own, recv, name):
    R, C = own.shape
    n = recv.shape[0]
    tr = _row_tile(R, C, n + 2)

    def body(own_ref, recv_ref, o_ref):
        s = own_ref[...]
        for k in range(n):
            s = s + recv_ref[k].astype(F32)
        o_ref[...] = s

    return pl.pallas_call(
        body,
        name=name,
        grid=(R // tr,),
        in_specs=[pl.BlockSpec((tr, C), lambda i: (i, 0)), pl.BlockSpec((n, tr, C), lambda i: (0, i, 0))],
        out_specs=pl.BlockSpec((tr, C), lambda i: (i, 0)),
        out_shape=jax.ShapeDtypeStruct((R, C), F32),
        compiler_params=_params(1),
    )(own, recv)


def _sum_chip_shard(g32, recv, chip, name):
    _, L, r, c = g32.shape
    n = recv.shape[0]
    tr = _row_tile(r, c, n + 2)

    def body(chip_ref, own_ref, recv_ref, o_ref):
        s = own_ref[...]
        for k in range(n):
            s = s + recv_ref[k].astype(F32)
        o_ref[...] = s

    return pl.pallas_call(
        body,
        name=name,
        grid_spec=pltpu.PrefetchScalarGridSpec(
            num_scalar_prefetch=1,
            grid=(L, r // tr),
            in_specs=[
                pl.BlockSpec((None, None, tr, c), lambda l, i, chip_ref: (chip_ref[0], l, i, 0)),
                pl.BlockSpec((n, None, tr, c), lambda l, i, chip_ref: (0, l, i, 0)),
            ],
            out_specs=pl.BlockSpec((None, tr, c), lambda l, i, chip_ref: (l, i, 0)),
        ),
        out_shape=jax.ShapeDtypeStruct((L, r, c), F32),
        compiler_params=_params(2),
    )(chip.reshape(1).astype(jnp.int32), g32, recv)


def _adamw(w, m, v, parts, name):
    R, C = w.shape
    n = len(parts)
    tr = _row_tile(R, C, n + 7)

    def body(*refs):
        w_ref, m_ref, v_ref = refs[:3]
        g_ref, d_ref, nm_ref, nv_ref = refs[3 + n :]
        g = refs[3][...]
        for p_ref in refs[4 : 3 + n]:
            g = g + p_ref[...]
        nm = ADAM_B1 * m_ref[...] + (1.0 - ADAM_B1) * g
        nv = ADAM_B2 * v_ref[...] + (1.0 - ADAM_B2) * (g * g)
        m_hat = nm / (1.0 - ADAM_B1**ADAM_STEP)
        v_hat = nv / (1.0 - ADAM_B2**ADAM_STEP)
        g_ref[...] = g
        d_ref[...] = -ADAM_LR * (m_hat / (jnp.sqrt(v_hat) + ADAM_EPS) + ADAM_WD * w_ref[...])
        nm_ref[...] = nm
        nv_ref[...] = nv

    spec = pl.BlockSpec((tr, C), lambda i: (i, 0))
    return pl.pallas_call(
        body,
        name=name,
        grid=(R // tr,),
        in_specs=[spec] * (3 + n),
        out_specs=[spec] * 4,
        out_shape=[jax.ShapeDtypeStruct((R, C), F32)] * 4,
        compiler_params=_params(1),
    )(w, m, v, *parts)


def _swap_with_sibling(parts, name):
    n = len(parts)

    def body(*refs):
        ins, outs = refs[:n], refs[n : 2 * n]
        send_sems, recv_sems = refs[2 * n :]
        sibling = (lax.axis_index("x"), lax.axis_index("y"), 1 - lax.axis_index("c"))
        copies = [
            pltpu.make_async_remote_copy(
                src_ref=ins[a],
                dst_ref=outs[a],
                send_sem=send_sems.at[a],
                recv_sem=recv_sems.at[a],
                device_id=sibling,
                device_id_type=MESH,
            )
            for a in range(n)
        ]
        for cp in copies:
            cp.start()
        for cp in copies:
            cp.wait_recv()
        for cp in copies:
            cp.wait_send()

    any_spec = pl.BlockSpec(memory_space=pl.ANY)
    return pl.pallas_call(
        body,
        name=name,
        in_specs=[any_spec] * n,
        out_specs=[any_spec] * n,
        out_shape=[jax.ShapeDtypeStruct(p.shape, p.dtype) for p in parts],
        scratch_shapes=[pltpu.SemaphoreType.DMA((n,)), pltpu.SemaphoreType.DMA((n,))],
        compiler_params=pltpu.CompilerParams(has_side_effects=True),
    )(*parts)


def _pack(pieces):
    flat = jnp.concatenate([p.reshape(-1) for p in pieces])
    return flat.reshape(-1, LANES)


def _unpack(packed, shapes):
    flat = packed.reshape(-1)
    out, off = [], 0
    for s in shapes:
        size = 1
        for d in s:
            size *= d
        out.append(flat[off : off + size].reshape(s))
        off += size
    return out


def kernel(x, norm_mix, norm_mlp, sb_wqkv, sb_wo, sgu_win, sgu_gain, sgu_ws, sgu_bs, sgu_wout, mlp_w1, mlp_w2, final_norm, loss_target, m_norm_mix, m_norm_mlp, m_sb_wqkv, m_sb_wo, m_sgu_win, m_sgu_gain, m_sgu_ws, m_sgu_bs, m_sgu_wout, m_mlp_w1, m_mlp_w2, m_final_norm, v_norm_mix, v_norm_mlp, v_sb_wqkv, v_sb_wo, v_sgu_win, v_sgu_gain, v_sgu_ws, v_sgu_bs, v_sgu_wout, v_mlp_w1, v_mlp_w2, v_final_norm):
    n_seq, S, D = x.shape
    T = n_seq * S
    depth = norm_mix.shape[0]
    n_sgu = sgu_win.shape[0]
    F = sgu_wout.shape[1] * N_CHIPS
    gw = F // SGU_GROUPS
    chip = 2 * lax.axis_index("x") + lax.axis_index("y")

    QKV, WO, WIN, WOUT, W1, W2, GAIN = range(7)
    big = [sb_wqkv, sb_wo, sgu_win, sgu_wout, mlp_w1, mlp_w2]
    n_sb = sb_wqkv.shape[0]
    shards = [w.astype(BF16) for w in big] + [sgu_gain.reshape(1, -1, LANES)]
    def gather_plan(i):
        j, mlp = i // 2, min(2, depth - i)
        plan = [(WO, j, 1), (W1, i, mlp), (W2, i, mlp)]
        if i + 1 < depth:
            plan += [(WIN, (i + 1) // 2, 1), (WOUT, (i + 1) // 2, 1)]
        if j + 1 < n_sb:
            plan += [(QKV, j + 1, 1)]
        return plan

    wg = _exchange_only(_gather_exchange(shards, None, [(QKV, 0, 1), (GAIN, 0, 1)]), "gather_first_weights")
    gain_full = jnp.transpose(wg[GAIN].reshape(N_CHIPS, n_sgu, F // N_CHIPS), (1, 0, 2)).reshape(n_sgu, F)
    bsb = [jnp.broadcast_to(sgu_bs[j][:, :, None], (SGU_GROUPS, SGU_CHUNK, gw)) for j in range(n_sgu)]

    xs = x.reshape(T, D)
    saved = []
    for i in range(depth):
        j = i // 2
        if i % 2 == 0:
            qkv, h = _norm_matmul(xs, norm_mix[i], wg[QKV], j, f"qkv_fwd_{i}")
            attn, wg = _attn_fwd(qkv, n_seq, S, D, _gather_exchange(shards, wg, gather_plan(i)), f"attn_fwd_{i}")
            wg_qkv, wg_wo, wg_win, wg_wout, wg_w1, wg_w2 = wg[:6]
            x_mid = _act_matmul_res(attn[0], wg_wo, j, xs, None, f"wo_fwd_{i}")
            mix = (qkv, attn)
        else:
            a, h = _norm_matmul(xs, norm_mix[i], wg_win, j, f"win_fwd_{i}")
            yg = _sgu_fwd(a, gain_full[j], sgu_ws[j], bsb[j], f"sgu_fwd_{i}")
            x_mid = _act_matmul_res(yg, wg_wout, j, xs, None, f"wout_fwd_{i}")
            mix = (a, yg)
        a2, h2 = _norm_matmul(x_mid, norm_mlp[i], wg_w1, i, f"w1_fwd_{i}")
        x_out = _act_matmul_res(a2, wg_w2, i, x_mid, "relu2", f"w2_fwd_{i}")
        saved.append((xs, h, mix, x_mid, h2, a2))
        xs = x_out

    sq, dx, dxb, g_final = _final_loss(xs, final_norm, loss_target.reshape(T, D), "loss_head")
    loss = lax.psum(0.5 * jnp.sum(sq) / D, ("x", "y", "c"))

    n_layers = [n_sb, n_sb, n_sgu, n_sgu, depth, depth]
    g32, g16, recv = [None] * 6, [None] * 6, [None] * 6
    done_from, sent_from = list(n_layers), list(n_layers)

    def grad(a, layer, lhs, rhs, shard_lhs, act, name):
        bufs = None if g32[a] is None else (g32[a], g16[a])
        g32[a], g16[a] = _matmul_tn(lhs, rhs, bufs, layer, n_layers[a], shard_lhs, act, name)
        done_from[a] = layer

    def unsent_plan():
        plan = [(a, done_from[a], sent_from[a] - done_from[a]) for a in range(6) if sent_from[a] > done_from[a]]
        for a, l0, _ in plan:
            sent_from[a] = l0
        return plan

    def scatter(plan, small):
        if not plan and small is None:
            return None, lambda moved: None
        arrays = sorted({a for a, _, _ in plan})
        have = [a for a in arrays if recv[a] is not None]
        made = [a for a in arrays if recv[a] is None]
        exch = _scatter_exchange(
            [g16[a] for a in arrays], [recv[a] for a in arrays], [(arrays.index(a), l0, n) for a, l0, n in plan], small
        )

        def take(moved):
            for a, buf in zip(arrays, moved):
                g16[a] = buf
            for a, buf in zip(have + made, moved[len(arrays) :]):
                recv[a] = buf
            return moved[-1]

        return exch, take

    g_mix, g_mlp = [None] * depth, [None] * depth
    g_ws, g_bs, g_gain = [None] * n_sgu, [None] * n_sgu, [None] * n_sgu
    for i in reversed(range(depth)):
        j = i // 2
        x_in, h, mix, x_mid, h2, a2 = saved[i]
        da2 = _matmul_nt(dxb, wg_w2, i, a2, f"w2_bwd_{i}")
        grad(W2, i, a2, dxb, True, "relu2", f"w2_grad_{i}")
        grad(W1, i, h2, da2, False, None, f"w1_grad_{i}")
        (dx, dxb, g_mlp[i]), _ = _matmul_nt_norm_bwd(da2, wg_w1, i, x_mid, norm_mlp[i], dx, None, f"w1_bwd_{i}")
        if i % 2 == 0:
            qkv, attn = mix
            do = _matmul_nt(dxb, wg_wo, j, None, f"wo_bwd_{i}")
            grad(WO, j, attn[0], dxb, True, None, f"wo_grad_{i}")
            sgu_small = _pack([jnp.stack(g_ws), jnp.stack(g_bs), jnp.stack(g_gain)]) if i == 0 and n_sgu else None
            exch, take = scatter(unsent_plan(), sgu_small)
            (dq, dk, dv), moved = _attn_bwd(qkv, attn, do, n_seq, S, D, exch, f"attn_bwd_{i}")
            last = take(moved)
            if sgu_small is not None:
                sgu_small_all = last
            dqkv = jnp.concatenate([dq, dk, dv], axis=1)
            grad(QKV, j, h, dqkv, False, None, f"qkv_grad_{i}")
            exch, take = scatter(unsent_plan(), None)
            (dx, dxb, g_mix[i]), moved = _matmul_nt_norm_bwd(
                dqkv, wg_qkv, j, x_in, norm_mix[i], dx, exch, f"qkv_bwd_{i}"
            )
            take(moved)
        else:
            a, yg = mix
            dyg = _matmul_nt(dxb, wg_wout, j, None, f"wout_bwd_{i}")
            grad(WOUT, j, yg, dxb, True, None, f"wout_grad_{i}")
            da, g_ws[j], dbs, g_gain[j] = _sgu_bwd(a, dyg, gain_full[j], sgu_ws[j], bsb[j], f"sgu_bwd_{i}")
            g_bs[j] = dbs[:, :, 0]
            grad(WIN, j, h, da, False, None, f"win_grad_{i}")
            (dx, dxb, g_mix[i]), _ = _matmul_nt_norm_bwd(da, wg_win, j, x_in, norm_mix[i], dx, None, f"win_bwd_{i}")
    grad_x = dx.reshape(n_seq, S, D)

    names = ["qkv", "wo", "win", "wout", "w1", "w2"]
    exch, take = scatter(unsent_plan(), _pack([jnp.stack(g_mix), jnp.stack(g_mlp), g_final]))
    norm_small_all = take(_exchange_only(exch, "gather_norm_grads"))
    partial = [_sum_chip_shard(g32[a], recv[a], chip, f"sum_{names[a]}") for a in range(6)]
    partial = [p.reshape(-1, p.shape[-1]) for p in partial]
    theirs = _swap_with_sibling(partial, "swap_partial_sums")
    g_small = _unpack(
        _sum_received(norm_small_all[0], norm_small_all[1:], "sum_norm_grads"),
        [norm_mix.shape, norm_mlp.shape, final_norm.shape],
    ) + _unpack(
        _sum_received(sgu_small_all[0], sgu_small_all[1:], "sum_sgu_small_grads"),
        [sgu_ws.shape, sgu_bs.shape, (n_sgu, F)],
    )

    ms = [m_sb_wqkv, m_sb_wo, m_sgu_win, m_sgu_wout, m_mlp_w1, m_mlp_w2]
    vs = [v_sb_wqkv, v_sb_wo, v_sgu_win, v_sgu_wout, v_mlp_w1, v_mlp_w2]
    res = {}
    keys = ["sb_wqkv", "sb_wo", "sgu_win", "sgu_wout", "mlp_w1", "mlp_w2"]
    for key, k, w, m, v, mine, other in zip(keys, names, big, ms, vs, partial, theirs):
        cols = w.shape[-1]
        outs = _adamw(w.reshape(-1, cols), m.reshape(-1, cols), v.reshape(-1, cols), [mine, other], f"adamw_{k}")
        res[key] = [o.reshape(w.shape) for o in outs]

    g_small[5] = lax.dynamic_slice_in_dim(g_small[5], chip * (F // N_CHIPS), F // N_CHIPS, axis=1)
    small_keys = ["norm_mix", "norm_mlp", "final_norm", "sgu_ws", "sgu_bs", "sgu_gain"]
    small_w = [norm_mix, norm_mlp, final_norm, sgu_ws, sgu_bs, sgu_gain]
    small_m = [m_norm_mix, m_norm_mlp, m_final_norm, m_sgu_ws, m_sgu_bs, m_sgu_gain]
    small_v = [v_norm_mix, v_norm_mlp, v_final_norm, v_sgu_ws, v_sgu_bs, v_sgu_gain]
    outs = _adamw(_pack(small_w), _pack(small_m), _pack(small_v), [_pack(g_small)], "adamw_small")
    local_shapes = [w.shape for w in small_w]
    for key, parts in zip(small_keys, zip(*[_unpack(o, local_shapes) for o in outs])):
        res[key] = list(parts)

    order = ["norm_mix", "norm_mlp", "sb_wqkv", "sb_wo", "sgu_win", "sgu_gain", "sgu_ws", "sgu_bs", "sgu_wout", "mlp_w1", "mlp_w2", "final_norm"]
    return (loss, grad_x, *[res[k][0] for k in order], *[res[k][1] for k in order], *[res[k][2] for k in order], *[res[k][3] for k in order])
```

```python
import jax
import jax.numpy as jnp
from jax import lax
from jax.experimental import pallas as pl
from jax.experimental.pallas import tpu as pltpu

F32 = jnp.float32
BF16 = jnp.bfloat16
MESH = pl.DeviceIdType.MESH

EPS = 1e-6
HEAD_DIM = 64
LANES = 128
Q_TILE = 128
SGU_CHUNK = 128
SGU_GROUPS = 8
N_CHIPS = 4
N_DEV = 8
ADAM_LR = 0.001
ADAM_B1 = 0.9
ADAM_B2 = 0.999
ADAM_EPS = 1e-08
ADAM_WD = 0.01
ADAM_STEP = 10
GELU_C0 = 0.7978845608028654
GELU_C1 = 0.044715
VMEM_LIMIT = 48 * 1024 * 1024
ROW_TILE = 1024
NT = (((1,), (1,)), ((), ()))
TN = (((0,), (0,)), ((), ()))


def _params(n_axes):
    return pltpu.CompilerParams(dimension_semantics=("arbitrary",) * n_axes, vmem_limit_bytes=VMEM_LIMIT)


def _rstd(x):
    return lax.rsqrt(jnp.mean(x * x, axis=-1, keepdims=True) + EPS)


def _norm_bwd(dh, x, gain):
    rstd = _rstd(x)
    xh = x * rstd
    dhg = dh * gain
    dx = rstd * (dhg - xh * jnp.mean(dhg * xh, axis=-1, keepdims=True))
    return dx, jnp.sum(dh * xh, axis=0, keepdims=True)


def _gelu(x):
    return 0.5 * x * (1.0 + jnp.tanh(GELU_C0 * (x + GELU_C1 * x * x * x)))


def _gelu_grad(x):
    t = jnp.tanh(GELU_C0 * (x + GELU_C1 * x * x * x))
    return 0.5 * (1.0 + t) + 0.5 * x * (1.0 - t * t) * (GELU_C0 * (1.0 + 3.0 * GELU_C1 * x * x))


def _act(a, act):
    if act == "relu2":
        r = jnp.maximum(a.astype(F32), 0.0)
        return (r * r).astype(BF16)
    return a.astype(BF16)


def _layer_spec(wg, layer):
    nsh, _, r, c = wg.shape
    return pl.BlockSpec((nsh, None, r, c), lambda i: (0, layer, 0, 0), pipeline_mode=pl.Buffered(1))


def _norm_matmul(x, gain, wg, layer, name):
    T, D = x.shape
    nsh, _, _, ns = wg.shape
    tm = min(T, ROW_TILE // 2)

    def body(x_ref, g_ref, w_ref, y_ref, h_ref):
        xv = x_ref[...]
        h = (xv * _rstd(xv) * g_ref[...]).astype(BF16)
        h_ref[...] = h
        for j in range(nsh):
            y_ref[:, j * ns : (j + 1) * ns] = jnp.dot(h, w_ref[j], preferred_element_type=F32).astype(BF16)

    return pl.pallas_call(
        body,
        name=name,
        grid=(T // tm,),
        in_specs=[pl.BlockSpec((tm, D), lambda i: (i, 0)), pl.BlockSpec((1, D), lambda i: (0, 0)), _layer_spec(wg, layer)],
        out_specs=[pl.BlockSpec((tm, nsh * ns), lambda i: (i, 0)), pl.BlockSpec((tm, D), lambda i: (i, 0))],
        out_shape=[jax.ShapeDtypeStruct((T, nsh * ns), BF16), jax.ShapeDtypeStruct((T, D), BF16)],
        compiler_params=_params(1),
    )(x, gain.reshape(1, D), wg)


def _act_matmul_res(a, wg, layer, x_in, act, name):
    T, K = a.shape
    nsh, _, kq, D = wg.shape
    tm = min(T, ROW_TILE // 2)

    def body(a_ref, w_ref, x_ref, o_ref):
        w = w_ref[...].reshape(nsh * kq, D)
        o_ref[...] = x_ref[...] + jnp.dot(_act(a_ref[...], act), w, preferred_element_type=F32)

    return pl.pallas_call(
        body,
        name=name,
        grid=(T // tm,),
        in_specs=[pl.BlockSpec((tm, K), lambda i: (i, 0)), _layer_spec(wg, layer), pl.BlockSpec((tm, D), lambda i: (i, 0))],
        out_specs=pl.BlockSpec((tm, D), lambda i: (i, 0)),
        out_shape=jax.ShapeDtypeStruct((T, D), F32),
        compiler_params=_params(1),
    )(a, wg, x_in)


def _matmul_nt(g, wg, layer, a, name):
    T, D = g.shape
    nsh, _, kq, _ = wg.shape
    tm = min(T, ROW_TILE // 2)

    def body(g_ref, w_ref, *rest):
        gv = g_ref[...]
        for k in range(nsh):
            cols = slice(k * kq, (k + 1) * kq)
            r = lax.dot_general(gv, w_ref[k], NT, preferred_element_type=F32)
            if a is not None:
                r = r * (2.0 * jnp.maximum(rest[0][:, cols].astype(F32), 0.0))
            rest[-1][:, cols] = r.astype(BF16)

    row = pl.BlockSpec((tm, nsh * kq), lambda i: (i, 0))
    in_specs = [pl.BlockSpec((tm, D), lambda i: (i, 0)), _layer_spec(wg, layer)]
    args = [g, wg]
    if a is not None:
        in_specs.append(row)
        args.append(a)
    return pl.pallas_call(
        body,
        name=name,
        grid=(T // tm,),
        in_specs=in_specs,
        out_specs=row,
        out_shape=jax.ShapeDtypeStruct((T, nsh * kq), BF16),
        compiler_params=_params(1),
    )(*args)


def _matmul_nt_norm_bwd(da, wg, layer, x, gain, dres, exch, name):
    T, D = x.shape
    nsh, _, _, ns = wg.shape
    tm = min(T, ROW_TILE // 2)

    def body(da_ref, w_ref, x_ref, g_ref, r_ref, dx_ref, dxb_ref, dg_ref):
        dh = lax.dot_general(da_ref[:, :ns], w_ref[0], NT, preferred_element_type=F32)
        for j in range(1, nsh):
            dh = dh + lax.dot_general(da_ref[:, j * ns : (j + 1) * ns], w_ref[j], NT, preferred_element_type=F32)
        dx, dg = _norm_bwd(dh, x_ref[...], g_ref[...])
        dx = dx + r_ref[...]
        dx_ref[...] = dx
        dxb_ref[...] = dx.astype(BF16)

        @pl.when(pl.program_id(0) == 0)
        def _():
            dg_ref[...] = dg

        @pl.when(pl.program_id(0) > 0)
        def _():
            dg_ref[...] += dg

    row = pl.BlockSpec((tm, D), lambda i: (i, 0))
    vec = pl.BlockSpec((1, D), lambda i: (0, 0))
    return _call_with_exchange(
        body,
        exch,
        name,
        grid=(T // tm,),
        in_specs=[pl.BlockSpec((tm, nsh * ns), lambda i: (i, 0)), _layer_spec(wg, layer), row, vec, row],
        out_specs=[row, row, vec],
        out_shape=[
            jax.ShapeDtypeStruct((T, D), F32),
            jax.ShapeDtypeStruct((T, D), BF16),
            jax.ShapeDtypeStruct((1, D), F32),
        ],
        scratch_shapes=[],
        args=(da, wg, x, gain.reshape(1, D), dres),
    )


def _matmul_tn(lhs, rhs, bufs, layer, n_layers, shard_lhs, act, name):
    T = lhs.shape[0]
    rows = lhs.shape[1] // N_CHIPS if shard_lhs else lhs.shape[1]
    cols = rhs.shape[1] if shard_lhs else rhs.shape[1] // N_CHIPS
    tt = min(T, 2 * ROW_TILE)
    n_t = T // tt

    def body(l_ref, r_ref, *rest):
        o32_ref, o16_ref = rest[-2:]
        t = pl.program_id(1)
        upd = lax.dot_general(_act(l_ref[...], act), r_ref[...].astype(BF16), TN, preferred_element_type=F32)

        @pl.when(t == 0)
        def _():
            o32_ref[...] = upd

        @pl.when(t > 0)
        def _():
            o32_ref[...] += upd

        @pl.when(t == n_t - 1)
        def _():
            o16_ref[...] = o32_ref[...].astype(BF16)

    if shard_lhs:
        in_specs = [pl.BlockSpec((tt, rows), lambda s, t: (t, s)), pl.BlockSpec((tt, cols), lambda s, t: (t, 0))]
    else:
        in_specs = [pl.BlockSpec((tt, rows), lambda s, t: (t, 0)), pl.BlockSpec((tt, cols), lambda s, t: (t, s))]
    args = [lhs, rhs]
    aliases = {}
    if bufs is not None:
        in_specs += [pl.BlockSpec(memory_space=pl.ANY)] * 2
        args += list(bufs)
        aliases = {2: 0, 3: 1}
    shape = (N_CHIPS, n_layers, rows, cols)
    return pl.pallas_call(
        body,
        name=name,
        grid=(N_CHIPS, n_t),
        in_specs=in_specs,
        out_specs=[pl.BlockSpec((None, None, rows, cols), lambda s, t: (s, layer, 0, 0))] * 2,
        out_shape=[jax.ShapeDtypeStruct(shape, F32), jax.ShapeDtypeStruct(shape, BF16)],
        input_output_aliases=aliases,
        compiler_params=_params(2),
    )(*args)


def _chip_peers(x, y):
    return [(1 - x, y), (x, 1 - y), (1 - x, 1 - y)]


def _remote(src, dst, sems, s, peer):
    return pltpu.make_async_remote_copy(
        src_ref=src, dst_ref=dst, send_sem=sems[0].at[s], recv_sem=sems[1].at[s], device_id=peer, device_id_type=MESH
    )


class _Exchange:
    def __init__(self, operands, n_alias, new_shapes, n_sems, build):
        self.operands, self.n_alias, self.new_shapes, self.n_sems, self.build = operands, n_alias, new_shapes, n_sems, build

    def out_shapes(self):
        return [jax.ShapeDtypeStruct(a.shape, a.dtype) for a in self.operands[: self.n_alias]] + list(self.new_shapes)

    def scratch(self):
        return [pltpu.SemaphoreType.DMA((n,)) for n in self.n_sems]

    def run(self, ins, outs, sems, first, last):
        starts, recvs, sends, locals_ = self.build(ins, outs, sems)

        def start_all():
            for cp in starts:
                cp.start()

        def wait_all():
            for cp in recvs:
                cp.wait_recv()
            for cp in sends:
                cp.wait_send()
            for cp in locals_:
                cp.wait()

        if first is True:
            start_all()
            return wait_all
        pl.when(first)(start_all)
        return lambda: pl.when(last)(wait_all)


def _gather_exchange(shards, bufs, plan):
    n_arr = len(shards)
    n_cp = len(plan) * (N_CHIPS - 1)

    def build(ins, outs, sems):
        shard_refs = ins[-n_arr:]
        x, y, c = lax.axis_index("x"), lax.axis_index("y"), lax.axis_index("c")
        me = 2 * x + y
        recvs, sends, locals_ = [], [], []
        for p, (a, l0, n) in enumerate(plan):
            src = shard_refs[a].at[pl.ds(l0, n)]
            cp = pltpu.make_async_copy(src, outs[a].at[me, pl.ds(l0, n)], sems[2].at[p])
            locals_.append(cp)
            for k, (px, py) in enumerate(_chip_peers(x, y)):
                s = p * (N_CHIPS - 1) + k
                sends.append(_remote(src, outs[a].at[me, pl.ds(l0, n)], sems, s, (px, py, c)))
                recvs.append(_remote(src, outs[a].at[2 * px + py, pl.ds(l0, n)], sems, s, (px, py, c)))
        return locals_ + sends, recvs, sends, locals_

    if bufs is None:
        new = [jax.ShapeDtypeStruct((N_CHIPS,) + s.shape, s.dtype) for s in shards]
        return _Exchange(list(shards), 0, new, [n_cp, n_cp, len(plan)], build)
    return _Exchange(list(bufs) + list(shards), n_arr, [], [n_cp, n_cp, len(plan)], build)


def _scatter_exchange(g16, recv, plan, small=None):
    n_arr = len(g16)
    have = [r for r in recv if r is not None]
    made = [a for a in range(n_arr) if recv[a] is None]
    n_cp = len(plan) * (N_CHIPS - 1) + (N_DEV - 1 if small is not None else 0)

    def build(ins, outs, sems):
        g_refs = ins[:n_arr]
        recv_refs, it_have, it_made = [], iter(outs[n_arr : n_arr + len(have)]), iter(outs[n_arr + len(have) :])
        for a in range(n_arr):
            recv_refs.append(next(it_made) if recv[a] is None else next(it_have))
        x, y, c = lax.axis_index("x"), lax.axis_index("y"), lax.axis_index("c")
        me = 2 * x + y
        recvs, sends, locals_ = [], [], []
        for p, (a, l0, n) in enumerate(plan):
            for k, (px, py) in enumerate(_chip_peers(x, y)):
                s = p * (N_CHIPS - 1) + k
                dst = recv_refs[a].at[k, pl.ds(l0, n)]
                sends.append(_remote(g_refs[a].at[2 * px + py, pl.ds(l0, n)], dst, sems, s, (px, py, c)))
                recvs.append(_remote(g_refs[a].at[me, pl.ds(l0, n)], dst, sems, s, (px, py, c)))
        if small is not None:
            small_ref, all_ref = ins[-1], outs[-1]
            slot = 4 * x + 2 * y + c
            locals_.append(pltpu.make_async_copy(small_ref, all_ref.at[slot], sems[2].at[0]))
            flips = [(fx, fy, fc) for fx in (0, 1) for fy in (0, 1) for fc in (0, 1)][1:]
            for k, (fx, fy, fc) in enumerate(flips):
                s = len(plan) * (N_CHIPS - 1) + k
                px, py, pc = x ^ fx, y ^ fy, c ^ fc
                sends.append(_remote(small_ref, all_ref.at[slot], sems, s, (px, py, pc)))
                recvs.append(_remote(small_ref, all_ref.at[4 * px + 2 * py + pc], sems, s, (px, py, pc)))
        return locals_ + sends, recvs, sends, locals_

    operands = list(g16) + have + ([small] if small is not None else [])
    new = [jax.ShapeDtypeStruct((N_CHIPS - 1,) + g16[a].shape[1:], BF16) for a in made]
    if small is not None:
        new.append(jax.ShapeDtypeStruct((N_DEV,) + small.shape, F32))
    return _Exchange(operands, n_arr + len(have), new, [n_cp, n_cp, 1], build)


def _call_with_exchange(body, exch, name, grid, in_specs, out_specs, out_shape, scratch_shapes, args):
    n_in, n_out, n_scr = len(in_specs), len(out_shape), len(scratch_shapes)
    if exch is None:
        outs = pl.pallas_call(
            body, name=name, grid=grid, in_specs=in_specs, out_specs=out_specs, out_shape=out_shape,
            scratch_shapes=scratch_shapes, compiler_params=_params(len(grid)),
        )(*args)
        return outs, []
    e_shapes = exch.out_shapes()
    e_in, e_out = len(exch.operands), len(e_shapes)

    def wrapped(*refs):
        ins, refs = refs[:n_in], refs[n_in:]
        e_ins, refs = refs[:e_in], refs[e_in:]
        outs, refs = refs[:n_out], refs[n_out:]
        e_outs, refs = refs[:e_out], refs[e_out:]
        scr, sems = refs[:n_scr], refs[n_scr:]
        first, last = True, True
        for d, g in enumerate(grid):
            first = (pl.program_id(d) == 0) & first
            last = (pl.program_id(d) == g - 1) & last
        finish = exch.run(e_ins, e_outs, sems, first, last)
        body(*ins, *outs, *scr)
        finish()

    any_spec = pl.BlockSpec(memory_space=pl.ANY)
    outs = pl.pallas_call(
        wrapped,
        name=name,
        grid=grid,
        in_specs=list(in_specs) + [any_spec] * e_in,
        out_specs=list(out_specs) + [any_spec] * e_out,
        out_shape=list(out_shape) + e_shapes,
        input_output_aliases={n_in + i: n_out + i for i in range(exch.n_alias)},
        scratch_shapes=list(scratch_shapes) + exch.scratch(),
        compiler_params=pltpu.CompilerParams(
            dimension_semantics=("arbitrary",) * len(grid), vmem_limit_bytes=VMEM_LIMIT, has_side_effects=True
        ),
    )(*args, *exch.operands)
    return outs[:n_out], outs[n_out:]


def _exchange_only(exch, name):
    n_in = len(exch.operands)
    shapes = exch.out_shapes()

    def body(*refs):
        ins, outs, sems = refs[:n_in], refs[n_in : n_in + len(shapes)], refs[n_in + len(shapes) :]
        exch.run(ins, outs, sems, True, True)()

    any_spec = pl.BlockSpec(memory_space=pl.ANY)
    return pl.pallas_call(
        body,
        name=name,
        in_specs=[any_spec] * n_in,
        out_specs=[any_spec] * len(shapes),
        out_shape=shapes,
        input_output_aliases={i: i for i in range(exch.n_alias)},
        scratch_shapes=exch.scratch(),
        compiler_params=pltpu.CompilerParams(has_side_effects=True),
    )(*exch.operands)


ATTN_LANE_TILES = 2
ATTN_UNROLL = 17


MASKED = -1e30


def _hi_lo(x):
    hi = x.astype(BF16)
    lo = (x - hi.astype(F32)).astype(BF16)
    return jnp.concatenate([hi, lo], axis=1)


def _suffix_matrix(inclusive):
    j = lax.broadcasted_iota(jnp.int32, (2 * Q_TILE, 2 * Q_TILE), 0) & (Q_TILE - 1)
    s = lax.broadcasted_iota(jnp.int32, (2 * Q_TILE, 2 * Q_TILE), 1)
    later = (j >= s) if inclusive else (j > s)
    return jnp.where((s >= Q_TILE) | later, 1.0, 0.0).astype(BF16)


def _log_beta(z):
    return jnp.minimum(z, 0.0) - jnp.log(1.0 + jnp.exp(-jnp.abs(z)))


def _head_masks(width):
    lane = lax.broadcasted_iota(jnp.int32, (1, width), 1)
    return [(lane >= h * HEAD_DIM) & (lane < (h + 1) * HEAD_DIM) for h in range(width // HEAD_DIM)]


def _per_head_rows(x, masks):
    return jnp.concatenate([jnp.where(hm, x, 0) for hm in masks], axis=0)


def _heads_to_lanes(x, n_heads):
    return jnp.concatenate([x[h * Q_TILE : (h + 1) * Q_TILE] for h in range(n_heads)], axis=1)


def _block_start(kb):
    return kb * Q_TILE if isinstance(kb, int) else pl.multiple_of(kb * Q_TILE, Q_TILE)


def _clamp(i, n):
    return jnp.minimum(i, n - 1)


def _next_block(pos):
    qi, kb = pos
    row_done = kb == 0
    nqi = jnp.where(row_done, qi + 1, qi)
    return nqi, jnp.where(row_done, nqi, kb - 1)


def _stream_unroll(n_blocks):
    return next(u for u in (ATTN_UNROLL, 2, 1) if n_blocks % u == 0)


def _past_mask(rows):
    t = lax.broadcasted_iota(jnp.int32, (rows, Q_TILE), 0) & (Q_TILE - 1)
    s = lax.broadcasted_iota(jnp.int32, (rows, Q_TILE), 1)
    return s < t


def _attn_fwd(qkv, n_seq, S, D, exch, name):
    T = n_seq * S
    width = min(D, ATTN_LANE_TILES * LANES)
    n_heads = width // HEAD_DIM
    rows = n_heads * Q_TILE
    nq = S // Q_TILE
    groups = D // width
    n_blocks = nq * (nq + 1) // 2
    unroll = _stream_unroll(n_blocks)
    scale = HEAD_DIM ** -0.5

    n_trips = n_blocks // unroll

    def body(q_ref, k_ref, v_ref, o_ref, a_out, b_out, qh_scr, vh_scr, bias_scr, a_stage, b_stage, sems):
        masks = _head_masks(width)
        sfx = _suffix_matrix(False)
        stream = pl.program_id(0) * groups + pl.program_id(1)

        def per_head_tables(i, c):
            blk = pl.ds(_block_start(i), Q_TILE)
            qh_scr[i] = _per_head_rows(q_ref[blk, :] * scale, masks)
            vh_scr[i] = _per_head_rows(v_ref[blk, :], masks)
            return c

        lax.fori_loop(0, nq, per_head_tables, 0)
        bias_scr[0] = jnp.zeros((rows, Q_TILE), F32)
        bias_scr[1] = jnp.where(_past_mask(rows), 0.0, MASKED)

        def save(n, slot):
            blocks = pl.ds(n * unroll, unroll)
            return [
                pltpu.make_async_copy(a_stage.at[slot], a_out.at[stream, blocks], sems.at[slot]),
                pltpu.make_async_copy(b_stage.at[slot], b_out.at[stream, blocks], sems.at[2 + slot]),
            ]

        def scores(pos):
            qi, kb = pos
            kt = k_ref[pl.ds(_block_start(_clamp(kb, nq)), Q_TILE), :]
            z = lax.dot_general(qh_scr[_clamp(qi, nq)], kt, NT, preferred_element_type=F32)
            z = z + bias_scr[(kb == qi).astype(jnp.int32)]
            lb = _log_beta(z)
            return lb, _hi_lo(lb - z)

        def weigh(pos, st, carry, acc, slot, u):
            qi, kb = pos
            lb, l1 = st
            r = jnp.dot(l1, sfx, preferred_element_type=F32)
            carry = jnp.where(kb == qi, 0.0, carry)
            a = jnp.exp(lb + r[:, :Q_TILE] + carry).astype(BF16)
            a_stage[slot, u] = a
            b_stage[slot, u] = jnp.exp(lb).astype(BF16)
            acc = jnp.where(kb == qi, 0.0, acc) + jnp.dot(
                _heads_to_lanes(a, n_heads), vh_scr[_clamp(kb, nq)], preferred_element_type=F32
            )
            o_ref[pl.ds(_block_start(_clamp(qi, nq)), Q_TILE), :] = acc
            return carry + r[:, Q_TILE:], acc

        def trip(n, c):
            pos, st, carry, acc = c
            slot = n % 2

            @pl.when(n >= 2)
            def _():
                for cp in save(n - 2, slot):
                    cp.wait()

            for u in range(unroll):
                nxt = _next_block(pos)
                st_nxt = scores(nxt)
                carry, acc = weigh(pos, st, carry, acc, slot, u)
                pos, st = nxt, st_nxt
            for cp in save(n, slot):
                cp.start()
            return pos, st, carry, acc

        first = (jnp.int32(0), jnp.int32(0))
        zero = bias_scr[0]
        init = (first, scores(first), zero, jnp.concatenate([zero[:Q_TILE]] * (width // Q_TILE), axis=1))
        lax.fori_loop(0, n_trips, trip, init)
        for n in range(max(n_trips - 2, 0), n_trips):
            for cp in save(n, n % 2):
                cp.wait()

    seq = lambda col0: pl.BlockSpec((S, width), lambda b, p: (b, col0 + p))
    saved = jax.ShapeDtypeStruct((n_seq * groups, n_blocks, rows, Q_TILE), BF16)
    stage = pltpu.VMEM((2, unroll, rows, Q_TILE), BF16)
    (o, a_w, beta), moved = _call_with_exchange(
        body,
        exch,
        name,
        grid=(n_seq, groups),
        in_specs=[seq(0), seq(groups), seq(2 * groups)],
        out_specs=[seq(0), pl.BlockSpec(memory_space=pl.ANY), pl.BlockSpec(memory_space=pl.ANY)],
        out_shape=[jax.ShapeDtypeStruct((T, D), F32), saved, saved],
        scratch_shapes=[
            pltpu.VMEM((nq, rows, width), BF16),
            pltpu.VMEM((nq, rows, width), BF16),
            pltpu.VMEM((2, rows, Q_TILE), F32),
            stage,
            stage,
            pltpu.SemaphoreType.DMA((4,)),
        ],
        args=(qkv, qkv, qkv),
    )
    return (o, a_w, beta), moved


def _attn_bwd(qkv, fwd, do, n_seq, S, D, exch, name):
    o, a_w, beta = fwd
    T = n_seq * S
    width = min(D, ATTN_LANE_TILES * LANES)
    n_heads = width // HEAD_DIM
    rows = n_heads * Q_TILE
    nq = S // Q_TILE
    groups = D // width
    n_blocks = nq * (nq + 1) // 2
    unroll = _stream_unroll(n_blocks)
    scale = HEAD_DIM ** -0.5

    n_trips = n_blocks // unroll

    def body(q_ref, k_ref, v_ref, o_ref, do_ref, a_in, b_in, dq_ref, dk_ref, dv_ref, dk_acc, dv_acc, qh_scr, doh_scr, delta_scr, a_stage, b_stage, sems):
        masks = _head_masks(width)
        sfx_incl = _suffix_matrix(True)
        stream = pl.program_id(0) * groups + pl.program_id(1)

        def fetch(n, slot):
            blocks = pl.ds(n * unroll, unroll)
            return [
                pltpu.make_async_copy(a_in.at[stream, blocks], a_stage.at[slot], sems.at[slot]),
                pltpu.make_async_copy(b_in.at[stream, blocks], b_stage.at[slot], sems.at[2 + slot]),
            ]

        for cp in fetch(0, 0):
            cp.start()
        dk_acc[...] = jnp.zeros_like(dk_acc)
        dv_acc[...] = jnp.zeros_like(dv_acc)

        def per_head_tables(i, c):
            blk = pl.ds(_block_start(i), Q_TILE)
            do = do_ref[blk, :]
            qh_scr[i] = _per_head_rows(q_ref[blk, :] * scale, masks)
            doh_scr[i] = _per_head_rows(do, masks)
            prod = do.astype(F32) * o_ref[blk, :]
            delta = jnp.concatenate(
                [jnp.sum(jnp.where(hm, prod, 0.0), axis=-1, keepdims=True) for hm in masks], axis=0
            )
            delta_scr[i] = jnp.broadcast_to(delta, (rows, Q_TILE))
            return c

        lax.fori_loop(0, nq, per_head_tables, 0)

        def weigh(pos, ab, beta, c2, dq):
            qi, kb = pos
            first = kb == qi
            blk = pl.ds(_block_start(kb), Q_TILE)
            g = ab.astype(F32) * lax.dot_general(doh_scr[qi], v_ref[blk, :], NT, preferred_element_type=F32)
            r2 = jnp.dot(_hi_lo(g), sfx_incl, preferred_element_type=F32)
            c2 = jnp.where(first, 0.0, c2)
            earlier = delta_scr[qi] - (r2[:, :Q_TILE] + c2)
            beta = beta.astype(F32)
            dzb = (g * (1.0 - beta) - earlier * beta).astype(BF16)
            kh = _per_head_rows(k_ref[blk, :], masks)
            dq = jnp.where(first, 0.0, dq) + jnp.dot(_heads_to_lanes(dzb, n_heads), kh, preferred_element_type=F32)
            dq_ref[pl.ds(_block_start(qi), Q_TILE), :] = (dq * scale).astype(BF16)
            dk_acc[blk, :] += lax.dot_general(dzb, qh_scr[qi], TN, preferred_element_type=F32)
            dv_acc[blk, :] += lax.dot_general(ab, doh_scr[qi], TN, preferred_element_type=F32)
            return c2 + r2[:, Q_TILE:], dq

        def trip(n, c):
            pos, c2, dq = c
            slot = n % 2
            for cp in fetch(n, slot):
                cp.wait()

            @pl.when(n + 1 < n_trips)
            def _():
                for cp in fetch(n + 1, 1 - slot):
                    cp.start()

            for u in range(unroll):
                c2, dq = weigh(pos, a_stage[slot, u], b_stage[slot, u], c2, dq)
                pos = _next_block(pos)
            return pos, c2, dq

        zero = dk_acc[pl.ds(0, Q_TILE), :]
        init = ((jnp.int32(0), jnp.int32(0)), jnp.concatenate([zero[:, :Q_TILE]] * n_heads, axis=0), zero)
        lax.fori_loop(0, n_trips, trip, init)
        dk_ref[...] = dk_acc[...].astype(BF16)
        dv_ref[...] = dv_acc[...].astype(BF16)

    seq = lambda col0: pl.BlockSpec((S, width), lambda b, p: (b, col0 + p))
    return _call_with_exchange(
        body,
        exch,
        name,
        grid=(n_seq, groups),
        in_specs=[seq(0), seq(groups), seq(2 * groups), seq(0), seq(0)] + [pl.BlockSpec(memory_space=pl.ANY)] * 2,
        out_specs=[seq(0)] * 3,
        out_shape=[jax.ShapeDtypeStruct((T, D), BF16)] * 3,
        scratch_shapes=[
            pltpu.VMEM((S, width), F32),
            pltpu.VMEM((S, width), F32),
            pltpu.VMEM((nq, rows, width), BF16),
            pltpu.VMEM((nq, rows, width), BF16),
            pltpu.VMEM((nq, rows, Q_TILE), F32),
            pltpu.VMEM((2, unroll, rows, Q_TILE), BF16),
            pltpu.VMEM((2, unroll, rows, Q_TILE), BF16),
            pltpu.SemaphoreType.DMA((4,)),
        ],
        args=(qkv, qkv, qkv, o, do, a_w, beta),
    )


def _causal_ws(ws_ref, g):
    t = lax.broadcasted_iota(jnp.int32, (SGU_CHUNK, SGU_CHUNK), 0)
    s = lax.broadcasted_iota(jnp.int32, (SGU_CHUNK, SGU_CHUNK), 1)
    return jnp.where(s <= t, ws_ref[g], 0.0)


def _sgu_fwd(a, gain, ws, bsb, name):
    T, F2 = a.shape
    F = F2 // 2
    gw = F // SGU_GROUPS

    def body(a_ref, gain_ref, ws_ref, bsb_ref, y_ref):
        v = _gelu(a_ref[:, F:].astype(F32))
        vn = (v * _rstd(v) * gain_ref[...]).astype(BF16)
        for g in range(SGU_GROUPS):
            cs = slice(g * gw, (g + 1) * gw)
            w = _causal_ws(ws_ref, g).astype(BF16)
            mixed = jnp.dot(w, vn[:, cs], preferred_element_type=F32) + bsb_ref[g]
            y_ref[:, cs] = (_gelu(a_ref[:, cs].astype(F32)) * mixed).astype(BF16)

    return pl.pallas_call(
        body,
        name=name,
        grid=(T // SGU_CHUNK,),
        in_specs=[
            pl.BlockSpec((SGU_CHUNK, F2), lambda i: (i, 0)),
            pl.BlockSpec((1, F), lambda i: (0, 0)),
            pl.BlockSpec((SGU_GROUPS, SGU_CHUNK, SGU_CHUNK), lambda i: (0, 0, 0)),
            pl.BlockSpec((SGU_GROUPS, SGU_CHUNK, gw), lambda i: (0, 0, 0)),
        ],
        out_specs=pl.BlockSpec((SGU_CHUNK, F), lambda i: (i, 0)),
        out_shape=jax.ShapeDtypeStruct((T, F), BF16),
        compiler_params=_params(1),
    )(a, gain.reshape(1, F), ws, bsb)


def _sgu_bwd(a, dy, gain, ws, bsb, name):
    T, F2 = a.shape
    F = F2 // 2
    gw = F // SGU_GROUPS

    def body(a_ref, dy_ref, gain_ref, ws_ref, bsb_ref, da_ref, dws_ref, dbs_ref, dgain_ref, dvn_ref):
        @pl.when(pl.program_id(0) == 0)
        def _():
            dws_ref[...] = jnp.zeros_like(dws_ref)
            dbs_ref[...] = jnp.zeros_like(dbs_ref)
            dgain_ref[...] = jnp.zeros_like(dgain_ref)

        av = a_ref[:, F:].astype(F32)
        v = _gelu(av)
        rstd = _rstd(v)
        vh = v * rstd
        gain = gain_ref[...]
        vn = (vh * gain).astype(BF16)
        ones = jnp.ones((gw, SGU_CHUNK), BF16)
        for g in range(SGU_GROUPS):
            cs = slice(g * gw, (g + 1) * gw)
            w = _causal_ws(ws_ref, g).astype(BF16)
            mixed = jnp.dot(w, vn[:, cs], preferred_element_type=F32) + bsb_ref[g]
            au = a_ref[:, cs].astype(F32)
            dyc = dy_ref[:, cs].astype(F32)
            da_ref[:, cs] = (dyc * mixed * _gelu_grad(au)).astype(BF16)
            dm = (dyc * _gelu(au)).astype(BF16)
            dbs_ref[g] += jnp.dot(dm, ones, preferred_element_type=F32)
            dws_ref[g] += _causal_mask_f32(lax.dot_general(dm, vn[:, cs], NT, preferred_element_type=F32))
            dvn_ref[:, cs] = lax.dot_general(w, dm, TN, preferred_element_type=F32)
        dvn = dvn_ref[...]
        dgain_ref[...] += jnp.sum(dvn * vh, axis=0, keepdims=True)
        dvh = dvn * gain
        dv = rstd * (dvh - vh * jnp.mean(dvh * vh, axis=-1, keepdims=True))
        da_ref[:, F:] = (dv * _gelu_grad(av)).astype(BF16)

    acc_spec = pl.BlockSpec((SGU_GROUPS, SGU_CHUNK, SGU_CHUNK), lambda i: (0, 0, 0))
    acc_shape = jax.ShapeDtypeStruct((SGU_GROUPS, SGU_CHUNK, SGU_CHUNK), F32)
    return pl.pallas_call(
        body,
        name=name,
        grid=(T // SGU_CHUNK,),
        in_specs=[
            pl.BlockSpec((SGU_CHUNK, F2), lambda i: (i, 0)),
            pl.BlockSpec((SGU_CHUNK, F), lambda i: (i, 0)),
            pl.BlockSpec((1, F), lambda i: (0, 0)),
            acc_spec,
            pl.BlockSpec((SGU_GROUPS, SGU_CHUNK, gw), lambda i: (0, 0, 0)),
        ],
        out_specs=[
            pl.BlockSpec((SGU_CHUNK, F2), lambda i: (i, 0)),
            acc_spec,
            acc_spec,
            pl.BlockSpec((1, F), lambda i: (0, 0)),
        ],
        out_shape=[
            jax.ShapeDtypeStruct((T, F2), BF16),
            acc_shape,
            acc_shape,
            jax.ShapeDtypeStruct((1, F), F32),
        ],
        scratch_shapes=[pltpu.VMEM((SGU_CHUNK, F), F32)],
        compiler_params=_params(1),
    )(a, dy, gain.reshape(1, F), ws, bsb)


def _causal_mask_f32(m):
    t = lax.broadcasted_iota(jnp.int32, m.shape, 0)
    s = lax.broadcasted_iota(jnp.int32, m.shape, 1)
    return jnp.where(s <= t, m, 0.0)


def _final_loss(x, gain, target, name):
    T, D = x.shape
    tm = min(T, ROW_TILE // 2)

    def body(x_ref, g_ref, t_ref, sq_ref, dx_ref, dxb_ref, dg_ref):
        xv = x_ref[...]
        gain = g_ref[...]
        err = xv * _rstd(xv) * gain - t_ref[...]
        dx, dg = _norm_bwd(err * (1.0 / D), xv, gain)
        dx_ref[...] = dx
        dxb_ref[...] = dx.astype(BF16)
        sq = jnp.sum(err * err, axis=0, keepdims=True)

        @pl.when(pl.program_id(0) == 0)
        def _():
            sq_ref[...] = sq
            dg_ref[...] = dg

        @pl.when(pl.program_id(0) > 0)
        def _():
            sq_ref[...] += sq
            dg_ref[...] += dg

    row = pl.BlockSpec((tm, D), lambda i: (i, 0))
    vec = pl.BlockSpec((1, D), lambda i: (0, 0))
    return pl.pallas_call(
        body,
        name=name,
        grid=(T // tm,),
        in_specs=[row, vec, row],
        out_specs=[vec, row, row, vec],
        out_shape=[
            jax.ShapeDtypeStruct((1, D), F32),
            jax.ShapeDtypeStruct((T, D), F32),
            jax.ShapeDtypeStruct((T, D), BF16),
            jax.ShapeDtypeStruct((1, D), F32),
        ],
        compiler_params=_params(1),
    )(x, gain.reshape(1, D), target)


def _row_tile(rows, cols, n_arrays):
    budget = VMEM_LIMIT // 2 // (2 * n_arrays * cols * 4)
    tr = rows
    while tr > budget and tr % 16 == 0:
        tr //= 2
    return tr


def _sum_received(own, recv, name):
    R, C = own.shape
    n = recv.shape[0]
    tr = _row_tile(R, C, n + 2)

    def body(own_ref, recv_ref, o_ref):
        s = own_ref[...]
        for k in range(n):
            s = s + recv_ref[k].astype(F32)
        o_ref[...] = s

    return pl.pallas_call(
        body,
        name=name,
        grid=(R // tr,),
        in_specs=[pl.BlockSpec((tr, C), lambda i: (i, 0)), pl.BlockSpec((n, tr, C), lambda i: (0, i, 0))],
        out_specs=pl.BlockSpec((tr, C), lambda i: (i, 0)),
        out_shape=jax.ShapeDtypeStruct((R, C), F32),
        compiler_params=_params(1),
    )(own, recv)


def _sum_chip_shard(g32, recv, chip, name):
    _, L, r, c = g32.shape
    n = recv.shape[0]
    tr = _row_tile(r, c, n + 2)

    def body(chip_ref, own_ref, recv_ref, o_ref):
        s = own_ref[...]
        for k in range(n):
            s = s + recv_ref[k].astype(F32)
        o_ref[...] = s

    return pl.pallas_call(
        body,
        name=name,
        grid_spec=pltpu.PrefetchScalarGridSpec(
            num_scalar_prefetch=1,
            grid=(L, r // tr),
            in_specs=[
                pl.BlockSpec((None, None, tr, c), lambda l, i, chip_ref: (chip_ref[0], l, i, 0)),
                pl.BlockSpec((n, None, tr, c), lambda l, i, chip_ref: (0, l, i, 0)),
            ],
            out_specs=pl.BlockSpec((None, tr, c), lambda l, i, chip_ref: (l, i, 0)),
        ),
        out_shape=jax.ShapeDtypeStruct((L, r, c), F32),
        compiler_params=_params(2),
    )(chip.reshape(1).astype(jnp.int32), g32, recv)


def _adamw(w, m, v, parts, name):
    R, C = w.shape
    n = len(parts)
    tr = _row_tile(R, C, n + 7)

    def body(*refs):
        w_ref, m_ref, v_ref = refs[:3]
        g_ref, d_ref, nm_ref, nv_ref = refs[3 + n :]
        g = refs[3][...]
        for p_ref in refs[4 : 3 + n]:
            g = g + p_ref[...]
        nm = ADAM_B1 * m_ref[...] + (1.0 - ADAM_B1) * g
        nv = ADAM_B2 * v_ref[...] + (1.0 - ADAM_B2) * (g * g)
        m_hat = nm / (1.0 - ADAM_B1**ADAM_STEP)
        v_hat = nv / (1.0 - ADAM_B2**ADAM_STEP)
        g_ref[...] = g
        d_ref[...] = -ADAM_LR * (m_hat / (jnp.sqrt(v_hat) + ADAM_EPS) + ADAM_WD * w_ref[...])
        nm_ref[...] = nm
        nv_ref[...] = nv

    spec = pl.BlockSpec((tr, C), lambda i: (i, 0))
    return pl.pallas_call(
        body,
        name=name,
        grid=(R // tr,),
        in_specs=[spec] * (3 + n),
        out_specs=[spec] * 4,
        out_shape=[jax.ShapeDtypeStruct((R, C), F32)] * 4,
        compiler_params=_params(1),
    )(w, m, v, *parts)


def _swap_with_sibling(parts, name):
    n = len(parts)

    def body(*refs):
        ins, outs = refs[:n], refs[n : 2 * n]
        send_sems, recv_sems = refs[2 * n :]
        sibling = (lax.axis_index("x"), lax.axis_index("y"), 1 - lax.axis_index("c"))
        copies = [
            pltpu.make_async_remote_copy(
                src_ref=ins[a],
                dst_ref=outs[a],
                send_sem=send_sems.at[a],
                recv_sem=recv_sems.at[a],
                device_id=sibling,
                device_id_type=MESH,
            )
            for a in range(n)
        ]
        for cp in copies:
            cp.start()
        for cp in copies:
            cp.wait_recv()
        for cp in copies:
            cp.wait_send()

    any_spec = pl.BlockSpec(memory_space=pl.ANY)
    return pl.pallas_call(
        body,
        name=name,
        in_specs=[any_spec] * n,
        out_specs=[any_spec] * n,
        out_shape=[jax.ShapeDtypeStruct(p.shape, p.dtype) for p in parts],
        scratch_shapes=[pltpu.SemaphoreType.DMA((n,)), pltpu.SemaphoreType.DMA((n,))],
        compiler_params=pltpu.CompilerParams(has_side_effects=True),
    )(*parts)


def _pack(pieces):
    flat = jnp.concatenate([p.reshape(-1) for p in pieces])
    return flat.reshape(-1, LANES)


def _unpack(packed, shapes):
    flat = packed.reshape(-1)
    out, off = [], 0
    for s in shapes:
        size = 1
        for d in s:
            size *= d
        out.append(flat[off : off + size].reshape(s))
        off += size
    return out


def kernel(x, norm_mix, norm_mlp, sb_wqkv, sb_wo, sgu_win, sgu_gain, sgu_ws, sgu_bs, sgu_wout, mlp_w1, mlp_w2, final_norm, loss_target, m_norm_mix, m_norm_mlp, m_sb_wqkv, m_sb_wo, m_sgu_win, m_sgu_gain, m_sgu_ws, m_sgu_bs, m_sgu_wout, m_mlp_w1, m_mlp_w2, m_final_norm, v_norm_mix, v_norm_mlp, v_sb_wqkv, v_sb_wo, v_sgu_win, v_sgu_gain, v_sgu_ws, v_sgu_bs, v_sgu_wout, v_mlp_w1, v_mlp_w2, v_final_norm):
    n_seq, S, D = x.shape
    T = n_seq * S
    depth = norm_mix.shape[0]
    n_sgu = sgu_win.shape[0]
    F = sgu_wout.shape[1] * N_CHIPS
    gw = F // SGU_GROUPS
    chip = 2 * lax.axis_index("x") + lax.axis_index("y")

    QKV, WO, WIN, WOUT, W1, W2, GAIN = range(7)
    big = [sb_wqkv, sb_wo, sgu_win, sgu_wout, mlp_w1, mlp_w2]
    n_sb = sb_wqkv.shape[0]
    shards = [w.astype(BF16) for w in big] + [sgu_gain.reshape(1, -1, LANES)]

    def gather_plan(i):
        j, mlp = i // 2, min(2, depth - i)
        plan = [(WO, j, 1), (W1, i, mlp), (W2, i, mlp)]
        if i + 1 < depth:
            plan += [(WIN, (i + 1) // 2, 1), (WOUT, (i + 1) // 2, 1)]
        if j + 1 < n_sb:
            plan += [(QKV, j + 1, 1)]
        return plan

    wg = _exchange_only(_gather_exchange(shards, None, [(QKV, 0, 1), (GAIN, 0, 1)]), "gather_first_weights")
    gain_full = jnp.transpose(wg[GAIN].reshape(N_CHIPS, n_sgu, F // N_CHIPS), (1, 0, 2)).reshape(n_sgu, F)
    bsb = [jnp.broadcast_to(sgu_bs[j][:, :, None], (SGU_GROUPS, SGU_CHUNK, gw)) for j in range(n_sgu)]

    xs = x.reshape(T, D)
    saved = []
    for i in range(depth):
        j = i // 2
        if i % 2 == 0:
            qkv, h = _norm_matmul(xs, norm_mix[i], wg[QKV], j, f"qkv_fwd_{i}")
            attn, wg = _attn_fwd(qkv, n_seq, S, D, _gather_exchange(shards, wg, gather_plan(i)), f"attn_fwd_{i}")
            wg_qkv, wg_wo, wg_win, wg_wout, wg_w1, wg_w2 = wg[:6]
            x_mid = _act_matmul_res(attn[0], wg_wo, j, xs, None, f"wo_fwd_{i}")
            mix = (qkv, attn)
        else:
            a, h = _norm_matmul(xs, norm_mix[i], wg_win, j, f"win_fwd_{i}")
            yg = _sgu_fwd(a, gain_full[j], sgu_ws[j], bsb[j], f"sgu_fwd_{i}")
            x_mid = _act_matmul_res(yg, wg_wout, j, xs, None, f"wout_fwd_{i}")
            mix = (a, yg)
        a2, h2 = _norm_matmul(x_mid, norm_mlp[i], wg_w1, i, f"w1_fwd_{i}")
        x_out = _act_matmul_res(a2, wg_w2, i, x_mid, "relu2", f"w2_fwd_{i}")
        saved.append((xs, h, mix, x_mid, h2, a2))
        xs = x_out

    sq, dx, dxb, g_final = _final_loss(xs, final_norm, loss_target.reshape(T, D), "loss_head")
    loss = lax.psum(0.5 * jnp.sum(sq) / D, ("x", "y", "c"))

    n_layers = [n_sb, n_sb, n_sgu, n_sgu, depth, depth]
    g32, g16, recv = [None] * 6, [None] * 6, [None] * 6
    done_from, sent_from = list(n_layers), list(n_layers)

    def grad(a, layer, lhs, rhs, shard_lhs, act, name):
        bufs = None if g32[a] is None else (g32[a], g16[a])
        g32[a], g16[a] = _matmul_tn(lhs, rhs, bufs, layer, n_layers[a], shard_lhs, act, name)
        done_from[a] = layer

    def unsent_plan():
        plan = [(a, done_from[a], sent_from[a] - done_from[a]) for a in range(6) if sent_from[a] > done_from[a]]
        for a, l0, _ in plan:
            sent_from[a] = l0
        return plan

    def scatter(plan, small):
        if not plan and small is None:
            return None, lambda moved: None
        arrays = sorted({a for a, _, _ in plan})
        have = [a for a in arrays if recv[a] is not None]
        made = [a for a in arrays if recv[a] is None]
        exch = _scatter_exchange(
            [g16[a] for a in arrays], [recv[a] for a in arrays], [(arrays.index(a), l0, n) for a, l0, n in plan], small
        )

        def take(moved):
            for a, buf in zip(arrays, moved):
                g16[a] = buf
            for a, buf in zip(have + made, moved[len(arrays) :]):
                recv[a] = buf
            return moved[-1]

        return exch, take

    g_mix, g_mlp = [None] * depth, [None] * depth
    g_ws, g_bs, g_gain = [None] * n_sgu, [None] * n_sgu, [None] * n_sgu
    for i in reversed(range(depth)):
        j = i // 2
        x_in, h, mix, x_mid, h2, a2 = saved[i]
        da2 = _matmul_nt(dxb, wg_w2, i, a2, f"w2_bwd_{i}")
        grad(W2, i, a2, dxb, True, "relu2", f"w2_grad_{i}")
        grad(W1, i, h2, da2, False, None, f"w1_grad_{i}")
        (dx, dxb, g_mlp[i]), _ = _matmul_nt_norm_bwd(da2, wg_w1, i, x_mid, norm_mlp[i], dx, None, f"w1_bwd_{i}")
        if i % 2 == 0:
            qkv, attn = mix
            do = _matmul_nt(dxb, wg_wo, j, None, f"wo_bwd_{i}")
            grad(WO, j, attn[0], dxb, True, None, f"wo_grad_{i}")
            sgu_small = _pack([jnp.stack(g_ws), jnp.stack(g_bs), jnp.stack(g_gain)]) if i == 0 and n_sgu else None
            exch, take = scatter(unsent_plan(), sgu_small)
            (dq, dk, dv), moved = _attn_bwd(qkv, attn, do, n_seq, S, D, exch, f"attn_bwd_{i}")
            last = take(moved)
            if sgu_small is not None:
                sgu_small_all = last
            dqkv = jnp.concatenate([dq, dk, dv], axis=1)
            grad(QKV, j, h, dqkv, False, None, f"qkv_grad_{i}")
            exch, take = scatter(unsent_plan(), None)
            (dx, dxb, g_mix[i]), moved = _matmul_nt_norm_bwd(
                dqkv, wg_qkv, j, x_in, norm_mix[i], dx, exch, f"qkv_bwd_{i}"
            )
            take(moved)
        else:
            a, yg = mix
            dyg = _matmul_nt(dxb, wg_wout, j, None, f"wout_bwd_{i}")
            grad(WOUT, j, yg, dxb, True, None, f"wout_grad_{i}")
            da, g_ws[j], dbs, g_gain[j] = _sgu_bwd(a, dyg, gain_full[j], sgu_ws[j], bsb[j], f"sgu_bwd_{i}")
            g_bs[j] = dbs[:, :, 0]
            grad(WIN, j, h, da, False, None, f"win_grad_{i}")
            (dx, dxb, g_mix[i]), _ = _matmul_nt_norm_bwd(da, wg_win, j, x_in, norm_mix[i], dx, None, f"win_bwd_{i}")
    grad_x = dx.reshape(n_seq, S, D)

    names = ["qkv", "wo", "win", "wout", "w1", "w2"]
    exch, take = scatter(unsent_plan(), _pack([jnp.stack(g_mix), jnp.stack(g_mlp), g_final]))
    norm_small_all = take(_exchange_only(exch, "gather_norm_grads"))
    partial = [_sum_chip_shard(g32[a], recv[a], chip, f"sum_{names[a]}") for a in range(6)]
    partial = [p.reshape(-1, p.shape[-1]) for p in partial]
    theirs = _swap_with_sibling(partial, "swap_partial_sums")
    g_small = _unpack(
        _sum_received(norm_small_all[0], norm_small_all[1:], "sum_norm_grads"),
        [norm_mix.shape, norm_mlp.shape, final_norm.shape],
    ) + _unpack(
        _sum_received(sgu_small_all[0], sgu_small_all[1:], "sum_sgu_small_grads"),
        [sgu_ws.shape, sgu_bs.shape, (n_sgu, F)],
    )

    ms = [m_sb_wqkv, m_sb_wo, m_sgu_win, m_sgu_wout, m_mlp_w1, m_mlp_w2]
    vs = [v_sb_wqkv, v_sb_wo, v_sgu_win, v_sgu_wout, v_mlp_w1, v_mlp_w2]
    res = {}
    keys = ["sb_wqkv", "sb_wo", "sgu_win", "sgu_wout", "mlp_w1", "mlp_w2"]
    for key, k, w, m, v, mine, other in zip(keys, names, big, ms, vs, partial, theirs):
        cols = w.shape[-1]
        outs = _adamw(w.reshape(-1, cols), m.reshape(-1, cols), v.reshape(-1, cols), [mine, other], f"adamw_{k}")
        res[key] = [o.reshape(w.shape) for o in outs]

    g_small[5] = lax.dynamic_slice_in_dim(g_small[5], chip * (F // N_CHIPS), F // N_CHIPS, axis=1)
    small_keys = ["norm_mix", "norm_mlp", "final_norm", "sgu_ws", "sgu_bs", "sgu_gain"]
    small_w = [norm_mix, norm_mlp, final_norm, sgu_ws, sgu_bs, sgu_gain]
    small_m = [m_norm_mix, m_norm_mlp, m_final_norm, m_sgu_ws, m_sgu_bs, m_sgu_gain]
    small_v = [v_norm_mix, v_norm_mlp, v_final_norm, v_sgu_ws, v_sgu_bs, v_sgu_gain]
    outs = _adamw(_pack(small_w), _pack(small_m), _pack(small_v), [_pack(g_small)], "adamw_small")
    local_shapes = [w.shape for w in small_w]
    for key, parts in zip(small_keys, zip(*[_unpack(o, local_shapes) for o in outs])):
        res[key] = list(parts)

    order = ["norm_mix", "norm_mlp", "sb_wqkv", "sb_wo", "sgu_win", "sgu_gain", "sgu_ws", "sgu_bs", "sgu_wout", "mlp_w1", "mlp_w2", "final_norm"]
    return (loss, grad_x, *[res[k][0] for k in order], *[res[k][1] for k in order], *[res[k][2] for k in order], *[res[k][3] for k in order])
```

```python
import jax
import jax.numpy as jnp
from jax import lax
from jax.experimental import pallas as pl
from jax.experimental.pallas import tpu as pltpu

F32 = jnp.float32
BF16 = jnp.bfloat16
MESH = pl.DeviceIdType.MESH

EPS = 1e-6
HEAD_DIM = 64
LANES = 128
Q_TILE = 128
SGU_CHUNK = 128
SGU_GROUPS = 8
N_CHIPS = 4
N_DEV = 8
ADAM_LR = 0.001
ADAM_B1 = 0.9
ADAM_B2 = 0.999
ADAM_EPS = 1e-08
ADAM_WD = 0.01
ADAM_STEP = 10
GELU_C0 = 0.7978845608028654
GELU_C1 = 0.044715
VMEM_LIMIT = 48 * 1024 * 1024
ROW_TILE = 1024
NT = (((1,), (1,)), ((), ()))
TN = (((0,), (0,)), ((), ()))


def _params(n_axes):
    return pltpu.CompilerParams(dimension_semantics=("arbitrary",) * n_axes, vmem_limit_bytes=VMEM_LIMIT)


def _rstd(x):
    return lax.rsqrt(jnp.mean(x * x, axis=-1, keepdims=True) + EPS)


def _norm_bwd(dh, x, gain):
    rstd = _rstd(x)
    xh = x * rstd
    dhg = dh * gain
    dx = rstd * (dhg - xh * jnp.mean(dhg * xh, axis=-1, keepdims=True))
    return dx, jnp.sum(dh * xh, axis=0, keepdims=True)


def _gelu(x):
    return (0.5 * x) * (1.0 + jnp.tanh(x * (GELU_C0 + (GELU_C0 * GELU_C1) * (x * x))))


def _gelu_and_grad(x):
    x2 = x * x
    t = jnp.tanh(x * (GELU_C0 + (GELU_C0 * GELU_C1) * x2))
    half_x, p = 0.5 * x, 1.0 + t
    slope = GELU_C0 + (3.0 * GELU_C0 * GELU_C1) * x2
    return half_x * p, 0.5 * p + (half_x * slope) * (1.0 - t * t)


def _act(a, act):
    if act == "relu2":
        r = jnp.maximum(a.astype(F32), 0.0)
        return (r * r).astype(BF16)
    return a.astype(BF16)


def _layer_spec(wg, layer):
    nsh, _, r, c = wg.shape
    return pl.BlockSpec((nsh, None, r, c), lambda i: (0, layer, 0, 0), pipeline_mode=pl.Buffered(1))


def _norm_matmul(x, gain, wg, layer, name):
    T, D = x.shape
    nsh, _, _, ns = wg.shape
    tm = min(T, ROW_TILE // 2)

    def body(x_ref, g_ref, w_ref, y_ref, h_ref):
        xv = x_ref[...]
        h = (xv * _rstd(xv) * g_ref[...]).astype(BF16)
        h_ref[...] = h
        for j in range(nsh):
            y_ref[:, j * ns : (j + 1) * ns] = jnp.dot(h, w_ref[j], preferred_element_type=F32).astype(BF16)

    return pl.pallas_call(
        body,
        name=name,
        grid=(T // tm,),
        in_specs=[pl.BlockSpec((tm, D), lambda i: (i, 0)), pl.BlockSpec((1, D), lambda i: (0, 0)), _layer_spec(wg, layer)],
        out_specs=[pl.BlockSpec((tm, nsh * ns), lambda i: (i, 0)), pl.BlockSpec((tm, D), lambda i: (i, 0))],
        out_shape=[jax.ShapeDtypeStruct((T, nsh * ns), BF16), jax.ShapeDtypeStruct((T, D), BF16)],
        compiler_params=_params(1),
    )(x, gain.reshape(1, D), wg)


def _act_matmul_res(a, wg, layer, x_in, act, name):
    T, K = a.shape
    nsh, _, kq, D = wg.shape
    tm = min(T, ROW_TILE // 2)

    def body(a_ref, w_ref, x_ref, o_ref):
        w = w_ref[...].reshape(nsh * kq, D)
        o_ref[...] = x_ref[...] + jnp.dot(_act(a_ref[...], act), w, preferred_element_type=F32)

    return pl.pallas_call(
        body,
        name=name,
        grid=(T // tm,),
        in_specs=[pl.BlockSpec((tm, K), lambda i: (i, 0)), _layer_spec(wg, layer), pl.BlockSpec((tm, D), lambda i: (i, 0))],
        out_specs=pl.BlockSpec((tm, D), lambda i: (i, 0)),
        out_shape=jax.ShapeDtypeStruct((T, D), F32),
        compiler_params=_params(1),
    )(a, wg, x_in)


def _matmul_nt(g, wg, layer, a, name):
    T, D = g.shape
    nsh, _, kq, _ = wg.shape
    tm = min(T, ROW_TILE // 2)

    def body(g_ref, w_ref, *rest):
        gv = g_ref[...]
        for k in range(nsh):
            cols = slice(k * kq, (k + 1) * kq)
            r = lax.dot_general(gv, w_ref[k], NT, preferred_element_type=F32)
            if a is not None:
                r = r * (2.0 * jnp.maximum(rest[0][:, cols].astype(F32), 0.0))
            rest[-1][:, cols] = r.astype(BF16)

    row = pl.BlockSpec((tm, nsh * kq), lambda i: (i, 0))
    in_specs = [pl.BlockSpec((tm, D), lambda i: (i, 0)), _layer_spec(wg, layer)]
    args = [g, wg]
    if a is not None:
        in_specs.append(row)
        args.append(a)
    return pl.pallas_call(
        body,
        name=name,
        grid=(T // tm,),
        in_specs=in_specs,
        out_specs=row,
        out_shape=jax.ShapeDtypeStruct((T, nsh * kq), BF16),
        compiler_params=_params(1),
    )(*args)


def _matmul_nt_norm_bwd(da, wg, layer, x, gain, dres, exch, name):
    T, D = x.shape
    nsh, _, _, ns = wg.shape
    tm = min(T, ROW_TILE // 2)

    def body(da_ref, w_ref, x_ref, g_ref, r_ref, dx_ref, dxb_ref, dg_ref):
        dh = lax.dot_general(da_ref[:, :ns], w_ref[0], NT, preferred_element_type=F32)
        for j in range(1, nsh):
            dh = dh + lax.dot_general(da_ref[:, j * ns : (j + 1) * ns], w_ref[j], NT, preferred_element_type=F32)
        dx, dg = _norm_bwd(dh, x_ref[...], g_ref[...])
        dx = dx + r_ref[...]
        dx_ref[...] = dx
        dxb_ref[...] = dx.astype(BF16)

        @pl.when(pl.program_id(0) == 0)
        def _():
            dg_ref[...] = dg

        @pl.when(pl.program_id(0) > 0)
        def _():
            dg_ref[...] += dg

    row = pl.BlockSpec((tm, D), lambda i: (i, 0))
    vec = pl.BlockSpec((1, D), lambda i: (0, 0))
    return _call_with_exchange(
        body,
        exch,
        name,
        grid=(T // tm,),
        in_specs=[pl.BlockSpec((tm, nsh * ns), lambda i: (i, 0)), _layer_spec(wg, layer), row, vec, row],
        out_specs=[row, row, vec],
        out_shape=[
            jax.ShapeDtypeStruct((T, D), F32),
            jax.ShapeDtypeStruct((T, D), BF16),
            jax.ShapeDtypeStruct((1, D), F32),
        ],
        scratch_shapes=[],
        args=(da, wg, x, gain.reshape(1, D), dres),
    )


def _matmul_tn(lhs, rhs, bufs, layer, n_layers, shard_lhs, act, name):
    T = lhs.shape[0]
    rows = lhs.shape[1] // N_CHIPS if shard_lhs else lhs.shape[1]
    cols = rhs.shape[1] if shard_lhs else rhs.shape[1] // N_CHIPS
    tt = min(T, 2 * ROW_TILE)
    n_t = T // tt

    def body(l_ref, r_ref, *rest):
        o32_ref, o16_ref = rest[-2:]
        t = pl.program_id(1)
        upd = lax.dot_general(_act(l_ref[...], act), r_ref[...].astype(BF16), TN, preferred_element_type=F32)

        @pl.when(t == 0)
        def _():
            o32_ref[...] = upd

        @pl.when(t > 0)
        def _():
            o32_ref[...] += upd

        @pl.when(t == n_t - 1)
        def _():
            o16_ref[...] = o32_ref[...].astype(BF16)

    if shard_lhs:
        in_specs = [pl.BlockSpec((tt, rows), lambda s, t: (t, s)), pl.BlockSpec((tt, cols), lambda s, t: (t, 0))]
    else:
        in_specs = [pl.BlockSpec((tt, rows), lambda s, t: (t, 0)), pl.BlockSpec((tt, cols), lambda s, t: (t, s))]
    args = [lhs, rhs]
    aliases = {}
    if bufs is not None:
        in_specs += [pl.BlockSpec(memory_space=pl.ANY)] * 2
        args += list(bufs)
        aliases = {2: 0, 3: 1}
    shape = (N_CHIPS, n_layers, rows, cols)
    return pl.pallas_call(
        body,
        name=name,
        grid=(N_CHIPS, n_t),
        in_specs=in_specs,
        out_specs=[pl.BlockSpec((None, None, rows, cols), lambda s, t: (s, layer, 0, 0))] * 2,
        out_shape=[jax.ShapeDtypeStruct(shape, F32), jax.ShapeDtypeStruct(shape, BF16)],
        input_output_aliases=aliases,
        compiler_params=_params(2),
    )(*args)


def _chip_peers(x, y):
    return [(1 - x, y), (x, 1 - y), (1 - x, 1 - y)]


def _remote(src, dst, sems, s, peer):
    return pltpu.make_async_remote_copy(
        src_ref=src, dst_ref=dst, send_sem=sems[0].at[s], recv_sem=sems[1].at[s], device_id=peer, device_id_type=MESH
    )


class _Exchange:
    def __init__(self, operands, n_alias, new_shapes, n_sems, build):
        self.operands, self.n_alias, self.new_shapes, self.n_sems, self.build = operands, n_alias, new_shapes, n_sems, build

    def out_shapes(self):
        return [jax.ShapeDtypeStruct(a.shape, a.dtype) for a in self.operands[: self.n_alias]] + list(self.new_shapes)

    def scratch(self):
        return [pltpu.SemaphoreType.DMA((n,)) for n in self.n_sems]

    def run(self, ins, outs, sems, first, last):
        starts, recvs, sends, locals_ = self.build(ins, outs, sems)

        def start_all():
            for cp in starts:
                cp.start()

        def wait_all():
            for cp in recvs:
                cp.wait_recv()
            for cp in sends:
                cp.wait_send()
            for cp in locals_:
                cp.wait()

        if first is True:
            start_all()
            return wait_all
        pl.when(first)(start_all)
        return lambda: pl.when(last)(wait_all)


def _gather_exchange(shards, bufs, plan):
    n_arr = len(shards)
    n_cp = len(plan) * (N_CHIPS - 1)

    def build(ins, outs, sems):
        shard_refs = ins[-n_arr:]
        x, y, c = lax.axis_index("x"), lax.axis_index("y"), lax.axis_index("c")
        me = 2 * x + y
        recvs, sends, locals_ = [], [], []
        for p, (a, l0, n) in enumerate(plan):
            src = shard_refs[a].at[pl.ds(l0, n)]
            cp = pltpu.make_async_copy(src, outs[a].at[me, pl.ds(l0, n)], sems[2].at[p])
            locals_.append(cp)
            for k, (px, py) in enumerate(_chip_peers(x, y)):
                s = p * (N_CHIPS - 1) + k
                sends.append(_remote(src, outs[a].at[me, pl.ds(l0, n)], sems, s, (px, py, c)))
                recvs.append(_remote(src, outs[a].at[2 * px + py, pl.ds(l0, n)], sems, s, (px, py, c)))
        return locals_ + sends, recvs, sends, locals_

    if bufs is None:
        new = [jax.ShapeDtypeStruct((N_CHIPS,) + s.shape, s.dtype) for s in shards]
        return _Exchange(list(shards), 0, new, [n_cp, n_cp, len(plan)], build)
    return _Exchange(list(bufs) + list(shards), n_arr, [], [n_cp, n_cp, len(plan)], build)


def _scatter_exchange(g16, recv, plan, small=None):
    n_arr = len(g16)
    have = [r for r in recv if r is not None]
    made = [a for a in range(n_arr) if recv[a] is None]
    n_cp = len(plan) * (N_CHIPS - 1) + (N_DEV - 1 if small is not None else 0)

    def build(ins, outs, sems):
        g_refs = ins[:n_arr]
        recv_refs, it_have, it_made = [], iter(outs[n_arr : n_arr + len(have)]), iter(outs[n_arr + len(have) :])
        for a in range(n_arr):
            recv_refs.append(next(it_made) if recv[a] is None else next(it_have))
        x, y, c = lax.axis_index("x"), lax.axis_index("y"), lax.axis_index("c")
        me = 2 * x + y
        recvs, sends, locals_ = [], [], []
        for p, (a, l0, n) in enumerate(plan):
            for k, (px, py) in enumerate(_chip_peers(x, y)):
                s = p * (N_CHIPS - 1) + k
                dst = recv_refs[a].at[k, pl.ds(l0, n)]
                sends.append(_remote(g_refs[a].at[2 * px + py, pl.ds(l0, n)], dst, sems, s, (px, py, c)))
                recvs.append(_remote(g_refs[a].at[me, pl.ds(l0, n)], dst, sems, s, (px, py, c)))
        if small is not None:
            small_ref, all_ref = ins[-1], outs[-1]
            slot = 4 * x + 2 * y + c
            locals_.append(pltpu.make_async_copy(small_ref, all_ref.at[slot], sems[2].at[0]))
            flips = [(fx, fy, fc) for fx in (0, 1) for fy in (0, 1) for fc in (0, 1)][1:]
            for k, (fx, fy, fc) in enumerate(flips):
                s = len(plan) * (N_CHIPS - 1) + k
                px, py, pc = x ^ fx, y ^ fy, c ^ fc
                sends.append(_remote(small_ref, all_ref.at[slot], sems, s, (px, py, pc)))
                recvs.append(_remote(small_ref, all_ref.at[4 * px + 2 * py + pc], sems, s, (px, py, pc)))
        return locals_ + sends, recvs, sends, locals_

    operands = list(g16) + have + ([small] if small is not None else [])
    new = [jax.ShapeDtypeStruct((N_CHIPS - 1,) + g16[a].shape[1:], BF16) for a in made]
    if small is not None:
        new.append(jax.ShapeDtypeStruct((N_DEV,) + small.shape, F32))
    return _Exchange(operands, n_arr + len(have), new, [n_cp, n_cp, 1], build)


def _call_with_exchange(body, exch, name, grid, in_specs, out_specs, out_shape, scratch_shapes, args):
    n_in, n_out, n_scr = len(in_specs), len(out_shape), len(scratch_shapes)
    if exch is None:
        outs = pl.pallas_call(
            body, name=name, grid=grid, in_specs=in_specs, out_specs=out_specs, out_shape=out_shape,
            scratch_shapes=scratch_shapes, compiler_params=_params(len(grid)),
        )(*args)
        return outs, []
    e_shapes = exch.out_shapes()
    e_in, e_out = len(exch.operands), len(e_shapes)

    def wrapped(*refs):
        ins, refs = refs[:n_in], refs[n_in:]
        e_ins, refs = refs[:e_in], refs[e_in:]
        outs, refs = refs[:n_out], refs[n_out:]
        e_outs, refs = refs[:e_out], refs[e_out:]
        scr, sems = refs[:n_scr], refs[n_scr:]
        first, last = True, True
        for d, g in enumerate(grid):
            first = (pl.program_id(d) == 0) & first
            last = (pl.program_id(d) == g - 1) & last
        finish = exch.run(e_ins, e_outs, sems, first, last)
        body(*ins, *outs, *scr)
        finish()

    any_spec = pl.BlockSpec(memory_space=pl.ANY)
    outs = pl.pallas_call(
        wrapped,
        name=name,
        grid=grid,
        in_specs=list(in_specs) + [any_spec] * e_in,
        out_specs=list(out_specs) + [any_spec] * e_out,
        out_shape=list(out_shape) + e_shapes,
        input_output_aliases={n_in + i: n_out + i for i in range(exch.n_alias)},
        scratch_shapes=list(scratch_shapes) + exch.scratch(),
        compiler_params=pltpu.CompilerParams(
            dimension_semantics=("arbitrary",) * len(grid), vmem_limit_bytes=VMEM_LIMIT, has_side_effects=True
        ),
    )(*args, *exch.operands)
    return outs[:n_out], outs[n_out:]


def _exchange_only(exch, name):
    n_in = len(exch.operands)
    shapes = exch.out_shapes()

    def body(*refs):
        ins, outs, sems = refs[:n_in], refs[n_in : n_in + len(shapes)], refs[n_in + len(shapes) :]
        exch.run(ins, outs, sems, True, True)()

    any_spec = pl.BlockSpec(memory_space=pl.ANY)
    return pl.pallas_call(
        body,
        name=name,
        in_specs=[any_spec] * n_in,
        out_specs=[any_spec] * len(shapes),
        out_shape=shapes,
        input_output_aliases={i: i for i in range(exch.n_alias)},
        scratch_shapes=exch.scratch(),
        compiler_params=pltpu.CompilerParams(has_side_effects=True),
    )(*exch.operands)


ATTN_LANE_TILES = 2
ATTN_UNROLL = 17


MASKED = -1e30


def _hi_lo(x):
    hi = x.astype(BF16)
    lo = (x - hi.astype(F32)).astype(BF16)
    return jnp.concatenate([hi, lo], axis=1)


def _suffix_matrix(inclusive):
    j = lax.broadcasted_iota(jnp.int32, (2 * Q_TILE, 2 * Q_TILE), 0) & (Q_TILE - 1)
    s = lax.broadcasted_iota(jnp.int32, (2 * Q_TILE, 2 * Q_TILE), 1)
    later = (j >= s) if inclusive else (j > s)
    return jnp.where((s >= Q_TILE) | later, 1.0, 0.0).astype(BF16)


def _log_beta(z):
    return jnp.minimum(z, 0.0) - jnp.log(1.0 + jnp.exp(-jnp.abs(z)))


def _head_masks(width):
    lane = lax.broadcasted_iota(jnp.int32, (1, width), 1)
    return [(lane >= h * HEAD_DIM) & (lane < (h + 1) * HEAD_DIM) for h in range(width // HEAD_DIM)]


def _per_head_rows(x, masks):
    return jnp.concatenate([jnp.where(hm, x, 0) for hm in masks], axis=0)


def _heads_to_lanes(x, n_heads):
    return jnp.concatenate([x[h * Q_TILE : (h + 1) * Q_TILE] for h in range(n_heads)], axis=1)


def _block_start(kb):
    return kb * Q_TILE if isinstance(kb, int) else pl.multiple_of(kb * Q_TILE, Q_TILE)


def _clamp(i, n):
    return jnp.minimum(i, n - 1)


def _next_block(pos):
    qi, kb = pos
    row_done = kb == 0
    nqi = jnp.where(row_done, qi + 1, qi)
    return nqi, jnp.where(row_done, nqi, kb - 1)


def _stream_unroll(n_blocks):
    return next(u for u in (ATTN_UNROLL, 2, 1) if n_blocks % u == 0)


def _past_mask(rows):
    t = lax.broadcasted_iota(jnp.int32, (rows, Q_TILE), 0) & (Q_TILE - 1)
    s = lax.broadcasted_iota(jnp.int32, (rows, Q_TILE), 1)
    return s < t


def _attn_fwd(qkv, n_seq, S, D, exch, name):
    T = n_seq * S
    width = min(D, ATTN_LANE_TILES * LANES)
    n_heads = width // HEAD_DIM
    rows = n_heads * Q_TILE
    nq = S // Q_TILE
    groups = D // width
    n_blocks = nq * (nq + 1) // 2
    unroll = _stream_unroll(n_blocks)
    scale = HEAD_DIM ** -0.5

    n_trips = n_blocks // unroll

    def body(q_ref, k_ref, v_ref, o_ref, a_out, b_out, qh_scr, vh_scr, bias_scr, a_stage, b_stage, sems):
        masks = _head_masks(width)
        sfx = _suffix_matrix(False)
        stream = pl.program_id(0) * groups + pl.program_id(1)

        def per_head_tables(i, c):
            blk = pl.ds(_block_start(i), Q_TILE)
            qh_scr[i] = _per_head_rows(q_ref[blk, :] * scale, masks)
            vh_scr[i] = _per_head_rows(v_ref[blk, :], masks)
            return c

        lax.fori_loop(0, nq, per_head_tables, 0)
        bias_scr[0] = jnp.zeros((rows, Q_TILE), F32)
        bias_scr[1] = jnp.where(_past_mask(rows), 0.0, MASKED)

        def save(n, slot):
            blocks = pl.ds(n * unroll, unroll)
            return [
                pltpu.make_async_copy(a_stage.at[slot], a_out.at[stream, blocks], sems.at[slot]),
                pltpu.make_async_copy(b_stage.at[slot], b_out.at[stream, blocks], sems.at[2 + slot]),
            ]

        def scores(pos):
            qi, kb = pos
            kt = k_ref[pl.ds(_block_start(_clamp(kb, nq)), Q_TILE), :]
            z = lax.dot_general(qh_scr[_clamp(qi, nq)], kt, NT, preferred_element_type=F32)
            z = z + bias_scr[(kb == qi).astype(jnp.int32)]
            lb = _log_beta(z)
            return lb, _hi_lo(lb - z)

        def weigh(pos, st, carry, acc, slot, u):
            qi, kb = pos
            lb, l1 = st
            r = jnp.dot(l1, sfx, preferred_element_type=F32)
            carry = jnp.where(kb == qi, 0.0, carry)
            a = jnp.exp(lb + r[:, :Q_TILE] + carry).astype(BF16)
            a_stage[slot, u] = a
            b_stage[slot, u] = jnp.exp(lb).astype(BF16)
            acc = jnp.where(kb == qi, 0.0, acc) + jnp.dot(
                _heads_to_lanes(a, n_heads), vh_scr[_clamp(kb, nq)], preferred_element_type=F32
            )
            o_ref[pl.ds(_block_start(_clamp(qi, nq)), Q_TILE), :] = acc
            return carry + r[:, Q_TILE:], acc

        def trip(n, c):
            pos, st, carry, acc = c
            slot = n % 2

            @pl.when((n >= 2) | (stream > 0))
            def _():
                for cp in save(0, slot):
                    cp.wait()

            for u in range(unroll):
                nxt = _next_block(pos)
                st_nxt = scores(nxt)
                carry, acc = weigh(pos, st, carry, acc, slot, u)
                pos, st = nxt, st_nxt
            for cp in save(n, slot):
                cp.start()
            return pos, st, carry, acc

        first = (jnp.int32(0), jnp.int32(0))
        zero = bias_scr[0]
        init = (first, scores(first), zero, jnp.concatenate([zero[:Q_TILE]] * (width // Q_TILE), axis=1))
        lax.fori_loop(0, n_trips, trip, init)

        @pl.when(stream == n_seq * groups - 1)
        def _():
            for slot in range(min(2, n_trips)):
                for cp in save(0, slot):
                    cp.wait()

    seq = lambda col0: pl.BlockSpec((S, width), lambda b, p: (b, col0 + p))
    saved = jax.ShapeDtypeStruct((n_seq * groups, n_blocks, rows, Q_TILE), BF16)
    stage = pltpu.VMEM((2, unroll, rows, Q_TILE), BF16)
    (o, a_w, beta), moved = _call_with_exchange(
        body,
        exch,
        name,
        grid=(n_seq, groups),
        in_specs=[seq(0), seq(groups), seq(2 * groups)],
        out_specs=[seq(0), pl.BlockSpec(memory_space=pl.ANY), pl.BlockSpec(memory_space=pl.ANY)],
        out_shape=[jax.ShapeDtypeStruct((T, D), F32), saved, saved],
        scratch_shapes=[
            pltpu.VMEM((nq, rows, width), BF16),
            pltpu.VMEM((nq, rows, width), BF16),
            pltpu.VMEM((2, rows, Q_TILE), F32),
            stage,
            stage,
            pltpu.SemaphoreType.DMA((4,)),
        ],
        args=(qkv, qkv, qkv),
    )
    return (o, a_w, beta), moved


def _attn_bwd(qkv, fwd, do, n_seq, S, D, exch, name):
    o, a_w, beta = fwd
    T = n_seq * S
    width = min(D, ATTN_LANE_TILES * LANES)
    n_heads = width // HEAD_DIM
    rows = n_heads * Q_TILE
    nq = S // Q_TILE
    groups = D // width
    n_blocks = nq * (nq + 1) // 2
    unroll = _stream_unroll(n_blocks)
    scale = HEAD_DIM ** -0.5

    n_trips = n_blocks // unroll

    def body(q_ref, k_ref, v_ref, o_ref, do_ref, a_in, b_in, dq_ref, dk_ref, dv_ref, dk_acc, dv_acc, qh_scr, doh_scr, delta_scr, a_stage, b_stage, sems):
        masks = _head_masks(width)
        sfx_incl = _suffix_matrix(True)
        stream = pl.program_id(0) * groups + pl.program_id(1)

        n_streams = n_seq * groups
        ahead = n_trips % 2 == 0

        def fetch(s, n, slot):
            blocks = pl.ds(n * unroll, unroll)
            return [
                pltpu.make_async_copy(a_in.at[s, blocks], a_stage.at[slot], sems.at[slot]),
                pltpu.make_async_copy(b_in.at[s, blocks], b_stage.at[slot], sems.at[2 + slot]),
            ]

        @pl.when((stream == 0) | (not ahead))
        def _():
            for cp in fetch(stream, 0, 0):
                cp.start()

        dk_acc[...] = jnp.zeros_like(dk_acc)
        dv_acc[...] = jnp.zeros_like(dv_acc)

        def per_head_tables(i, c):
            blk = pl.ds(_block_start(i), Q_TILE)
            do = do_ref[blk, :]
            qh_scr[i] = _per_head_rows(q_ref[blk, :] * scale, masks)
            doh_scr[i] = _per_head_rows(do, masks)
            prod = do.astype(F32) * o_ref[blk, :]
            delta = jnp.concatenate(
                [jnp.sum(jnp.where(hm, prod, 0.0), axis=-1, keepdims=True) for hm in masks], axis=0
            )
            delta_scr[i] = jnp.broadcast_to(delta, (rows, Q_TILE))
            return c

        lax.fori_loop(0, nq, per_head_tables, 0)

        def weigh(pos, ab, beta, c2, dq):
            qi, kb = pos
            first = kb == qi
            blk = pl.ds(_block_start(kb), Q_TILE)
            g = ab.astype(F32) * lax.dot_general(doh_scr[qi], v_ref[blk, :], NT, preferred_element_type=F32)
            r2 = jnp.dot(_hi_lo(g), sfx_incl, preferred_element_type=F32)
            c2 = jnp.where(first, 0.0, c2)
            earlier = delta_scr[qi] - (r2[:, :Q_TILE] + c2)
            beta = beta.astype(F32)
            dzb = (g * (1.0 - beta) - earlier * beta).astype(BF16)
            kh = _per_head_rows(k_ref[blk, :], masks)
            dq = jnp.where(first, 0.0, dq) + jnp.dot(_heads_to_lanes(dzb, n_heads), kh, preferred_element_type=F32)
            dq_ref[pl.ds(_block_start(qi), Q_TILE), :] = (dq * scale).astype(BF16)
            dk_acc[blk, :] += lax.dot_general(dzb, qh_scr[qi], TN, preferred_element_type=F32)
            dv_acc[blk, :] += lax.dot_general(ab, doh_scr[qi], TN, preferred_element_type=F32)
            return c2 + r2[:, Q_TILE:], dq

        def trip(n, c):
            pos, c2, dq = c
            slot = n % 2
            for cp in fetch(stream, n, slot):
                cp.wait()
            more = n + 1 < n_trips
            if ahead:
                nxt = (jnp.where(more, stream, stream + 1), jnp.where(more, n + 1, 0))
                more = more | (stream + 1 < n_streams)
            else:
                nxt = (stream, n + 1)

            @pl.when(more)
            def _():
                for cp in fetch(*nxt, 1 - slot):
                    cp.start()

            for u in range(unroll):
                c2, dq = weigh(pos, a_stage[slot, u], b_stage[slot, u], c2, dq)
                pos = _next_block(pos)
            return pos, c2, dq

        zero = dk_acc[pl.ds(0, Q_TILE), :]
        init = ((jnp.int32(0), jnp.int32(0)), jnp.concatenate([zero[:, :Q_TILE]] * n_heads, axis=0), zero)
        lax.fori_loop(0, n_trips, trip, init)
        dk_ref[...] = dk_acc[...].astype(BF16)
        dv_ref[...] = dv_acc[...].astype(BF16)

    seq = lambda col0: pl.BlockSpec((S, width), lambda b, p: (b, col0 + p))
    return _call_with_exchange(
        body,
        exch,
        name,
        grid=(n_seq, groups),
        in_specs=[seq(0), seq(groups), seq(2 * groups), seq(0), seq(0)] + [pl.BlockSpec(memory_space=pl.ANY)] * 2,
        out_specs=[seq(0)] * 3,
        out_shape=[jax.ShapeDtypeStruct((T, D), BF16)] * 3,
        scratch_shapes=[
            pltpu.VMEM((S, width), F32),
            pltpu.VMEM((S, width), F32),
            pltpu.VMEM((nq, rows, width), BF16),
            pltpu.VMEM((nq, rows, width), BF16),
            pltpu.VMEM((nq, rows, Q_TILE), F32),
            pltpu.VMEM((2, unroll, rows, Q_TILE), BF16),
            pltpu.VMEM((2, unroll, rows, Q_TILE), BF16),
            pltpu.SemaphoreType.DMA((4,)),
        ],
        args=(qkv, qkv, qkv, o, do, a_w, beta),
    )


def _causal_ws(ws_ref, g):
    t = lax.broadcasted_iota(jnp.int32, (SGU_CHUNK, SGU_CHUNK), 0)
    s = lax.broadcasted_iota(jnp.int32, (SGU_CHUNK, SGU_CHUNK), 1)
    return jnp.where(s <= t, ws_ref[g], 0.0)


def _sgu_fwd(a, gain, ws, bsb, name):
    T, F2 = a.shape
    F = F2 // 2
    gw = F // SGU_GROUPS

    def body(a_ref, gain_ref, ws_ref, bsb_ref, y_ref):
        v = _gelu(a_ref[:, F:].astype(F32))
        vn = (v * _rstd(v) * gain_ref[...]).astype(BF16)
        for g in range(SGU_GROUPS):
            cs = slice(g * gw, (g + 1) * gw)
            w = _causal_ws(ws_ref, g).astype(BF16)
            mixed = jnp.dot(w, vn[:, cs], preferred_element_type=F32) + bsb_ref[g]
            y_ref[:, cs] = (_gelu(a_ref[:, cs].astype(F32)) * mixed).astype(BF16)

    return pl.pallas_call(
        body,
        name=name,
        grid=(T // SGU_CHUNK,),
        in_specs=[
            pl.BlockSpec((SGU_CHUNK, F2), lambda i: (i, 0)),
            pl.BlockSpec((1, F), lambda i: (0, 0)),
            pl.BlockSpec((SGU_GROUPS, SGU_CHUNK, SGU_CHUNK), lambda i: (0, 0, 0)),
            pl.BlockSpec((SGU_GROUPS, SGU_CHUNK, gw), lambda i: (0, 0, 0)),
        ],
        out_specs=pl.BlockSpec((SGU_CHUNK, F), lambda i: (i, 0)),
        out_shape=jax.ShapeDtypeStruct((T, F), BF16),
        compiler_params=_params(1),
    )(a, gain.reshape(1, F), ws, bsb)


def _sgu_bwd(a, dy, gain, ws, bsb, name):
    T, F2 = a.shape
    F = F2 // 2
    gw = F // SGU_GROUPS

    def body(a_ref, dy_ref, gain_ref, ws_ref, bsb_ref, da_ref, dws_ref, dbs_ref, dgain_ref, dvn_ref):
        @pl.when(pl.program_id(0) == 0)
        def _():
            dws_ref[...] = jnp.zeros_like(dws_ref)
            dbs_ref[...] = jnp.zeros_like(dbs_ref)
            dgain_ref[...] = jnp.zeros_like(dgain_ref)

        v, v_slope = _gelu_and_grad(a_ref[:, F:].astype(F32))
        rstd = _rstd(v)
        vh = v * rstd
        gain = gain_ref[...]
        vn = (vh * gain).astype(BF16)
        ones = jnp.ones((gw, SGU_CHUNK), BF16)
        for g in range(SGU_GROUPS):
            cs = slice(g * gw, (g + 1) * gw)
            w = _causal_ws(ws_ref, g).astype(BF16)
            mixed = jnp.dot(w, vn[:, cs], preferred_element_type=F32) + bsb_ref[g]
            u, u_slope = _gelu_and_grad(a_ref[:, cs].astype(F32))
            dyc = dy_ref[:, cs].astype(F32)
            da_ref[:, cs] = (dyc * mixed * u_slope).astype(BF16)
            dm = (dyc * u).astype(BF16)
            dbs_ref[g] += jnp.dot(dm, ones, preferred_element_type=F32)
            dws_ref[g] += _causal_mask_f32(lax.dot_general(dm, vn[:, cs], NT, preferred_element_type=F32))
            dvn_ref[:, cs] = lax.dot_general(w, dm, TN, preferred_element_type=F32)
        dvn = dvn_ref[...]
        dgain_ref[...] += jnp.sum(dvn * vh, axis=0, keepdims=True)
        dvh = dvn * gain
        dv = rstd * (dvh - vh * jnp.mean(dvh * vh, axis=-1, keepdims=True))
        da_ref[:, F:] = (dv * v_slope).astype(BF16)

    acc_spec = pl.BlockSpec((SGU_GROUPS, SGU_CHUNK, SGU_CHUNK), lambda i: (0, 0, 0))
    acc_shape = jax.ShapeDtypeStruct((SGU_GROUPS, SGU_CHUNK, SGU_CHUNK), F32)
    return pl.pallas_call(
        body,
        name=name,
        grid=(T // SGU_CHUNK,),
        in_specs=[
            pl.BlockSpec((SGU_CHUNK, F2), lambda i: (i, 0)),
            pl.BlockSpec((SGU_CHUNK, F), lambda i: (i, 0)),
            pl.BlockSpec((1, F), lambda i: (0, 0)),
            acc_spec,
            pl.BlockSpec((SGU_GROUPS, SGU_CHUNK, gw), lambda i: (0, 0, 0)),
        ],
        out_specs=[
            pl.BlockSpec((SGU_CHUNK, F2), lambda i: (i, 0)),
            acc_spec,
            acc_spec,
            pl.BlockSpec((1, F), lambda i: (0, 0)),
        ],
        out_shape=[
            jax.ShapeDtypeStruct((T, F2), BF16),
            acc_shape,
            acc_shape,
            jax.ShapeDtypeStruct((1, F), F32),
        ],
        scratch_shapes=[pltpu.VMEM((SGU_CHUNK, F), F32)],
        compiler_params=_params(1),
    )(a, dy, gain.reshape(1, F), ws, bsb)


def _causal_mask_f32(m):
    t = lax.broadcasted_iota(jnp.int32, m.shape, 0)
    s = lax.broadcasted_iota(jnp.int32, m.shape, 1)
    return jnp.where(s <= t, m, 0.0)


def _final_loss(x, gain, target, name):
    T, D = x.shape
    tm = min(T, ROW_TILE // 2)

    def body(x_ref, g_ref, t_ref, sq_ref, dx_ref, dxb_ref, dg_ref):
        xv = x_ref[...]
        gain = g_ref[...]
        err = xv * _rstd(xv) * gain - t_ref[...]
        dx, dg = _norm_bwd(err * (1.0 / D), xv, gain)
        dx_ref[...] = dx
        dxb_ref[...] = dx.astype(BF16)
        sq = jnp.sum(err * err, axis=0, keepdims=True)

        @pl.when(pl.program_id(0) == 0)
        def _():
            sq_ref[...] = sq
            dg_ref[...] = dg

        @pl.when(pl.program_id(0) > 0)
        def _():
            sq_ref[...] += sq
            dg_ref[...] += dg

    row = pl.BlockSpec((tm, D), lambda i: (i, 0))
    vec = pl.BlockSpec((1, D), lambda i: (0, 0))
    return pl.pallas_call(
        body,
        name=name,
        grid=(T // tm,),
        in_specs=[row, vec, row],
        out_specs=[vec, row, row, vec],
        out_shape=[
            jax.ShapeDtypeStruct((1, D), F32),
            jax.ShapeDtypeStruct((T, D), F32),
            jax.ShapeDtypeStruct((T, D), BF16),
            jax.ShapeDtypeStruct((1, D), F32),
        ],
        compiler_params=_params(1),
    )(x, gain.reshape(1, D), target)


def _row_tile(rows, cols, n_arrays):
    budget = VMEM_LIMIT // 2 // (2 * n_arrays * cols * 4)
    tr = rows
    while tr > budget and tr % 16 == 0:
        tr //= 2
    return tr


def _sum_received(own, recv, name):
    R, C = own.shape
    n = recv.shape[0]
    tr = _row_tile(R, C, n + 2)

    def body(own_ref, recv_ref, o_ref):
        s = own_ref[...]
        for k in range(n):
            s = s + recv_ref[k].astype(F32)
        o_ref[...] = s

    return pl.pallas_call(
        body,
        name=name,
        grid=(R // tr,),
        in_specs=[pl.BlockSpec((tr, C), lambda i: (i, 0)), pl.BlockSpec((n, tr, C), lambda i: (0, i, 0))],
        out_specs=pl.BlockSpec((tr, C), lambda i: (i, 0)),
        out_shape=jax.ShapeDtypeStruct((R, C), F32),
        compiler_params=_params(1),
    )(own, recv)


def _sum_chip_shard(g32, recv, chip, name):
    _, L, r, c = g32.shape
    n = recv.shape[0]
    tr = _row_tile(r, c, n + 2)

    def body(chip_ref, own_ref, recv_ref, o_ref):
        s = own_ref[...]
        for k in range(n):
            s = s + recv_ref[k].astype(F32)
        o_ref[...] = s

    return pl.pallas_call(
        body,
        name=name,
        grid_spec=pltpu.PrefetchScalarGridSpec(
            num_scalar_prefetch=1,
            grid=(L, r // tr),
            in_specs=[
                pl.BlockSpec((None, None, tr, c), lambda l, i, chip_ref: (chip_ref[0], l, i, 0)),
                pl.BlockSpec((n, None, tr, c), lambda l, i, chip_ref: (0, l, i, 0)),
            ],
            out_specs=pl.BlockSpec((None, tr, c), lambda l, i, chip_ref: (l, i, 0)),
        ),
        out_shape=jax.ShapeDtypeStruct((L, r, c), F32),
        compiler_params=_params(2),
    )(chip.reshape(1).astype(jnp.int32), g32, recv)


def _adamw(w, m, v, parts, name):
    R, C = w.shape
    n = len(parts)
    tr = _row_tile(R, C, n + 7)

    def body(*refs):
        w_ref, m_ref, v_ref = refs[:3]
        g_ref, d_ref, nm_ref, nv_ref = refs[3 + n :]
        g = refs[3][...]
        for p_ref in refs[4 : 3 + n]:
            g = g + p_ref[...]
        nm = ADAM_B1 * m_ref[...] + (1.0 - ADAM_B1) * g
        nv = ADAM_B2 * v_ref[...] + (1.0 - ADAM_B2) * (g * g)
        m_hat = nm / (1.0 - ADAM_B1**ADAM_STEP)
        v_hat = nv / (1.0 - ADAM_B2**ADAM_STEP)
        g_ref[...] = g
        d_ref[...] = -ADAM_LR * (m_hat / (jnp.sqrt(v_hat) + ADAM_EPS) + ADAM_WD * w_ref[...])
        nm_ref[...] = nm
        nv_ref[...] = nv

    spec = pl.BlockSpec((tr, C), lambda i: (i, 0))
    return pl.pallas_call(
        body,
        name=name,
        grid=(R // tr,),
        in_specs=[spec] * (3 + n),
        out_specs=[spec] * 4,
        out_shape=[jax.ShapeDtypeStruct((R, C), F32)] * 4,
        compiler_params=_params(1),
    )(w, m, v, *parts)


def _swap_with_sibling(parts, name):
    n = len(parts)

    def body(*refs):
        ins, outs = refs[:n], refs[n : 2 * n]
        send_sems, recv_sems = refs[2 * n :]
        sibling = (lax.axis_index("x"), lax.axis_index("y"), 1 - lax.axis_index("c"))
        copies = [
            pltpu.make_async_remote_copy(
                src_ref=ins[a],
                dst_ref=outs[a],
                send_sem=send_sems.at[a],
                recv_sem=recv_sems.at[a],
                device_id=sibling,
                device_id_type=MESH,
            )
            for a in range(n)
        ]
        for cp in copies:
            cp.start()
        for cp in copies:
            cp.wait_recv()
        for cp in copies:
            cp.wait_send()

    any_spec = pl.BlockSpec(memory_space=pl.ANY)
    return pl.pallas_call(
        body,
        name=name,
        in_specs=[any_spec] * n,
        out_specs=[any_spec] * n,
        out_shape=[jax.ShapeDtypeStruct(p.shape, p.dtype) for p in parts],
        scratch_shapes=[pltpu.SemaphoreType.DMA((n,)), pltpu.SemaphoreType.DMA((n,))],
        compiler_params=pltpu.CompilerParams(has_side_effects=True),
    )(*parts)


def _pack(pieces):
    flat = jnp.concatenate([p.reshape(-1) for p in pieces])
    return flat.reshape(-1, LANES)


def _unpack(packed, shapes):
    flat = packed.reshape(-1)
    out, off = [], 0
    for s in shapes:
        size = 1
        for d in s:
            size *= d
        out.append(flat[off : off + size].reshape(s))
        off += size
    return out


def kernel(x, norm_mix, norm_mlp, sb_wqkv, sb_wo, sgu_win, sgu_gain, sgu_ws, sgu_bs, sgu_wout, mlp_w1, mlp_w2, final_norm, loss_target, m_norm_mix, m_norm_mlp, m_sb_wqkv, m_sb_wo, m_sgu_win, m_sgu_gain, m_sgu_ws, m_sgu_bs, m_sgu_wout, m_mlp_w1, m_mlp_w2, m_final_norm, v_norm_mix, v_norm_mlp, v_sb_wqkv, v_sb_wo, v_sgu_win, v_sgu_gain, v_sgu_ws, v_sgu_bs, v_sgu_wout, v_mlp_w1, v_mlp_w2, v_final_norm):
    n_seq, S, D = x.shape
    T = n_seq * S
    depth = norm_mix.shape[0]
    n_sgu = sgu_win.shape[0]
    F = sgu_wout.shape[1] * N_CHIPS
    gw = F // SGU_GROUPS
    chip = 2 * lax.axis_index("x") + lax.axis_index("y")

    QKV, WO, WIN, WOUT, W1, W2, GAIN = range(7)
    big = [sb_wqkv, sb_wo, sgu_win, sgu_wout, mlp_w1, mlp_w2]
    n_sb = sb_wqkv.shape[0]
    shards = [w.astype(BF16) for w in big] + [sgu_gain.reshape(1, -1, LANES)]

    def gather_plan(i):
        j, mlp = i // 2, min(2, depth - i)
        plan = [(WO, j, 1), (W1, i, mlp), (W2, i, mlp)]
        if i + 1 < depth:
            plan += [(WIN, (i + 1) // 2, 1), (WOUT, (i + 1) // 2, 1)]
        if j + 1 < n_sb:
            plan += [(QKV, j + 1, 1)]
        return plan

    wg = _exchange_only(_gather_exchange(shards, None, [(QKV, 0, 1), (GAIN, 0, 1)]), "gather_first_weights")
    gain_full = jnp.transpose(wg[GAIN].reshape(N_CHIPS, n_sgu, F // N_CHIPS), (1, 0, 2)).reshape(n_sgu, F)
    bsb = [jnp.broadcast_to(sgu_bs[j][:, :, None], (SGU_GROUPS, SGU_CHUNK, gw)) for j in range(n_sgu)]

    xs = x.reshape(T, D)
    saved = []
    for i in range(depth):
        j = i // 2
        if i % 2 == 0:
            qkv, h = _norm_matmul(xs, norm_mix[i], wg[QKV], j, f"qkv_fwd_{i}")
            attn, wg = _attn_fwd(qkv, n_seq, S, D, _gather_exchange(shards, wg, gather_plan(i)), f"attn_fwd_{i}")
            wg_qkv, wg_wo, wg_win, wg_wout, wg_w1, wg_w2 = wg[:6]
            x_mid = _act_matmul_res(attn[0], wg_wo, j, xs, None, f"wo_fwd_{i}")
            mix = (qkv, attn)
        else:
            a, h = _norm_matmul(xs, norm_mix[i], wg_win, j, f"win_fwd_{i}")
            yg = _sgu_fwd(a, gain_full[j], sgu_ws[j], bsb[j], f"sgu_fwd_{i}")
            x_mid = _act_matmul_res(yg, wg_wout, j, xs, None, f"wout_fwd_{i}")
            mix = (a, yg)
        a2, h2 = _norm_matmul(x_mid, norm_mlp[i], wg_w1, i, f"w1_fwd_{i}")
        x_out = _act_matmul_res(a2, wg_w2, i, x_mid, "relu2", f"w2_fwd_{i}")
        saved.append((xs, h, mix, x_mid, h2, a2))
        xs = x_out

    sq, dx, dxb, g_final = _final_loss(xs, final_norm, loss_target.reshape(T, D), "loss_head")
    loss = lax.psum(0.5 * jnp.sum(sq) / D, ("x", "y", "c"))

    n_layers = [n_sb, n_sb, n_sgu, n_sgu, depth, depth]
    g32, g16, recv = [None] * 6, [None] * 6, [None] * 6
    done_from, sent_from = list(n_layers), list(n_layers)

    def grad(a, layer, lhs, rhs, shard_lhs, act, name):
        bufs = None if g32[a] is None else (g32[a], g16[a])
        g32[a], g16[a] = _matmul_tn(lhs, rhs, bufs, layer, n_layers[a], shard_lhs, act, name)
        done_from[a] = layer

    def unsent_plan():
        plan = [(a, done_from[a], sent_from[a] - done_from[a]) for a in range(6) if sent_from[a] > done_from[a]]
        for a, l0, _ in plan:
            sent_from[a] = l0
        return plan

    def scatter(plan, small):
        if not plan and small is None:
            return None, lambda moved: None
        arrays = sorted({a for a, _, _ in plan})
        have = [a for a in arrays if recv[a] is not None]
        made = [a for a in arrays if recv[a] is None]
        exch = _scatter_exchange(
            [g16[a] for a in arrays], [recv[a] for a in arrays], [(arrays.index(a), l0, n) for a, l0, n in plan], small
        )

        def take(moved):
            for a, buf in zip(arrays, moved):
                g16[a] = buf
            for a, buf in zip(have + made, moved[len(arrays) :]):
                recv[a] = buf
            return moved[-1]

        return exch, take

    g_mix, g_mlp = [None] * depth, [None] * depth
    g_ws, g_bs, g_gain = [None] * n_sgu, [None] * n_sgu, [None] * n_sgu
    for i in reversed(range(depth)):
        j = i // 2
        x_in, h, mix, x_mid, h2, a2 = saved[i]
        da2 = _matmul_nt(dxb, wg_w2, i, a2, f"w2_bwd_{i}")
        grad(W2, i, a2, dxb, True, "relu2", f"w2_grad_{i}")
        grad(W1, i, h2, da2, False, None, f"w1_grad_{i}")
        (dx, dxb, g_mlp[i]), _ = _matmul_nt_norm_bwd(da2, wg_w1, i, x_mid, norm_mlp[i], dx, None, f"w1_bwd_{i}")
        if i % 2 == 0:
            qkv, attn = mix
            do = _matmul_nt(dxb, wg_wo, j, None, f"wo_bwd_{i}")
            grad(WO, j, attn[0], dxb, True, None, f"wo_grad_{i}")
            sgu_small = _pack([jnp.stack(g_ws), jnp.stack(g_bs), jnp.stack(g_gain)]) if i == 0 and n_sgu else None
            exch, take = scatter(unsent_plan(), sgu_small)
            (dq, dk, dv), moved = _attn_bwd(qkv, attn, do, n_seq, S, D, exch, f"attn_bwd_{i}")
            last = take(moved)
            if sgu_small is not None:
                sgu_small_all = last
            dqkv = jnp.concatenate([dq, dk, dv], axis=1)
            grad(QKV, j, h, dqkv, False, None, f"qkv_grad_{i}")
            exch, take = scatter(unsent_plan(), None)
            (dx, dxb, g_mix[i]), moved = _matmul_nt_norm_bwd(
                dqkv, wg_qkv, j, x_in, norm_mix[i], dx, exch, f"qkv_bwd_{i}"
            )
            take(moved)
        else:
            a, yg = mix
            dyg = _matmul_nt(dxb, wg_wout, j, None, f"wout_bwd_{i}")
            grad(WOUT, j, yg, dxb, True, None, f"wout_grad_{i}")
            da, g_ws[j], dbs, g_gain[j] = _sgu_bwd(a, dyg, gain_full[j], sgu_ws[j], bsb[j], f"sgu_bwd_{i}")
            g_bs[j] = dbs[:, :, 0]
            grad(WIN, j, h, da, False, None, f"win_grad_{i}")
            (dx, dxb, g_mix[i]), _ = _matmul_nt_norm_bwd(da, wg_win, j, x_in, norm_mix[i], dx, None, f"win_bwd_{i}")
    grad_x = dx.reshape(n_seq, S, D)

    names = ["qkv", "wo", "win", "wout", "w1", "w2"]
    exch, take = scatter(unsent_plan(), _pack([jnp.stack(g_mix), jnp.stack(g_mlp), g_final]))
    norm_small_all = take(_exchange_only(exch, "gather_norm_grads"))
    partial = [_sum_chip_shard(g32[a], recv[a], chip, f"sum_{names[a]}") for a in range(6)]
    partial = [p.reshape(-1, p.shape[-1]) for p in partial]
    theirs = _swap_with_sibling(partial, "swap_partial_sums")
    g_small = _unpack(
        _sum_received(norm_small_all[0], norm_small_all[1:], "sum_norm_grads"),
        [norm_mix.shape, norm_mlp.shape, final_norm.shape],
    ) + _unpack(
        _sum_received(sgu_small_all[0], sgu_small_all[1:], "sum_sgu_small_grads"),
        [sgu_ws.shape, sgu_bs.shape, (n_sgu, F)],
    )

    ms = [m_sb_wqkv, m_sb_wo, m_sgu_win, m_sgu_wout, m_mlp_w1, m_mlp_w2]
    vs = [v_sb_wqkv, v_sb_wo, v_sgu_win, v_sgu_wout, v_mlp_w1, v_mlp_w2]
    res = {}
    keys = ["sb_wqkv", "sb_wo", "sgu_win", "sgu_wout", "mlp_w1", "mlp_w2"]
    for key, k, w, m, v, mine, other in zip(keys, names, big, ms, vs, partial, theirs):
        cols = w.shape[-1]
        outs = _adamw(w.reshape(-1, cols), m.reshape(-1, cols), v.reshape(-1, cols), [mine, other], f"adamw_{k}")
        res[key] = [o.reshape(w.shape) for o in outs]

    g_small[5] = lax.dynamic_slice_in_dim(g_small[5], chip * (F // N_CHIPS), F // N_CHIPS, axis=1)
    small_keys = ["norm_mix", "norm_mlp", "final_norm", "sgu_ws", "sgu_bs", "sgu_gain"]
    small_w = [norm_mix, norm_mlp, final_norm, sgu_ws, sgu_bs, sgu_gain]
    small_m = [m_norm_mix, m_norm_mlp, m_final_norm, m_sgu_ws, m_sgu_bs, m_sgu_gain]
    small_v = [v_norm_mix, v_norm_mlp, v_final_norm, v_sgu_ws, v_sgu_bs, v_sgu_gain]
    outs = _adamw(_pack(small_w), _pack(small_m), _pack(small_v), [_pack(g_small)], "adamw_small")
    local_shapes = [w.shape for w in small_w]
    for key, parts in zip(small_keys, zip(*[_unpack(o, local_shapes) for o in outs])):
        res[key] = list(parts)

    order = ["norm_mix", "norm_mlp", "sb_wqkv", "sb_wo", "sgu_win", "sgu_gain", "sgu_ws", "sgu_bs", "sgu_wout", "mlp_w1", "mlp_w2", "final_norm"]
    return (loss, grad_x, *[res[k][0] for k in order], *[res[k][1] for k in order], *[res[k][2] for k in order], *[res[k][3] for k in order])
```

```python
import jax
import jax.numpy as jnp
from jax import lax
from jax.experimental import pallas as pl
from jax.experimental.pallas import tpu as pltpu

F32 = jnp.float32
BF16 = jnp.bfloat16
MESH = pl.DeviceIdType.MESH

EPS = 1e-6
HEAD_DIM = 64
LANES = 128
Q_TILE = 128
SGU_CHUNK = 128
SGU_GROUPS = 8
N_CHIPS = 4
N_DEV = 8
ADAM_LR = 0.001
ADAM_B1 = 0.9
ADAM_B2 = 0.999
ADAM_EPS = 1e-08
ADAM_WD = 0.01
ADAM_STEP = 10
GELU_C0 = 0.7978845608028654
GELU_C1 = 0.044715
VMEM_LIMIT = 48 * 1024 * 1024
ROW_TILE = 1024
NT = (((1,), (1,)), ((), ()))
TN = (((0,), (0,)), ((), ()))


def _params(n_axes):
    return pltpu.CompilerParams(dimension_semantics=("arbitrary",) * n_axes, vmem_limit_bytes=VMEM_LIMIT)


def _rstd(x):
    return lax.rsqrt(jnp.mean(x * x, axis=-1, keepdims=True) + EPS)


def _norm_bwd(dh, x, gain):
    rstd = _rstd(x)
    xh = x * rstd
    dhg = dh * gain
    dx = rstd * (dhg - xh * jnp.mean(dhg * xh, axis=-1, keepdims=True))
    return dx, jnp.sum(dh * xh, axis=0, keepdims=True)


def _gelu(x):
    return (0.5 * x) * (1.0 + jnp.tanh(x * (GELU_C0 + (GELU_C0 * GELU_C1) * (x * x))))


def _gelu_and_grad(x):
    x2 = x * x
    t = jnp.tanh(x * (GELU_C0 + (GELU_C0 * GELU_C1) * x2))
    half_x, p = 0.5 * x, 1.0 + t
    slope = GELU_C0 + (3.0 * GELU_C0 * GELU_C1) * x2
    return half_x * p, 0.5 * p + (half_x * slope) * (1.0 - t * t)


def _act(a, act):
    if act == "relu2":
        r = jnp.maximum(a.astype(F32), 0.0)
        return (r * r).astype(BF16)
    return a.astype(BF16)


def _layer_spec(wg, layer):
    nsh, _, r, c = wg.shape
    return pl.BlockSpec((nsh, None, r, c), lambda i: (0, layer, 0, 0), pipeline_mode=pl.Buffered(1))


def _norm_matmul(x, gain, wg, layer, name):
    T, D = x.shape
    nsh, _, _, ns = wg.shape
    tm = min(T, ROW_TILE // 2)

    def body(x_ref, g_ref, w_ref, y_ref, h_ref):
        xv = x_ref[...]
        h = (xv * _rstd(xv) * g_ref[...]).astype(BF16)
        h_ref[...] = h
        for j in range(nsh):
            y_ref[:, j * ns : (j + 1) * ns] = jnp.dot(h, w_ref[j], preferred_element_type=F32).astype(BF16)

    return pl.pallas_call(
        body,
        name=name,
        grid=(T // tm,),
        in_specs=[pl.BlockSpec((tm, D), lambda i: (i, 0)), pl.BlockSpec((1, D), lambda i: (0, 0)), _layer_spec(wg, layer)],
        out_specs=[pl.BlockSpec((tm, nsh * ns), lambda i: (i, 0)), pl.BlockSpec((tm, D), lambda i: (i, 0))],
        out_shape=[jax.ShapeDtypeStruct((T, nsh * ns), BF16), jax.ShapeDtypeStruct((T, D), BF16)],
        compiler_params=_params(1),
    )(x, gain.reshape(1, D), wg)


def _act_matmul_res(a, wg, layer, x_in, act, name):
    T, K = a.shape
    nsh, _, kq, D = wg.shape
    tm = min(T, ROW_TILE // 2)

    def body(a_ref, w_ref, x_ref, o_ref):
        w = w_ref[...].reshape(nsh * kq, D)
        o_ref[...] = x_ref[...] + jnp.dot(_act(a_ref[...], act), w, preferred_element_type=F32)

    return pl.pallas_call(
        body,
        name=name,
        grid=(T // tm,),
        in_specs=[pl.BlockSpec((tm, K), lambda i: (i, 0)), _layer_spec(wg, layer), pl.BlockSpec((tm, D), lambda i: (i, 0))],
        out_specs=pl.BlockSpec((tm, D), lambda i: (i, 0)),
        out_shape=jax.ShapeDtypeStruct((T, D), F32),
        compiler_params=_params(1),
    )(a, wg, x_in)


def _matmul_nt(g, wg, layer, a, name):
    T, D = g.shape
    nsh, _, kq, _ = wg.shape
    tm = min(T, ROW_TILE // 2)

    def body(g_ref, w_ref, *rest):
        gv = g_ref[...]
        for k in range(nsh):
            cols = slice(k * kq, (k + 1) * kq)
            r = lax.dot_general(gv, w_ref[k], NT, preferred_element_type=F32)
            if a is not None:
                r = r * (2.0 * jnp.maximum(rest[0][:, cols].astype(F32), 0.0))
            rest[-1][:, cols] = r.astype(BF16)

    row = pl.BlockSpec((tm, nsh * kq), lambda i: (i, 0))
    in_specs = [pl.BlockSpec((tm, D), lambda i: (i, 0)), _layer_spec(wg, layer)]
    args = [g, wg]
    if a is not None:
        in_specs.append(row)
        args.append(a)
    return pl.pallas_call(
        body,
        name=name,
        grid=(T // tm,),
        in_specs=in_specs,
        out_specs=row,
        out_shape=jax.ShapeDtypeStruct((T, nsh * kq), BF16),
        compiler_params=_params(1),
    )(*args)


def _matmul_nt_norm_bwd(da, wg, layer, x, gain, dres, exch, name):
    T, D = x.shape
    nsh, _, _, ns = wg.shape
    tm = min(T, ROW_TILE // 2)

    def body(da_ref, w_ref, x_ref, g_ref, r_ref, dx_ref, dxb_ref, dg_ref):
        dh = lax.dot_general(da_ref[:, :ns], w_ref[0], NT, preferred_element_type=F32)
        for j in range(1, nsh):
            dh = dh + lax.dot_general(da_ref[:, j * ns : (j + 1) * ns], w_ref[j], NT, preferred_element_type=F32)
        dx, dg = _norm_bwd(dh, x_ref[...], g_ref[...])
        dx = dx + r_ref[...]
        dx_ref[...] = dx
        dxb_ref[...] = dx.astype(BF16)

        @pl.when(pl.program_id(0) == 0)
        def _():
            dg_ref[...] = dg

        @pl.when(pl.program_id(0) > 0)
        def _():
            dg_ref[...] += dg

    row = pl.BlockSpec((tm, D), lambda i: (i, 0))
    vec = pl.BlockSpec((1, D), lambda i: (0, 0))
    return _call_with_exchange(
        body,
        exch,
        name,
        grid=(T // tm,),
        in_specs=[pl.BlockSpec((tm, nsh * ns), lambda i: (i, 0)), _layer_spec(wg, layer), row, vec, row],
        out_specs=[row, row, vec],
        out_shape=[
            jax.ShapeDtypeStruct((T, D), F32),
            jax.ShapeDtypeStruct((T, D), BF16),
            jax.ShapeDtypeStruct((1, D), F32),
        ],
        scratch_shapes=[],
        args=(da, wg, x, gain.reshape(1, D), dres),
    )


def _matmul_tn(lhs, rhs, bufs, layer, n_layers, shard_lhs, act, name):
    T = lhs.shape[0]
    rows = lhs.shape[1] // N_CHIPS if shard_lhs else lhs.shape[1]
    cols = rhs.shape[1] if shard_lhs else rhs.shape[1] // N_CHIPS
    tt = min(T, 2 * ROW_TILE)
    n_t = T // tt

    def body(l_ref, r_ref, *rest):
        o32_ref, o16_ref = rest[-2:]
        t = pl.program_id(1)
        upd = lax.dot_general(_act(l_ref[...], act), r_ref[...].astype(BF16), TN, preferred_element_type=F32)

        @pl.when(t == 0)
        def _():
            o32_ref[...] = upd

        @pl.when(t > 0)
        def _():
            o32_ref[...] += upd

        @pl.when(t == n_t - 1)
        def _():
            o16_ref[...] = o32_ref[...].astype(BF16)

    if shard_lhs:
        in_specs = [pl.BlockSpec((tt, rows), lambda s, t: (t, s)), pl.BlockSpec((tt, cols), lambda s, t: (t, 0))]
    else:
        in_specs = [pl.BlockSpec((tt, rows), lambda s, t: (t, 0)), pl.BlockSpec((tt, cols), lambda s, t: (t, s))]
    args = [lhs, rhs]
    aliases = {}
    if bufs is not None:
        in_specs += [pl.BlockSpec(memory_space=pl.ANY)] * 2
        args += list(bufs)
        aliases = {2: 0, 3: 1}
    shape = (N_CHIPS, n_layers, rows, cols)
    return pl.pallas_call(
        body,
        name=name,
        grid=(N_CHIPS, n_t),
        in_specs=in_specs,
        out_specs=[pl.BlockSpec((None, None, rows, cols), lambda s, t: (s, layer, 0, 0))] * 2,
        out_shape=[jax.ShapeDtypeStruct(shape, F32), jax.ShapeDtypeStruct(shape, BF16)],
        input_output_aliases=aliases,
        compiler_params=_params(2),
    )(*args)


def _chip_peers(x, y):
    return [(1 - x, y), (x, 1 - y), (1 - x, 1 - y)]


def _remote(src, dst, sems, s, peer):
    return pltpu.make_async_remote_copy(
        src_ref=src, dst_ref=dst, send_sem=sems[0].at[s], recv_sem=sems[1].at[s], device_id=peer, device_id_type=MESH
    )


class _Exchange:
    def __init__(self, operands, n_alias, new_shapes, n_sems, build):
        self.operands, self.n_alias, self.new_shapes, self.n_sems, self.build = operands, n_alias, new_shapes, n_sems, build

    def out_shapes(self):
        return [jax.ShapeDtypeStruct(a.shape, a.dtype) for a in self.operands[: self.n_alias]] + list(self.new_shapes)

    def scratch(self):
        return [pltpu.SemaphoreType.DMA((n,)) for n in self.n_sems]

    def run(self, ins, outs, sems, first, last):
        starts, recvs, sends, locals_ = self.build(ins, outs, sems)

        def start_all():
            for cp in starts:
                cp.start()

        def wait_all():
            for cp in recvs:
                cp.wait_recv()
            for cp in sends:
                cp.wait_send()
            for cp in locals_:
                cp.wait()

        if first is True:
            start_all()
            return wait_all
        pl.when(first)(start_all)
        return lambda: pl.when(last)(wait_all)


def _gather_exchange(shards, bufs, plan):
    n_arr = len(shards)
    n_cp = len(plan) * (N_CHIPS - 1)

    def build(ins, outs, sems):
        shard_refs = ins[-n_arr:]
        x, y, c = lax.axis_index("x"), lax.axis_index("y"), lax.axis_index("c")
        me = 2 * x + y
        recvs, sends, locals_ = [], [], []
        for p, (a, l0, n) in enumerate(plan):
            src = shard_refs[a].at[pl.ds(l0, n)]
            cp = pltpu.make_async_copy(src, outs[a].at[me, pl.ds(l0, n)], sems[2].at[p])
            locals_.append(cp)
            for k, (px, py) in enumerate(_chip_peers(x, y)):
                s = p * (N_CHIPS - 1) + k
                sends.append(_remote(src, outs[a].at[me, pl.ds(l0, n)], sems, s, (px, py, c)))
                recvs.append(_remote(src, outs[a].at[2 * px + py, pl.ds(l0, n)], sems, s, (px, py, c)))
        return locals_ + sends, recvs, sends, locals_

    if bufs is None:
        new = [jax.ShapeDtypeStruct((N_CHIPS,) + s.shape, s.dtype) for s in shards]
        return _Exchange(list(shards), 0, new, [n_cp, n_cp, len(plan)], build)
    return _Exchange(list(bufs) + list(shards), n_arr, [], [n_cp, n_cp, len(plan)], build)


def _scatter_exchange(g16, recv, plan, small=None):
    n_arr = len(g16)
    have = [r for r in recv if r is not None]
    made = [a for a in range(n_arr) if recv[a] is None]
    n_cp = len(plan) * (N_CHIPS - 1) + (N_DEV - 1 if small is not None else 0)

    def build(ins, outs, sems):
        g_refs = ins[:n_arr]
        recv_refs, it_have, it_made = [], iter(outs[n_arr : n_arr + len(have)]), iter(outs[n_arr + len(have) :])
        for a in range(n_arr):
            recv_refs.append(next(it_made) if recv[a] is None else next(it_have))
        x, y, c = lax.axis_index("x"), lax.axis_index("y"), lax.axis_index("c")
        me = 2 * x + y
        recvs, sends, locals_ = [], [], []
        for p, (a, l0, n) in enumerate(plan):
            for k, (px, py) in enumerate(_chip_peers(x, y)):
                s = p * (N_CHIPS - 1) + k
                dst = recv_refs[a].at[k, pl.ds(l0, n)]
                sends.append(_remote(g_refs[a].at[2 * px + py, pl.ds(l0, n)], dst, sems, s, (px, py, c)))
                recvs.append(_remote(g_refs[a].at[me, pl.ds(l0, n)], dst, sems, s, (px, py, c)))
        if small is not None:
            small_ref, all_ref = ins[-1], outs[-1]
            slot = 4 * x + 2 * y + c
            locals_.append(pltpu.make_async_copy(small_ref, all_ref.at[slot], sems[2].at[0]))
            flips = [(fx, fy, fc) for fx in (0, 1) for fy in (0, 1) for fc in (0, 1)][1:]
            for k, (fx, fy, fc) in enumerate(flips):
                s = len(plan) * (N_CHIPS - 1) + k
                px, py, pc = x ^ fx, y ^ fy, c ^ fc
                sends.append(_remote(small_ref, all_ref.at[slot], sems, s, (px, py, pc)))
                recvs.append(_remote(small_ref, all_ref.at[4 * px + 2 * py + pc], sems, s, (px, py, pc)))
        return locals_ + sends, recvs, sends, locals_

    operands = list(g16) + have + ([small] if small is not None else [])
    new = [jax.ShapeDtypeStruct((N_CHIPS - 1,) + g16[a].shape[1:], BF16) for a in made]
    if small is not None:
        new.append(jax.ShapeDtypeStruct((N_DEV,) + small.shape, F32))
    return _Exchange(operands, n_arr + len(have), new, [n_cp, n_cp, 1], build)


def _call_with_exchange(body, exch, name, grid, in_specs, out_specs, out_shape, scratch_shapes, args):
    n_in, n_out, n_scr = len(in_specs), len(out_shape), len(scratch_shapes)
    if exch is None:
        outs = pl.pallas_call(
            body, name=name, grid=grid, in_specs=in_specs, out_specs=out_specs, out_shape=out_shape,
            scratch_shapes=scratch_shapes, compiler_params=_params(len(grid)),
        )(*args)
        return outs, []
    e_shapes = exch.out_shapes()
    e_in, e_out = len(exch.operands), len(e_shapes)

    def wrapped(*refs):
        ins, refs = refs[:n_in], refs[n_in:]
        e_ins, refs = refs[:e_in], refs[e_in:]
        outs, refs = refs[:n_out], refs[n_out:]
        e_outs, refs = refs[:e_out], refs[e_out:]
        scr, sems = refs[:n_scr], refs[n_scr:]
        first, last = True, True
        for d, g in enumerate(grid):
            first = (pl.program_id(d) == 0) & first
            last = (pl.program_id(d) == g - 1) & last
        finish = exch.run(e_ins, e_outs, sems, first, last)
        body(*ins, *outs, *scr)
        finish()

    any_spec = pl.BlockSpec(memory_space=pl.ANY)
    outs = pl.pallas_call(
        wrapped,
        name=name,
        grid=grid,
        in_specs=list(in_specs) + [any_spec] * e_in,
        out_specs=list(out_specs) + [any_spec] * e_out,
        out_shape=list(out_shape) + e_shapes,
        input_output_aliases={n_in + i: n_out + i for i in range(exch.n_alias)},
        scratch_shapes=list(scratch_shapes) + exch.scratch(),
        compiler_params=pltpu.CompilerParams(
            dimension_semantics=("arbitrary",) * len(grid), vmem_limit_bytes=VMEM_LIMIT, has_side_effects=True
        ),
    )(*args, *exch.operands)
    return outs[:n_out], outs[n_out:]


def _exchange_only(exch, name):
    n_in = len(exch.operands)
    shapes = exch.out_shapes()

    def body(*refs):
        ins, outs, sems = refs[:n_in], refs[n_in : n_in + len(shapes)], refs[n_in + len(shapes) :]
        exch.run(ins, outs, sems, True, True)()

    any_spec = pl.BlockSpec(memory_space=pl.ANY)
    return pl.pallas_call(
        body,
        name=name,
        in_specs=[any_spec] * n_in,
        out_specs=[any_spec] * len(shapes),
        out_shape=shapes,
        input_output_aliases={i: i for i in range(exch.n_alias)},
        scratch_shapes=exch.scratch(),
        compiler_params=pltpu.CompilerParams(has_side_effects=True),
    )(*exch.operands)


ATTN_LANE_TILES = 2
ATTN_UNROLL = 17


MASKED = -1e30


def _hi_lo(x):
    hi = x.astype(BF16)
    lo = (x - hi.astype(F32)).astype(BF16)
    return jnp.concatenate([hi, lo], axis=1)


def _suffix_matrix(inclusive):
    j = lax.broadcasted_iota(jnp.int32, (2 * Q_TILE, 2 * Q_TILE), 0) & (Q_TILE - 1)
    s = lax.broadcasted_iota(jnp.int32, (2 * Q_TILE, 2 * Q_TILE), 1)
    later = (j >= s) if inclusive else (j > s)
    return jnp.where((s >= Q_TILE) | later, 1.0, 0.0).astype(BF16)


def _log_beta(z):
    return jnp.minimum(z, 0.0) - jnp.log(1.0 + jnp.exp(-jnp.abs(z)))


def _head_masks(width):
    lane = lax.broadcasted_iota(jnp.int32, (1, width), 1)
    return [(lane >= h * HEAD_DIM) & (lane < (h + 1) * HEAD_DIM) for h in range(width // HEAD_DIM)]


def _per_head_rows(x, masks):
    return jnp.concatenate([jnp.where(hm, x, 0) for hm in masks], axis=0)


def _heads_to_lanes(x, n_heads):
    return jnp.concatenate([x[h * Q_TILE : (h + 1) * Q_TILE] for h in range(n_heads)], axis=1)


def _block_start(kb):
    return kb * Q_TILE if isinstance(kb, int) else pl.multiple_of(kb * Q_TILE, Q_TILE)


def _clamp(i, n):
    return jnp.minimum(i, n - 1)


def _next_block(pos):
    qi, kb = pos
    row_done = kb == 0
    nqi = jnp.where(row_done, qi + 1, qi)
    return nqi, jnp.where(row_done, nqi, kb - 1)


def _stream_unroll(n_blocks):
    return next(u for u in (ATTN_UNROLL, 2, 1) if n_blocks % u == 0)


def _past_mask(rows):
    t = lax.broadcasted_iota(jnp.int32, (rows, Q_TILE), 0) & (Q_TILE - 1)
    s = lax.broadcasted_iota(jnp.int32, (rows, Q_TILE), 1)
    return s < t


def _attn_fwd(qkv, n_seq, S, D, exch, name):
    T = n_seq * S
    width = min(D, ATTN_LANE_TILES * LANES)
    n_heads = width // HEAD_DIM
    rows = n_heads * Q_TILE
    nq = S // Q_TILE
    groups = D // width
    n_blocks = nq * (nq + 1) // 2
    unroll = _stream_unroll(n_blocks)
    scale = HEAD_DIM ** -0.5

    n_trips = n_blocks // unroll

    def body(q_ref, k_ref, v_ref, o_ref, a_out, b_out, qh_scr, vh_scr, bias_scr, a_stage, b_stage, sems):
        masks = _head_masks(width)
        sfx = _suffix_matrix(False)
        stream = pl.program_id(0) * groups + pl.program_id(1)

        def per_head_tables(i, c):
            blk = pl.ds(_block_start(i), Q_TILE)
            qh_scr[i] = _per_head_rows(q_ref[blk, :] * scale, masks)
            vh_scr[i] = _per_head_rows(v_ref[blk, :], masks)
            return c

        lax.fori_loop(0, nq, per_head_tables, 0)
        bias_scr[0] = jnp.zeros((rows, Q_TILE), F32)
        bias_scr[1] = jnp.where(_past_mask(rows), 0.0, MASKED)

        def save(n, slot):
            blocks = pl.ds(n * unroll, unroll)
            return [
                pltpu.make_async_copy(a_stage.at[slot], a_out.at[stream, blocks], sems.at[slot]),
                pltpu.make_async_copy(b_stage.at[slot], b_out.at[stream, blocks], sems.at[2 + slot]),
            ]

        def scores(pos):
            qi, kb = pos
            kt = k_ref[pl.ds(_block_start(_clamp(kb, nq)), Q_TILE), :]
            z = lax.dot_general(qh_scr[_clamp(qi, nq)], kt, NT, preferred_element_type=F32)
            z = z + bias_scr[(kb == qi).astype(jnp.int32)]
            lb = _log_beta(z)
            return lb, _hi_lo(lb - z)

        def weigh(pos, st, carry, acc, slot, u):
            qi, kb = pos
            lb, l1 = st
            r = jnp.dot(l1, sfx, preferred_element_type=F32)
            carry = jnp.where(kb == qi, 0.0, carry)
            a = jnp.exp(lb + r[:, :Q_TILE] + carry).astype(BF16)
            a_stage[slot, u] = a
            b_stage[slot, u] = jnp.exp(lb).astype(BF16)
            acc = jnp.where(kb == qi, 0.0, acc) + jnp.dot(
                _heads_to_lanes(a, n_heads), vh_scr[_clamp(kb, nq)], preferred_element_type=F32
            )
            o_ref[pl.ds(_block_start(_clamp(qi, nq)), Q_TILE), :] = acc
            return carry + r[:, Q_TILE:], acc

        def trip(n, c):
            pos, st, carry, acc = c
            slot = n % 2

            @pl.when((n >= 2) | (stream > 0))
            def _():
                for cp in save(0, slot):
                    cp.wait()

            for u in range(unroll):
                nxt = _next_block(pos)
                st_nxt = scores(nxt)
                carry, acc = weigh(pos, st, carry, acc, slot, u)
                pos, st = nxt, st_nxt
            for cp in save(n, slot):
                cp.start()
            return pos, st, carry, acc

        first = (jnp.int32(0), jnp.int32(0))
        zero = bias_scr[0]
        init = (first, scores(first), zero, jnp.concatenate([zero[:Q_TILE]] * (width // Q_TILE), axis=1))
        lax.fori_loop(0, n_trips, trip, init)

        @pl.when(stream == n_seq * groups - 1)
        def _():
            for slot in range(min(2, n_trips)):
                for cp in save(0, slot):
                    cp.wait()

    seq = lambda col0: pl.BlockSpec((S, width), lambda b, p: (b, col0 + p))
    saved = jax.ShapeDtypeStruct((n_seq * groups, n_blocks, rows, Q_TILE), BF16)
    stage = pltpu.VMEM((2, unroll, rows, Q_TILE), BF16)
    (o, a_w, beta), moved = _call_with_exchange(
        body,
        exch,
        name,
        grid=(n_seq, groups),
        in_specs=[seq(0), seq(groups), seq(2 * groups)],
        out_specs=[seq(0), pl.BlockSpec(memory_space=pl.ANY), pl.BlockSpec(memory_space=pl.ANY)],
        out_shape=[jax.ShapeDtypeStruct((T, D), F32), saved, saved],
        scratch_shapes=[
            pltpu.VMEM((nq, rows, width), BF16),
            pltpu.VMEM((nq, rows, width), BF16),
            pltpu.VMEM((2, rows, Q_TILE), F32),
            stage,
            stage,
            pltpu.SemaphoreType.DMA((4,)),
        ],
        args=(qkv, qkv, qkv),
    )
    return (o, a_w, beta), moved


def _attn_bwd(qkv, fwd, do, n_seq, S, D, exch, name):
    o, a_w, beta = fwd
    T = n_seq * S
    width = min(D, ATTN_LANE_TILES * LANES)
    n_heads = width // HEAD_DIM
    rows = n_heads * Q_TILE
    nq = S // Q_TILE
    groups = D // width
    n_blocks = nq * (nq + 1) // 2
    unroll = _stream_unroll(n_blocks)
    scale = HEAD_DIM ** -0.5

    n_trips = n_blocks // unroll

    def body(q_ref, k_ref, v_ref, o_ref, do_ref, a_in, b_in, dq_ref, dk_ref, dv_ref, dk_acc, dv_acc, qh_scr, doh_scr, delta_scr, a_stage, b_stage, sems):
        masks = _head_masks(width)
        sfx_incl = _suffix_matrix(True)
        stream = pl.program_id(0) * groups + pl.program_id(1)

        n_streams = n_seq * groups
        ahead = n_trips % 2 == 0

        def fetch(s, n, slot):
            blocks = pl.ds(n * unroll, unroll)
            return [
                pltpu.make_async_copy(a_in.at[s, blocks], a_stage.at[slot], sems.at[slot]),
                pltpu.make_async_copy(b_in.at[s, blocks], b_stage.at[slot], sems.at[2 + slot]),
            ]

        @pl.when((stream == 0) | (not ahead))
        def _():
            for cp in fetch(stream, 0, 0):
                cp.start()

        dk_acc[...] = jnp.zeros_like(dk_acc)
        dv_acc[...] = jnp.zeros_like(dv_acc)

        lane = lax.broadcasted_iota(jnp.int32, (2 * width, rows), 0) % width
        col = lax.broadcasted_iota(jnp.int32, (2 * width, rows), 1)
        head_sums = jnp.where(lane // HEAD_DIM == col // Q_TILE, 1.0, 0.0).astype(BF16)

        def per_head_tables(i, c):
            blk = pl.ds(_block_start(i), Q_TILE)
            do = do_ref[blk, :]
            qh_scr[i] = _per_head_rows(q_ref[blk, :] * scale, masks)
            doh_scr[i] = _per_head_rows(do, masks)
            prod = do.astype(F32) * o_ref[blk, :]
            d = jnp.dot(_hi_lo(prod), head_sums, preferred_element_type=F32)
            delta_scr[i] = jnp.concatenate([d[:, h * Q_TILE : (h + 1) * Q_TILE] for h in range(n_heads)], axis=0)
            return c

        lax.fori_loop(0, nq, per_head_tables, 0)

        def weigh(pos, ab, beta, c2, dq):
            qi, kb = pos
            first = kb == qi
            blk = pl.ds(_block_start(kb), Q_TILE)
            g = ab.astype(F32) * lax.dot_general(doh_scr[qi], v_ref[blk, :], NT, preferred_element_type=F32)
            r2 = jnp.dot(_hi_lo(g), sfx_incl, preferred_element_type=F32)
            c2 = jnp.where(first, 0.0, c2)
            earlier = delta_scr[qi] - (r2[:, :Q_TILE] + c2)
            beta = beta.astype(F32)
            dzb = (g * (1.0 - beta) - earlier * beta).astype(BF16)
            kh = _per_head_rows(k_ref[blk, :], masks)
            dq = jnp.where(first, 0.0, dq) + jnp.dot(_heads_to_lanes(dzb, n_heads), kh, preferred_element_type=F32)
            dq_ref[pl.ds(_block_start(qi), Q_TILE), :] = (dq * scale).astype(BF16)
            dk_acc[blk, :] += lax.dot_general(dzb, qh_scr[qi], TN, preferred_element_type=F32)
            dv_acc[blk, :] += lax.dot_general(ab, doh_scr[qi], TN, preferred_element_type=F32)
            return c2 + r2[:, Q_TILE:], dq

        def trip(n, c):
            pos, c2, dq = c
            slot = n % 2
            for cp in fetch(stream, n, slot):
                cp.wait()
            more = n + 1 < n_trips
            if ahead:
                nxt = (jnp.where(more, stream, stream + 1), jnp.where(more, n + 1, 0))
                more = more | (stream + 1 < n_streams)
            else:
                nxt = (stream, n + 1)

            @pl.when(more)
            def _():
                for cp in fetch(*nxt, 1 - slot):
                    cp.start()

            for u in range(unroll):
                c2, dq = weigh(pos, a_stage[slot, u], b_stage[slot, u], c2, dq)
                pos = _next_block(pos)
            return pos, c2, dq

        zero = dk_acc[pl.ds(0, Q_TILE), :]
        init = ((jnp.int32(0), jnp.int32(0)), jnp.concatenate([zero[:, :Q_TILE]] * n_heads, axis=0), zero)
        lax.fori_loop(0, n_trips, trip, init)
        dk_ref[...] = dk_acc[...].astype(BF16)
        dv_ref[...] = dv_acc[...].astype(BF16)

    seq = lambda col0: pl.BlockSpec((S, width), lambda b, p: (b, col0 + p))
    return _call_with_exchange(
        body,
        exch,
        name,
        grid=(n_seq, groups),
        in_specs=[seq(0), seq(groups), seq(2 * groups), seq(0), seq(0)] + [pl.BlockSpec(memory_space=pl.ANY)] * 2,
        out_specs=[seq(0)] * 3,
        out_shape=[jax.ShapeDtypeStruct((T, D), BF16)] * 3,
        scratch_shapes=[
            pltpu.VMEM((S, width), F32),
            pltpu.VMEM((S, width), F32),
            pltpu.VMEM((nq, rows, width), BF16),
            pltpu.VMEM((nq, rows, width), BF16),
            pltpu.VMEM((nq, rows, Q_TILE), F32),
            pltpu.VMEM((2, unroll, rows, Q_TILE), BF16),
            pltpu.VMEM((2, unroll, rows, Q_TILE), BF16),
            pltpu.SemaphoreType.DMA((4,)),
        ],
        args=(qkv, qkv, qkv, o, do, a_w, beta),
    )


def _causal_ws(ws_ref, g):
    t = lax.broadcasted_iota(jnp.int32, (SGU_CHUNK, SGU_CHUNK), 0)
    s = lax.broadcasted_iota(jnp.int32, (SGU_CHUNK, SGU_CHUNK), 1)
    return jnp.where(s <= t, ws_ref[g], 0.0)


def _sgu_fwd(a, gain, ws, bsb, name):
    T, F2 = a.shape
    F = F2 // 2
    gw = F // SGU_GROUPS

    def body(a_ref, gain_ref, ws_ref, bsb_ref, y_ref):
        v = _gelu(a_ref[:, F:].astype(F32))
        vn = (v * _rstd(v) * gain_ref[...]).astype(BF16)
        for g in range(SGU_GROUPS):
            cs = slice(g * gw, (g + 1) * gw)
            w = _causal_ws(ws_ref, g).astype(BF16)
            mixed = jnp.dot(w, vn[:, cs], preferred_element_type=F32) + bsb_ref[g]
            y_ref[:, cs] = (_gelu(a_ref[:, cs].astype(F32)) * mixed).astype(BF16)

    return pl.pallas_call(
        body,
        name=name,
        grid=(T // SGU_CHUNK,),
        in_specs=[
            pl.BlockSpec((SGU_CHUNK, F2), lambda i: (i, 0)),
            pl.BlockSpec((1, F), lambda i: (0, 0)),
            pl.BlockSpec((SGU_GROUPS, SGU_CHUNK, SGU_CHUNK), lambda i: (0, 0, 0)),
            pl.BlockSpec((SGU_GROUPS, SGU_CHUNK, gw), lambda i: (0, 0, 0)),
        ],
        out_specs=pl.BlockSpec((SGU_CHUNK, F), lambda i: (i, 0)),
        out_shape=jax.ShapeDtypeStruct((T, F), BF16),
        compiler_params=_params(1),
    )(a, gain.reshape(1, F), ws, bsb)


def _sgu_bwd(a, dy, gain, ws, bsb, name):
    T, F2 = a.shape
    F = F2 // 2
    gw = F // SGU_GROUPS

    def body(a_ref, dy_ref, gain_ref, ws_ref, bsb_ref, da_ref, dws_ref, dbs_ref, dgain_ref, dvn_ref):
        @pl.when(pl.program_id(0) == 0)
        def _():
            dws_ref[...] = jnp.zeros_like(dws_ref)
            dbs_ref[...] = jnp.zeros_like(dbs_ref)
            dgain_ref[...] = jnp.zeros_like(dgain_ref)

        v, v_slope = _gelu_and_grad(a_ref[:, F:].astype(F32))
        rstd = _rstd(v)
        vh = v * rstd
        gain = gain_ref[...]
        vn = (vh * gain).astype(BF16)
        ones = jnp.ones((gw, SGU_CHUNK), BF16)
        for g in range(SGU_GROUPS):
            cs = slice(g * gw, (g + 1) * gw)
            w = _causal_ws(ws_ref, g).astype(BF16)
            mixed = jnp.dot(w, vn[:, cs], preferred_element_type=F32) + bsb_ref[g]
            u, u_slope = _gelu_and_grad(a_ref[:, cs].astype(F32))
            dyc = dy_ref[:, cs].astype(F32)
            da_ref[:, cs] = (dyc * mixed * u_slope).astype(BF16)
            dm = (dyc * u).astype(BF16)
            dbs_ref[g] += jnp.dot(dm, ones, preferred_element_type=F32)
            dws_ref[g] += _causal_mask_f32(lax.dot_general(dm, vn[:, cs], NT, preferred_element_type=F32))
            dvn_ref[:, cs] = lax.dot_general(w, dm, TN, preferred_element_type=F32)
        dvn = dvn_ref[...]
        dgain_ref[...] += jnp.sum(dvn * vh, axis=0, keepdims=True)
        dvh = dvn * gain
        dv = rstd * (dvh - vh * jnp.mean(dvh * vh, axis=-1, keepdims=True))
        da_ref[:, F:] = (dv * v_slope).astype(BF16)

    acc_spec = pl.BlockSpec((SGU_GROUPS, SGU_CHUNK, SGU_CHUNK), lambda i: (0, 0, 0))
    acc_shape = jax.ShapeDtypeStruct((SGU_GROUPS, SGU_CHUNK, SGU_CHUNK), F32)
    return pl.pallas_call(
        body,
        name=name,
        grid=(T // SGU_CHUNK,),
        in_specs=[
            pl.BlockSpec((SGU_CHUNK, F2), lambda i: (i, 0)),
            pl.BlockSpec((SGU_CHUNK, F), lambda i: (i, 0)),
            pl.BlockSpec((1, F), lambda i: (0, 0)),
            acc_spec,
            pl.BlockSpec((SGU_GROUPS, SGU_CHUNK, gw), lambda i: (0, 0, 0)),
        ],
        out_specs=[
            pl.BlockSpec((SGU_CHUNK, F2), lambda i: (i, 0)),
            acc_spec,
            acc_spec,
            pl.BlockSpec((1, F), lambda i: (0, 0)),
        ],
        out_shape=[
            jax.ShapeDtypeStruct((T, F2), BF16),
            acc_shape,
            acc_shape,
            jax.ShapeDtypeStruct((1, F), F32),
        ],
        scratch_shapes=[pltpu.VMEM((SGU_CHUNK, F), F32)],
        compiler_params=_params(1),
    )(a, dy, gain.reshape(1, F), ws, bsb)


def _causal_mask_f32(m):
    t = lax.broadcasted_iota(jnp.int32, m.shape, 0)
    s = lax.broadcasted_iota(jnp.int32, m.shape, 1)
    return jnp.where(s <= t, m, 0.0)


def _final_loss(x, gain, target, name):
    T, D = x.shape
    tm = min(T, ROW_TILE // 2)

    def body(x_ref, g_ref, t_ref, sq_ref, dx_ref, dxb_ref, dg_ref):
        xv = x_ref[...]
        gain = g_ref[...]
        err = xv * _rstd(xv) * gain - t_ref[...]
        dx, dg = _norm_bwd(err * (1.0 / D), xv, gain)
        dx_ref[...] = dx
        dxb_ref[...] = dx.astype(BF16)
        sq = jnp.sum(err * err, axis=0, keepdims=True)

        @pl.when(pl.program_id(0) == 0)
        def _():
            sq_ref[...] = sq
            dg_ref[...] = dg

        @pl.when(pl.program_id(0) > 0)
        def _():
            sq_ref[...] += sq
            dg_ref[...] += dg

    row = pl.BlockSpec((tm, D), lambda i: (i, 0))
    vec = pl.BlockSpec((1, D), lambda i: (0, 0))
    return pl.pallas_call(
        body,
        name=name,
        grid=(T // tm,),
        in_specs=[row, vec, row],
        out_specs=[vec, row, row, vec],
        out_shape=[
            jax.ShapeDtypeStruct((1, D), F32),
            jax.ShapeDtypeStruct((T, D), F32),
            jax.ShapeDtypeStruct((T, D), BF16),
            jax.ShapeDtypeStruct((1, D), F32),
        ],
        compiler_params=_params(1),
    )(x, gain.reshape(1, D), target)


def _row_tile(rows, cols, n_arrays):
    budget = VMEM_LIMIT // 2 // (2 * n_arrays * cols * 4)
    tr = rows
    while tr > budget and tr % 16 == 0:
        tr //= 2
    return tr


def _sum_received(own, recv, name):
    R, C = own.shape
    n = recv.shape[0]
    tr = _row_tile(R, C, n + 2)

    def body(own_ref, recv_ref, o_ref):
        s = own_ref[...]
        for k in range(n):
            s = s + recv_ref[k].astype(F32)
        o_ref[...] = s

    return pl.pallas_call(
        body,
        name=name,
        grid=(R // tr,),
        in_specs=[pl.BlockSpec((tr, C), lambda i: (i, 0)), pl.BlockSpec((n, tr, C), lambda i: (0, i, 0))],
        out_specs=pl.BlockSpec((tr, C), lambda i: (i, 0)),
        out_shape=jax.ShapeDtypeStruct((R, C), F32),
        compiler_params=_params(1),
    )(own, recv)


def _sum_chip_shard(g32, recv, chip, name):
    _, L, r, c = g32.shape
    n = recv.shape[0]
    tr = _row_tile(r, c, n + 2)

    def body(chip_ref, own_ref, recv_ref, o_ref):
        s = own_ref[...]
        for k in range(n):
            s = s + recv_ref[k].astype(F32)
        o_ref[...] = s

    return pl.pallas_call(
        body,
        name=name,
        grid_spec=pltpu.PrefetchScalarGridSpec(
            num_scalar_prefetch=1,
            grid=(L, r // tr),
            in_specs=[
                pl.BlockSpec((None, None, tr, c), lambda l, i, chip_ref: (chip_ref[0], l, i, 0)),
                pl.BlockSpec((n, None, tr, c), lambda l, i, chip_ref: (0, l, i, 0)),
            ],
            out_specs=pl.BlockSpec((None, tr, c), lambda l, i, chip_ref: (l, i, 0)),
        ),
        out_shape=jax.ShapeDtypeStruct((L, r, c), F32),
        compiler_params=_params(2),
    )(chip.reshape(1).astype(jnp.int32), g32, recv)


def _adamw(w, m, v, parts, name):
    R, C = w.shape
    n = len(parts)
    tr = _row_tile(R, C, n + 7)

    def body(*refs):
        w_ref, m_ref, v_ref = refs[:3]
        g_ref, d_ref, nm_ref, nv_ref = refs[3 + n :]
        g = refs[3][...]
        for p_ref in refs[4 : 3 + n]:
            g = g + p_ref[...]
        nm = ADAM_B1 * m_ref[...] + (1.0 - ADAM_B1) * g
        nv = ADAM_B2 * v_ref[...] + (1.0 - ADAM_B2) * (g * g)
        m_hat = nm / (1.0 - ADAM_B1**ADAM_STEP)
        v_hat = nv / (1.0 - ADAM_B2**ADAM_STEP)
        g_ref[...] = g
        d_ref[...] = -ADAM_LR * (m_hat / (jnp.sqrt(v_hat) + ADAM_EPS) + ADAM_WD * w_ref[...])
        nm_ref[...] = nm
        nv_ref[...] = nv

    spec = pl.BlockSpec((tr, C), lambda i: (i, 0))
    return pl.pallas_call(
        body,
        name=name,
        grid=(R // tr,),
        in_specs=[spec] * (3 + n),
        out_specs=[spec] * 4,
        out_shape=[jax.ShapeDtypeStruct((R, C), F32)] * 4,
        compiler_params=_params(1),
    )(w, m, v, *parts)


def _swap_with_sibling(parts, name):
    n = len(parts)

    def body(*refs):
        ins, outs = refs[:n], refs[n : 2 * n]
        send_sems, recv_sems = refs[2 * n :]
        sibling = (lax.axis_index("x"), lax.axis_index("y"), 1 - lax.axis_index("c"))
        copies = [
            pltpu.make_async_remote_copy(
                src_ref=ins[a],
                dst_ref=outs[a],
                send_sem=send_sems.at[a],
                recv_sem=recv_sems.at[a],
                device_id=sibling,
                device_id_type=MESH,
            )
            for a in range(n)
        ]
        for cp in copies:
            cp.start()
        for cp in copies:
            cp.wait_recv()
        for cp in copies:
            cp.wait_send()

    any_spec = pl.BlockSpec(memory_space=pl.ANY)
    return pl.pallas_call(
        body,
        name=name,
        in_specs=[any_spec] * n,
        out_specs=[any_spec] * n,
        out_shape=[jax.ShapeDtypeStruct(p.shape, p.dtype) for p in parts],
        scratch_shapes=[pltpu.SemaphoreType.DMA((n,)), pltpu.SemaphoreType.DMA((n,))],
        compiler_params=pltpu.CompilerParams(has_side_effects=True),
    )(*parts)


def _pack(pieces):
    flat = jnp.concatenate([p.reshape(-1) for p in pieces])
    return flat.reshape(-1, LANES)


def _unpack(packed, shapes):
    flat = packed.reshape(-1)
    out, off = [], 0
    for s in shapes:
        size = 1
        for d in s:
            size *= d
        out.append(flat[off : off + size].reshape(s))
        off += size
    return out


def kernel(x, norm_mix, norm_mlp, sb_wqkv, sb_wo, sgu_win, sgu_gain, sgu_ws, sgu_bs, sgu_wout, mlp_w1, mlp_w2, final_norm, loss_target, m_norm_mix, m_norm_mlp, m_sb_wqkv, m_sb_wo, m_sgu_win, m_sgu_gain, m_sgu_ws, m_sgu_bs, m_sgu_wout, m_mlp_w1, m_mlp_w2, m_final_norm, v_norm_mix, v_norm_mlp, v_sb_wqkv, v_sb_wo, v_sgu_win, v_sgu_gain, v_sgu_ws, v_sgu_bs, v_sgu_wout, v_mlp_w1, v_mlp_w2, v_final_norm):
    n_seq, S, D = x.shape
    T = n_seq * S
    depth = norm_mix.shape[0]
    n_sgu = sgu_win.shape[0]
    F = sgu_wout.shape[1] * N_CHIPS
    gw = F // SGU_GROUPS
    chip = 2 * lax.axis_index("x") + lax.axis_index("y")

    QKV, WO, WIN, WOUT, W1, W2, GAIN = range(7)
    big = [sb_wqkv, sb_wo, sgu_win, sgu_wout, mlp_w1, mlp_w2]
    n_sb = sb_wqkv.shape[0]
    shards = [w.astype(BF16) for w in big] + [sgu_gain.reshape(1, -1, LANES)]

    def gather_plan(i):
        j, mlp = i // 2, min(2, depth - i)
        plan = [(WO, j, 1), (W1, i, mlp), (W2, i, mlp)]
        if i + 1 < depth:
            plan += [(WIN, (i + 1) // 2, 1), (WOUT, (i + 1) // 2, 1)]
        if j + 1 < n_sb:
            plan += [(QKV, j + 1, 1)]
        return plan

    wg = _exchange_only(_gather_exchange(shards, None, [(QKV, 0, 1), (GAIN, 0, 1)]), "gather_first_weights")
    gain_full = jnp.transpose(wg[GAIN].reshape(N_CHIPS, n_sgu, F // N_CHIPS), (1, 0, 2)).reshape(n_sgu, F)
    bsb = [jnp.broadcast_to(sgu_bs[j][:, :, None], (SGU_GROUPS, SGU_CHUNK, gw)) for j in range(n_sgu)]

    xs = x.reshape(T, D)
    saved = []
    for i in range(depth):
        j = i // 2
        if i % 2 == 0:
            qkv, h = _norm_matmul(xs, norm_mix[i], wg[QKV], j, f"qkv_fwd_{i}")
            attn, wg = _attn_fwd(qkv, n_seq, S, D, _gather_exchange(shards, wg, gather_plan(i)), f"attn_fwd_{i}")
            wg_qkv, wg_wo, wg_win, wg_wout, wg_w1, wg_w2 = wg[:6]
            x_mid = _act_matmul_res(attn[0], wg_wo, j, xs, None, f"wo_fwd_{i}")
            mix = (qkv, attn)
        else:
            a, h = _norm_matmul(xs, norm_mix[i], wg_win, j, f"win_fwd_{i}")
            yg = _sgu_fwd(a, gain_full[j], sgu_ws[j], bsb[j], f"sgu_fwd_{i}")
            x_mid = _act_matmul_res(yg, wg_wout, j, xs, None, f"wout_fwd_{i}")
            mix = (a, yg)
        a2, h2 = _norm_matmul(x_mid, norm_mlp[i], wg_w1, i, f"w1_fwd_{i}")
        x_out = _act_matmul_res(a2, wg_w2, i, x_mid, "relu2", f"w2_fwd_{i}")
        saved.append((xs, h, mix, x_mid, h2, a2))
        xs = x_out

    sq, dx, dxb, g_final = _final_loss(xs, final_norm, loss_target.reshape(T, D), "loss_head")
    loss = lax.psum(0.5 * jnp.sum(sq) / D, ("x", "y", "c"))

    n_layers = [n_sb, n_sb, n_sgu, n_sgu, depth, depth]
    g32, g16, recv = [None] * 6, [None] * 6, [None] * 6
    done_from, sent_from = list(n_layers), list(n_layers)

    def grad(a, layer, lhs, rhs, shard_lhs, act, name):
        bufs = None if g32[a] is None else (g32[a], g16[a])
        g32[a], g16[a] = _matmul_tn(lhs, rhs, bufs, layer, n_layers[a], shard_lhs, act, name)
        done_from[a] = layer

    def unsent_plan():
        plan = [(a, done_from[a], sent_from[a] - done_from[a]) for a in range(6) if sent_from[a] > done_from[a]]
        for a, l0, _ in plan:
            sent_from[a] = l0
        return plan

    def scatter(plan, small):
        if not plan and small is None:
            return None, lambda moved: None
        arrays = sorted({a for a, _, _ in plan})
        have = [a for a in arrays if recv[a] is not None]
        made = [a for a in arrays if recv[a] is None]
        exch = _scatter_exchange(
            [g16[a] for a in arrays], [recv[a] for a in arrays], [(arrays.index(a), l0, n) for a, l0, n in plan], small
        )

        def take(moved):
            for a, buf in zip(arrays, moved):
                g16[a] = buf
            for a, buf in zip(have + made, moved[len(arrays) :]):
                recv[a] = buf
            return moved[-1]

        return exch, take

    g_mix, g_mlp = [None] * depth, [None] * depth
    g_ws, g_bs, g_gain = [None] * n_sgu, [None] * n_sgu, [None] * n_sgu
    for i in reversed(range(depth)):
        j = i // 2
        x_in, h, mix, x_mid, h2, a2 = saved[i]
        da2 = _matmul_nt(dxb, wg_w2, i, a2, f"w2_bwd_{i}")
        grad(W2, i, a2, dxb, True, "relu2", f"w2_grad_{i}")
        grad(W1, i, h2, da2, False, None, f"w1_grad_{i}")
        (dx, dxb, g_mlp[i]), _ = _matmul_nt_norm_bwd(da2, wg_w1, i, x_mid, norm_mlp[i], dx, None, f"w1_bwd_{i}")
        if i % 2 == 0:
            qkv, attn = mix
            do = _matmul_nt(dxb, wg_wo, j, None, f"wo_bwd_{i}")
            grad(WO, j, attn[0], dxb, True, None, f"wo_grad_{i}")
            sgu_small = _pack([jnp.stack(g_ws), jnp.stack(g_bs), jnp.stack(g_gain)]) if i == 0 and n_sgu else None
            exch, take = scatter(unsent_plan(), sgu_small)
            (dq, dk, dv), moved = _attn_bwd(qkv, attn, do, n_seq, S, D, exch, f"attn_bwd_{i}")
            last = take(moved)
            if sgu_small is not None:
                sgu_small_all = last
            dqkv = jnp.concatenate([dq, dk, dv], axis=1)
            grad(QKV, j, h, dqkv, False, None, f"qkv_grad_{i}")
            exch, take = scatter(unsent_plan(), None)
            (dx, dxb, g_mix[i]), moved = _matmul_nt_norm_bwd(
                dqkv, wg_qkv, j, x_in, norm_mix[i], dx, exch, f"qkv_bwd_{i}"
            )
            take(moved)
        else:
            a, yg = mix
            dyg = _matmul_nt(dxb, wg_wout, j, None, f"wout_bwd_{i}")
            grad(WOUT, j, yg, dxb, True, None, f"wout_grad_{i}")
            da, g_ws[j], dbs, g_gain[j] = _sgu_bwd(a, dyg, gain_full[j], sgu_ws[j], bsb[j], f"sgu_bwd_{i}")
            g_bs[j] = dbs[:, :, 0]
            grad(WIN, j, h, da, False, None, f"win_grad_{i}")
            (dx, dxb, g_mix[i]), _ = _matmul_nt_norm_bwd(da, wg_win, j, x_in, norm_mix[i], dx, None, f"win_bwd_{i}")
    grad_x = dx.reshape(n_seq, S, D)

    names = ["qkv", "wo", "win", "wout", "w1", "w2"]
    exch, take = scatter(unsent_plan(), _pack([jnp.stack(g_mix), jnp.stack(g_mlp), g_final]))
    norm_small_all = take(_exchange_only(exch, "gather_norm_grads"))
    partial = [_sum_chip_shard(g32[a], recv[a], chip, f"sum_{names[a]}") for a in range(6)]
    partial = [p.reshape(-1, p.shape[-1]) for p in partial]
    theirs = _swap_with_sibling(partial, "swap_partial_sums")
    g_small = _unpack(
        _sum_received(norm_small_all[0], norm_small_all[1:], "sum_norm_grads"),
        [norm_mix.shape, norm_mlp.shape, final_norm.shape],
    ) + _unpack(
        _sum_received(sgu_small_all[0], sgu_small_all[1:], "sum_sgu_small_grads"),
        [sgu_ws.shape, sgu_bs.shape, (n_sgu, F)],
    )

    ms = [m_sb_wqkv, m_sb_wo, m_sgu_win, m_sgu_wout, m_mlp_w1, m_mlp_w2]
    vs = [v_sb_wqkv, v_sb_wo, v_sgu_win, v_sgu_wout, v_mlp_w1, v_mlp_w2]
    res = {}
    keys = ["sb_wqkv", "sb_wo", "sgu_win", "sgu_wout", "mlp_w1", "mlp_w2"]
    for key, k, w, m, v, mine, other in zip(keys, names, big, ms, vs, partial, theirs):
        cols = w.shape[-1]
        outs = _adamw(w.reshape(-1, cols), m.reshape(-1, cols), v.reshape(-1, cols), [mine, other], f"adamw_{k}")
        res[key] = [o.reshape(w.shape) for o in outs]

    g_small[5] = lax.dynamic_slice_in_dim(g_small[5], chip * (F // N_CHIPS), F // N_CHIPS, axis=1)
    small_keys = ["norm_mix", "norm_mlp", "final_norm", "sgu_ws", "sgu_bs", "sgu_gain"]
    small_w = [norm_mix, norm_mlp, final_norm, sgu_ws, sgu_bs, sgu_gain]
    small_m = [m_norm_mix, m_norm_mlp, m_final_norm, m_sgu_ws, m_sgu_bs, m_sgu_gain]
    small_v = [v_norm_mix, v_norm_mlp, v_final_norm, v_sgu_ws, v_sgu_bs, v_sgu_gain]
    outs = _adamw(_pack(small_w), _pack(small_m), _pack(small_v), [_pack(g_small)], "adamw_small")
    local_shapes = [w.shape for w in small_w]
    for key, parts in zip(small_keys, zip(*[_unpack(o, local_shapes) for o in outs])):
        res[key] = list(parts)

    order = ["norm_mix", "norm_mlp", "sb_wqkv", "sb_wo", "sgu_win", "sgu_gain", "sgu_ws", "sgu_bs", "sgu_wout", "mlp_w1", "mlp_w2", "final_norm"]
    return (loss, grad_x, *[res[k][0] for k in order], *[res[k][1] for k in order], *[res[k][2] for k in order], *[res[k][3] for k in order])
```

```python
import jax
import jax.numpy as jnp
from jax import lax
from jax.experimental import pallas as pl
from jax.experimental.pallas import tpu as pltpu

F32 = jnp.float32
BF16 = jnp.bfloat16
MESH = pl.DeviceIdType.MESH

EPS = 1e-6
HEAD_DIM = 64
LANES = 128
Q_TILE = 128
SGU_CHUNK = 128
SGU_GROUPS = 8
N_CHIPS = 4
N_DEV = 8
ADAM_LR = 0.001
ADAM_B1 = 0.9
ADAM_B2 = 0.999
ADAM_EPS = 1e-08
ADAM_WD = 0.01
ADAM_STEP = 10
GELU_C0 = 0.7978845608028654
GELU_C1 = 0.044715
VMEM_LIMIT = 48 * 1024 * 1024
ROW_TILE = 1024
NT = (((1,), (1,)), ((), ()))
TN = (((0,), (0,)), ((), ()))


def _params(n_axes):
    return pltpu.CompilerParams(dimension_semantics=("arbitrary",) * n_axes, vmem_limit_bytes=VMEM_LIMIT)


def _rstd(x):
    return lax.rsqrt(jnp.mean(x * x, axis=-1, keepdims=True) + EPS)


def _norm_bwd(dh, x, gain):
    rstd = _rstd(x)
    xh = x * rstd
    dhg = dh * gain
    dx = rstd * (dhg - xh * jnp.mean(dhg * xh, axis=-1, keepdims=True))
    return dx, jnp.sum(dh * xh, axis=0, keepdims=True)


def _gelu(x):
    return (0.5 * x) * (1.0 + jnp.tanh(x * (GELU_C0 + (GELU_C0 * GELU_C1) * (x * x))))


def _gelu_and_grad(x):
    x2 = x * x
    t = jnp.tanh(x * (GELU_C0 + (GELU_C0 * GELU_C1) * x2))
    half_x, p = 0.5 * x, 1.0 + t
    slope = GELU_C0 + (3.0 * GELU_C0 * GELU_C1) * x2
    return half_x * p, 0.5 * p + (half_x * slope) * (1.0 - t * t)


def _act(a, act):
    if act == "relu2":
        r = jnp.maximum(a.astype(F32), 0.0)
        return (r * r).astype(BF16)
    return a.astype(BF16)


def _layer_spec(wg, layer):
    nsh, _, r, c = wg.shape
    return pl.BlockSpec((nsh, None, r, c), lambda i: (0, layer, 0, 0), pipeline_mode=pl.Buffered(1))


def _norm_matmul(x, gain, wg, layer, name):
    T, D = x.shape
    nsh, _, _, ns = wg.shape
    tm = min(T, ROW_TILE // 2)

    def body(x_ref, g_ref, w_ref, y_ref, h_ref):
        xv = x_ref[...]
        h = (xv * _rstd(xv) * g_ref[...]).astype(BF16)
        h_ref[...] = h
        for j in range(nsh):
            y_ref[:, j * ns : (j + 1) * ns] = jnp.dot(h, w_ref[j], preferred_element_type=F32).astype(BF16)

    return pl.pallas_call(
        body,
        name=name,
        grid=(T // tm,),
        in_specs=[pl.BlockSpec((tm, D), lambda i: (i, 0)), pl.BlockSpec((1, D), lambda i: (0, 0)), _layer_spec(wg, layer)],
        out_specs=[pl.BlockSpec((tm, nsh * ns), lambda i: (i, 0)), pl.BlockSpec((tm, D), lambda i: (i, 0))],
        out_shape=[jax.ShapeDtypeStruct((T, nsh * ns), BF16), jax.ShapeDtypeStruct((T, D), BF16)],
        compiler_params=_params(1),
    )(x, gain.reshape(1, D), wg)


def _act_matmul_res(a, wg, layer, x_in, act, name):
    T, K = a.shape
    nsh, _, kq, D = wg.shape
    tm = min(T, ROW_TILE // 2)

    def body(a_ref, w_ref, x_ref, o_ref):
        w = w_ref[...].reshape(nsh * kq, D)
        o_ref[...] = x_ref[...] + jnp.dot(_act(a_ref[...], act), w, preferred_element_type=F32)

    return pl.pallas_call(
        body,
        name=name,
        grid=(T // tm,),
        in_specs=[pl.BlockSpec((tm, K), lambda i: (i, 0)), _layer_spec(wg, layer), pl.BlockSpec((tm, D), lambda i: (i, 0))],
        out_specs=pl.BlockSpec((tm, D), lambda i: (i, 0)),
        out_shape=jax.ShapeDtypeStruct((T, D), F32),
        compiler_params=_params(1),
    )(a, wg, x_in)


def _matmul_nt(g, wg, layer, a, name):
    T, D = g.shape
    nsh, _, kq, _ = wg.shape
    tm = min(T, ROW_TILE // 2)

    def body(g_ref, w_ref, *rest):
        gv = g_ref[...]
        for k in range(nsh):
            cols = slice(k * kq, (k + 1) * kq)
            r = lax.dot_general(gv, w_ref[k], NT, preferred_element_type=F32)
            if a is not None:
                r = r * (2.0 * jnp.maximum(rest[0][:, cols].astype(F32), 0.0))
            rest[-1][:, cols] = r.astype(BF16)

    row = pl.BlockSpec((tm, nsh * kq), lambda i: (i, 0))
    in_specs = [pl.BlockSpec((tm, D), lambda i: (i, 0)), _layer_spec(wg, layer)]
    args = [g, wg]
    if a is not None:
        in_specs.append(row)
        args.append(a)
    return pl.pallas_call(
        body,
        name=name,
        grid=(T // tm,),
        in_specs=in_specs,
        out_specs=row,
        out_shape=jax.ShapeDtypeStruct((T, nsh * kq), BF16),
        compiler_params=_params(1),
    )(*args)


def _matmul_nt_norm_bwd(da, wg, layer, x, gain, dres, exch, name):
    T, D = x.shape
    nsh, _, _, ns = wg.shape
    tm = min(T, ROW_TILE // 2)

    def body(da_ref, w_ref, x_ref, g_ref, r_ref, dx_ref, dxb_ref, dg_ref):
        dh = lax.dot_general(da_ref[:, :ns], w_ref[0], NT, preferred_element_type=F32)
        for j in range(1, nsh):
            dh = dh + lax.dot_general(da_ref[:, j * ns : (j + 1) * ns], w_ref[j], NT, preferred_element_type=F32)
        dx, dg = _norm_bwd(dh, x_ref[...], g_ref[...])
        dx = dx + r_ref[...]
        dx_ref[...] = dx
        dxb_ref[...] = dx.astype(BF16)

        @pl.when(pl.program_id(0) == 0)
        def _():
            dg_ref[...] = dg

        @pl.when(pl.program_id(0) > 0)
        def _():
            dg_ref[...] += dg

    row = pl.BlockSpec((tm, D), lambda i: (i, 0))
    vec = pl.BlockSpec((1, D), lambda i: (0, 0))
    return _call_with_exchange(
        body,
        exch,
        name,
        grid=(T // tm,),
        in_specs=[pl.BlockSpec((tm, nsh * ns), lambda i: (i, 0)), _layer_spec(wg, layer), row, vec, row],
        out_specs=[row, row, vec],
        out_shape=[
            jax.ShapeDtypeStruct((T, D), F32),
            jax.ShapeDtypeStruct((T, D), BF16),
            jax.ShapeDtypeStruct((1, D), F32),
        ],
        scratch_shapes=[],
        args=(da, wg, x, gain.reshape(1, D), dres),
    )


def _matmul_tn(lhs, rhs, bufs, layer, n_layers, shard_lhs, act, name):
    T = lhs.shape[0]
    rows = lhs.shape[1] // N_CHIPS if shard_lhs else lhs.shape[1]
    cols = rhs.shape[1] if shard_lhs else rhs.shape[1] // N_CHIPS
    tt = min(T, 2 * ROW_TILE)
    n_t = T // tt

    def body(l_ref, r_ref, *rest):
        o32_ref, o16_ref = rest[-2:]
        t = pl.program_id(1)
        upd = lax.dot_general(_act(l_ref[...], act), r_ref[...].astype(BF16), TN, preferred_element_type=F32)

        @pl.when(t == 0)
        def _():
            o32_ref[...] = upd

        @pl.when(t > 0)
        def _():
            o32_ref[...] += upd

        @pl.when(t == n_t - 1)
        def _():
            o16_ref[...] = o32_ref[...].astype(BF16)

    if shard_lhs:
        in_specs = [pl.BlockSpec((tt, rows), lambda s, t: (t, s)), pl.BlockSpec((tt, cols), lambda s, t: (t, 0))]
    else:
        in_specs = [pl.BlockSpec((tt, rows), lambda s, t: (t, 0)), pl.BlockSpec((tt, cols), lambda s, t: (t, s))]
    args = [lhs, rhs]
    aliases = {}
    if bufs is not None:
        in_specs += [pl.BlockSpec(memory_space=pl.ANY)] * 2
        args += list(bufs)
        aliases = {2: 0, 3: 1}
    shape = (N_CHIPS, n_layers, rows, cols)
    return pl.pallas_call(
        body,
        name=name,
        grid=(N_CHIPS, n_t),
        in_specs=in_specs,
        out_specs=[pl.BlockSpec((None, None, rows, cols), lambda s, t: (s, layer, 0, 0))] * 2,
        out_shape=[jax.ShapeDtypeStruct(shape, F32), jax.ShapeDtypeStruct(shape, BF16)],
        input_output_aliases=aliases,
        compiler_params=_params(2),
    )(*args)


def _chip_peers(x, y):
    return [(1 - x, y), (x, 1 - y), (1 - x, 1 - y)]


def _remote(src, dst, sems, s, peer):
    return pltpu.make_async_remote_copy(
        src_ref=src, dst_ref=dst, send_sem=sems[0].at[s], recv_sem=sems[1].at[s], device_id=peer, device_id_type=MESH
    )


class _Exchange:
    def __init__(self, operands, n_alias, new_shapes, n_sems, build):
        self.operands, self.n_alias, self.new_shapes, self.n_sems, self.build = operands, n_alias, new_shapes, n_sems, build

    def out_shapes(self):
        return [jax.ShapeDtypeStruct(a.shape, a.dtype) for a in self.operands[: self.n_alias]] + list(self.new_shapes)

    def scratch(self):
        return [pltpu.SemaphoreType.DMA((n,)) for n in self.n_sems]

    def run(self, ins, outs, sems, first, last):
        starts, recvs, sends, locals_ = self.build(ins, outs, sems)

        def start_all():
            for cp in starts:
                cp.start()

        def wait_all():
            for cp in recvs:
                cp.wait_recv()
            for cp in sends:
                cp.wait_send()
            for cp in locals_:
                cp.wait()

        if first is True:
            start_all()
            return wait_all
        pl.when(first)(start_all)
        return lambda: pl.when(last)(wait_all)


def _gather_exchange(shards, bufs, plan):
    n_arr = len(shards)
    n_cp = len(plan) * (N_CHIPS - 1)

    def build(ins, outs, sems):
        shard_refs = ins[-n_arr:]
        x, y, c = lax.axis_index("x"), lax.axis_index("y"), lax.axis_index("c")
        me = 2 * x + y
        recvs, sends, locals_ = [], [], []
        for p, (a, l0, n) in enumerate(plan):
            src = shard_refs[a].at[pl.ds(l0, n)]
            cp = pltpu.make_async_copy(src, outs[a].at[me, pl.ds(l0, n)], sems[2].at[p])
            locals_.append(cp)
            for k, (px, py) in enumerate(_chip_peers(x, y)):
                s = p * (N_CHIPS - 1) + k
                sends.append(_remote(src, outs[a].at[me, pl.ds(l0, n)], sems, s, (px, py, c)))
                recvs.append(_remote(src, outs[a].at[2 * px + py, pl.ds(l0, n)], sems, s, (px, py, c)))
        return locals_ + sends, recvs, sends, locals_

    if bufs is None:
        new = [jax.ShapeDtypeStruct((N_CHIPS,) + s.shape, s.dtype) for s in shards]
        return _Exchange(list(shards), 0, new, [n_cp, n_cp, len(plan)], build)
    return _Exchange(list(bufs) + list(shards), n_arr, [], [n_cp, n_cp, len(plan)], build)


def _scatter_exchange(g16, recv, plan, small=None):
    n_arr = len(g16)
    have = [r for r in recv if r is not None]
    made = [a for a in range(n_arr) if recv[a] is None]
    n_cp = len(plan) * (N_CHIPS - 1) + (N_DEV - 1 if small is not None else 0)

    def build(ins, outs, sems):
        g_refs = ins[:n_arr]
        recv_refs, it_have, it_made = [], iter(outs[n_arr : n_arr + len(have)]), iter(outs[n_arr + len(have) :])
        for a in range(n_arr):
            recv_refs.append(next(it_made) if recv[a] is None else next(it_have))
        x, y, c = lax.axis_index("x"), lax.axis_index("y"), lax.axis_index("c")
        me = 2 * x + y
        recvs, sends, locals_ = [], [], []
        for p, (a, l0, n) in enumerate(plan):
            for k, (px, py) in enumerate(_chip_peers(x, y)):
                s = p * (N_CHIPS - 1) + k
                dst = recv_refs[a].at[k, pl.ds(l0, n)]
                sends.append(_remote(g_refs[a].at[2 * px + py, pl.ds(l0, n)], dst, sems, s, (px, py, c)))
                recvs.append(_remote(g_refs[a].at[me, pl.ds(l0, n)], dst, sems, s, (px, py, c)))
        if small is not None:
            small_ref, all_ref = ins[-1], outs[-1]
            slot = 4 * x + 2 * y + c
            locals_.append(pltpu.make_async_copy(small_ref, all_ref.at[slot], sems[2].at[0]))
            flips = [(fx, fy, fc) for fx in (0, 1) for fy in (0, 1) for fc in (0, 1)][1:]
            for k, (fx, fy, fc) in enumerate(flips):
                s = len(plan) * (N_CHIPS - 1) + k
                px, py, pc = x ^ fx, y ^ fy, c ^ fc
                sends.append(_remote(small_ref, all_ref.at[slot], sems, s, (px, py, pc)))
                recvs.append(_remote(small_ref, all_ref.at[4 * px + 2 * py + pc], sems, s, (px, py, pc)))
        return locals_ + sends, recvs, sends, locals_

    operands = list(g16) + have + ([small] if small is not None else [])
    new = [jax.ShapeDtypeStruct((N_CHIPS - 1,) + g16[a].shape[1:], BF16) for a in made]
    if small is not None:
        new.append(jax.ShapeDtypeStruct((N_DEV,) + small.shape, F32))
    return _Exchange(operands, n_arr + len(have), new, [n_cp, n_cp, 1], build)


def _call_with_exchange(body, exch, name, grid, in_specs, out_specs, out_shape, scratch_shapes, args):
    n_in, n_out, n_scr = len(in_specs), len(out_shape), len(scratch_shapes)
    if exch is None:
        outs = pl.pallas_call(
            body, name=name, grid=grid, in_specs=in_specs, out_specs=out_specs, out_shape=out_shape,
            scratch_shapes=scratch_shapes, compiler_params=_params(len(grid)),
        )(*args)
        return outs, []
    e_shapes = exch.out_shapes()
    e_in, e_out = len(exch.operands), len(e_shapes)

    def wrapped(*refs):
        ins, refs = refs[:n_in], refs[n_in:]
        e_ins, refs = refs[:e_in], refs[e_in:]
        outs, refs = refs[:n_out], refs[n_out:]
        e_outs, refs = refs[:e_out], refs[e_out:]
        scr, sems = refs[:n_scr], refs[n_scr:]
        first, last = True, True
        for d, g in enumerate(grid):
            first = (pl.program_id(d) == 0) & first
            last = (pl.program_id(d) == g - 1) & last
        finish = exch.run(e_ins, e_outs, sems, first, last)
        body(*ins, *outs, *scr)
        finish()

    any_spec = pl.BlockSpec(memory_space=pl.ANY)
    outs = pl.pallas_call(
        wrapped,
        name=name,
        grid=grid,
        in_specs=list(in_specs) + [any_spec] * e_in,
        out_specs=list(out_specs) + [any_spec] * e_out,
        out_shape=list(out_shape) + e_shapes,
        input_output_aliases={n_in + i: n_out + i for i in range(exch.n_alias)},
        scratch_shapes=list(scratch_shapes) + exch.scratch(),
        compiler_params=pltpu.CompilerParams(
            dimension_semantics=("arbitrary",) * len(grid), vmem_limit_bytes=VMEM_LIMIT, has_side_effects=True
        ),
    )(*args, *exch.operands)
    return outs[:n_out], outs[n_out:]


def _exchange_only(exch, name):
    n_in = len(exch.operands)
    shapes = exch.out_shapes()

    def body(*refs):
        ins, outs, sems = refs[:n_in], refs[n_in : n_in + len(shapes)], refs[n_in + len(shapes) :]
        exch.run(ins, outs, sems, True, True)()

    any_spec = pl.BlockSpec(memory_space=pl.ANY)
    return pl.pallas_call(
        body,
        name=name,
        in_specs=[any_spec] * n_in,
        out_specs=[any_spec] * len(shapes),
        out_shape=shapes,
        input_output_aliases={i: i for i in range(exch.n_alias)},
        scratch_shapes=exch.scratch(),
        compiler_params=pltpu.CompilerParams(has_side_effects=True),
    )(*exch.operands)


ATTN_LANE_TILES = 2
ATTN_FWD_UNROLL = 34
ATTN_BWD_UNROLL = 17


MASKED = -1e30


def _hi_lo(x):
    hi = x.astype(BF16)
    lo = (x - hi.astype(F32)).astype(BF16)
    return jnp.concatenate([hi, lo], axis=1)


def _suffix_matrix(inclusive):
    j = lax.broadcasted_iota(jnp.int32, (2 * Q_TILE, 2 * Q_TILE), 0) & (Q_TILE - 1)
    s = lax.broadcasted_iota(jnp.int32, (2 * Q_TILE, 2 * Q_TILE), 1)
    later = (j >= s) if inclusive else (j > s)
    return jnp.where((s >= Q_TILE) | later, 1.0, 0.0).astype(BF16)


def _log_beta(z):
    return jnp.minimum(z, 0.0) - jnp.log(1.0 + jnp.exp(-jnp.abs(z)))


def _head_masks(width):
    lane = lax.broadcasted_iota(jnp.int32, (1, width), 1)
    return [(lane >= h * HEAD_DIM) & (lane < (h + 1) * HEAD_DIM) for h in range(width // HEAD_DIM)]


def _per_head_rows(x, masks):
    return jnp.concatenate([jnp.where(hm, x, 0) for hm in masks], axis=0)


def _heads_to_lanes(x, n_heads):
    return jnp.concatenate([x[h * Q_TILE : (h + 1) * Q_TILE] for h in range(n_heads)], axis=1)


def _block_start(kb):
    return kb * Q_TILE if isinstance(kb, int) else pl.multiple_of(kb * Q_TILE, Q_TILE)


def _clamp(i, n):
    return jnp.minimum(i, n - 1)


def _next_block(pos):
    qi, kb = pos
    row_done = kb == 0
    nqi = jnp.where(row_done, qi + 1, qi)
    return nqi, jnp.where(row_done, nqi, kb - 1)


def _stream_unroll(n_blocks, wanted):
    return next(u for u in (wanted, 2, 1) if n_blocks % u == 0)


def _past_mask(rows):
    t = lax.broadcasted_iota(jnp.int32, (rows, Q_TILE), 0) & (Q_TILE - 1)
    s = lax.broadcasted_iota(jnp.int32, (rows, Q_TILE), 1)
    return s < t


def _attn_fwd(qkv, n_seq, S, D, exch, name):
    T = n_seq * S
    width = min(D, ATTN_LANE_TILES * LANES)
    n_heads = width // HEAD_DIM
    rows = n_heads * Q_TILE
    nq = S // Q_TILE
    groups = D // width
    n_blocks = nq * (nq + 1) // 2
    unroll = _stream_unroll(n_blocks, ATTN_FWD_UNROLL)
    scale = HEAD_DIM ** -0.5
    n_trips = n_blocks // unroll

    def body(q_ref, k_ref, v_ref, o_ref, a_out, b_out, qh_scr, vh_scr, bias_scr, a_stage, b_stage, sems):
        masks = _head_masks(width)
        sfx = _suffix_matrix(False)
        stream = pl.program_id(0) * groups + pl.program_id(1)

        def per_head_tables(i, c):
            blk = pl.ds(_block_start(i), Q_TILE)
            qh_scr[i] = _per_head_rows(q_ref[blk, :] * scale, masks)
            vh_scr[i] = _per_head_rows(v_ref[blk, :], masks)
            return c

        lax.fori_loop(0, nq, per_head_tables, 0)
        bias_scr[0] = jnp.zeros((rows, Q_TILE), F32)
        bias_scr[1] = jnp.where(_past_mask(rows), 0.0, MASKED)

        def save(n, slot):
            blocks = pl.ds(n * unroll, unroll)
            return [
                pltpu.make_async_copy(a_stage.at[slot], a_out.at[stream, blocks], sems.at[slot]),
                pltpu.make_async_copy(b_stage.at[slot], b_out.at[stream, blocks], sems.at[2 + slot]),
            ]

        def scores(pos):
            qi, kb = pos
            kt = k_ref[pl.ds(_block_start(_clamp(kb, nq)), Q_TILE), :]
            z = lax.dot_general(qh_scr[_clamp(qi, nq)], kt, NT, preferred_element_type=F32)
            z = z + bias_scr[(kb == qi).astype(jnp.int32)]
            lb = _log_beta(z)
            return lb, _hi_lo(lb - z)

        def weigh(pos, st, carry, acc, slot, u):
            qi, kb = pos
            lb, l1 = st
            r = jnp.dot(l1, sfx, preferred_element_type=F32)
            carry = jnp.where(kb == qi, 0.0, carry)
            a = jnp.exp(lb + r[:, :Q_TILE] + carry).astype(BF16)
            a_stage[slot, u] = a
            b_stage[slot, u] = jnp.exp(lb).astype(BF16)
            acc = jnp.where(kb == qi, 0.0, acc) + jnp.dot(
                _heads_to_lanes(a, n_heads), vh_scr[_clamp(kb, nq)], preferred_element_type=F32
            )
            o_ref[pl.ds(_block_start(_clamp(qi, nq)), Q_TILE), :] = acc
            return carry + r[:, Q_TILE:], acc

        def trip(n, c):
            pos, st, carry, acc = c
            slot = n % 2

            @pl.when((n >= 2) | (stream > 0))
            def _():
                for cp in save(0, slot):
                    cp.wait()

            for u in range(unroll):
                nxt = _next_block(pos)
                st_nxt = scores(nxt)
                carry, acc = weigh(pos, st, carry, acc, slot, u)
                pos, st = nxt, st_nxt
            for cp in save(n, slot):
                cp.start()
            return pos, st, carry, acc

        first = (jnp.int32(0), jnp.int32(0))
        zero = bias_scr[0]
        init = (first, scores(first), zero, jnp.concatenate([zero[:Q_TILE]] * (width // Q_TILE), axis=1))
        lax.fori_loop(0, n_trips, trip, init)

        @pl.when(stream == n_seq * groups - 1)
        def _():
            for slot in range(min(2, n_trips)):
                for cp in save(0, slot):
                    cp.wait()

    seq = lambda col0: pl.BlockSpec((S, width), lambda b, p: (b, col0 + p))
    saved = jax.ShapeDtypeStruct((n_seq * groups, n_blocks, rows, Q_TILE), BF16)
    stage = pltpu.VMEM((2, unroll, rows, Q_TILE), BF16)
    (o, a_w, beta), moved = _call_with_exchange(
        body,
        exch,
        name,
        grid=(n_seq, groups),
        in_specs=[seq(0), seq(groups), seq(2 * groups)],
        out_specs=[seq(0), pl.BlockSpec(memory_space=pl.ANY), pl.BlockSpec(memory_space=pl.ANY)],
        out_shape=[jax.ShapeDtypeStruct((T, D), F32), saved, saved],
        scratch_shapes=[
            pltpu.VMEM((nq, rows, width), BF16),
            pltpu.VMEM((nq, rows, width), BF16),
            pltpu.VMEM((2, rows, Q_TILE), F32),
            stage,
            stage,
            pltpu.SemaphoreType.DMA((4,)),
        ],
        args=(qkv, qkv, qkv),
    )
    return (o, a_w, beta), moved


def _attn_bwd(qkv, fwd, do, n_seq, S, D, exch, name):
    o, a_w, beta = fwd
    T = n_seq * S
    width = min(D, ATTN_LANE_TILES * LANES)
    n_heads = width // HEAD_DIM
    rows = n_heads * Q_TILE
    nq = S // Q_TILE
    groups = D // width
    n_blocks = nq * (nq + 1) // 2
    unroll = _stream_unroll(n_blocks, ATTN_BWD_UNROLL)
    scale = HEAD_DIM ** -0.5
    n_trips = n_blocks // unroll

    def body(q_ref, k_ref, v_ref, o_ref, do_ref, a_in, b_in, dq_ref, dk_ref, dv_ref, dk_acc, dv_acc, qh_scr, doh_scr, delta_scr, a_stage, b_stage, sems):
        masks = _head_masks(width)
        sfx_incl = _suffix_matrix(True)
        stream = pl.program_id(0) * groups + pl.program_id(1)

        n_streams = n_seq * groups
        ahead = n_trips % 2 == 0

        def fetch(s, n, slot):
            blocks = pl.ds(n * unroll, unroll)
            return [
                pltpu.make_async_copy(a_in.at[s, blocks], a_stage.at[slot], sems.at[slot]),
                pltpu.make_async_copy(b_in.at[s, blocks], b_stage.at[slot], sems.at[2 + slot]),
            ]

        @pl.when((stream == 0) | (not ahead))
        def _():
            for cp in fetch(stream, 0, 0):
                cp.start()

        dk_acc[...] = jnp.zeros_like(dk_acc)
        dv_acc[...] = jnp.zeros_like(dv_acc)

        lane = lax.broadcasted_iota(jnp.int32, (2 * width, rows), 0) % width
        col = lax.broadcasted_iota(jnp.int32, (2 * width, rows), 1)
        head_sums = jnp.where(lane // HEAD_DIM == col // Q_TILE, 1.0, 0.0).astype(BF16)

        def per_head_tables(i, c):
            blk = pl.ds(_block_start(i), Q_TILE)
            do = do_ref[blk, :]
            qh_scr[i] = _per_head_rows(q_ref[blk, :] * scale, masks)
            doh_scr[i] = _per_head_rows(do, masks)
            prod = do.astype(F32) * o_ref[blk, :]
            d = jnp.dot(_hi_lo(prod), head_sums, preferred_element_type=F32)
            delta_scr[i] = jnp.concatenate([d[:, h * Q_TILE : (h + 1) * Q_TILE] for h in range(n_heads)], axis=0)
            return c

        lax.fori_loop(0, nq, per_head_tables, 0)

        def weigh(pos, ab, beta, c2, dq):
            qi, kb = pos
            first = kb == qi
            blk = pl.ds(_block_start(kb), Q_TILE)
            g = ab.astype(F32) * lax.dot_general(doh_scr[qi], v_ref[blk, :], NT, preferred_element_type=F32)
            r2 = jnp.dot(_hi_lo(g), sfx_incl, preferred_element_type=F32)
            c2 = jnp.where(first, 0.0, c2)
            earlier = delta_scr[qi] - (r2[:, :Q_TILE] + c2)
            beta = beta.astype(F32)
            dzb = (g * (1.0 - beta) - earlier * beta).astype(BF16)
            kh = _per_head_rows(k_ref[blk, :], masks)
            dq = jnp.where(first, 0.0, dq) + jnp.dot(_heads_to_lanes(dzb, n_heads), kh, preferred_element_type=F32)
            dq_ref[pl.ds(_block_start(qi), Q_TILE), :] = (dq * scale).astype(BF16)
            dk_acc[blk, :] += lax.dot_general(dzb, qh_scr[qi], TN, preferred_element_type=F32)
            dv_acc[blk, :] += lax.dot_general(ab, doh_scr[qi], TN, preferred_element_type=F32)
            return c2 + r2[:, Q_TILE:], dq

        def trip(n, c):
            pos, c2, dq = c
            slot = n % 2
            for cp in fetch(stream, n, slot):
                cp.wait()
            more = n + 1 < n_trips
            if ahead:
                nxt = (jnp.where(more, stream, stream + 1), jnp.where(more, n + 1, 0))
                more = more | (stream + 1 < n_streams)
            else:
                nxt = (stream, n + 1)

            @pl.when(more)
            def _():
                for cp in fetch(*nxt, 1 - slot):
                    cp.start()

            for u in range(unroll):
                c2, dq = weigh(pos, a_stage[slot, u], b_stage[slot, u], c2, dq)
                pos = _next_block(pos)
            return pos, c2, dq

        zero = dk_acc[pl.ds(0, Q_TILE), :]
        init = ((jnp.int32(0), jnp.int32(0)), jnp.concatenate([zero[:, :Q_TILE]] * n_heads, axis=0), zero)
        lax.fori_loop(0, n_trips, trip, init)
        dk_ref[...] = dk_acc[...].astype(BF16)
        dv_ref[...] = dv_acc[...].astype(BF16)

    seq = lambda col0: pl.BlockSpec((S, width), lambda b, p: (b, col0 + p))
    return _call_with_exchange(
        body,
        exch,
        name,
        grid=(n_seq, groups),
        in_specs=[seq(0), seq(groups), seq(2 * groups), seq(0), seq(0)] + [pl.BlockSpec(memory_space=pl.ANY)] * 2,
        out_specs=[seq(0)] * 3,
        out_shape=[jax.ShapeDtypeStruct((T, D), BF16)] * 3,
        scratch_shapes=[
            pltpu.VMEM((S, width), F32),
            pltpu.VMEM((S, width), F32),
            pltpu.VMEM((nq, rows, width), BF16),
            pltpu.VMEM((nq, rows, width), BF16),
            pltpu.VMEM((nq, rows, Q_TILE), F32),
            pltpu.VMEM((2, unroll, rows, Q_TILE), BF16),
            pltpu.VMEM((2, unroll, rows, Q_TILE), BF16),
            pltpu.SemaphoreType.DMA((4,)),
        ],
        args=(qkv, qkv, qkv, o, do, a_w, beta),
    )


def _causal_ws(ws_ref, g):
    t = lax.broadcasted_iota(jnp.int32, (SGU_CHUNK, SGU_CHUNK), 0)
    s = lax.broadcasted_iota(jnp.int32, (SGU_CHUNK, SGU_CHUNK), 1)
    return jnp.where(s <= t, ws_ref[g], 0.0)


def _sgu_fwd(a, gain, ws, bsb, name):
    T, F2 = a.shape
    F = F2 // 2
    gw = F // SGU_GROUPS

    def body(a_ref, gain_ref, ws_ref, bsb_ref, y_ref):
        v = _gelu(a_ref[:, F:].astype(F32))
        vn = (v * _rstd(v) * gain_ref[...]).astype(BF16)
        for g in range(SGU_GROUPS):
            cs = slice(g * gw, (g + 1) * gw)
            w = _causal_ws(ws_ref, g).astype(BF16)
            mixed = jnp.dot(w, vn[:, cs], preferred_element_type=F32) + bsb_ref[g]
            y_ref[:, cs] = (_gelu(a_ref[:, cs].astype(F32)) * mixed).astype(BF16)

    return pl.pallas_call(
        body,
        name=name,
        grid=(T // SGU_CHUNK,),
        in_specs=[
            pl.BlockSpec((SGU_CHUNK, F2), lambda i: (i, 0)),
            pl.BlockSpec((1, F), lambda i: (0, 0)),
            pl.BlockSpec((SGU_GROUPS, SGU_CHUNK, SGU_CHUNK), lambda i: (0, 0, 0)),
            pl.BlockSpec((SGU_GROUPS, SGU_CHUNK, gw), lambda i: (0, 0, 0)),
        ],
        out_specs=pl.BlockSpec((SGU_CHUNK, F), lambda i: (i, 0)),
        out_shape=jax.ShapeDtypeStruct((T, F), BF16),
        compiler_params=_params(1),
    )(a, gain.reshape(1, F), ws, bsb)


def _sgu_bwd(a, dy, gain, ws, bsb, name):
    T, F2 = a.shape
    F = F2 // 2
    gw = F // SGU_GROUPS

    def body(a_ref, dy_ref, gain_ref, ws_ref, bsb_ref, da_ref, dws_ref, dbs_ref, dgain_ref, dvn_ref):
        @pl.when(pl.program_id(0) == 0)
        def _():
            dws_ref[...] = jnp.zeros_like(dws_ref)
            dbs_ref[...] = jnp.zeros_like(dbs_ref)
            dgain_ref[...] = jnp.zeros_like(dgain_ref)

        v, v_slope = _gelu_and_grad(a_ref[:, F:].astype(F32))
        rstd = _rstd(v)
        vh = v * rstd
        gain = gain_ref[...]
        vn = (vh * gain).astype(BF16)
        ones = jnp.ones((gw, SGU_CHUNK), BF16)
        for g in range(SGU_GROUPS):
            cs = slice(g * gw, (g + 1) * gw)
            w = _causal_ws(ws_ref, g).astype(BF16)
            mixed = jnp.dot(w, vn[:, cs], preferred_element_type=F32) + bsb_ref[g]
            u, u_slope = _gelu_and_grad(a_ref[:, cs].astype(F32))
            dyc = dy_ref[:, cs].astype(F32)
            da_ref[:, cs] = (dyc * mixed * u_slope).astype(BF16)
            dm = (dyc * u).astype(BF16)
            dbs_ref[g] += jnp.dot(dm, ones, preferred_element_type=F32)
            dws_ref[g] += _causal_mask_f32(lax.dot_general(dm, vn[:, cs], NT, preferred_element_type=F32))
            dvn_ref[:, cs] = lax.dot_general(w, dm, TN, preferred_element_type=F32)
        dvn = dvn_ref[...]
        dgain_ref[...] += jnp.sum(dvn * vh, axis=0, keepdims=True)
        dvh = dvn * gain
        dv = rstd * (dvh - vh * jnp.mean(dvh * vh, axis=-1, keepdims=True))
        da_ref[:, F:] = (dv * v_slope).astype(BF16)

    acc_spec = pl.BlockSpec((SGU_GROUPS, SGU_CHUNK, SGU_CHUNK), lambda i: (0, 0, 0))
    acc_shape = jax.ShapeDtypeStruct((SGU_GROUPS, SGU_CHUNK, SGU_CHUNK), F32)
    return pl.pallas_call(
        body,
        name=name,
        grid=(T // SGU_CHUNK,),
        in_specs=[
            pl.BlockSpec((SGU_CHUNK, F2), lambda i: (i, 0)),
            pl.BlockSpec((SGU_CHUNK, F), lambda i: (i, 0)),
            pl.BlockSpec((1, F), lambda i: (0, 0)),
            acc_spec,
            pl.BlockSpec((SGU_GROUPS, SGU_CHUNK, gw), lambda i: (0, 0, 0)),
        ],
        out_specs=[
            pl.BlockSpec((SGU_CHUNK, F2), lambda i: (i, 0)),
            acc_spec,
            acc_spec,
            pl.BlockSpec((1, F), lambda i: (0, 0)),
        ],
        out_shape=[
            jax.ShapeDtypeStruct((T, F2), BF16),
            acc_shape,
            acc_shape,
            jax.ShapeDtypeStruct((1, F), F32),
        ],
        scratch_shapes=[pltpu.VMEM((SGU_CHUNK, F), F32)],
        compiler_params=_params(1),
    )(a, dy, gain.reshape(1, F), ws, bsb)


def _causal_mask_f32(m):
    t = lax.broadcasted_iota(jnp.int32, m.shape, 0)
    s = lax.broadcasted_iota(jnp.int32, m.shape, 1)
    return jnp.where(s <= t, m, 0.0)


def _final_loss(x, gain, target, name):
    T, D = x.shape
    tm = min(T, ROW_TILE // 2)

    def body(x_ref, g_ref, t_ref, sq_ref, dx_ref, dxb_ref, dg_ref):
        xv = x_ref[...]
        gain = g_ref[...]
        err = xv * _rstd(xv) * gain - t_ref[...]
        dx, dg = _norm_bwd(err * (1.0 / D), xv, gain)
        dx_ref[...] = dx
        dxb_ref[...] = dx.astype(BF16)
        sq = jnp.sum(err * err, axis=0, keepdims=True)

        @pl.when(pl.program_id(0) == 0)
        def _():
            sq_ref[...] = sq
            dg_ref[...] = dg

        @pl.when(pl.program_id(0) > 0)
        def _():
            sq_ref[...] += sq
            dg_ref[...] += dg

    row = pl.BlockSpec((tm, D), lambda i: (i, 0))
    vec = pl.BlockSpec((1, D), lambda i: (0, 0))
    return pl.pallas_call(
        body,
        name=name,
        grid=(T // tm,),
        in_specs=[row, vec, row],
        out_specs=[vec, row, row, vec],
        out_shape=[
            jax.ShapeDtypeStruct((1, D), F32),
            jax.ShapeDtypeStruct((T, D), F32),
            jax.ShapeDtypeStruct((T, D), BF16),
            jax.ShapeDtypeStruct((1, D), F32),
        ],
        compiler_params=_params(1),
    )(x, gain.reshape(1, D), target)


def _row_tile(rows, cols, n_arrays):
    budget = VMEM_LIMIT // 2 // (2 * n_arrays * cols * 4)
    tr = rows
    while tr > budget and tr % 16 == 0:
        tr //= 2
    return tr


def _sum_received(own, recv, name):
    R, C = own.shape
    n = recv.shape[0]
    tr = _row_tile(R, C, n + 2)

    def body(own_ref, recv_ref, o_ref):
        s = own_ref[...]
        for k in range(n):
            s = s + recv_ref[k].astype(F32)
        o_ref[...] = s

    return pl.pallas_call(
        body,
        name=name,
        grid=(R // tr,),
        in_specs=[pl.BlockSpec((tr, C), lambda i: (i, 0)), pl.BlockSpec((n, tr, C), lambda i: (0, i, 0))],
        out_specs=pl.BlockSpec((tr, C), lambda i: (i, 0)),
        out_shape=jax.ShapeDtypeStruct((R, C), F32),
        compiler_params=_params(1),
    )(own, recv)


def _sum_chip_shard(g32, recv, chip, name):
    _, L, r, c = g32.shape
    n = recv.shape[0]
    tr = _row_tile(r, c, n + 2)

    def body(chip_ref, own_ref, recv_ref, o_ref):
        s = own_ref[...]
        for k in range(n):
            s = s + recv_ref[k].astype(F32)
        o_ref[...] = s

    return pl.pallas_call(
        body,
        name=name,
        grid_spec=pltpu.PrefetchScalarGridSpec(
            num_scalar_prefetch=1,
            grid=(L, r // tr),
            in_specs=[
                pl.BlockSpec((None, None, tr, c), lambda l, i, chip_ref: (chip_ref[0], l, i, 0)),
                pl.BlockSpec((n, None, tr, c), lambda l, i, chip_ref: (0, l, i, 0)),
            ],
            out_specs=pl.BlockSpec((None, tr, c), lambda l, i, chip_ref: (l, i, 0)),
        ),
        out_shape=jax.ShapeDtypeStruct((L, r, c), F32),
        compiler_params=_params(2),
    )(chip.reshape(1).astype(jnp.int32), g32, recv)


def _adamw(w, m, v, parts, name):
    R, C = w.shape
    n = len(parts)
    tr = _row_tile(R, C, n + 7)

    def body(*refs):
        w_ref, m_ref, v_ref = refs[:3]
        g_ref, d_ref, nm_ref, nv_ref = refs[3 + n :]
        g = refs[3][...]
        for p_ref in refs[4 : 3 + n]:
            g = g + p_ref[...]
        nm = ADAM_B1 * m_ref[...] + (1.0 - ADAM_B1) * g
        nv = ADAM_B2 * v_ref[...] + (1.0 - ADAM_B2) * (g * g)
        m_hat = nm / (1.0 - ADAM_B1**ADAM_STEP)
        v_hat = nv / (1.0 - ADAM_B2**ADAM_STEP)
        g_ref[...] = g
        d_ref[...] = -ADAM_LR * (m_hat / (jnp.sqrt(v_hat) + ADAM_EPS) + ADAM_WD * w_ref[...])
        nm_ref[...] = nm
        nv_ref[...] = nv

    spec = pl.BlockSpec((tr, C), lambda i: (i, 0))
    return pl.pallas_call(
        body,
        name=name,
        grid=(R // tr,),
        in_specs=[spec] * (3 + n),
        out_specs=[spec] * 4,
        out_shape=[jax.ShapeDtypeStruct((R, C), F32)] * 4,
        compiler_params=_params(1),
    )(w, m, v, *parts)


def _swap_with_sibling(parts, name):
    n = len(parts)

    def body(*refs):
        ins, outs = refs[:n], refs[n : 2 * n]
        send_sems, recv_sems = refs[2 * n :]
        sibling = (lax.axis_index("x"), lax.axis_index("y"), 1 - lax.axis_index("c"))
        copies = [
            pltpu.make_async_remote_copy(
                src_ref=ins[a],
                dst_ref=outs[a],
                send_sem=send_sems.at[a],
                recv_sem=recv_sems.at[a],
                device_id=sibling,
                device_id_type=MESH,
            )
            for a in range(n)
        ]
        for cp in copies:
            cp.start()
        for cp in copies:
            cp.wait_recv()
        for cp in copies:
            cp.wait_send()

    any_spec = pl.BlockSpec(memory_space=pl.ANY)
    return pl.pallas_call(
        body,
        name=name,
        in_specs=[any_spec] * n,
        out_specs=[any_spec] * n,
        out_shape=[jax.ShapeDtypeStruct(p.shape, p.dtype) for p in parts],
        scratch_shapes=[pltpu.SemaphoreType.DMA((n,)), pltpu.SemaphoreType.DMA((n,))],
        compiler_params=pltpu.CompilerParams(has_side_effects=True),
    )(*parts)


def _pack(pieces):
    flat = jnp.concatenate([p.reshape(-1) for p in pieces])
    return flat.reshape(-1, LANES)


def _unpack(packed, shapes):
    flat = packed.reshape(-1)
    out, off = [], 0
    for s in shapes:
        size = 1
        for d in s:
            size *= d
        out.append(flat[off : off + size].reshape(s))
        off += size
    return out


def kernel(x, norm_mix, norm_mlp, sb_wqkv, sb_wo, sgu_win, sgu_gain, sgu_ws, sgu_bs, sgu_wout, mlp_w1, mlp_w2, final_norm, loss_target, m_norm_mix, m_norm_mlp, m_sb_wqkv, m_sb_wo, m_sgu_win, m_sgu_gain, m_sgu_ws, m_sgu_bs, m_sgu_wout, m_mlp_w1, m_mlp_w2, m_final_norm, v_norm_mix, v_norm_mlp, v_sb_wqkv, v_sb_wo, v_sgu_win, v_sgu_gain, v_sgu_ws, v_sgu_bs, v_sgu_wout, v_mlp_w1, v_mlp_w2, v_final_norm):
    n_seq, S, D = x.shape
    T = n_seq * S
    depth = norm_mix.shape[0]
    n_sgu = sgu_win.shape[0]
    F = sgu_wout.shape[1] * N_CHIPS
    gw = F // SGU_GROUPS
    chip = 2 * lax.axis_index("x") + lax.axis_index("y")

    QKV, WO, WIN, WOUT, W1, W2, GAIN = range(7)
    big = [sb_wqkv, sb_wo, sgu_win, sgu_wout, mlp_w1, mlp_w2]
    n_sb = sb_wqkv.shape[0]
    shards = [w.astype(BF16) for w in big] + [sgu_gain.reshape(1, -1, LANES)]

    def gather_plan(i):
        j, mlp = i // 2, min(2, depth - i)
        plan = [(WO, j, 1), (W1, i, mlp), (W2, i, mlp)]
        if i + 1 < depth:
            plan += [(WIN, (i + 1) // 2, 1), (WOUT, (i + 1) // 2, 1)]
        if j + 1 < n_sb:
            plan += [(QKV, j + 1, 1)]
        return plan

    wg = _exchange_only(_gather_exchange(shards, None, [(QKV, 0, 1), (GAIN, 0, 1)]), "gather_first_weights")
    gain_full = jnp.transpose(wg[GAIN].reshape(N_CHIPS, n_sgu, F // N_CHIPS), (1, 0, 2)).reshape(n_sgu, F)
    bsb = [jnp.broadcast_to(sgu_bs[j][:, :, None], (SGU_GROUPS, SGU_CHUNK, gw)) for j in range(n_sgu)]

    xs = x.reshape(T, D)
    saved = []
    for i in range(depth):
        j = i // 2
        if i % 2 == 0:
            qkv, h = _norm_matmul(xs, norm_mix[i], wg[QKV], j, f"qkv_fwd_{i}")
            attn, wg = _attn_fwd(qkv, n_seq, S, D, _gather_exchange(shards, wg, gather_plan(i)), f"attn_fwd_{i}")
            wg_qkv, wg_wo, wg_win, wg_wout, wg_w1, wg_w2 = wg[:6]
            x_mid = _act_matmul_res(attn[0], wg_wo, j, xs, None, f"wo_fwd_{i}")
            mix = (qkv, attn)
        else:
            a, h = _norm_matmul(xs, norm_mix[i], wg_win, j, f"win_fwd_{i}")
            yg = _sgu_fwd(a, gain_full[j], sgu_ws[j], bsb[j], f"sgu_fwd_{i}")
            x_mid = _act_matmul_res(yg, wg_wout, j, xs, None, f"wout_fwd_{i}")
            mix = (a, yg)
        a2, h2 = _norm_matmul(x_mid, norm_mlp[i], wg_w1, i, f"w1_fwd_{i}")
        x_out = _act_matmul_res(a2, wg_w2, i, x_mid, "relu2", f"w2_fwd_{i}")
        saved.append((xs, h, mix, x_mid, h2, a2))
        xs = x_out

    sq, dx, dxb, g_final = _final_loss(xs, final_norm, loss_target.reshape(T, D), "loss_head")
    loss = lax.psum(0.5 * jnp.sum(sq) / D, ("x", "y", "c"))

    n_layers = [n_sb, n_sb, n_sgu, n_sgu, depth, depth]
    g32, g16, recv = [None] * 6, [None] * 6, [None] * 6
    done_from, sent_from = list(n_layers), list(n_layers)

    def grad(a, layer, lhs, rhs, shard_lhs, act, name):
        bufs = None if g32[a] is None else (g32[a], g16[a])
        g32[a], g16[a] = _matmul_tn(lhs, rhs, bufs, layer, n_layers[a], shard_lhs, act, name)
        done_from[a] = layer

    def unsent_plan():
        plan = [(a, done_from[a], sent_from[a] - done_from[a]) for a in range(6) if sent_from[a] > done_from[a]]
        for a, l0, _ in plan:
            sent_from[a] = l0
        return plan

    def scatter(plan, small):
        if not plan and small is None:
            return None, lambda moved: None
        arrays = sorted({a for a, _, _ in plan})
        have = [a for a in arrays if recv[a] is not None]
        made = [a for a in arrays if recv[a] is None]
        exch = _scatter_exchange(
            [g16[a] for a in arrays], [recv[a] for a in arrays], [(arrays.index(a), l0, n) for a, l0, n in plan], small
        )

        def take(moved):
            for a, buf in zip(arrays, moved):
                g16[a] = buf
            for a, buf in zip(have + made, moved[len(arrays) :]):
                recv[a] = buf
            return moved[-1]

        return exch, take

    g_mix, g_mlp = [None] * depth, [None] * depth
    g_ws, g_bs, g_gain = [None] * n_sgu, [None] * n_sgu, [None] * n_sgu
    for i in reversed(range(depth)):
        j = i // 2
        x_in, h, mix, x_mid, h2, a2 = saved[i]
        da2 = _matmul_nt(dxb, wg_w2, i, a2, f"w2_bwd_{i}")
        grad(W2, i, a2, dxb, True, "relu2", f"w2_grad_{i}")
        grad(W1, i, h2, da2, False, None, f"w1_grad_{i}")
        (dx, dxb, g_mlp[i]), _ = _matmul_nt_norm_bwd(da2, wg_w1, i, x_mid, norm_mlp[i], dx, None, f"w1_bwd_{i}")
        if i % 2 == 0:
            qkv, attn = mix
            do = _matmul_nt(dxb, wg_wo, j, None, f"wo_bwd_{i}")
            grad(WO, j, attn[0], dxb, True, None, f"wo_grad_{i}")
            sgu_small = _pack([jnp.stack(g_ws), jnp.stack(g_bs), jnp.stack(g_gain)]) if i == 0 and n_sgu else None
            exch, take = scatter(unsent_plan(), sgu_small)
            (dq, dk, dv), moved = _attn_bwd(qkv, attn, do, n_seq, S, D, exch, f"attn_bwd_{i}")
            last = take(moved)
            if sgu_small is not None:
                sgu_small_all = last
            dqkv = jnp.concatenate([dq, dk, dv], axis=1)
            grad(QKV, j, h, dqkv, False, None, f"qkv_grad_{i}")
            exch, take = scatter(unsent_plan(), None)
            (dx, dxb, g_mix[i]), moved = _matmul_nt_norm_bwd(
                dqkv, wg_qkv, j, x_in, norm_mix[i], dx, exch, f"qkv_bwd_{i}"
            )
            take(moved)
        else:
            a, yg = mix
            dyg = _matmul_nt(dxb, wg_wout, j, None, f"wout_bwd_{i}")
            grad(WOUT, j, yg, dxb, True, None, f"wout_grad_{i}")
            da, g_ws[j], dbs, g_gain[j] = _sgu_bwd(a, dyg, gain_full[j], sgu_ws[j], bsb[j], f"sgu_bwd_{i}")
            g_bs[j] = dbs[:, :, 0]
            grad(WIN, j, h, da, False, None, f"win_grad_{i}")
            (dx, dxb, g_mix[i]), _ = _matmul_nt_norm_bwd(da, wg_win, j, x_in, norm_mix[i], dx, None, f"win_bwd_{i}")
    grad_x = dx.reshape(n_seq, S, D)

    names = ["qkv", "wo", "win", "wout", "w1", "w2"]
    exch, take = scatter(unsent_plan(), _pack([jnp.stack(g_mix), jnp.stack(g_mlp), g_final]))
    norm_small_all = take(_exchange_only(exch, "gather_norm_grads"))
    partial = [_sum_chip_shard(g32[a], recv[a], chip, f"sum_{names[a]}") for a in range(6)]
    partial = [p.reshape(-1, p.shape[-1]) for p in partial]
    theirs = _swap_with_sibling(partial, "swap_partial_sums")
    g_small = _unpack(
        _sum_received(norm_small_all[0], norm_small_all[1:], "sum_norm_grads"),
        [norm_mix.shape, norm_mlp.shape, final_norm.shape],
    ) + _unpack(
        _sum_received(sgu_small_all[0], sgu_small_all[1:], "sum_sgu_small_grads"),
        [sgu_ws.shape, sgu_bs.shape, (n_sgu, F)],
    )

    ms = [m_sb_wqkv, m_sb_wo, m_sgu_win, m_sgu_wout, m_mlp_w1, m_mlp_w2]
    vs = [v_sb_wqkv, v_sb_wo, v_sgu_win, v_sgu_wout, v_mlp_w1, v_mlp_w2]
    res = {}
    keys = ["sb_wqkv", "sb_wo", "sgu_win", "sgu_wout", "mlp_w1", "mlp_w2"]
    for key, k, w, m, v, mine, other in zip(keys, names, big, ms, vs, partial, theirs):
        cols = w.shape[-1]
        outs = _adamw(w.reshape(-1, cols), m.reshape(-1, cols), v.reshape(-1, cols), [mine, other], f"adamw_{k}")
        res[key] = [o.reshape(w.shape) for o in outs]

    g_small[5] = lax.dynamic_slice_in_dim(g_small[5], chip * (F // N_CHIPS), F // N_CHIPS, axis=1)
    small_keys = ["norm_mix", "norm_mlp", "final_norm", "sgu_ws", "sgu_bs", "sgu_gain"]
    small_w = [norm_mix, norm_mlp, final_norm, sgu_ws, sgu_bs, sgu_gain]
    small_m = [m_norm_mix, m_norm_mlp, m_final_norm, m_sgu_ws, m_sgu_bs, m_sgu_gain]
    small_v = [v_norm_mix, v_norm_mlp, v_final_norm, v_sgu_ws, v_sgu_bs, v_sgu_gain]
    outs = _adamw(_pack(small_w), _pack(small_m), _pack(small_v), [_pack(g_small)], "adamw_small")
    local_shapes = [w.shape for w in small_w]
    for key, parts in zip(small_keys, zip(*[_unpack(o, local_shapes) for o in outs])):
        res[key] = list(parts)

    order = ["norm_mix", "norm_mlp", "sb_wqkv", "sb_wo", "sgu_win", "sgu_gain", "sgu_ws", "sgu_bs", "sgu_wout", "mlp_w1", "mlp_w2", "final_norm"]
    return (loss, grad_x, *[res[k][0] for k in order], *[res[k][1] for k in order], *[res[k][2] for k in order], *[res[k][3] for k in order])
```

```python
import jax
import jax.numpy as jnp
from jax import lax
from jax.experimental import pallas as pl
from jax.experimental.pallas import tpu as pltpu

F32 = jnp.float32
BF16 = jnp.bfloat16
MESH = pl.DeviceIdType.MESH

EPS = 1e-6
HEAD_DIM = 64
LANES = 128
Q_TILE = 128
SGU_CHUNK = 128
SGU_GROUPS = 8
N_CHIPS = 4
N_DEV = 8
ADAM_LR = 0.001
ADAM_B1 = 0.9
ADAM_B2 = 0.999
ADAM_EPS = 1e-08
ADAM_WD = 0.01
ADAM_STEP = 10
GELU_C0 = 0.7978845608028654
GELU_C1 = 0.044715
VMEM_LIMIT = 48 * 1024 * 1024
ROW_TILE = 1024
NT = (((1,), (1,)), ((), ()))
TN = (((0,), (0,)), ((), ()))


def _params(n_axes):
    return pltpu.CompilerParams(dimension_semantics=("arbitrary",) * n_axes, vmem_limit_bytes=VMEM_LIMIT)


def _rstd(x):
    return lax.rsqrt(jnp.mean(x * x, axis=-1, keepdims=True) + EPS)


def _norm_bwd(dh, x, gain):
    rstd = _rstd(x)
    xh = x * rstd
    dhg = dh * gain
    dx = rstd * (dhg - xh * jnp.mean(dhg * xh, axis=-1, keepdims=True))
    return dx, jnp.sum(dh * xh, axis=0, keepdims=True)


def _gelu(x):
    return (0.5 * x) * (1.0 + jnp.tanh(x * (GELU_C0 + (GELU_C0 * GELU_C1) * (x * x))))


def _gelu_and_grad(x):
    x2 = x * x
    t = jnp.tanh(x * (GELU_C0 + (GELU_C0 * GELU_C1) * x2))
    half_x, p = 0.5 * x, 1.0 + t
    slope = GELU_C0 + (3.0 * GELU_C0 * GELU_C1) * x2
    return half_x * p, 0.5 * p + (half_x * slope) * (1.0 - t * t)


def _act(a, act):
    if act == "relu2":
        r = jnp.maximum(a.astype(F32), 0.0)
        return (r * r).astype(BF16)
    return a.astype(BF16)


def _layer_spec(wg, layer):
    nsh, _, r, c = wg.shape
    return pl.BlockSpec((nsh, None, r, c), lambda i: (0, layer, 0, 0), pipeline_mode=pl.Buffered(1))


def _token_tile(T, wide_row_bytes):
    return min(T, ROW_TILE if ROW_TILE * wide_row_bytes <= 4 * 1024 * 1024 else ROW_TILE // 2)


def _norm_matmul(x, gain, wg, layer, name):
    T, D = x.shape
    nsh, _, _, ns = wg.shape
    tm = min(T, ROW_TILE)

    def body(x_ref, g_ref, w_ref, y_ref, h_ref):
        xv = x_ref[...]
        h = (xv * _rstd(xv) * g_ref[...]).astype(BF16)
        h_ref[...] = h
        for j in range(nsh):
            y_ref[:, j * ns : (j + 1) * ns] = jnp.dot(h, w_ref[j], preferred_element_type=F32).astype(BF16)

    return pl.pallas_call(
        body,
        name=name,
        grid=(T // tm,),
        in_specs=[pl.BlockSpec((tm, D), lambda i: (i, 0)), pl.BlockSpec((1, D), lambda i: (0, 0)), _layer_spec(wg, layer)],
        out_specs=[pl.BlockSpec((tm, nsh * ns), lambda i: (i, 0)), pl.BlockSpec((tm, D), lambda i: (i, 0))],
        out_shape=[jax.ShapeDtypeStruct((T, nsh * ns), BF16), jax.ShapeDtypeStruct((T, D), BF16)],
        compiler_params=_params(1),
    )(x, gain.reshape(1, D), wg)


def _act_matmul_res(a, wg, layer, x_in, act, name):
    T, K = a.shape
    nsh, _, kq, D = wg.shape
    tm = _token_tile(T, K * a.dtype.itemsize)

    def body(a_ref, w_ref, x_ref, o_ref):
        w = w_ref[...].reshape(nsh * kq, D)
        o_ref[...] = x_ref[...] + jnp.dot(_act(a_ref[...], act), w, preferred_element_type=F32)

    return pl.pallas_call(
        body,
        name=name,
        grid=(T // tm,),
        in_specs=[pl.BlockSpec((tm, K), lambda i: (i, 0)), _layer_spec(wg, layer), pl.BlockSpec((tm, D), lambda i: (i, 0))],
        out_specs=pl.BlockSpec((tm, D), lambda i: (i, 0)),
        out_shape=jax.ShapeDtypeStruct((T, D), F32),
        compiler_params=_params(1),
    )(a, wg, x_in)


def _matmul_nt(g, wg, layer, a, name):
    T, D = g.shape
    nsh, _, kq, _ = wg.shape
    tm = _token_tile(T, nsh * kq * jnp.dtype(BF16).itemsize)

    def body(g_ref, w_ref, *rest):
        gv = g_ref[...]
        for k in range(nsh):
            cols = slice(k * kq, (k + 1) * kq)
            r = lax.dot_general(gv, w_ref[k], NT, preferred_element_type=F32)
            if a is not None:
                r = r * (2.0 * jnp.maximum(rest[0][:, cols].astype(F32), 0.0))
            rest[-1][:, cols] = r.astype(BF16)

    row = pl.BlockSpec((tm, nsh * kq), lambda i: (i, 0))
    in_specs = [pl.BlockSpec((tm, D), lambda i: (i, 0)), _layer_spec(wg, layer)]
    args = [g, wg]
    if a is not None:
        in_specs.append(row)
        args.append(a)
    return pl.pallas_call(
        body,
        name=name,
        grid=(T // tm,),
        in_specs=in_specs,
        out_specs=row,
        out_shape=jax.ShapeDtypeStruct((T, nsh * kq), BF16),
        compiler_params=_params(1),
    )(*args)


def _matmul_nt_norm_bwd(da, wg, layer, x, gain, dres, exch, name):
    T, D = x.shape
    nsh, _, _, ns = wg.shape
    tm = min(T, ROW_TILE // 2)

    def body(da_ref, w_ref, x_ref, g_ref, r_ref, dx_ref, dxb_ref, dg_ref):
        dh = lax.dot_general(da_ref[:, :ns], w_ref[0], NT, preferred_element_type=F32)
        for j in range(1, nsh):
            dh = dh + lax.dot_general(da_ref[:, j * ns : (j + 1) * ns], w_ref[j], NT, preferred_element_type=F32)
        dx, dg = _norm_bwd(dh, x_ref[...], g_ref[...])
        dx = dx + r_ref[...]
        dx_ref[...] = dx
        dxb_ref[...] = dx.astype(BF16)

        @pl.when(pl.program_id(0) == 0)
        def _():
            dg_ref[...] = dg

        @pl.when(pl.program_id(0) > 0)
        def _():
            dg_ref[...] += dg

    row = pl.BlockSpec((tm, D), lambda i: (i, 0))
    vec = pl.BlockSpec((1, D), lambda i: (0, 0))
    return _call_with_exchange(
        body,
        exch,
        name,
        grid=(T // tm,),
        in_specs=[pl.BlockSpec((tm, nsh * ns), lambda i: (i, 0)), _layer_spec(wg, layer), row, vec, row],
        out_specs=[row, row, vec],
        out_shape=[
            jax.ShapeDtypeStruct((T, D), F32),
            jax.ShapeDtypeStruct((T, D), BF16),
            jax.ShapeDtypeStruct((1, D), F32),
        ],
        scratch_shapes=[],
        args=(da, wg, x, gain.reshape(1, D), dres),
    )


def _matmul_tn(lhs, rhs, bufs, layer, n_layers, shard_lhs, act, name):
    T = lhs.shape[0]
    rows = lhs.shape[1] // N_CHIPS if shard_lhs else lhs.shape[1]
    cols = rhs.shape[1] if shard_lhs else rhs.shape[1] // N_CHIPS
    tt = min(T, 2 * ROW_TILE)
    n_t = T // tt

    def body(l_ref, r_ref, *rest):
        o32_ref, o16_ref = rest[-2:]
        t = pl.program_id(1)
        upd = lax.dot_general(_act(l_ref[...], act), r_ref[...].astype(BF16), TN, preferred_element_type=F32)

        @pl.when(t == 0)
        def _():
            o32_ref[...] = upd

        @pl.when(t > 0)
        def _():
            o32_ref[...] += upd

        @pl.when(t == n_t - 1)
        def _():
            o16_ref[...] = o32_ref[...].astype(BF16)

    if shard_lhs:
        in_specs = [pl.BlockSpec((tt, rows), lambda s, t: (t, s)), pl.BlockSpec((tt, cols), lambda s, t: (t, 0))]
    else:
        in_specs = [pl.BlockSpec((tt, rows), lambda s, t: (t, 0)), pl.BlockSpec((tt, cols), lambda s, t: (t, s))]
    args = [lhs, rhs]
    aliases = {}
    if bufs is not None:
        in_specs += [pl.BlockSpec(memory_space=pl.ANY)] * 2
        args += list(bufs)
        aliases = {2: 0, 3: 1}
    shape = (N_CHIPS, n_layers, rows, cols)
    return pl.pallas_call(
        body,
        name=name,
        grid=(N_CHIPS, n_t),
        in_specs=in_specs,
        out_specs=[pl.BlockSpec((None, None, rows, cols), lambda s, t: (s, layer, 0, 0))] * 2,
        out_shape=[jax.ShapeDtypeStruct(shape, F32), jax.ShapeDtypeStruct(shape, BF16)],
        input_output_aliases=aliases,
        compiler_params=_params(2),
    )(*args)


def _chip_peers(x, y):
    return [(1 - x, y), (x, 1 - y), (1 - x, 1 - y)]


def _remote(src, dst, sems, s, peer):
    return pltpu.make_async_remote_copy(
        src_ref=src, dst_ref=dst, send_sem=sems[0].at[s], recv_sem=sems[1].at[s], device_id=peer, device_id_type=MESH
    )


class _Exchange:
    def __init__(self, operands, n_alias, new_shapes, n_sems, build):
        self.operands, self.n_alias, self.new_shapes, self.n_sems, self.build = operands, n_alias, new_shapes, n_sems, build

    def out_shapes(self):
        return [jax.ShapeDtypeStruct(a.shape, a.dtype) for a in self.operands[: self.n_alias]] + list(self.new_shapes)

    def scratch(self):
        return [pltpu.SemaphoreType.DMA((n,)) for n in self.n_sems]

    def run(self, ins, outs, sems, first, last):
        starts, recvs, sends, locals_ = self.build(ins, outs, sems)

        def start_all():
            for cp in starts:
                cp.start()

        def wait_all():
            for cp in recvs:
                cp.wait_recv()
            for cp in sends:
                cp.wait_send()
            for cp in locals_:
                cp.wait()

        if first is True:
            start_all()
            return wait_all
        pl.when(first)(start_all)
        return lambda: pl.when(last)(wait_all)


def _gather_exchange(shards, bufs, plan):
    n_arr = len(shards)
    n_cp = len(plan) * (N_CHIPS - 1)

    def build(ins, outs, sems):
        shard_refs = ins[-n_arr:]
        x, y, c = lax.axis_index("x"), lax.axis_index("y"), lax.axis_index("c")
        me = 2 * x + y
        recvs, sends, locals_ = [], [], []
        for p, (a, l0, n) in enumerate(plan):
            src = shard_refs[a].at[pl.ds(l0, n)]
            cp = pltpu.make_async_copy(src, outs[a].at[me, pl.ds(l0, n)], sems[2].at[p])
            locals_.append(cp)
            for k, (px, py) in enumerate(_chip_peers(x, y)):
                s = p * (N_CHIPS - 1) + k
                sends.append(_remote(src, outs[a].at[me, pl.ds(l0, n)], sems, s, (px, py, c)))
                recvs.append(_remote(src, outs[a].at[2 * px + py, pl.ds(l0, n)], sems, s, (px, py, c)))
        return locals_ + sends, recvs, sends, locals_

    if bufs is None:
        new = [jax.ShapeDtypeStruct((N_CHIPS,) + s.shape, s.dtype) for s in shards]
        return _Exchange(list(shards), 0, new, [n_cp, n_cp, len(plan)], build)
    return _Exchange(list(bufs) + list(shards), n_arr, [], [n_cp, n_cp, len(plan)], build)


def _scatter_exchange(g16, recv, plan, small=None):
    n_arr = len(g16)
    have = [r for r in recv if r is not None]
    made = [a for a in range(n_arr) if recv[a] is None]
    n_cp = len(plan) * (N_CHIPS - 1) + (N_DEV - 1 if small is not None else 0)

    def build(ins, outs, sems):
        g_refs = ins[:n_arr]
        recv_refs, it_have, it_made = [], iter(outs[n_arr : n_arr + len(have)]), iter(outs[n_arr + len(have) :])
        for a in range(n_arr):
            recv_refs.append(next(it_made) if recv[a] is None else next(it_have))
        x, y, c = lax.axis_index("x"), lax.axis_index("y"), lax.axis_index("c")
        me = 2 * x + y
        recvs, sends, locals_ = [], [], []
        for p, (a, l0, n) in enumerate(plan):
            for k, (px, py) in enumerate(_chip_peers(x, y)):
                s = p * (N_CHIPS - 1) + k
                dst = recv_refs[a].at[k, pl.ds(l0, n)]
                sends.append(_remote(g_refs[a].at[2 * px + py, pl.ds(l0, n)], dst, sems, s, (px, py, c)))
                recvs.append(_remote(g_refs[a].at[me, pl.ds(l0, n)], dst, sems, s, (px, py, c)))
        if small is not None:
            small_ref, all_ref = ins[-1], outs[-1]
            slot = 4 * x + 2 * y + c
            locals_.append(pltpu.make_async_copy(small_ref, all_ref.at[slot], sems[2].at[0]))
            flips = [(fx, fy, fc) for fx in (0, 1) for fy in (0, 1) for fc in (0, 1)][1:]
            for k, (fx, fy, fc) in enumerate(flips):
                s = len(plan) * (N_CHIPS - 1) + k
                px, py, pc = x ^ fx, y ^ fy, c ^ fc
                sends.append(_remote(small_ref, all_ref.at[slot], sems, s, (px, py, pc)))
                recvs.append(_remote(small_ref, all_ref.at[4 * px + 2 * py + pc], sems, s, (px, py, pc)))
        return locals_ + sends, recvs, sends, locals_

    operands = list(g16) + have + ([small] if small is not None else [])
    new = [jax.ShapeDtypeStruct((N_CHIPS - 1,) + g16[a].shape[1:], BF16) for a in made]
    if small is not None:
        new.append(jax.ShapeDtypeStruct((N_DEV,) + small.shape, F32))
    return _Exchange(operands, n_arr + len(have), new, [n_cp, n_cp, 1], build)


def _call_with_exchange(body, exch, name, grid, in_specs, out_specs, out_shape, scratch_shapes, args):
    n_in, n_out, n_scr = len(in_specs), len(out_shape), len(scratch_shapes)
    if exch is None:
        outs = pl.pallas_call(
            body, name=name, grid=grid, in_specs=in_specs, out_specs=out_specs, out_shape=out_shape,
            scratch_shapes=scratch_shapes, compiler_params=_params(len(grid)),
        )(*args)
        return outs, []
    e_shapes = exch.out_shapes()
    e_in, e_out = len(exch.operands), len(e_shapes)

    def wrapped(*refs):
        ins, refs = refs[:n_in], refs[n_in:]
        e_ins, refs = refs[:e_in], refs[e_in:]
        outs, refs = refs[:n_out], refs[n_out:]
        e_outs, refs = refs[:e_out], refs[e_out:]
        scr, sems = refs[:n_scr], refs[n_scr:]
        first, last = True, True
        for d, g in enumerate(grid):
            first = (pl.program_id(d) == 0) & first
            last = (pl.program_id(d) == g - 1) & last
        finish = exch.run(e_ins, e_outs, sems, first, last)
        body(*ins, *outs, *scr)
        finish()

    any_spec = pl.BlockSpec(memory_space=pl.ANY)
    outs = pl.pallas_call(
        wrapped,
        name=name,
        grid=grid,
        in_specs=list(in_specs) + [any_spec] * e_in,
        out_specs=list(out_specs) + [any_spec] * e_out,
        out_shape=list(out_shape) + e_shapes,
        input_output_aliases={n_in + i: n_out + i for i in range(exch.n_alias)},
        scratch_shapes=list(scratch_shapes) + exch.scratch(),
        compiler_params=pltpu.CompilerParams(
            dimension_semantics=("arbitrary",) * len(grid), vmem_limit_bytes=VMEM_LIMIT, has_side_effects=True
        ),
    )(*args, *exch.operands)
    return outs[:n_out], outs[n_out:]


def _exchange_only(exch, name):
    n_in = len(exch.operands)
    shapes = exch.out_shapes()

    def body(*refs):
        ins, outs, sems = refs[:n_in], refs[n_in : n_in + len(shapes)], refs[n_in + len(shapes) :]
        exch.run(ins, outs, sems, True, True)()

    any_spec = pl.BlockSpec(memory_space=pl.ANY)
    return pl.pallas_call(
        body,
        name=name,
        in_specs=[any_spec] * n_in,
        out_specs=[any_spec] * len(shapes),
        out_shape=shapes,
        input_output_aliases={i: i for i in range(exch.n_alias)},
        scratch_shapes=exch.scratch(),
        compiler_params=pltpu.CompilerParams(has_side_effects=True),
    )(*exch.operands)


ATTN_LANE_TILES = 2
ATTN_FWD_UNROLL = 34
ATTN_BWD_UNROLL = 17


MASKED = -1e30


def _hi_lo(x):
    hi = x.astype(BF16)
    lo = (x - hi.astype(F32)).astype(BF16)
    return jnp.concatenate([hi, lo], axis=1)


def _suffix_matrix(inclusive):
    j = lax.broadcasted_iota(jnp.int32, (2 * Q_TILE, 2 * Q_TILE), 0) & (Q_TILE - 1)
    s = lax.broadcasted_iota(jnp.int32, (2 * Q_TILE, 2 * Q_TILE), 1)
    later = (j >= s) if inclusive else (j > s)
    return jnp.where((s >= Q_TILE) | later, 1.0, 0.0).astype(BF16)


def _log_beta(z):
    return jnp.minimum(z, 0.0) - jnp.log(1.0 + jnp.exp(-jnp.abs(z)))


def _head_masks(width):
    lane = lax.broadcasted_iota(jnp.int32, (1, width), 1)
    return [(lane >= h * HEAD_DIM) & (lane < (h + 1) * HEAD_DIM) for h in range(width // HEAD_DIM)]


def _per_head_rows(x, masks):
    return jnp.concatenate([jnp.where(hm, x, 0) for hm in masks], axis=0)


def _heads_to_lanes(x, n_heads):
    return jnp.concatenate([x[h * Q_TILE : (h + 1) * Q_TILE] for h in range(n_heads)], axis=1)


def _block_start(kb):
    return kb * Q_TILE if isinstance(kb, int) else pl.multiple_of(kb * Q_TILE, Q_TILE)


def _clamp(i, n):
    return jnp.minimum(i, n - 1)


def _next_block(pos):
    qi, kb = pos
    row_done = kb == 0
    nqi = jnp.where(row_done, qi + 1, qi)
    return nqi, jnp.where(row_done, nqi, kb - 1)


def _stream_unroll(n_blocks, wanted):
    return next(u for u in (wanted, 2, 1) if n_blocks % u == 0)


def _past_mask(rows):
    t = lax.broadcasted_iota(jnp.int32, (rows, Q_TILE), 0) & (Q_TILE - 1)
    s = lax.broadcasted_iota(jnp.int32, (rows, Q_TILE), 1)
    return s < t


def _attn_fwd(qkv, n_seq, S, D, exch, name):
    T = n_seq * S
    width = min(D, ATTN_LANE_TILES * LANES)
    n_heads = width // HEAD_DIM
    rows = n_heads * Q_TILE
    nq = S // Q_TILE
    groups = D // width
    n_blocks = nq * (nq + 1) // 2
    unroll = _stream_unroll(n_blocks, ATTN_FWD_UNROLL)
    scale = HEAD_DIM ** -0.5
    n_trips = n_blocks // unroll

    def body(q_ref, k_ref, v_ref, o_ref, a_out, b_out, qh_scr, vh_scr, bias_scr, a_stage, b_stage, sems):
        masks = _head_masks(width)
        sfx = _suffix_matrix(False)
        stream = pl.program_id(0) * groups + pl.program_id(1)

        def per_head_tables(i, c):
            blk = pl.ds(_block_start(i), Q_TILE)
            qh_scr[i] = _per_head_rows(q_ref[blk, :] * scale, masks)
            vh_scr[i] = _per_head_rows(v_ref[blk, :], masks)
            return c

        lax.fori_loop(0, nq, per_head_tables, 0)
        bias_scr[0] = jnp.zeros((rows, Q_TILE), F32)
        bias_scr[1] = jnp.where(_past_mask(rows), 0.0, MASKED)

        def save(n, slot):
            blocks = pl.ds(n * unroll, unroll)
            return [
                pltpu.make_async_copy(a_stage.at[slot], a_out.at[stream, blocks], sems.at[slot]),
                pltpu.make_async_copy(b_stage.at[slot], b_out.at[stream, blocks], sems.at[2 + slot]),
            ]

        def scores(pos):
            qi, kb = pos
            kt = k_ref[pl.ds(_block_start(_clamp(kb, nq)), Q_TILE), :]
            z = lax.dot_general(qh_scr[_clamp(qi, nq)], kt, NT, preferred_element_type=F32)
            z = z + bias_scr[(kb == qi).astype(jnp.int32)]
            lb = _log_beta(z)
            return lb, _hi_lo(lb - z)

        def weigh(pos, st, carry, acc, slot, u):
            qi, kb = pos
            lb, l1 = st
            r = jnp.dot(l1, sfx, preferred_element_type=F32)
            carry = jnp.where(kb == qi, 0.0, carry)
            a = jnp.exp(lb + r[:, :Q_TILE] + carry).astype(BF16)
            a_stage[slot, u] = a
            b_stage[slot, u] = jnp.exp(lb).astype(BF16)
            acc = jnp.where(kb == qi, 0.0, acc) + jnp.dot(
                _heads_to_lanes(a, n_heads), vh_scr[_clamp(kb, nq)], preferred_element_type=F32
            )
            o_ref[pl.ds(_block_start(_clamp(qi, nq)), Q_TILE), :] = acc
            return carry + r[:, Q_TILE:], acc

        def trip(n, c):
            pos, st, carry, acc = c
            slot = n % 2

            @pl.when((n >= 2) | (stream > 0))
            def _():
                for cp in save(0, slot):
                    cp.wait()

            for u in range(unroll):
                nxt = _next_block(pos)
                st_nxt = scores(nxt)
                carry, acc = weigh(pos, st, carry, acc, slot, u)
                pos, st = nxt, st_nxt
            for cp in save(n, slot):
                cp.start()
            return pos, st, carry, acc

        first = (jnp.int32(0), jnp.int32(0))
        zero = bias_scr[0]
        init = (first, scores(first), zero, jnp.concatenate([zero[:Q_TILE]] * (width // Q_TILE), axis=1))
        lax.fori_loop(0, n_trips, trip, init)

        @pl.when(stream == n_seq * groups - 1)
        def _():
            for slot in range(min(2, n_trips)):
                for cp in save(0, slot):
                    cp.wait()

    seq = lambda col0: pl.BlockSpec((S, width), lambda b, p: (b, col0 + p))
    saved = jax.ShapeDtypeStruct((n_seq * groups, n_blocks, rows, Q_TILE), BF16)
    stage = pltpu.VMEM((2, unroll, rows, Q_TILE), BF16)
    (o, a_w, beta), moved = _call_with_exchange(
        body,
        exch,
        name,
        grid=(n_seq, groups),
        in_specs=[seq(0), seq(groups), seq(2 * groups)],
        out_specs=[seq(0), pl.BlockSpec(memory_space=pl.ANY), pl.BlockSpec(memory_space=pl.ANY)],
        out_shape=[jax.ShapeDtypeStruct((T, D), F32), saved, saved],
        scratch_shapes=[
            pltpu.VMEM((nq, rows, width), BF16),
            pltpu.VMEM((nq, rows, width), BF16),
            pltpu.VMEM((2, rows, Q_TILE), F32),
            stage,
            stage,
            pltpu.SemaphoreType.DMA((4,)),
        ],
        args=(qkv, qkv, qkv),
    )
    return (o, a_w, beta), moved


def _attn_bwd(qkv, fwd, do, n_seq, S, D, exch, name):
    o, a_w, beta = fwd
    T = n_seq * S
    width = min(D, ATTN_LANE_TILES * LANES)
    n_heads = width // HEAD_DIM
    rows = n_heads * Q_TILE
    nq = S // Q_TILE
    groups = D // width
    n_blocks = nq * (nq + 1) // 2
    unroll = _stream_unroll(n_blocks, ATTN_BWD_UNROLL)
    scale = HEAD_DIM ** -0.5
    n_trips = n_blocks // unroll

    def body(q_ref, k_ref, v_ref, o_ref, do_ref, a_in, b_in, dq_ref, dk_ref, dv_ref, dk_acc, dv_acc, qh_scr, doh_scr, delta_scr, a_stage, b_stage, sems):
        masks = _head_masks(width)
        sfx_incl = _suffix_matrix(True)
        stream = pl.program_id(0) * groups + pl.program_id(1)

        n_streams = n_seq * groups
        ahead = n_trips % 2 == 0

        def fetch(s, n, slot):
            blocks = pl.ds(n * unroll, unroll)
            return [
                pltpu.make_async_copy(a_in.at[s, blocks], a_stage.at[slot], sems.at[slot]),
                pltpu.make_async_copy(b_in.at[s, blocks], b_stage.at[slot], sems.at[2 + slot]),
            ]

        @pl.when((stream == 0) | (not ahead))
        def _():
            for cp in fetch(stream, 0, 0):
                cp.start()

        dk_acc[...] = jnp.zeros_like(dk_acc)
        dv_acc[...] = jnp.zeros_like(dv_acc)

        lane = lax.broadcasted_iota(jnp.int32, (2 * width, rows), 0) % width
        col = lax.broadcasted_iota(jnp.int32, (2 * width, rows), 1)
        head_sums = jnp.where(lane // HEAD_DIM == col // Q_TILE, 1.0, 0.0).astype(BF16)

        def per_head_tables(i, c):
            blk = pl.ds(_block_start(i), Q_TILE)
            do = do_ref[blk, :]
            qh_scr[i] = _per_head_rows(q_ref[blk, :] * scale, masks)
            doh_scr[i] = _per_head_rows(do, masks)
            prod = do.astype(F32) * o_ref[blk, :]
            d = jnp.dot(_hi_lo(prod), head_sums, preferred_element_type=F32)
            delta_scr[i] = jnp.concatenate([d[:, h * Q_TILE : (h + 1) * Q_TILE] for h in range(n_heads)], axis=0)
            return c

        lax.fori_loop(0, nq, per_head_tables, 0)

        def weigh(pos, ab, beta, c2, dq):
            qi, kb = pos
            first = kb == qi
            blk = pl.ds(_block_start(kb), Q_TILE)
            g = ab.astype(F32) * lax.dot_general(doh_scr[qi], v_ref[blk, :], NT, preferred_element_type=F32)
            r2 = jnp.dot(_hi_lo(g), sfx_incl, preferred_element_type=F32)
            c2 = jnp.where(first, 0.0, c2)
            earlier = delta_scr[qi] - (r2[:, :Q_TILE] + c2)
            beta = beta.astype(F32)
            dzb = (g * (1.0 - beta) - earlier * beta).astype(BF16)
            kh = _per_head_rows(k_ref[blk, :], masks)
            dq = jnp.where(first, 0.0, dq) + jnp.dot(_heads_to_lanes(dzb, n_heads), kh, preferred_element_type=F32)
            dq_ref[pl.ds(_block_start(qi), Q_TILE), :] = (dq * scale).astype(BF16)
            dk_acc[blk, :] += lax.dot_general(dzb, qh_scr[qi], TN, preferred_element_type=F32)
            dv_acc[blk, :] += lax.dot_general(ab, doh_scr[qi], TN, preferred_element_type=F32)
            return c2 + r2[:, Q_TILE:], dq

        def trip(n, c):
            pos, c2, dq = c
            slot = n % 2
            for cp in fetch(stream, n, slot):
                cp.wait()
            more = n + 1 < n_trips
            if ahead:
                nxt = (jnp.where(more, stream, stream + 1), jnp.where(more, n + 1, 0))
                more = more | (stream + 1 < n_streams)
            else:
                nxt = (stream, n + 1)

            @pl.when(more)
            def _():
                for cp in fetch(*nxt, 1 - slot):
                    cp.start()

            for u in range(unroll):
                c2, dq = weigh(pos, a_stage[slot, u], b_stage[slot, u], c2, dq)
                pos = _next_block(pos)
            return pos, c2, dq

        zero = dk_acc[pl.ds(0, Q_TILE), :]
        init = ((jnp.int32(0), jnp.int32(0)), jnp.concatenate([zero[:, :Q_TILE]] * n_heads, axis=0), zero)
        lax.fori_loop(0, n_trips, trip, init)
        dk_ref[...] = dk_acc[...].astype(BF16)
        dv_ref[...] = dv_acc[...].astype(BF16)

    seq = lambda col0: pl.BlockSpec((S, width), lambda b, p: (b, col0 + p))
    return _call_with_exchange(
        body,
        exch,
        name,
        grid=(n_seq, groups),
        in_specs=[seq(0), seq(groups), seq(2 * groups), seq(0), seq(0)] + [pl.BlockSpec(memory_space=pl.ANY)] * 2,
        out_specs=[seq(0)] * 3,
        out_shape=[jax.ShapeDtypeStruct((T, D), BF16)] * 3,
        scratch_shapes=[
            pltpu.VMEM((S, width), F32),
            pltpu.VMEM((S, width), F32),
            pltpu.VMEM((nq, rows, width), BF16),
            pltpu.VMEM((nq, rows, width), BF16),
            pltpu.VMEM((nq, rows, Q_TILE), F32),
            pltpu.VMEM((2, unroll, rows, Q_TILE), BF16),
            pltpu.VMEM((2, unroll, rows, Q_TILE), BF16),
            pltpu.SemaphoreType.DMA((4,)),
        ],
        args=(qkv, qkv, qkv, o, do, a_w, beta),
    )


def _causal_ws(ws_ref, g):
    t = lax.broadcasted_iota(jnp.int32, (SGU_CHUNK, SGU_CHUNK), 0)
    s = lax.broadcasted_iota(jnp.int32, (SGU_CHUNK, SGU_CHUNK), 1)
    return jnp.where(s <= t, ws_ref[g], 0.0)


def _sgu_fwd(a, gain, ws, bsb, name):
    T, F2 = a.shape
    F = F2 // 2
    gw = F // SGU_GROUPS

    def body(a_ref, gain_ref, ws_ref, bsb_ref, y_ref):
        v = _gelu(a_ref[:, F:].astype(F32))
        vn = (v * _rstd(v) * gain_ref[...]).astype(BF16)
        for g in range(SGU_GROUPS):
            cs = slice(g * gw, (g + 1) * gw)
            w = _causal_ws(ws_ref, g).astype(BF16)
            mixed = jnp.dot(w, vn[:, cs], preferred_element_type=F32) + bsb_ref[g]
            y_ref[:, cs] = (_gelu(a_ref[:, cs].astype(F32)) * mixed).astype(BF16)

    return pl.pallas_call(
        body,
        name=name,
        grid=(T // SGU_CHUNK,),
        in_specs=[
            pl.BlockSpec((SGU_CHUNK, F2), lambda i: (i, 0)),
            pl.BlockSpec((1, F), lambda i: (0, 0)),
            pl.BlockSpec((SGU_GROUPS, SGU_CHUNK, SGU_CHUNK), lambda i: (0, 0, 0)),
            pl.BlockSpec((SGU_GROUPS, SGU_CHUNK, gw), lambda i: (0, 0, 0)),
        ],
        out_specs=pl.BlockSpec((SGU_CHUNK, F), lambda i: (i, 0)),
        out_shape=jax.ShapeDtypeStruct((T, F), BF16),
        compiler_params=_params(1),
    )(a, gain.reshape(1, F), ws, bsb)


def _sgu_bwd(a, dy, gain, ws, bsb, name):
    T, F2 = a.shape
    F = F2 // 2
    gw = F // SGU_GROUPS

    def body(a_ref, dy_ref, gain_ref, ws_ref, bsb_ref, da_ref, dws_ref, dbs_ref, dgain_ref, dvn_ref):
        @pl.when(pl.program_id(0) == 0)
        def _():
            dws_ref[...] = jnp.zeros_like(dws_ref)
            dbs_ref[...] = jnp.zeros_like(dbs_ref)
            dgain_ref[...] = jnp.zeros_like(dgain_ref)

        v, v_slope = _gelu_and_grad(a_ref[:, F:].astype(F32))
        rstd = _rstd(v)
        vh = v * rstd
        gain = gain_ref[...]
        vn = (vh * gain).astype(BF16)
        ones = jnp.ones((gw, SGU_CHUNK), BF16)
        for g in range(SGU_GROUPS):
            cs = slice(g * gw, (g + 1) * gw)
            w = _causal_ws(ws_ref, g).astype(BF16)
            mixed = jnp.dot(w, vn[:, cs], preferred_element_type=F32) + bsb_ref[g]
            u, u_slope = _gelu_and_grad(a_ref[:, cs].astype(F32))
            dyc = dy_ref[:, cs].astype(F32)
            da_ref[:, cs] = (dyc * mixed * u_slope).astype(BF16)
            dm = (dyc * u).astype(BF16)
            dbs_ref[g] += jnp.dot(dm, ones, preferred_element_type=F32)
            dws_ref[g] += _causal_mask_f32(lax.dot_general(dm, vn[:, cs], NT, preferred_element_type=F32))
            dvn_ref[:, cs] = lax.dot_general(w, dm, TN, preferred_element_type=F32)
        dvn = dvn_ref[...]
        dgain_ref[...] += jnp.sum(dvn * vh, axis=0, keepdims=True)
        dvh = dvn * gain
        dv = rstd * (dvh - vh * jnp.mean(dvh * vh, axis=-1, keepdims=True))
        da_ref[:, F:] = (dv * v_slope).astype(BF16)

    acc_spec = pl.BlockSpec((SGU_GROUPS, SGU_CHUNK, SGU_CHUNK), lambda i: (0, 0, 0))
    acc_shape = jax.ShapeDtypeStruct((SGU_GROUPS, SGU_CHUNK, SGU_CHUNK), F32)
    return pl.pallas_call(
        body,
        name=name,
        grid=(T // SGU_CHUNK,),
        in_specs=[
            pl.BlockSpec((SGU_CHUNK, F2), lambda i: (i, 0)),
            pl.BlockSpec((SGU_CHUNK, F), lambda i: (i, 0)),
            pl.BlockSpec((1, F), lambda i: (0, 0)),
            acc_spec,
            pl.BlockSpec((SGU_GROUPS, SGU_CHUNK, gw), lambda i: (0, 0, 0)),
        ],
        out_specs=[
            pl.BlockSpec((SGU_CHUNK, F2), lambda i: (i, 0)),
            acc_spec,
            acc_spec,
            pl.BlockSpec((1, F), lambda i: (0, 0)),
        ],
        out_shape=[
            jax.ShapeDtypeStruct((T, F2), BF16),
            acc_shape,
            acc_shape,
            jax.ShapeDtypeStruct((1, F), F32),
        ],
        scratch_shapes=[pltpu.VMEM((SGU_CHUNK, F), F32)],
        compiler_params=_params(1),
    )(a, dy, gain.reshape(1, F), ws, bsb)


def _causal_mask_f32(m):
    t = lax.broadcasted_iota(jnp.int32, m.shape, 0)
    s = lax.broadcasted_iota(jnp.int32, m.shape, 1)
    return jnp.where(s <= t, m, 0.0)


def _final_loss(x, gain, target, name):
    T, D = x.shape
    tm = min(T, ROW_TILE // 2)

    def body(x_ref, g_ref, t_ref, sq_ref, dx_ref, dxb_ref, dg_ref):
        xv = x_ref[...]
        gain = g_ref[...]
        err = xv * _rstd(xv) * gain - t_ref[...]
        dx, dg = _norm_bwd(err * (1.0 / D), xv, gain)
        dx_ref[...] = dx
        dxb_ref[...] = dx.astype(BF16)
        sq = jnp.sum(err * err, axis=0, keepdims=True)

        @pl.when(pl.program_id(0) == 0)
        def _():
            sq_ref[...] = sq
            dg_ref[...] = dg

        @pl.when(pl.program_id(0) > 0)
        def _():
            sq_ref[...] += sq
            dg_ref[...] += dg

    row = pl.BlockSpec((tm, D), lambda i: (i, 0))
    vec = pl.BlockSpec((1, D), lambda i: (0, 0))
    return pl.pallas_call(
        body,
        name=name,
        grid=(T // tm,),
        in_specs=[row, vec, row],
        out_specs=[vec, row, row, vec],
        out_shape=[
            jax.ShapeDtypeStruct((1, D), F32),
            jax.ShapeDtypeStruct((T, D), F32),
            jax.ShapeDtypeStruct((T, D), BF16),
            jax.ShapeDtypeStruct((1, D), F32),
        ],
        compiler_params=_params(1),
    )(x, gain.reshape(1, D), target)


def _row_tile(rows, cols, n_arrays):
    budget = VMEM_LIMIT // 2 // (2 * n_arrays * cols * 4)
    tr = rows
    while tr > budget and tr % 16 == 0:
        tr //= 2
    return tr


def _sum_received(own, recv, name):
    R, C = own.shape
    n = recv.shape[0]
    tr = _row_tile(R, C, n + 2)

    def body(own_ref, recv_ref, o_ref):
        s = own_ref[...]
        for k in range(n):
            s = s + recv_ref[k].astype(F32)
        o_ref[...] = s

    return pl.pallas_call(
        body,
        name=name,
        grid=(R // tr,),
        in_specs=[pl.BlockSpec((tr, C), lambda i: (i, 0)), pl.BlockSpec((n, tr, C), lambda i: (0, i, 0))],
        out_specs=pl.BlockSpec((tr, C), lambda i: (i, 0)),
        out_shape=jax.ShapeDtypeStruct((R, C), F32),
        compiler_params=_params(1),
    )(own, recv)


def _sum_chip_shard(g32, recv, chip, name):
    _, L, r, c = g32.shape
    n = recv.shape[0]
    tr = _row_tile(r, c, n + 2)

    def body(chip_ref, own_ref, recv_ref, o_ref):
        s = own_ref[...]
        for k in range(n):
            s = s + recv_ref[k].astype(F32)
        o_ref[...] = s

    return pl.pallas_call(
        body,
        name=name,
        grid_spec=pltpu.PrefetchScalarGridSpec(
            num_scalar_prefetch=1,
            grid=(L, r // tr),
            in_specs=[
                pl.BlockSpec((None, None, tr, c), lambda l, i, chip_ref: (chip_ref[0], l, i, 0)),
                pl.BlockSpec((n, None, tr, c), lambda l, i, chip_ref: (0, l, i, 0)),
            ],
            out_specs=pl.BlockSpec((None, tr, c), lambda l, i, chip_ref: (l, i, 0)),
        ),
        out_shape=jax.ShapeDtypeStruct((L, r, c), F32),
        compiler_params=_params(2),
    )(chip.reshape(1).astype(jnp.int32), g32, recv)


def _adamw(w, m, v, parts, name):
    R, C = w.shape
    n = len(parts)
    tr = _row_tile(R, C, n + 7)

    def body(*refs):
        w_ref, m_ref, v_ref = refs[:3]
        g_ref, d_ref, nm_ref, nv_ref = refs[3 + n :]
        g = refs[3][...]
        for p_ref in refs[4 : 3 + n]:
            g = g + p_ref[...]
        nm = ADAM_B1 * m_ref[...] + (1.0 - ADAM_B1) * g
        nv = ADAM_B2 * v_ref[...] + (1.0 - ADAM_B2) * (g * g)
        m_hat = nm / (1.0 - ADAM_B1**ADAM_STEP)
        v_hat = nv / (1.0 - ADAM_B2**ADAM_STEP)
        g_ref[...] = g
        d_ref[...] = -ADAM_LR * (m_hat / (jnp.sqrt(v_hat) + ADAM_EPS) + ADAM_WD * w_ref[...])
        nm_ref[...] = nm
        nv_ref[...] = nv

    spec = pl.BlockSpec((tr, C), lambda i: (i, 0))
    return pl.pallas_call(
        body,
        name=name,
        grid=(R // tr,),
        in_specs=[spec] * (3 + n),
        out_specs=[spec] * 4,
        out_shape=[jax.ShapeDtypeStruct((R, C), F32)] * 4,
        compiler_params=_params(1),
    )(w, m, v, *parts)


def _swap_with_sibling(parts, name):
    n = len(parts)

    def body(*refs):
        ins, outs = refs[:n], refs[n : 2 * n]
        send_sems, recv_sems = refs[2 * n :]
        sibling = (lax.axis_index("x"), lax.axis_index("y"), 1 - lax.axis_index("c"))
        copies = [
            pltpu.make_async_remote_copy(
                src_ref=ins[a],
                dst_ref=outs[a],
                send_sem=send_sems.at[a],
                recv_sem=recv_sems.at[a],
                device_id=sibling,
                device_id_type=MESH,
            )
            for a in range(n)
        ]
        for cp in copies:
            cp.start()
        for cp in copies:
            cp.wait_recv()
        for cp in copies:
            cp.wait_send()

    any_spec = pl.BlockSpec(memory_space=pl.ANY)
    return pl.pallas_call(
        body,
        name=name,
        in_specs=[any_spec] * n,
        out_specs=[any_spec] * n,
        out_shape=[jax.ShapeDtypeStruct(p.shape, p.dtype) for p in parts],
        scratch_shapes=[pltpu.SemaphoreType.DMA((n,)), pltpu.SemaphoreType.DMA((n,))],
        compiler_params=pltpu.CompilerParams(has_side_effects=True),
    )(*parts)


def _pack(pieces):
    flat = jnp.concatenate([p.reshape(-1) for p in pieces])
    return flat.reshape(-1, LANES)


def _unpack(packed, shapes):
    flat = packed.reshape(-1)
    out, off = [], 0
    for s in shapes:
        size = 1
        for d in s:
            size *= d
        out.append(flat[off : off + size].reshape(s))
        off += size
    return out


def kernel(x, norm_mix, norm_mlp, sb_wqkv, sb_wo, sgu_win, sgu_gain, sgu_ws, sgu_bs, sgu_wout, mlp_w1, mlp_w2, final_norm, loss_target, m_norm_mix, m_norm_mlp, m_sb_wqkv, m_sb_wo, m_sgu_win, m_sgu_gain, m_sgu_ws, m_sgu_bs, m_sgu_wout, m_mlp_w1, m_mlp_w2, m_final_norm, v_norm_mix, v_norm_mlp, v_sb_wqkv, v_sb_wo, v_sgu_win, v_sgu_gain, v_sgu_ws, v_sgu_bs, v_sgu_wout, v_mlp_w1, v_mlp_w2, v_final_norm):
    n_seq, S, D = x.shape
    T = n_seq * S
    depth = norm_mix.shape[0]
    n_sgu = sgu_win.shape[0]
    F = sgu_wout.shape[1] * N_CHIPS
    gw = F // SGU_GROUPS
    chip = 2 * lax.axis_index("x") + lax.axis_index("y")

    QKV, WO, WIN, WOUT, W1, W2, GAIN = range(7)
    big = [sb_wqkv, sb_wo, sgu_win, sgu_wout, mlp_w1, mlp_w2]
    n_sb = sb_wqkv.shape[0]
    shards = [w.astype(BF16) for w in big] + [sgu_gain.reshape(1, -1, LANES)]

    def gather_plan(i):
        j, mlp = i // 2, min(2, depth - i)
        plan = [(WO, j, 1), (W1, i, mlp), (W2, i, mlp)]
        if i + 1 < depth:
            plan += [(WIN, (i + 1) // 2, 1), (WOUT, (i + 1) // 2, 1)]
        if j + 1 < n_sb:
            plan += [(QKV, j + 1, 1)]
        return plan

    wg = _exchange_only(_gather_exchange(shards, None, [(QKV, 0, 1), (GAIN, 0, 1)]), "gather_first_weights")
    gain_full = jnp.transpose(wg[GAIN].reshape(N_CHIPS, n_sgu, F // N_CHIPS), (1, 0, 2)).reshape(n_sgu, F)
    bsb = [jnp.broadcast_to(sgu_bs[j][:, :, None], (SGU_GROUPS, SGU_CHUNK, gw)) for j in range(n_sgu)]

    xs = x.reshape(T, D)
    saved = []
    for i in range(depth):
        j = i // 2
        if i % 2 == 0:
            qkv, h = _norm_matmul(xs, norm_mix[i], wg[QKV], j, f"qkv_fwd_{i}")
            attn, wg = _attn_fwd(qkv, n_seq, S, D, _gather_exchange(shards, wg, gather_plan(i)), f"attn_fwd_{i}")
            wg_qkv, wg_wo, wg_win, wg_wout, wg_w1, wg_w2 = wg[:6]
            x_mid = _act_matmul_res(attn[0], wg_wo, j, xs, None, f"wo_fwd_{i}")
            mix = (qkv, attn)
        else:
            a, h = _norm_matmul(xs, norm_mix[i], wg_win, j, f"win_fwd_{i}")
            yg = _sgu_fwd(a, gain_full[j], sgu_ws[j], bsb[j], f"sgu_fwd_{i}")
            x_mid = _act_matmul_res(yg, wg_wout, j, xs, None, f"wout_fwd_{i}")
            mix = (a, yg)
        a2, h2 = _norm_matmul(x_mid, norm_mlp[i], wg_w1, i, f"w1_fwd_{i}")
        x_out = _act_matmul_res(a2, wg_w2, i, x_mid, "relu2", f"w2_fwd_{i}")
        saved.append((xs, h, mix, x_mid, h2, a2))
        xs = x_out

    sq, dx, dxb, g_final = _final_loss(xs, final_norm, loss_target.reshape(T, D), "loss_head")
    loss = lax.psum(0.5 * jnp.sum(sq) / D, ("x", "y", "c"))

    n_layers = [n_sb, n_sb, n_sgu, n_sgu, depth, depth]
    g32, g16, recv = [None] * 6, [None] * 6, [None] * 6
    done_from, sent_from = list(n_layers), list(n_layers)

    def grad(a, layer, lhs, rhs, shard_lhs, act, name):
        bufs = None if g32[a] is None else (g32[a], g16[a])
        g32[a], g16[a] = _matmul_tn(lhs, rhs, bufs, layer, n_layers[a], shard_lhs, act, name)
        done_from[a] = layer

    def unsent_plan():
        plan = [(a, done_from[a], sent_from[a] - done_from[a]) for a in range(6) if sent_from[a] > done_from[a]]
        for a, l0, _ in plan:
            sent_from[a] = l0
        return plan

    def scatter(plan, small):
        if not plan and small is None:
            return None, lambda moved: None
        arrays = sorted({a for a, _, _ in plan})
        have = [a for a in arrays if recv[a] is not None]
        made = [a for a in arrays if recv[a] is None]
        exch = _scatter_exchange(
            [g16[a] for a in arrays], [recv[a] for a in arrays], [(arrays.index(a), l0, n) for a, l0, n in plan], small
        )

        def take(moved):
            for a, buf in zip(arrays, moved):
                g16[a] = buf
            for a, buf in zip(have + made, moved[len(arrays) :]):
                recv[a] = buf
            return moved[-1]

        return exch, take

    g_mix, g_mlp = [None] * depth, [None] * depth
    g_ws, g_bs, g_gain = [None] * n_sgu, [None] * n_sgu, [None] * n_sgu
    for i in reversed(range(depth)):
        j = i // 2
        x_in, h, mix, x_mid, h2, a2 = saved[i]
        da2 = _matmul_nt(dxb, wg_w2, i, a2, f"w2_bwd_{i}")
        grad(W2, i, a2, dxb, True, "relu2", f"w2_grad_{i}")
        grad(W1, i, h2, da2, False, None, f"w1_grad_{i}")
        (dx, dxb, g_mlp[i]), _ = _matmul_nt_norm_bwd(da2, wg_w1, i, x_mid, norm_mlp[i], dx, None, f"w1_bwd_{i}")
        if i % 2 == 0:
            qkv, attn = mix
            do = _matmul_nt(dxb, wg_wo, j, None, f"wo_bwd_{i}")
            grad(WO, j, attn[0], dxb, True, None, f"wo_grad_{i}")
            sgu_small = _pack([jnp.stack(g_ws), jnp.stack(g_bs), jnp.stack(g_gain)]) if i == 0 and n_sgu else None
            exch, take = scatter(unsent_plan(), sgu_small)
            (dq, dk, dv), moved = _attn_bwd(qkv, attn, do, n_seq, S, D, exch, f"attn_bwd_{i}")
            last = take(moved)
            if sgu_small is not None:
                sgu_small_all = last
            dqkv = jnp.concatenate([dq, dk, dv], axis=1)
            grad(QKV, j, h, dqkv, False, None, f"qkv_grad_{i}")
            exch, take = scatter(unsent_plan(), None)
            (dx, dxb, g_mix[i]), moved = _matmul_nt_norm_bwd(
                dqkv, wg_qkv, j, x_in, norm_mix[i], dx, exch, f"qkv_bwd_{i}"
            )
            take(moved)
        else:
            a, yg = mix
            dyg = _matmul_nt(dxb, wg_wout, j, None, f"wout_bwd_{i}")
            grad(WOUT, j, yg, dxb, True, None, f"wout_grad_{i}")
            da, g_ws[j], dbs, g_gain[j] = _sgu_bwd(a, dyg, gain_full[j], sgu_ws[j], bsb[j], f"sgu_bwd_{i}")
            g_bs[j] = dbs[:, :, 0]
            grad(WIN, j, h, da, False, None, f"win_grad_{i}")
            (dx, dxb, g_mix[i]), _ = _matmul_nt_norm_bwd(da, wg_win, j, x_in, norm_mix[i], dx, None, f"win_bwd_{i}")
    grad_x = dx.reshape(n_seq, S, D)

    names = ["qkv", "wo", "win", "wout", "w1", "w2"]
    exch, take = scatter(unsent_plan(), _pack([jnp.stack(g_mix), jnp.stack(g_mlp), g_final]))
    norm_small_all = take(_exchange_only(exch, "gather_norm_grads"))
    partial = [_sum_chip_shard(g32[a], recv[a], chip, f"sum_{names[a]}") for a in range(6)]
    partial = [p.reshape(-1, p.shape[-1]) for p in partial]
    theirs = _swap_with_sibling(partial, "swap_partial_sums")
    g_small = _unpack(
        _sum_received(norm_small_all[0], norm_small_all[1:], "sum_norm_grads"),
        [norm_mix.shape, norm_mlp.shape, final_norm.shape],
    ) + _unpack(
        _sum_received(sgu_small_all[0], sgu_small_all[1:], "sum_sgu_small_grads"),
        [sgu_ws.shape, sgu_bs.shape, (n_sgu, F)],
    )

    ms = [m_sb_wqkv, m_sb_wo, m_sgu_win, m_sgu_wout, m_mlp_w1, m_mlp_w2]
    vs = [v_sb_wqkv, v_sb_wo, v_sgu_win, v_sgu_wout, v_mlp_w1, v_mlp_w2]
    res = {}
    keys = ["sb_wqkv", "sb_wo", "sgu_win", "sgu_wout", "mlp_w1", "mlp_w2"]
    for key, k, w, m, v, mine, other in zip(keys, names, big, ms, vs, partial, theirs):
        cols = w.shape[-1]
        outs = _adamw(w.reshape(-1, cols), m.reshape(-1, cols), v.reshape(-1, cols), [mine, other], f"adamw_{k}")
        res[key] = [o.reshape(w.shape) for o in outs]

    g_small[5] = lax.dynamic_slice_in_dim(g_small[5], chip * (F // N_CHIPS), F // N_CHIPS, axis=1)
    small_keys = ["norm_mix", "norm_mlp", "final_norm", "sgu_ws", "sgu_bs", "sgu_gain"]
    small_w = [norm_mix, norm_mlp, final_norm, sgu_ws, sgu_bs, sgu_gain]
    small_m = [m_norm_mix, m_norm_mlp, m_final_norm, m_sgu_ws, m_sgu_bs, m_sgu_gain]
    small_v = [v_norm_mix, v_norm_mlp, v_final_norm, v_sgu_ws, v_sgu_bs, v_sgu_gain]
    outs = _adamw(_pack(small_w), _pack(small_m), _pack(small_v), [_pack(g_small)], "adamw_small")
    local_shapes = [w.shape for w in small_w]
    for key, parts in zip(small_keys, zip(*[_unpack(o, local_shapes) for o in outs])):
        res[key] = list(parts)

    order = ["norm_mix", "norm_mlp", "sb_wqkv", "sb_wo", "sgu_win", "sgu_gain", "sgu_ws", "sgu_bs", "sgu_wout", "mlp_w1", "mlp_w2", "final_norm"]
    return (loss, grad_x, *[res[k][0] for k in order], *[res[k][1] for k in order], *[res[k][2] for k in order], *[res[k][3] for k in order])
```

```python
import jax
import jax.numpy as jnp
from jax import lax
from jax.experimental import pallas as pl
from jax.experimental.pallas import tpu as pltpu

F32 = jnp.float32
BF16 = jnp.bfloat16
MESH = pl.DeviceIdType.MESH

EPS = 1e-6
HEAD_DIM = 64
LANES = 128
Q_TILE = 128
SGU_CHUNK = 128
SGU_GROUPS = 8
N_CHIPS = 4
N_DEV = 8
ADAM_LR = 0.001
ADAM_B1 = 0.9
ADAM_B2 = 0.999
ADAM_EPS = 1e-08
ADAM_WD = 0.01
ADAM_STEP = 10
GELU_C0 = 0.7978845608028654
GELU_C1 = 0.044715
VMEM_LIMIT = 48 * 1024 * 1024
ROW_TILE = 1024
NT = (((1,), (1,)), ((), ()))
TN = (((0,), (0,)), ((), ()))


def _params(n_axes):
    return pltpu.CompilerParams(dimension_semantics=("arbitrary",) * n_axes, vmem_limit_bytes=VMEM_LIMIT)


def _rstd(x):
    return lax.rsqrt(jnp.mean(x * x, axis=-1, keepdims=True) + EPS)


def _norm_bwd(dh, x, gain):
    rstd = _rstd(x)
    xh = x * rstd
    dhg = dh * gain
    dx = rstd * (dhg - xh * jnp.mean(dhg * xh, axis=-1, keepdims=True))
    return dx, jnp.sum(dh * xh, axis=0, keepdims=True)


def _gelu(x):
    return (0.5 * x) * (1.0 + jnp.tanh(x * (GELU_C0 + (GELU_C0 * GELU_C1) * (x * x))))


def _gelu_and_grad(x):
    x2 = x * x
    t = jnp.tanh(x * (GELU_C0 + (GELU_C0 * GELU_C1) * x2))
    half_x, p = 0.5 * x, 1.0 + t
    slope = GELU_C0 + (3.0 * GELU_C0 * GELU_C1) * x2
    return half_x * p, 0.5 * p + (half_x * slope) * (1.0 - t * t)


def _act(a, act):
    if act == "relu2":
        r = jnp.maximum(a.astype(F32), 0.0)
        return (r * r).astype(BF16)
    return a.astype(BF16)


def _layer_spec(wg, layer):
    nsh, _, r, c = wg.shape
    return pl.BlockSpec((nsh, None, r, c), lambda i: (0, layer, 0, 0), pipeline_mode=pl.Buffered(1))


def _token_tile(T, wide_row_bytes):
    return min(T, ROW_TILE if ROW_TILE * wide_row_bytes <= 4 * 1024 * 1024 else ROW_TILE // 2)


def _norm_matmul(x, gain, wg, layer, name):
    T, D = x.shape
    nsh, _, _, ns = wg.shape
    tm = min(T, ROW_TILE)

    def body(x_ref, g_ref, w_ref, y_ref, h_ref):
        xv = x_ref[...]
        h = (xv * _rstd(xv) * g_ref[...]).astype(BF16)
        h_ref[...] = h
        for j in range(nsh):
            y_ref[:, j * ns : (j + 1) * ns] = jnp.dot(h, w_ref[j], preferred_element_type=F32).astype(BF16)

    return pl.pallas_call(
        body,
        name=name,
        grid=(T // tm,),
        in_specs=[pl.BlockSpec((tm, D), lambda i: (i, 0)), pl.BlockSpec((1, D), lambda i: (0, 0)), _layer_spec(wg, layer)],
        out_specs=[pl.BlockSpec((tm, nsh * ns), lambda i: (i, 0)), pl.BlockSpec((tm, D), lambda i: (i, 0))],
        out_shape=[jax.ShapeDtypeStruct((T, nsh * ns), BF16), jax.ShapeDtypeStruct((T, D), BF16)],
        compiler_params=_params(1),
    )(x, gain.reshape(1, D), wg)


def _act_matmul_res(a, wg, layer, x_in, act, name):
    T, K = a.shape
    nsh, _, kq, D = wg.shape
    tm = _token_tile(T, K * a.dtype.itemsize)

    def body(a_ref, w_ref, x_ref, o_ref):
        w = w_ref[...].reshape(nsh * kq, D)
        o_ref[...] = x_ref[...] + jnp.dot(_act(a_ref[...], act), w, preferred_element_type=F32)

    return pl.pallas_call(
        body,
        name=name,
        grid=(T // tm,),
        in_specs=[pl.BlockSpec((tm, K), lambda i: (i, 0)), _layer_spec(wg, layer), pl.BlockSpec((tm, D), lambda i: (i, 0))],
        out_specs=pl.BlockSpec((tm, D), lambda i: (i, 0)),
        out_shape=jax.ShapeDtypeStruct((T, D), F32),
        compiler_params=_params(1),
    )(a, wg, x_in)


def _matmul_nt(g, wg, layer, a, name):
    T, D = g.shape
    nsh, _, kq, _ = wg.shape
    tm = _token_tile(T, nsh * kq * jnp.dtype(BF16).itemsize)

    def body(g_ref, w_ref, *rest):
        gv = g_ref[...]
        for k in range(nsh):
            cols = slice(k * kq, (k + 1) * kq)
            r = lax.dot_general(gv, w_ref[k], NT, preferred_element_type=F32)
            if a is not None:
                r = r * (2.0 * jnp.maximum(rest[0][:, cols].astype(F32), 0.0))
            rest[-1][:, cols] = r.astype(BF16)

    row = pl.BlockSpec((tm, nsh * kq), lambda i: (i, 0))
    in_specs = [pl.BlockSpec((tm, D), lambda i: (i, 0)), _layer_spec(wg, layer)]
    args = [g, wg]
    if a is not None:
        in_specs.append(row)
        args.append(a)
    return pl.pallas_call(
        body,
        name=name,
        grid=(T // tm,),
        in_specs=in_specs,
        out_specs=row,
        out_shape=jax.ShapeDtypeStruct((T, nsh * kq), BF16),
        compiler_params=_params(1),
    )(*args)


def _matmul_nt_norm_bwd(da, wg, layer, x, gain, dres, exch, name):
    T, D = x.shape
    nsh, _, _, ns = wg.shape
    tm = min(T, ROW_TILE // 2)

    def body(da_ref, w_ref, x_ref, g_ref, r_ref, dx_ref, dxb_ref, dg_ref):
        dh = lax.dot_general(da_ref[:, :ns], w_ref[0], NT, preferred_element_type=F32)
        for j in range(1, nsh):
            dh = dh + lax.dot_general(da_ref[:, j * ns : (j + 1) * ns], w_ref[j], NT, preferred_element_type=F32)
        dx, dg = _norm_bwd(dh, x_ref[...], g_ref[...])
        dx = dx + r_ref[...]
        dx_ref[...] = dx
        dxb_ref[...] = dx.astype(BF16)

        @pl.when(pl.program_id(0) == 0)
        def _():
            dg_ref[...] = dg

        @pl.when(pl.program_id(0) > 0)
        def _():
            dg_ref[...] += dg

    row = pl.BlockSpec((tm, D), lambda i: (i, 0))
    vec = pl.BlockSpec((1, D), lambda i: (0, 0))
    return _call_with_exchange(
        body,
        exch,
        name,
        grid=(T // tm,),
        in_specs=[pl.BlockSpec((tm, nsh * ns), lambda i: (i, 0)), _layer_spec(wg, layer), row, vec, row],
        out_specs=[row, row, vec],
        out_shape=[
            jax.ShapeDtypeStruct((T, D), F32),
            jax.ShapeDtypeStruct((T, D), BF16),
            jax.ShapeDtypeStruct((1, D), F32),
        ],
        scratch_shapes=[],
        args=(da, wg, x, gain.reshape(1, D), dres),
    )


def _matmul_tn(lhs, rhs, bufs, layer, n_layers, shard_lhs, act, name):
    T = lhs.shape[0]
    rows = lhs.shape[1] // N_CHIPS if shard_lhs else lhs.shape[1]
    cols = rhs.shape[1] if shard_lhs else rhs.shape[1] // N_CHIPS
    tt = min(T, 2 * ROW_TILE)
    n_t = T // tt

    def body(l_ref, r_ref, *rest):
        o32_ref, o16_ref = rest[-2:]
        t = pl.program_id(1)
        upd = lax.dot_general(_act(l_ref[...], act), r_ref[...].astype(BF16), TN, preferred_element_type=F32)

        @pl.when(t == 0)
        def _():
            o32_ref[...] = upd

        @pl.when(t > 0)
        def _():
            o32_ref[...] += upd

        @pl.when(t == n_t - 1)
        def _():
            o16_ref[...] = o32_ref[...].astype(BF16)

    if shard_lhs:
        in_specs = [pl.BlockSpec((tt, rows), lambda s, t: (t, s)), pl.BlockSpec((tt, cols), lambda s, t: (t, 0))]
    else:
        in_specs = [pl.BlockSpec((tt, rows), lambda s, t: (t, 0)), pl.BlockSpec((tt, cols), lambda s, t: (t, s))]
    args = [lhs, rhs]
    aliases = {}
    if bufs is not None:
        in_specs += [pl.BlockSpec(memory_space=pl.ANY)] * 2
        args += list(bufs)
        aliases = {2: 0, 3: 1}
    shape = (N_CHIPS, n_layers, rows, cols)
    return pl.pallas_call(
        body,
        name=name,
        grid=(N_CHIPS, n_t),
        in_specs=in_specs,
        out_specs=[pl.BlockSpec((None, None, rows, cols), lambda s, t: (s, layer, 0, 0))] * 2,
        out_shape=[jax.ShapeDtypeStruct(shape, F32), jax.ShapeDtypeStruct(shape, BF16)],
        input_output_aliases=aliases,
        compiler_params=_params(2),
    )(*args)


def _chip_peers(x, y):
    return [(1 - x, y), (x, 1 - y), (1 - x, 1 - y)]


def _remote(src, dst, sems, s, peer):
    return pltpu.make_async_remote_copy(
        src_ref=src, dst_ref=dst, send_sem=sems[0].at[s], recv_sem=sems[1].at[s], device_id=peer, device_id_type=MESH
    )


class _Exchange:
    def __init__(self, operands, n_alias, new_shapes, n_sems, build):
        self.operands, self.n_alias, self.new_shapes, self.n_sems, self.build = operands, n_alias, new_shapes, n_sems, build

    def out_shapes(self):
        return [jax.ShapeDtypeStruct(a.shape, a.dtype) for a in self.operands[: self.n_alias]] + list(self.new_shapes)

    def scratch(self):
        return [pltpu.SemaphoreType.DMA((n,)) for n in self.n_sems]

    def run(self, ins, outs, sems, first, last):
        starts, recvs, sends, locals_ = self.build(ins, outs, sems)

        def start_all():
            for cp in starts:
                cp.start()

        def wait_all():
            for cp in recvs:
                cp.wait_recv()
            for cp in sends:
                cp.wait_send()
            for cp in locals_:
                cp.wait()

        if first is True:
            start_all()
            return wait_all
        pl.when(first)(start_all)
        return lambda: pl.when(last)(wait_all)


def _gather_exchange(shards, bufs, plan):
    n_arr = len(shards)
    n_cp = len(plan) * (N_CHIPS - 1)

    def build(ins, outs, sems):
        shard_refs = ins[-n_arr:]
        x, y, c = lax.axis_index("x"), lax.axis_index("y"), lax.axis_index("c")
        me = 2 * x + y
        recvs, sends, locals_ = [], [], []
        for p, (a, l0, n) in enumerate(plan):
            src = shard_refs[a].at[pl.ds(l0, n)]
            cp = pltpu.make_async_copy(src, outs[a].at[me, pl.ds(l0, n)], sems[2].at[p])
            locals_.append(cp)
            for k, (px, py) in enumerate(_chip_peers(x, y)):
                s = p * (N_CHIPS - 1) + k
                sends.append(_remote(src, outs[a].at[me, pl.ds(l0, n)], sems, s, (px, py, c)))
                recvs.append(_remote(src, outs[a].at[2 * px + py, pl.ds(l0, n)], sems, s, (px, py, c)))
        return locals_ + sends, recvs, sends, locals_

    if bufs is None:
        new = [jax.ShapeDtypeStruct((N_CHIPS,) + s.shape, s.dtype) for s in shards]
        return _Exchange(list(shards), 0, new, [n_cp, n_cp, len(plan)], build)
    return _Exchange(list(bufs) + list(shards), n_arr, [], [n_cp, n_cp, len(plan)], build)


def _scatter_exchange(g16, recv, plan, small=None):
    n_arr = len(g16)
    have = [r for r in recv if r is not None]
    made = [a for a in range(n_arr) if recv[a] is None]
    n_cp = len(plan) * (N_CHIPS - 1) + (N_DEV - 1 if small is not None else 0)

    def build(ins, outs, sems):
        g_refs = ins[:n_arr]
        recv_refs, it_have, it_made = [], iter(outs[n_arr : n_arr + len(have)]), iter(outs[n_arr + len(have) :])
        for a in range(n_arr):
            recv_refs.append(next(it_made) if recv[a] is None else next(it_have))
        x, y, c = lax.axis_index("x"), lax.axis_index("y"), lax.axis_index("c")
        me = 2 * x + y
        recvs, sends, locals_ = [], [], []
        for p, (a, l0, n) in enumerate(plan):
            for k, (px, py) in enumerate(_chip_peers(x, y)):
                s = p * (N_CHIPS - 1) + k
                dst = recv_refs[a].at[k, pl.ds(l0, n)]
                sends.append(_remote(g_refs[a].at[2 * px + py, pl.ds(l0, n)], dst, sems, s, (px, py, c)))
                recvs.append(_remote(g_refs[a].at[me, pl.ds(l0, n)], dst, sems, s, (px, py, c)))
        if small is not None:
            small_ref, all_ref = ins[-1], outs[-1]
            slot = 4 * x + 2 * y + c
            locals_.append(pltpu.make_async_copy(small_ref, all_ref.at[slot], sems[2].at[0]))
            flips = [(fx, fy, fc) for fx in (0, 1) for fy in (0, 1) for fc in (0, 1)][1:]
            for k, (fx, fy, fc) in enumerate(flips):
                s = len(plan) * (N_CHIPS - 1) + k
                px, py, pc = x ^ fx, y ^ fy, c ^ fc
                sends.append(_remote(small_ref, all_ref.at[slot], sems, s, (px, py, pc)))
                recvs.append(_remote(small_ref, all_ref.at[4 * px + 2 * py + pc], sems, s, (px, py, pc)))
        return locals_ + sends, recvs, sends, locals_

    operands = list(g16) + have + ([small] if small is not None else [])
    new = [jax.ShapeDtypeStruct((N_CHIPS - 1,) + g16[a].shape[1:], BF16) for a in made]
    if small is not None:
        new.append(jax.ShapeDtypeStruct((N_DEV,) + small.shape, F32))
    return _Exchange(operands, n_arr + len(have), new, [n_cp, n_cp, 1], build)


def _call_with_exchange(body, exch, name, grid, in_specs, out_specs, out_shape, scratch_shapes, args):
    n_in, n_out, n_scr = len(in_specs), len(out_shape), len(scratch_shapes)
    if exch is None:
        outs = pl.pallas_call(
            body, name=name, grid=grid, in_specs=in_specs, out_specs=out_specs, out_shape=out_shape,
            scratch_shapes=scratch_shapes, compiler_params=_params(len(grid)),
        )(*args)
        return outs, []
    e_shapes = exch.out_shapes()
    e_in, e_out = len(exch.operands), len(e_shapes)

    def wrapped(*refs):
        ins, refs = refs[:n_in], refs[n_in:]
        e_ins, refs = refs[:e_in], refs[e_in:]
        outs, refs = refs[:n_out], refs[n_out:]
        e_outs, refs = refs[:e_out], refs[e_out:]
        scr, sems = refs[:n_scr], refs[n_scr:]
        first, last = True, True
        for d, g in enumerate(grid):
            first = (pl.program_id(d) == 0) & first
            last = (pl.program_id(d) == g - 1) & last
        finish = exch.run(e_ins, e_outs, sems, first, last)
        body(*ins, *outs, *scr)
        finish()

    any_spec = pl.BlockSpec(memory_space=pl.ANY)
    outs = pl.pallas_call(
        wrapped,
        name=name,
        grid=grid,
        in_specs=list(in_specs) + [any_spec] * e_in,
        out_specs=list(out_specs) + [any_spec] * e_out,
        out_shape=list(out_shape) + e_shapes,
        input_output_aliases={n_in + i: n_out + i for i in range(exch.n_alias)},
        scratch_shapes=list(scratch_shapes) + exch.scratch(),
        compiler_params=pltpu.CompilerParams(
            dimension_semantics=("arbitrary",) * len(grid), vmem_limit_bytes=VMEM_LIMIT, has_side_effects=True
        ),
    )(*args, *exch.operands)
    return outs[:n_out], outs[n_out:]


def _exchange_only(exch, name):
    n_in = len(exch.operands)
    shapes = exch.out_shapes()

    def body(*refs):
        ins, outs, sems = refs[:n_in], refs[n_in : n_in + len(shapes)], refs[n_in + len(shapes) :]
        exch.run(ins, outs, sems, True, True)()

    any_spec = pl.BlockSpec(memory_space=pl.ANY)
    return pl.pallas_call(
        body,
        name=name,
        in_specs=[any_spec] * n_in,
        out_specs=[any_spec] * len(shapes),
        out_shape=shapes,
        input_output_aliases={i: i for i in range(exch.n_alias)},
        scratch_shapes=exch.scratch(),
        compiler_params=pltpu.CompilerParams(has_side_effects=True),
    )(*exch.operands)


ATTN_LANE_TILES = 2
ATTN_FWD_UNROLL = 34
ATTN_BWD_UNROLL = 17


MASKED = -1e30


def _hi_lo(x):
    hi = x.astype(BF16)
    lo = (x - hi.astype(F32)).astype(BF16)
    return jnp.concatenate([hi, lo], axis=1)


def _suffix_matrix(inclusive):
    j = lax.broadcasted_iota(jnp.int32, (2 * Q_TILE, 2 * Q_TILE), 0) & (Q_TILE - 1)
    s = lax.broadcasted_iota(jnp.int32, (2 * Q_TILE, 2 * Q_TILE), 1)
    later = (j >= s) if inclusive else (j > s)
    return jnp.where((s >= Q_TILE) | later, 1.0, 0.0).astype(BF16)


def _log_beta(z):
    return jnp.minimum(z, 0.0) - jnp.log(1.0 + jnp.exp(-jnp.abs(z)))


def _head_masks(width):
    lane = lax.broadcasted_iota(jnp.int32, (1, width), 1)
    return [(lane >= h * HEAD_DIM) & (lane < (h + 1) * HEAD_DIM) for h in range(width // HEAD_DIM)]


def _per_head_rows(x, masks):
    return jnp.concatenate([jnp.where(hm, x, 0) for hm in masks], axis=0)


def _heads_to_lanes(x, n_heads):
    return jnp.concatenate([x[h * Q_TILE : (h + 1) * Q_TILE] for h in range(n_heads)], axis=1)


def _block_start(kb):
    return kb * Q_TILE if isinstance(kb, int) else pl.multiple_of(kb * Q_TILE, Q_TILE)


def _clamp(i, n):
    return jnp.minimum(i, n - 1)


def _next_block(pos):
    qi, kb = pos
    row_done = kb == 0
    nqi = jnp.where(row_done, qi + 1, qi)
    return nqi, jnp.where(row_done, nqi, kb - 1)


def _stream_unroll(n_blocks, wanted):
    return next(u for u in (wanted, 2, 1) if n_blocks % u == 0)


def _past_mask(rows):
    t = lax.broadcasted_iota(jnp.int32, (rows, Q_TILE), 0) & (Q_TILE - 1)
    s = lax.broadcasted_iota(jnp.int32, (rows, Q_TILE), 1)
    return s < t


def _attn_fwd(qkv, n_seq, S, D, exch, name):
    T = n_seq * S
    width = min(D, ATTN_LANE_TILES * LANES)
    n_heads = width // HEAD_DIM
    rows = n_heads * Q_TILE
    nq = S // Q_TILE
    groups = D // width
    n_blocks = nq * (nq + 1) // 2
    unroll = _stream_unroll(n_blocks, ATTN_FWD_UNROLL)
    scale = HEAD_DIM ** -0.5
    n_trips = n_blocks // unroll

    def body(q_ref, k_ref, v_ref, o_ref, a_out, b_out, qh_scr, vh_scr, bias_scr, a_stage, b_stage, sems):
        masks = _head_masks(width)
        sfx = _suffix_matrix(False)
        stream = pl.program_id(0) * groups + pl.program_id(1)

        def per_head_tables(i, c):
            blk = pl.ds(_block_start(i), Q_TILE)
            qh_scr[i] = _per_head_rows(q_ref[blk, :] * scale, masks)
            vh_scr[i] = _per_head_rows(v_ref[blk, :], masks)
            return c

        lax.fori_loop(0, nq, per_head_tables, 0)
        bias_scr[0] = jnp.zeros((rows, Q_TILE), F32)
        bias_scr[1] = jnp.where(_past_mask(rows), 0.0, MASKED)

        def save(n, slot):
            blocks = pl.ds(n * unroll, unroll)
            return [
                pltpu.make_async_copy(a_stage.at[slot], a_out.at[stream, blocks], sems.at[slot]),
                pltpu.make_async_copy(b_stage.at[slot], b_out.at[stream, blocks], sems.at[2 + slot]),
            ]

        def scores(pos):
            qi, kb = pos
            kt = k_ref[pl.ds(_block_start(_clamp(kb, nq)), Q_TILE), :]
            z = lax.dot_general(qh_scr[_clamp(qi, nq)], kt, NT, preferred_element_type=F32)
            z = z + bias_scr[(kb == qi).astype(jnp.int32)]
            lb = _log_beta(z)
            return lb, _hi_lo(lb - z)

        def weigh(pos, st, carry, acc, slot, u):
            qi, kb = pos
            lb, l1 = st
            r = jnp.dot(l1, sfx, preferred_element_type=F32)
            carry = jnp.where(kb == qi, 0.0, carry)
            a = jnp.exp(lb + r[:, :Q_TILE] + carry).astype(BF16)
            a_stage[slot, u] = a
            b_stage[slot, u] = jnp.exp(lb).astype(BF16)
            acc = jnp.where(kb == qi, 0.0, acc) + jnp.dot(
                _heads_to_lanes(a, n_heads), vh_scr[_clamp(kb, nq)], preferred_element_type=F32
            )
            o_ref[pl.ds(_block_start(_clamp(qi, nq)), Q_TILE), :] = acc
            return carry + r[:, Q_TILE:], acc

        def trip(n, c):
            pos, st, carry, acc = c
            slot = n % 2

            @pl.when((n >= 2) | (stream > 0))
            def _():
                for cp in save(0, slot):
                    cp.wait()

            for u in range(unroll):
                nxt = _next_block(pos)
                st_nxt = scores(nxt)
                carry, acc = weigh(pos, st, carry, acc, slot, u)
                pos, st = nxt, st_nxt
            for cp in save(n, slot):
                cp.start()
            return pos, st, carry, acc

        first = (jnp.int32(0), jnp.int32(0))
        zero = bias_scr[0]
        init = (first, scores(first), zero, jnp.concatenate([zero[:Q_TILE]] * (width // Q_TILE), axis=1))
        lax.fori_loop(0, n_trips, trip, init)

        @pl.when(stream == n_seq * groups - 1)
        def _():
            for slot in range(min(2, n_trips)):
                for cp in save(0, slot):
                    cp.wait()

    seq = lambda col0: pl.BlockSpec((S, width), lambda b, p: (b, col0 + p))
    saved = jax.ShapeDtypeStruct((n_seq * groups, n_blocks, rows, Q_TILE), BF16)
    stage = pltpu.VMEM((2, unroll, rows, Q_TILE), BF16)
    (o, a_w, beta), moved = _call_with_exchange(
        body,
        exch,
        name,
        grid=(n_seq, groups),
        in_specs=[seq(0), seq(groups), seq(2 * groups)],
        out_specs=[seq(0), pl.BlockSpec(memory_space=pl.ANY), pl.BlockSpec(memory_space=pl.ANY)],
        out_shape=[jax.ShapeDtypeStruct((T, D), F32), saved, saved],
        scratch_shapes=[
            pltpu.VMEM((nq, rows, width), BF16),
            pltpu.VMEM((nq, rows, width), BF16),
            pltpu.VMEM((2, rows, Q_TILE), F32),
            stage,
            stage,
            pltpu.SemaphoreType.DMA((4,)),
        ],
        args=(qkv, qkv, qkv),
    )
    return (o, a_w, beta), moved


def _attn_bwd(qkv, fwd, do, n_seq, S, D, exch, name):
    o, a_w, beta = fwd
    T = n_seq * S
    width = min(D, ATTN_LANE_TILES * LANES)
    n_heads = width // HEAD_DIM
    rows = n_heads * Q_TILE
    nq = S // Q_TILE
    groups = D // width
    n_blocks = nq * (nq + 1) // 2
    unroll = _stream_unroll(n_blocks, ATTN_BWD_UNROLL)
    scale = HEAD_DIM ** -0.5
    n_trips = n_blocks // unroll

    def body(q_ref, k_ref, v_ref, o_ref, do_ref, a_in, b_in, dq_ref, dk_ref, dv_ref, dk_acc, dv_acc, qh_scr, doh_scr, delta_scr, a_stage, b_stage, sems):
        masks = _head_masks(width)
        sfx_incl = _suffix_matrix(True)
        stream = pl.program_id(0) * groups + pl.program_id(1)

        n_streams = n_seq * groups
        ahead = n_trips % 2 == 0

        def fetch(s, n, slot):
            blocks = pl.ds(n * unroll, unroll)
            return [
                pltpu.make_async_copy(a_in.at[s, blocks], a_stage.at[slot], sems.at[slot]),
                pltpu.make_async_copy(b_in.at[s, blocks], b_stage.at[slot], sems.at[2 + slot]),
            ]

        @pl.when((stream == 0) | (not ahead))
        def _():
            for cp in fetch(stream, 0, 0):
                cp.start()

        dk_acc[...] = jnp.zeros_like(dk_acc)
        dv_acc[...] = jnp.zeros_like(dv_acc)

        lane = lax.broadcasted_iota(jnp.int32, (2 * width, rows), 0) % width
        col = lax.broadcasted_iota(jnp.int32, (2 * width, rows), 1)
        head_sums = jnp.where(lane // HEAD_DIM == col // Q_TILE, 1.0, 0.0).astype(BF16)

        def per_head_tables(i, c):
            blk = pl.ds(_block_start(i), Q_TILE)
            do = do_ref[blk, :]
            qh_scr[i] = _per_head_rows(q_ref[blk, :] * scale, masks)
            doh_scr[i] = _per_head_rows(do, masks)
            prod = do.astype(F32) * o_ref[blk, :]
            d = jnp.dot(_hi_lo(prod), head_sums, preferred_element_type=F32)
            delta_scr[i] = jnp.concatenate([d[:, h * Q_TILE : (h + 1) * Q_TILE] for h in range(n_heads)], axis=0)
            return c

        lax.fori_loop(0, nq, per_head_tables, 0)

        def weigh(pos, ab, beta, c2, dq):
            qi, kb = pos
            first = kb == qi
            blk = pl.ds(_block_start(kb), Q_TILE)
            g = ab.astype(F32) * lax.dot_general(doh_scr[qi], v_ref[blk, :], NT, preferred_element_type=F32)
            r2 = jnp.dot(_hi_lo(g), sfx_incl, preferred_element_type=F32)
            c2 = jnp.where(first, 0.0, c2)
            earlier = delta_scr[qi] - (r2[:, :Q_TILE] + c2)
            beta = beta.astype(F32)
            dzb = (g * (1.0 - beta) - earlier * beta).astype(BF16)
            kh = _per_head_rows(k_ref[blk, :], masks)
            dq = jnp.where(first, 0.0, dq) + jnp.dot(_heads_to_lanes(dzb, n_heads), kh, preferred_element_type=F32)
            dq_ref[pl.ds(_block_start(qi), Q_TILE), :] = (dq * scale).astype(BF16)
            dk_acc[blk, :] += lax.dot_general(dzb, qh_scr[qi], TN, preferred_element_type=F32)
            dv_acc[blk, :] += lax.dot_general(ab, doh_scr[qi], TN, preferred_element_type=F32)
            return c2 + r2[:, Q_TILE:], dq

        def trip(n, c):
            pos, c2, dq = c
            slot = n % 2
            for cp in fetch(stream, n, slot):
                cp.wait()
            more = n + 1 < n_trips
            if ahead:
                nxt = (jnp.where(more, stream, stream + 1), jnp.where(more, n + 1, 0))
                more = more | (stream + 1 < n_streams)
            else:
                nxt = (stream, n + 1)

            @pl.when(more)
            def _():
                for cp in fetch(*nxt, 1 - slot):
                    cp.start()

            for u in range(unroll):
                c2, dq = weigh(pos, a_stage[slot, u], b_stage[slot, u], c2, dq)
                pos = _next_block(pos)
            return pos, c2, dq

        zero = dk_acc[pl.ds(0, Q_TILE), :]
        init = ((jnp.int32(0), jnp.int32(0)), jnp.concatenate([zero[:, :Q_TILE]] * n_heads, axis=0), zero)
        lax.fori_loop(0, n_trips, trip, init)
        dk_ref[...] = dk_acc[...].astype(BF16)
        dv_ref[...] = dv_acc[...].astype(BF16)

    seq = lambda col0: pl.BlockSpec((S, width), lambda b, p: (b, col0 + p))
    return _call_with_exchange(
        body,
        exch,
        name,
        grid=(n_seq, groups),
        in_specs=[seq(0), seq(groups), seq(2 * groups), seq(0), seq(0)] + [pl.BlockSpec(memory_space=pl.ANY)] * 2,
        out_specs=[seq(0)] * 3,
        out_shape=[jax.ShapeDtypeStruct((T, D), BF16)] * 3,
        scratch_shapes=[
            pltpu.VMEM((S, width), F32),
            pltpu.VMEM((S, width), F32),
            pltpu.VMEM((nq, rows, width), BF16),
            pltpu.VMEM((nq, rows, width), BF16),
            pltpu.VMEM((nq, rows, Q_TILE), F32),
            pltpu.VMEM((2, unroll, rows, Q_TILE), BF16),
            pltpu.VMEM((2, unroll, rows, Q_TILE), BF16),
            pltpu.SemaphoreType.DMA((4,)),
        ],
        args=(qkv, qkv, qkv, o, do, a_w, beta),
    )


def _causal_ws(ws_ref, g):
    t = lax.broadcasted_iota(jnp.int32, (SGU_CHUNK, SGU_CHUNK), 0)
    s = lax.broadcasted_iota(jnp.int32, (SGU_CHUNK, SGU_CHUNK), 1)
    return jnp.where(s <= t, ws_ref[g], 0.0)


def _sgu_chunks_per_step(T):
    return 2 if T % (2 * SGU_CHUNK) == 0 else 1


def _sgu_fwd(a, gain, ws, bsb, name):
    T, F2 = a.shape
    F = F2 // 2
    gw = F // SGU_GROUPS

    step = SGU_CHUNK * _sgu_chunks_per_step(T)

    def body(a_ref, gain_ref, ws_ref, bsb_ref, y_ref):
        for r0 in range(0, step, SGU_CHUNK):
            rs = slice(r0, r0 + SGU_CHUNK)
            v = _gelu(a_ref[rs, F:].astype(F32))
            vn = (v * _rstd(v) * gain_ref[...]).astype(BF16)
            for g in range(SGU_GROUPS):
                cs = slice(g * gw, (g + 1) * gw)
                w = _causal_ws(ws_ref, g).astype(BF16)
                mixed = jnp.dot(w, vn[:, cs], preferred_element_type=F32) + bsb_ref[g]
                y_ref[rs, cs] = (_gelu(a_ref[rs, cs].astype(F32)) * mixed).astype(BF16)

    return pl.pallas_call(
        body,
        name=name,
        grid=(T // step,),
        in_specs=[
            pl.BlockSpec((step, F2), lambda i: (i, 0)),
            pl.BlockSpec((1, F), lambda i: (0, 0)),
            pl.BlockSpec((SGU_GROUPS, SGU_CHUNK, SGU_CHUNK), lambda i: (0, 0, 0)),
            pl.BlockSpec((SGU_GROUPS, SGU_CHUNK, gw), lambda i: (0, 0, 0)),
        ],
        out_specs=pl.BlockSpec((step, F), lambda i: (i, 0)),
        out_shape=jax.ShapeDtypeStruct((T, F), BF16),
        compiler_params=_params(1),
    )(a, gain.reshape(1, F), ws, bsb)


def _sgu_bwd(a, dy, gain, ws, bsb, name):
    T, F2 = a.shape
    F = F2 // 2
    gw = F // SGU_GROUPS
    step = SGU_CHUNK * _sgu_chunks_per_step(T)

    def body(a_ref, dy_ref, gain_ref, ws_ref, bsb_ref, da_ref, dws_ref, dbs_ref, dgain_ref, dvn_ref):
        @pl.when(pl.program_id(0) == 0)
        def _():
            dws_ref[...] = jnp.zeros_like(dws_ref)
            dbs_ref[...] = jnp.zeros_like(dbs_ref)
            dgain_ref[...] = jnp.zeros_like(dgain_ref)

        gain = gain_ref[...]
        ones = jnp.ones((gw, SGU_CHUNK), BF16)
        for r0 in range(0, step, SGU_CHUNK):
            rs = slice(r0, r0 + SGU_CHUNK)
            v, v_slope = _gelu_and_grad(a_ref[rs, F:].astype(F32))
            rstd = _rstd(v)
            vh = v * rstd
            vn = (vh * gain).astype(BF16)
            for g in range(SGU_GROUPS):
                cs = slice(g * gw, (g + 1) * gw)
                w = _causal_ws(ws_ref, g).astype(BF16)
                mixed = jnp.dot(w, vn[:, cs], preferred_element_type=F32) + bsb_ref[g]
                u, u_slope = _gelu_and_grad(a_ref[rs, cs].astype(F32))
                dyc = dy_ref[rs, cs].astype(F32)
                da_ref[rs, cs] = (dyc * mixed * u_slope).astype(BF16)
                dm = (dyc * u).astype(BF16)
                dbs_ref[g] += jnp.dot(dm, ones, preferred_element_type=F32)
                dws_ref[g] += _causal_mask_f32(lax.dot_general(dm, vn[:, cs], NT, preferred_element_type=F32))
                dvn_ref[:, cs] = lax.dot_general(w, dm, TN, preferred_element_type=F32)
            dvn = dvn_ref[...]
            dgain_ref[...] += jnp.sum(dvn * vh, axis=0, keepdims=True)
            dvh = dvn * gain
            dv = rstd * (dvh - vh * jnp.mean(dvh * vh, axis=-1, keepdims=True))
            da_ref[rs, F:] = (dv * v_slope).astype(BF16)

    acc_spec = pl.BlockSpec((SGU_GROUPS, SGU_CHUNK, SGU_CHUNK), lambda i: (0, 0, 0))
    acc_shape = jax.ShapeDtypeStruct((SGU_GROUPS, SGU_CHUNK, SGU_CHUNK), F32)
    return pl.pallas_call(
        body,
        name=name,
        grid=(T // step,),
        in_specs=[
            pl.BlockSpec((step, F2), lambda i: (i, 0)),
            pl.BlockSpec((step, F), lambda i: (i, 0)),
            pl.BlockSpec((1, F), lambda i: (0, 0)),
            acc_spec,
            pl.BlockSpec((SGU_GROUPS, SGU_CHUNK, gw), lambda i: (0, 0, 0)),
        ],
        out_specs=[
            pl.BlockSpec((step, F2), lambda i: (i, 0)),
            acc_spec,
            acc_spec,
            pl.BlockSpec((1, F), lambda i: (0, 0)),
        ],
        out_shape=[
            jax.ShapeDtypeStruct((T, F2), BF16),
            acc_shape,
            acc_shape,
            jax.ShapeDtypeStruct((1, F), F32),
        ],
        scratch_shapes=[pltpu.VMEM((SGU_CHUNK, F), F32)],
        compiler_params=_params(1),
    )(a, dy, gain.reshape(1, F), ws, bsb)


def _causal_mask_f32(m):
    t = lax.broadcasted_iota(jnp.int32, m.shape, 0)
    s = lax.broadcasted_iota(jnp.int32, m.shape, 1)
    return jnp.where(s <= t, m, 0.0)


def _final_loss(x, gain, target, name):
    T, D = x.shape
    tm = min(T, ROW_TILE // 2)

    def body(x_ref, g_ref, t_ref, sq_ref, dx_ref, dxb_ref, dg_ref):
        xv = x_ref[...]
        gain = g_ref[...]
        err = xv * _rstd(xv) * gain - t_ref[...]
        dx, dg = _norm_bwd(err * (1.0 / D), xv, gain)
        dx_ref[...] = dx
        dxb_ref[...] = dx.astype(BF16)
        sq = jnp.sum(err * err, axis=0, keepdims=True)

        @pl.when(pl.program_id(0) == 0)
        def _():
            sq_ref[...] = sq
            dg_ref[...] = dg

        @pl.when(pl.program_id(0) > 0)
        def _():
            sq_ref[...] += sq
            dg_ref[...] += dg

    row = pl.BlockSpec((tm, D), lambda i: (i, 0))
    vec = pl.BlockSpec((1, D), lambda i: (0, 0))
    return pl.pallas_call(
        body,
        name=name,
        grid=(T // tm,),
        in_specs=[row, vec, row],
        out_specs=[vec, row, row, vec],
        out_shape=[
            jax.ShapeDtypeStruct((1, D), F32),
            jax.ShapeDtypeStruct((T, D), F32),
            jax.ShapeDtypeStruct((T, D), BF16),
            jax.ShapeDtypeStruct((1, D), F32),
        ],
        compiler_params=_params(1),
    )(x, gain.reshape(1, D), target)


def _row_tile(rows, cols, n_arrays):
    budget = VMEM_LIMIT // 2 // (2 * n_arrays * cols * 4)
    tr = rows
    while tr > budget and tr % 16 == 0:
        tr //= 2
    return tr


def _sum_received(own, recv, name):
    R, C = own.shape
    n = recv.shape[0]
    tr = _row_tile(R, C, n + 2)

    def body(own_ref, recv_ref, o_ref):
        s = own_ref[...]
        for k in range(n):
            s = s + recv_ref[k].astype(F32)
        o_ref[...] = s

    return pl.pallas_call(
        body,
        name=name,
        grid=(R // tr,),
        in_specs=[pl.BlockSpec((tr, C), lambda i: (i, 0)), pl.BlockSpec((n, tr, C), lambda i: (0, i, 0))],
        out_specs=pl.BlockSpec((tr, C), lambda i: (i, 0)),
        out_shape=jax.ShapeDtypeStruct((R, C), F32),
        compiler_params=_params(1),
    )(own, recv)


def _sum_chip_shard(g32, recv, chip, name):
    _, L, r, c = g32.shape
    n = recv.shape[0]
    tr = _row_tile(r, c, n + 2)

    def body(chip_ref, own_ref, recv_ref, o_ref):
        s = own_ref[...]
        for k in range(n):
            s = s + recv_ref[k].astype(F32)
        o_ref[...] = s

    return pl.pallas_call(
        body,
        name=name,
        grid_spec=pltpu.PrefetchScalarGridSpec(
            num_scalar_prefetch=1,
            grid=(L, r // tr),
            in_specs=[
                pl.BlockSpec((None, None, tr, c), lambda l, i, chip_ref: (chip_ref[0], l, i, 0)),
                pl.BlockSpec((n, None, tr, c), lambda l, i, chip_ref: (0, l, i, 0)),
            ],
            out_specs=pl.BlockSpec((None, tr, c), lambda l, i, chip_ref: (l, i, 0)),
        ),
        out_shape=jax.ShapeDtypeStruct((L, r, c), F32),
        compiler_params=_params(2),
    )(chip.reshape(1).astype(jnp.int32), g32, recv)


def _adamw(w, m, v, parts, name):
    R, C = w.shape
    n = len(parts)
    tr = _row_tile(R, C, n + 7)

    def body(*refs):
        w_ref, m_ref, v_ref = refs[:3]
        g_ref, d_ref, nm_ref, nv_ref = refs[3 + n :]
        g = refs[3][...]
        for p_ref in refs[4 : 3 + n]:
            g = g + p_ref[...]
        nm = ADAM_B1 * m_ref[...] + (1.0 - ADAM_B1) * g
        nv = ADAM_B2 * v_ref[...] + (1.0 - ADAM_B2) * (g * g)
        m_hat = nm / (1.0 - ADAM_B1**ADAM_STEP)
        v_hat = nv / (1.0 - ADAM_B2**ADAM_STEP)
        g_ref[...] = g
        d_ref[...] = -ADAM_LR * (m_hat / (jnp.sqrt(v_hat) + ADAM_EPS) + ADAM_WD * w_ref[...])
        nm_ref[...] = nm
        nv_ref[...] = nv

    spec = pl.BlockSpec((tr, C), lambda i: (i, 0))
    return pl.pallas_call(
        body,
        name=name,
        grid=(R // tr,),
        in_specs=[spec] * (3 + n),
        out_specs=[spec] * 4,
        out_shape=[jax.ShapeDtypeStruct((R, C), F32)] * 4,
        compiler_params=_params(1),
    )(w, m, v, *parts)


def _swap_with_sibling(parts, name):
    n = len(parts)

    def body(*refs):
        ins, outs = refs[:n], refs[n : 2 * n]
        send_sems, recv_sems = refs[2 * n :]
        sibling = (lax.axis_index("x"), lax.axis_index("y"), 1 - lax.axis_index("c"))
        copies = [
            pltpu.make_async_remote_copy(
                src_ref=ins[a],
                dst_ref=outs[a],
                send_sem=send_sems.at[a],
                recv_sem=recv_sems.at[a],
                device_id=sibling,
                device_id_type=MESH,
            )
            for a in range(n)
        ]
        for cp in copies:
            cp.start()
        for cp in copies:
            cp.wait_recv()
        for cp in copies:
            cp.wait_send()

    any_spec = pl.BlockSpec(memory_space=pl.ANY)
    return pl.pallas_call(
        body,
        name=name,
        in_specs=[any_spec] * n,
        out_specs=[any_spec] * n,
        out_shape=[jax.ShapeDtypeStruct(p.shape, p.dtype) for p in parts],
        scratch_shapes=[pltpu.SemaphoreType.DMA((n,)), pltpu.SemaphoreType.DMA((n,))],
        compiler_params=pltpu.CompilerParams(has_side_effects=True),
    )(*parts)


def _pack(pieces):
    flat = jnp.concatenate([p.reshape(-1) for p in pieces])
    return flat.reshape(-1, LANES)


def _unpack(packed, shapes):
    flat = packed.reshape(-1)
    out, off = [], 0
    for s in shapes:
        size = 1
        for d in s:
            size *= d
        out.append(flat[off : off + size].reshape(s))
        off += size
    return out


def kernel(x, norm_mix, norm_mlp, sb_wqkv, sb_wo, sgu_win, sgu_gain, sgu_ws, sgu_bs, sgu_wout, mlp_w1, mlp_w2, final_norm, loss_target, m_norm_mix, m_norm_mlp, m_sb_wqkv, m_sb_wo, m_sgu_win, m_sgu_gain, m_sgu_ws, m_sgu_bs, m_sgu_wout, m_mlp_w1, m_mlp_w2, m_final_norm, v_norm_mix, v_norm_mlp, v_sb_wqkv, v_sb_wo, v_sgu_win, v_sgu_gain, v_sgu_ws, v_sgu_bs, v_sgu_wout, v_mlp_w1, v_mlp_w2, v_final_norm):
    n_seq, S, D = x.shape
    T = n_seq * S
    depth = norm_mix.shape[0]
    n_sgu = sgu_win.shape[0]
    F = sgu_wout.shape[1] * N_CHIPS
    gw = F // SGU_GROUPS
    chip = 2 * lax.axis_index("x") + lax.axis_index("y")

    QKV, WO, WIN, WOUT, W1, W2, GAIN = range(7)
    big = [sb_wqkv, sb_wo, sgu_win, sgu_wout, mlp_w1, mlp_w2]
    n_sb = sb_wqkv.shape[0]
    shards = [w.astype(BF16) for w in big] + [sgu_gain.reshape(1, -1, LANES)]

    def gather_plan(i):
        j, mlp = i // 2, min(2, depth - i)
        plan = [(WO, j, 1), (W1, i, mlp), (W2, i, mlp)]
        if i + 1 < depth:
            plan += [(WIN, (i + 1) // 2, 1), (WOUT, (i + 1) // 2, 1)]
        if j + 1 < n_sb:
            plan += [(QKV, j + 1, 1)]
        return plan

    wg = _exchange_only(_gather_exchange(shards, None, [(QKV, 0, 1), (GAIN, 0, 1)]), "gather_first_weights")
    gain_full = jnp.transpose(wg[GAIN].reshape(N_CHIPS, n_sgu, F // N_CHIPS), (1, 0, 2)).reshape(n_sgu, F)
    bsb = [jnp.broadcast_to(sgu_bs[j][:, :, None], (SGU_GROUPS, SGU_CHUNK, gw)) for j in range(n_sgu)]

    xs = x.reshape(T, D)
    saved = []
    for i in range(depth):
        j = i // 2
        if i % 2 == 0:
            qkv, h = _norm_matmul(xs, norm_mix[i], wg[QKV], j, f"qkv_fwd_{i}")
            attn, wg = _attn_fwd(qkv, n_seq, S, D, _gather_exchange(shards, wg, gather_plan(i)), f"attn_fwd_{i}")
            wg_qkv, wg_wo, wg_win, wg_wout, wg_w1, wg_w2 = wg[:6]
            x_mid = _act_matmul_res(attn[0], wg_wo, j, xs, None, f"wo_fwd_{i}")
            mix = (qkv, attn)
        else:
            a, h = _norm_matmul(xs, norm_mix[i], wg_win, j, f"win_fwd_{i}")
            yg = _sgu_fwd(a, gain_full[j], sgu_ws[j], bsb[j], f"sgu_fwd_{i}")
            x_mid = _act_matmul_res(yg, wg_wout, j, xs, None, f"wout_fwd_{i}")
            mix = (a, yg)
        a2, h2 = _norm_matmul(x_mid, norm_mlp[i], wg_w1, i, f"w1_fwd_{i}")
        x_out = _act_matmul_res(a2, wg_w2, i, x_mid, "relu2", f"w2_fwd_{i}")
        saved.append((xs, h, mix, x_mid, h2, a2))
        xs = x_out

    sq, dx, dxb, g_final = _final_loss(xs, final_norm, loss_target.reshape(T, D), "loss_head")
    loss = lax.psum(0.5 * jnp.sum(sq) / D, ("x", "y", "c"))

    n_layers = [n_sb, n_sb, n_sgu, n_sgu, depth, depth]
    g32, g16, recv = [None] * 6, [None] * 6, [None] * 6
    done_from, sent_from = list(n_layers), list(n_layers)

    def grad(a, layer, lhs, rhs, shard_lhs, act, name):
        bufs = None if g32[a] is None else (g32[a], g16[a])
        g32[a], g16[a] = _matmul_tn(lhs, rhs, bufs, layer, n_layers[a], shard_lhs, act, name)
        done_from[a] = layer

    def unsent_plan():
        plan = [(a, done_from[a], sent_from[a] - done_from[a]) for a in range(6) if sent_from[a] > done_from[a]]
        for a, l0, _ in plan:
            sent_from[a] = l0
        return plan

    def scatter(plan, small):
        if not plan and small is None:
            return None, lambda moved: None
        arrays = sorted({a for a, _, _ in plan})
        have = [a for a in arrays if recv[a] is not None]
        made = [a for a in arrays if recv[a] is None]
        exch = _scatter_exchange(
            [g16[a] for a in arrays], [recv[a] for a in arrays], [(arrays.index(a), l0, n) for a, l0, n in plan], small
        )

        def take(moved):
            for a, buf in zip(arrays, moved):
                g16[a] = buf
            for a, buf in zip(have + made, moved[len(arrays) :]):
                recv[a] = buf
            return moved[-1]

        return exch, take

    g_mix, g_mlp = [None] * depth, [None] * depth
    g_ws, g_bs, g_gain = [None] * n_sgu, [None] * n_sgu, [None] * n_sgu
    for i in reversed(range(depth)):
        j = i // 2
        x_in, h, mix, x_mid, h2, a2 = saved[i]
        da2 = _matmul_nt(dxb, wg_w2, i, a2, f"w2_bwd_{i}")
        grad(W2, i, a2, dxb, True, "relu2", f"w2_grad_{i}")
        grad(W1, i, h2, da2, False, None, f"w1_grad_{i}")
        (dx, dxb, g_mlp[i]), _ = _matmul_nt_norm_bwd(da2, wg_w1, i, x_mid, norm_mlp[i], dx, None, f"w1_bwd_{i}")
        if i % 2 == 0:
            qkv, attn = mix
            do = _matmul_nt(dxb, wg_wo, j, None, f"wo_bwd_{i}")
            grad(WO, j, attn[0], dxb, True, None, f"wo_grad_{i}")
            sgu_small = _pack([jnp.stack(g_ws), jnp.stack(g_bs), jnp.stack(g_gain)]) if i == 0 and n_sgu else None
            exch, take = scatter(unsent_plan(), sgu_small)
            (dq, dk, dv), moved = _attn_bwd(qkv, attn, do, n_seq, S, D, exch, f"attn_bwd_{i}")
            last = take(moved)
            if sgu_small is not None:
                sgu_small_all = last
            dqkv = jnp.concatenate([dq, dk, dv], axis=1)
            grad(QKV, j, h, dqkv, False, None, f"qkv_grad_{i}")
            exch, take = scatter(unsent_plan(), None)
            (dx, dxb, g_mix[i]), moved = _matmul_nt_norm_bwd(
                dqkv, wg_qkv, j, x_in, norm_mix[i], dx, exch, f"qkv_bwd_{i}"
            )
            take(moved)
        else:
            a, yg = mix
            dyg = _matmul_nt(dxb, wg_wout, j, None, f"wout_bwd_{i}")
            grad(WOUT, j, yg, dxb, True, None, f"wout_grad_{i}")
            da, g_ws[j], dbs, g_gain[j] = _sgu_bwd(a, dyg, gain_full[j], sgu_ws[j], bsb[j], f"sgu_bwd_{i}")
            g_bs[j] = dbs[:, :, 0]
            grad(WIN, j, h, da, False, None, f"win_grad_{i}")
            (dx, dxb, g_mix[i]), _ = _matmul_nt_norm_bwd(da, wg_win, j, x_in, norm_mix[i], dx, None, f"win_bwd_{i}")
    grad_x = dx.reshape(n_seq, S, D)

    names = ["qkv", "wo", "win", "wout", "w1", "w2"]
    exch, take = scatter(unsent_plan(), _pack([jnp.stack(g_mix), jnp.stack(g_mlp), g_final]))
    norm_small_all = take(_exchange_only(exch, "gather_norm_grads"))
    partial = [_sum_chip_shard(g32[a], recv[a], chip, f"sum_{names[a]}") for a in range(6)]
    partial = [p.reshape(-1, p.shape[-1]) for p in partial]
    theirs = _swap_with_sibling(partial, "swap_partial_sums")
    g_small = _unpack(
        _sum_received(norm_small_all[0], norm_small_all[1:], "sum_norm_grads"),
        [norm_mix.shape, norm_mlp.shape, final_norm.shape],
    ) + _unpack(
        _sum_received(sgu_small_all[0], sgu_small_all[1:], "sum_sgu_small_grads"),
        [sgu_ws.shape, sgu_bs.shape, (n_sgu, F)],
    )

    ms = [m_sb_wqkv, m_sb_wo, m_sgu_win, m_sgu_wout, m_mlp_w1, m_mlp_w2]
    vs = [v_sb_wqkv, v_sb_wo, v_sgu_win, v_sgu_wout, v_mlp_w1, v_mlp_w2]
    res = {}
    keys = ["sb_wqkv", "sb_wo", "sgu_win", "sgu_wout", "mlp_w1", "mlp_w2"]
    for key, k, w, m, v, mine, other in zip(keys, names, big, ms, vs, partial, theirs):
        cols = w.shape[-1]
        outs = _adamw(w.reshape(-1, cols), m.reshape(-1, cols), v.reshape(-1, cols), [mine, other], f"adamw_{k}")
        res[key] = [o.reshape(w.shape) for o in outs]

    g_small[5] = lax.dynamic_slice_in_dim(g_small[5], chip * (F // N_CHIPS), F // N_CHIPS, axis=1)
    small_keys = ["norm_mix", "norm_mlp", "final_norm", "sgu_ws", "sgu_bs", "sgu_gain"]
    small_w = [norm_mix, norm_mlp, final_norm, sgu_ws, sgu_bs, sgu_gain]
    small_m = [m_norm_mix, m_norm_mlp, m_final_norm, m_sgu_ws, m_sgu_bs, m_sgu_gain]
    small_v = [v_norm_mix, v_norm_mlp, v_final_norm, v_sgu_ws, v_sgu_bs, v_sgu_gain]
    outs = _adamw(_pack(small_w), _pack(small_m), _pack(small_v), [_pack(g_small)], "adamw_small")
    local_shapes = [w.shape for w in small_w]
    for key, parts in zip(small_keys, zip(*[_unpack(o, local_shapes) for o in outs])):
        res[key] = list(parts)

    order = ["norm_mix", "norm_mlp", "sb_wqkv", "sb_wo", "sgu_win", "sgu_gain", "sgu_ws", "sgu_bs", "sgu_wout", "mlp_w1", "mlp_w2", "final_norm"]
    return (loss, grad_x, *[res[k][0] for k in order], *[res[k][1] for k in order], *[res[k][2] for k in order], *[res[k][3] for k in order])
```

```python
import jax
import jax.numpy as jnp
from jax import lax
from jax.experimental import pallas as pl
from jax.experimental.pallas import tpu as pltpu

F32 = jnp.float32
BF16 = jnp.bfloat16
MESH = pl.DeviceIdType.MESH

EPS = 1e-6
HEAD_DIM = 64
LANES = 128
Q_TILE = 128
SGU_CHUNK = 128
SGU_GROUPS = 8
N_CHIPS = 4
N_DEV = 8
ADAM_LR = 0.001
ADAM_B1 = 0.9
ADAM_B2 = 0.999
ADAM_EPS = 1e-08
ADAM_WD = 0.01
ADAM_STEP = 10
GELU_C0 = 0.7978845608028654
GELU_C1 = 0.044715
VMEM_LIMIT = 48 * 1024 * 1024
ROW_TILE = 1024
NT = (((1,), (1,)), ((), ()))
TN = (((0,), (0,)), ((), ()))


def _params(n_axes):
    return pltpu.CompilerParams(dimension_semantics=("arbitrary",) * n_axes, vmem_limit_bytes=VMEM_LIMIT)


def _rstd(x):
    return lax.rsqrt(jnp.mean(x * x, axis=-1, keepdims=True) + EPS)


def _norm_bwd(dh, x, gain):
    rstd = _rstd(x)
    xh = x * rstd
    dhg = dh * gain
    dx = rstd * (dhg - xh * jnp.mean(dhg * xh, axis=-1, keepdims=True))
    return dx, jnp.sum(dh * xh, axis=0, keepdims=True)


def _gelu(x):
    return (0.5 * x) * (1.0 + jnp.tanh(x * (GELU_C0 + (GELU_C0 * GELU_C1) * (x * x))))


def _gelu_and_grad(x):
    x2 = x * x
    t = jnp.tanh(x * (GELU_C0 + (GELU_C0 * GELU_C1) * x2))
    half_x, p = 0.5 * x, 1.0 + t
    slope = GELU_C0 + (3.0 * GELU_C0 * GELU_C1) * x2
    return half_x * p, 0.5 * p + (half_x * slope) * (1.0 - t * t)


def _act(a, act):
    if act == "relu2":
        r = jnp.maximum(a.astype(F32), 0.0)
        return (r * r).astype(BF16)
    return a.astype(BF16)


def _layer_spec(wg, layer):
    nsh, _, r, c = wg.shape
    return pl.BlockSpec((nsh, None, r, c), lambda i: (0, layer, 0, 0), pipeline_mode=pl.Buffered(1))


def _token_tile(T, wide_row_bytes):
    return min(T, ROW_TILE if ROW_TILE * wide_row_bytes <= 4 * 1024 * 1024 else ROW_TILE // 2)


def _norm_matmul(x, gain, wg, layer, name):
    T, D = x.shape
    nsh, _, _, ns = wg.shape
    tm = min(T, ROW_TILE)

    def body(x_ref, g_ref, w_ref, y_ref, h_ref):
        xv = x_ref[...]
        h = (xv * _rstd(xv) * g_ref[...]).astype(BF16)
        h_ref[...] = h
        for j in range(nsh):
            y_ref[:, j * ns : (j + 1) * ns] = jnp.dot(h, w_ref[j], preferred_element_type=F32).astype(BF16)

    return pl.pallas_call(
        body,
        name=name,
        grid=(T // tm,),
        in_specs=[pl.BlockSpec((tm, D), lambda i: (i, 0)), pl.BlockSpec((1, D), lambda i: (0, 0)), _layer_spec(wg, layer)],
        out_specs=[pl.BlockSpec((tm, nsh * ns), lambda i: (i, 0)), pl.BlockSpec((tm, D), lambda i: (i, 0))],
        out_shape=[jax.ShapeDtypeStruct((T, nsh * ns), BF16), jax.ShapeDtypeStruct((T, D), BF16)],
        compiler_params=_params(1),
    )(x, gain.reshape(1, D), wg)


def _act_matmul_res(a, wg, layer, x_in, act, name):
    T, K = a.shape
    nsh, _, kq, D = wg.shape
    tm = _token_tile(T, K * a.dtype.itemsize)

    def body(a_ref, w_ref, x_ref, o_ref):
        w = w_ref[...].reshape(nsh * kq, D)
        o_ref[...] = x_ref[...] + jnp.dot(_act(a_ref[...], act), w, preferred_element_type=F32)

    return pl.pallas_call(
        body,
        name=name,
        grid=(T // tm,),
        in_specs=[pl.BlockSpec((tm, K), lambda i: (i, 0)), _layer_spec(wg, layer), pl.BlockSpec((tm, D), lambda i: (i, 0))],
        out_specs=pl.BlockSpec((tm, D), lambda i: (i, 0)),
        out_shape=jax.ShapeDtypeStruct((T, D), F32),
        compiler_params=_params(1),
    )(a, wg, x_in)


def _matmul_nt(g, wg, layer, a, name):
    T, D = g.shape
    nsh, _, kq, _ = wg.shape
    tm = _token_tile(T, nsh * kq * jnp.dtype(BF16).itemsize)

    def body(g_ref, w_ref, *rest):
        gv = g_ref[...]
        for k in range(nsh):
            cols = slice(k * kq, (k + 1) * kq)
            r = lax.dot_general(gv, w_ref[k], NT, preferred_element_type=F32)
            if a is not None:
                r = r * (2.0 * jnp.maximum(rest[0][:, cols].astype(F32), 0.0))
            rest[-1][:, cols] = r.astype(BF16)

    row = pl.BlockSpec((tm, nsh * kq), lambda i: (i, 0))
    in_specs = [pl.BlockSpec((tm, D), lambda i: (i, 0)), _layer_spec(wg, layer)]
    args = [g, wg]
    if a is not None:
        in_specs.append(row)
        args.append(a)
    return pl.pallas_call(
        body,
        name=name,
        grid=(T // tm,),
        in_specs=in_specs,
        out_specs=row,
        out_shape=jax.ShapeDtypeStruct((T, nsh * kq), BF16),
        compiler_params=_params(1),
    )(*args)


def _matmul_nt_norm_bwd(da, wg, layer, x, gain, dres, exch, name):
    T, D = x.shape
    nsh, _, _, ns = wg.shape
    tm = min(T, ROW_TILE // 2)

    def body(da_ref, w_ref, x_ref, g_ref, r_ref, dx_ref, dxb_ref, dg_ref):
        dh = lax.dot_general(da_ref[:, :ns], w_ref[0], NT, preferred_element_type=F32)
        for j in range(1, nsh):
            dh = dh + lax.dot_general(da_ref[:, j * ns : (j + 1) * ns], w_ref[j], NT, preferred_element_type=F32)
        dx, dg = _norm_bwd(dh, x_ref[...], g_ref[...])
        dx = dx + r_ref[...]
        dx_ref[...] = dx
        dxb_ref[...] = dx.astype(BF16)

        @pl.when(pl.program_id(0) == 0)
        def _():
            dg_ref[...] = dg

        @pl.when(pl.program_id(0) > 0)
        def _():
            dg_ref[...] += dg

    row = pl.BlockSpec((tm, D), lambda i: (i, 0))
    vec = pl.BlockSpec((1, D), lambda i: (0, 0))
    return _call_with_exchange(
        body,
        exch,
        name,
        grid=(T // tm,),
        in_specs=[pl.BlockSpec((tm, nsh * ns), lambda i: (i, 0)), _layer_spec(wg, layer), row, vec, row],
        out_specs=[row, row, vec],
        out_shape=[
            jax.ShapeDtypeStruct((T, D), F32),
            jax.ShapeDtypeStruct((T, D), BF16),
            jax.ShapeDtypeStruct((1, D), F32),
        ],
        scratch_shapes=[],
        args=(da, wg, x, gain.reshape(1, D), dres),
    )


def _matmul_tn(lhs, rhs, bufs, layer, n_layers, shard_lhs, act, name):
    T = lhs.shape[0]
    rows = lhs.shape[1] // N_CHIPS if shard_lhs else lhs.shape[1]
    cols = rhs.shape[1] if shard_lhs else rhs.shape[1] // N_CHIPS
    tt = min(T, 2 * ROW_TILE)
    n_t = T // tt

    def body(l_ref, r_ref, *rest):
        o32_ref, o16_ref = rest[-2:]
        t = pl.program_id(1)
        upd = lax.dot_general(_act(l_ref[...], act), r_ref[...].astype(BF16), TN, preferred_element_type=F32)

        @pl.when(t == 0)
        def _():
            o32_ref[...] = upd

        @pl.when(t > 0)
        def _():
            o32_ref[...] += upd

        @pl.when(t == n_t - 1)
        def _():
            o16_ref[...] = o32_ref[...].astype(BF16)

    if shard_lhs:
        in_specs = [pl.BlockSpec((tt, rows), lambda s, t: (t, s)), pl.BlockSpec((tt, cols), lambda s, t: (t, 0))]
    else:
        in_specs = [pl.BlockSpec((tt, rows), lambda s, t: (t, 0)), pl.BlockSpec((tt, cols), lambda s, t: (t, s))]
    args = [lhs, rhs]
    aliases = {}
    if bufs is not None:
        in_specs += [pl.BlockSpec(memory_space=pl.ANY)] * 2
        args += list(bufs)
        aliases = {2: 0, 3: 1}
    shape = (N_CHIPS, n_layers, rows, cols)
    return pl.pallas_call(
        body,
        name=name,
        grid=(N_CHIPS, n_t),
        in_specs=in_specs,
        out_specs=[pl.BlockSpec((None, None, rows, cols), lambda s, t: (s, layer, 0, 0))] * 2,
        out_shape=[jax.ShapeDtypeStruct(shape, F32), jax.ShapeDtypeStruct(shape, BF16)],
        input_output_aliases=aliases,
        compiler_params=_params(2),
    )(*args)


def _chip_peers(x, y):
    return [(1 - x, y), (x, 1 - y), (1 - x, 1 - y)]


def _remote(src, dst, sems, s, peer):
    return pltpu.make_async_remote_copy(
        src_ref=src, dst_ref=dst, send_sem=sems[0].at[s], recv_sem=sems[1].at[s], device_id=peer, device_id_type=MESH
    )


class _Exchange:
    def __init__(self, operands, n_alias, new_shapes, n_sems, build):
        self.operands, self.n_alias, self.new_shapes, self.n_sems, self.build = operands, n_alias, new_shapes, n_sems, build

    def out_shapes(self):
        return [jax.ShapeDtypeStruct(a.shape, a.dtype) for a in self.operands[: self.n_alias]] + list(self.new_shapes)

    def scratch(self):
        return [pltpu.SemaphoreType.DMA((n,)) for n in self.n_sems]

    def run(self, ins, outs, sems, first, last):
        starts, recvs, sends, locals_ = self.build(ins, outs, sems)

        def start_all():
            for cp in starts:
                cp.start()

        def wait_all():
            for cp in recvs:
                cp.wait_recv()
            for cp in sends:
                cp.wait_send()
            for cp in locals_:
                cp.wait()

        if first is True:
            start_all()
            return wait_all
        pl.when(first)(start_all)
        return lambda: pl.when(last)(wait_all)


def _gather_exchange(shards, bufs, plan):
    n_arr = len(shards)
    n_cp = len(plan) * (N_CHIPS - 1)

    def build(ins, outs, sems):
        shard_refs = ins[-n_arr:]
        x, y, c = lax.axis_index("x"), lax.axis_index("y"), lax.axis_index("c")
        me = 2 * x + y
        recvs, sends, locals_ = [], [], []
        for p, (a, l0, n) in enumerate(plan):
            src = shard_refs[a].at[pl.ds(l0, n)]
            cp = pltpu.make_async_copy(src, outs[a].at[me, pl.ds(l0, n)], sems[2].at[p])
            locals_.append(cp)
            for k, (px, py) in enumerate(_chip_peers(x, y)):
                s = p * (N_CHIPS - 1) + k
                sends.append(_remote(src, outs[a].at[me, pl.ds(l0, n)], sems, s, (px, py, c)))
                recvs.append(_remote(src, outs[a].at[2 * px + py, pl.ds(l0, n)], sems, s, (px, py, c)))
        return locals_ + sends, recvs, sends, locals_

    if bufs is None:
        new = [jax.ShapeDtypeStruct((N_CHIPS,) + s.shape, s.dtype) for s in shards]
        return _Exchange(list(shards), 0, new, [n_cp, n_cp, len(plan)], build)
    return _Exchange(list(bufs) + list(shards), n_arr, [], [n_cp, n_cp, len(plan)], build)


def _scatter_exchange(g16, recv, plan, small=None):
    n_arr = len(g16)
    have = [r for r in recv if r is not None]
    made = [a for a in range(n_arr) if recv[a] is None]
    n_cp = len(plan) * (N_CHIPS - 1) + (N_DEV - 1 if small is not None else 0)

    def build(ins, outs, sems):
        g_refs = ins[:n_arr]
        recv_refs, it_have, it_made = [], iter(outs[n_arr : n_arr + len(have)]), iter(outs[n_arr + len(have) :])
        for a in range(n_arr):
            recv_refs.append(next(it_made) if recv[a] is None else next(it_have))
        x, y, c = lax.axis_index("x"), lax.axis_index("y"), lax.axis_index("c")
        me = 2 * x + y
        recvs, sends, locals_ = [], [], []
        for p, (a, l0, n) in enumerate(plan):
            for k, (px, py) in enumerate(_chip_peers(x, y)):
                s = p * (N_CHIPS - 1) + k
                dst = recv_refs[a].at[k, pl.ds(l0, n)]
                sends.append(_remote(g_refs[a].at[2 * px + py, pl.ds(l0, n)], dst, sems, s, (px, py, c)))
                recvs.append(_remote(g_refs[a].at[me, pl.ds(l0, n)], dst, sems, s, (px, py, c)))
        if small is not None:
            small_ref, all_ref = ins[-1], outs[-1]
            slot = 4 * x + 2 * y + c
            locals_.append(pltpu.make_async_copy(small_ref, all_ref.at[slot], sems[2].at[0]))
            flips = [(fx, fy, fc) for fx in (0, 1) for fy in (0, 1) for fc in (0, 1)][1:]
            for k, (fx, fy, fc) in enumerate(flips):
                s = len(plan) * (N_CHIPS - 1) + k
                px, py, pc = x ^ fx, y ^ fy, c ^ fc
                sends.append(_remote(small_ref, all_ref.at[slot], sems, s, (px, py, pc)))
                recvs.append(_remote(small_ref, all_ref.at[4 * px + 2 * py + pc], sems, s, (px, py, pc)))
        return locals_ + sends, recvs, sends, locals_

    operands = list(g16) + have + ([small] if small is not None else [])
    new = [jax.ShapeDtypeStruct((N_CHIPS - 1,) + g16[a].shape[1:], BF16) for a in made]
    if small is not None:
        new.append(jax.ShapeDtypeStruct((N_DEV,) + small.shape, F32))
    return _Exchange(operands, n_arr + len(have), new, [n_cp, n_cp, 1], build)


def _call_with_exchange(body, exch, name, grid, in_specs, out_specs, out_shape, scratch_shapes, args):
    n_in, n_out, n_scr = len(in_specs), len(out_shape), len(scratch_shapes)
    if exch is None:
        outs = pl.pallas_call(
            body, name=name, grid=grid, in_specs=in_specs, out_specs=out_specs, out_shape=out_shape,
            scratch_shapes=scratch_shapes, compiler_params=_params(len(grid)),
        )(*args)
        return outs, []
    e_shapes = exch.out_shapes()
    e_in, e_out = len(exch.operands), len(e_shapes)

    def wrapped(*refs):
        ins, refs = refs[:n_in], refs[n_in:]
        e_ins, refs = refs[:e_in], refs[e_in:]
        outs, refs = refs[:n_out], refs[n_out:]
        e_outs, refs = refs[:e_out], refs[e_out:]
        scr, sems = refs[:n_scr], refs[n_scr:]
        first, last = True, True
        for d, g in enumerate(grid):
            first = (pl.program_id(d) == 0) & first
            last = (pl.program_id(d) == g - 1) & last
        finish = exch.run(e_ins, e_outs, sems, first, last)
        body(*ins, *outs, *scr)
        finish()

    any_spec = pl.BlockSpec(memory_space=pl.ANY)
    outs = pl.pallas_call(
        wrapped,
        name=name,
        grid=grid,
        in_specs=list(in_specs) + [any_spec] * e_in,
        out_specs=list(out_specs) + [any_spec] * e_out,
        out_shape=list(out_shape) + e_shapes,
        input_output_aliases={n_in + i: n_out + i for i in range(exch.n_alias)},
        scratch_shapes=list(scratch_shapes) + exch.scratch(),
        compiler_params=pltpu.CompilerParams(
            dimension_semantics=("arbitrary",) * len(grid), vmem_limit_bytes=VMEM_LIMIT, has_side_effects=True
        ),
    )(*args, *exch.operands)
    return outs[:n_out], outs[n_out:]


def _exchange_only(exch, name):
    n_in = len(exch.operands)
    shapes = exch.out_shapes()

    def body(*refs):
        ins, outs, sems = refs[:n_in], refs[n_in : n_in + len(shapes)], refs[n_in + len(shapes) :]
        exch.run(ins, outs, sems, True, True)()

    any_spec = pl.BlockSpec(memory_space=pl.ANY)
    return pl.pallas_call(
        body,
        name=name,
        in_specs=[any_spec] * n_in,
        out_specs=[any_spec] * len(shapes),
        out_shape=shapes,
        input_output_aliases={i: i for i in range(exch.n_alias)},
        scratch_shapes=exch.scratch(),
        compiler_params=pltpu.CompilerParams(has_side_effects=True),
    )(*exch.operands)


ATTN_LANE_TILES = 2
ATTN_FWD_UNROLL = 34
ATTN_BWD_UNROLL = 17


MASKED = -1e30


def _hi_lo(x):
    hi = x.astype(BF16)
    lo = (x - hi.astype(F32)).astype(BF16)
    return jnp.concatenate([hi, lo], axis=1)


def _suffix_matrix(inclusive):
    j = lax.broadcasted_iota(jnp.int32, (2 * Q_TILE, 2 * Q_TILE), 0) & (Q_TILE - 1)
    s = lax.broadcasted_iota(jnp.int32, (2 * Q_TILE, 2 * Q_TILE), 1)
    later = (j >= s) if inclusive else (j > s)
    return jnp.where((s >= Q_TILE) | later, 1.0, 0.0).astype(BF16)


def _log_beta(z):
    return jnp.minimum(z, 0.0) - jnp.log(1.0 + jnp.exp(-jnp.abs(z)))


def _head_masks(width):
    lane = lax.broadcasted_iota(jnp.int32, (1, width), 1)
    return [(lane >= h * HEAD_DIM) & (lane < (h + 1) * HEAD_DIM) for h in range(width // HEAD_DIM)]


def _per_head_rows(x, masks):
    return jnp.concatenate([jnp.where(hm, x, 0) for hm in masks], axis=0)


def _heads_to_lanes(x, n_heads):
    return jnp.concatenate([x[h * Q_TILE : (h + 1) * Q_TILE] for h in range(n_heads)], axis=1)


def _block_start(kb):
    return kb * Q_TILE if isinstance(kb, int) else pl.multiple_of(kb * Q_TILE, Q_TILE)


def _clamp(i, n):
    return jnp.minimum(i, n - 1)


def _next_block(pos):
    qi, kb = pos
    row_done = kb == 0
    nqi = jnp.where(row_done, qi + 1, qi)
    return nqi, jnp.where(row_done, nqi, kb - 1)


def _stream_unroll(n_blocks, wanted):
    return next(u for u in (wanted, 2, 1) if n_blocks % u == 0)


def _past_mask(rows):
    t = lax.broadcasted_iota(jnp.int32, (rows, Q_TILE), 0) & (Q_TILE - 1)
    s = lax.broadcasted_iota(jnp.int32, (rows, Q_TILE), 1)
    return s < t


def _attn_fwd(qkv, n_seq, S, D, exch, name):
    T = n_seq * S
    width = min(D, ATTN_LANE_TILES * LANES)
    n_heads = width // HEAD_DIM
    rows = n_heads * Q_TILE
    nq = S // Q_TILE
    groups = D // width
    n_blocks = nq * (nq + 1) // 2
    unroll = _stream_unroll(n_blocks, ATTN_FWD_UNROLL)
    scale = HEAD_DIM ** -0.5
    n_trips = n_blocks // unroll

    def body(q_ref, k_ref, v_ref, o_ref, a_out, b_out, qh_scr, vh_scr, bias_scr, a_stage, b_stage, sems):
        masks = _head_masks(width)
        sfx = _suffix_matrix(False)
        stream = pl.program_id(0) * groups + pl.program_id(1)

        def per_head_tables(i, c):
            blk = pl.ds(_block_start(i), Q_TILE)
            qh_scr[i] = _per_head_rows(q_ref[blk, :] * scale, masks)
            vh_scr[i] = _per_head_rows(v_ref[blk, :], masks)
            return c

        lax.fori_loop(0, nq, per_head_tables, 0)
        bias_scr[0] = jnp.zeros((rows, Q_TILE), F32)
        bias_scr[1] = jnp.where(_past_mask(rows), 0.0, MASKED)

        def save(n, slot):
            blocks = pl.ds(n * unroll, unroll)
            return [
                pltpu.make_async_copy(a_stage.at[slot], a_out.at[stream, blocks], sems.at[slot]),
                pltpu.make_async_copy(b_stage.at[slot], b_out.at[stream, blocks], sems.at[2 + slot]),
            ]

        def scores(pos):
            qi, kb = pos
            kt = k_ref[pl.ds(_block_start(_clamp(kb, nq)), Q_TILE), :]
            z = lax.dot_general(qh_scr[_clamp(qi, nq)], kt, NT, preferred_element_type=F32)
            z = z + bias_scr[(kb == qi).astype(jnp.int32)]
            lb = _log_beta(z)
            return lb, _hi_lo(lb - z)

        def weigh(pos, st, carry, acc, slot, u):
            qi, kb = pos
            lb, l1 = st
            r = jnp.dot(l1, sfx, preferred_element_type=F32)
            carry = jnp.where(kb == qi, 0.0, carry)
            a = jnp.exp(lb + r[:, :Q_TILE] + carry).astype(BF16)
            a_stage[slot, u] = a
            b_stage[slot, u] = jnp.exp(lb).astype(BF16)
            acc = jnp.where(kb == qi, 0.0, acc) + jnp.dot(
                _heads_to_lanes(a, n_heads), vh_scr[_clamp(kb, nq)], preferred_element_type=F32
            )
            o_ref[pl.ds(_block_start(_clamp(qi, nq)), Q_TILE), :] = acc
            return carry + r[:, Q_TILE:], acc

        def trip(n, c):
            pos, st, carry, acc = c
            slot = n % 2

            @pl.when((n >= 2) | (stream > 0))
            def _():
                for cp in save(0, slot):
                    cp.wait()

            for u in range(unroll):
                nxt = _next_block(pos)
                st_nxt = scores(nxt)
                carry, acc = weigh(pos, st, carry, acc, slot, u)
                pos, st = nxt, st_nxt
            for cp in save(n, slot):
                cp.start()
            return pos, st, carry, acc

        first = (jnp.int32(0), jnp.int32(0))
        zero = bias_scr[0]
        init = (first, scores(first), zero, jnp.concatenate([zero[:Q_TILE]] * (width // Q_TILE), axis=1))
        lax.fori_loop(0, n_trips, trip, init)

        @pl.when(stream == n_seq * groups - 1)
        def _():
            for slot in range(min(2, n_trips)):
                for cp in save(0, slot):
                    cp.wait()

    seq = lambda col0: pl.BlockSpec((S, width), lambda b, p: (b, col0 + p))
    saved = jax.ShapeDtypeStruct((n_seq * groups, n_blocks, rows, Q_TILE), BF16)
    stage = pltpu.VMEM((2, unroll, rows, Q_TILE), BF16)
    (o, a_w, beta), moved = _call_with_exchange(
        body,
        exch,
        name,
        grid=(n_seq, groups),
        in_specs=[seq(0), seq(groups), seq(2 * groups)],
        out_specs=[seq(0), pl.BlockSpec(memory_space=pl.ANY), pl.BlockSpec(memory_space=pl.ANY)],
        out_shape=[jax.ShapeDtypeStruct((T, D), F32), saved, saved],
        scratch_shapes=[
            pltpu.VMEM((nq, rows, width), BF16),
            pltpu.VMEM((nq, rows, width), BF16),
            pltpu.VMEM((2, rows, Q_TILE), F32),
            stage,
            stage,
            pltpu.SemaphoreType.DMA((4,)),
        ],
        args=(qkv, qkv, qkv),
    )
    return (o, a_w, beta), moved


def _attn_bwd(qkv, fwd, do, n_seq, S, D, exch, name):
    o, a_w, beta = fwd
    T = n_seq * S
    width = min(D, ATTN_LANE_TILES * LANES)
    n_heads = width // HEAD_DIM
    rows = n_heads * Q_TILE
    nq = S // Q_TILE
    groups = D // width
    n_blocks = nq * (nq + 1) // 2
    unroll = _stream_unroll(n_blocks, ATTN_BWD_UNROLL)
    scale = HEAD_DIM ** -0.5
    n_trips = n_blocks // unroll

    def body(q_ref, k_ref, v_ref, o_ref, do_ref, a_in, b_in, dq_ref, dk_ref, dv_ref, dk_acc, dv_acc, qh_scr, doh_scr, delta_scr, a_stage, b_stage, sems):
        masks = _head_masks(width)
        sfx_incl = _suffix_matrix(True)
        stream = pl.program_id(0) * groups + pl.program_id(1)

        n_streams = n_seq * groups
        ahead = n_trips % 2 == 0

        def fetch(s, n, slot):
            blocks = pl.ds(n * unroll, unroll)
            return [
                pltpu.make_async_copy(a_in.at[s, blocks], a_stage.at[slot], sems.at[slot]),
                pltpu.make_async_copy(b_in.at[s, blocks], b_stage.at[slot], sems.at[2 + slot]),
            ]

        @pl.when((stream == 0) | (not ahead))
        def _():
            for cp in fetch(stream, 0, 0):
                cp.start()

        dk_acc[...] = jnp.zeros_like(dk_acc)
        dv_acc[...] = jnp.zeros_like(dv_acc)

        lane = lax.broadcasted_iota(jnp.int32, (2 * width, rows), 0) % width
        col = lax.broadcasted_iota(jnp.int32, (2 * width, rows), 1)
        head_sums = jnp.where(lane // HEAD_DIM == col // Q_TILE, 1.0, 0.0).astype(BF16)

        def per_head_tables(i, c):
            blk = pl.ds(_block_start(i), Q_TILE)
            do = do_ref[blk, :]
            qh_scr[i] = _per_head_rows(q_ref[blk, :] * scale, masks)
            doh_scr[i] = _per_head_rows(do, masks)
            prod = do.astype(F32) * o_ref[blk, :]
            d = jnp.dot(_hi_lo(prod), head_sums, preferred_element_type=F32)
            delta_scr[i] = jnp.concatenate([d[:, h * Q_TILE : (h + 1) * Q_TILE] for h in range(n_heads)], axis=0)
            return c

        lax.fori_loop(0, nq, per_head_tables, 0)

        def weigh(pos, ab, beta, c2, dq):
            qi, kb = pos
            first = kb == qi
            blk = pl.ds(_block_start(kb), Q_TILE)
            g = ab.astype(F32) * lax.dot_general(doh_scr[qi], v_ref[blk, :], NT, preferred_element_type=F32)
            r2 = jnp.dot(_hi_lo(g), sfx_incl, preferred_element_type=F32)
            c2 = jnp.where(first, 0.0, c2)
            earlier = delta_scr[qi] - (r2[:, :Q_TILE] + c2)
            beta = beta.astype(F32)
            dzb = (g * (1.0 - beta) - earlier * beta).astype(BF16)
            kh = _per_head_rows(k_ref[blk, :], masks)
            dq = jnp.where(first, 0.0, dq) + jnp.dot(_heads_to_lanes(dzb, n_heads), kh, preferred_element_type=F32)
            dq_ref[pl.ds(_block_start(qi), Q_TILE), :] = (dq * scale).astype(BF16)
            dk_acc[blk, :] += lax.dot_general(dzb, qh_scr[qi], TN, preferred_element_type=F32)
            dv_acc[blk, :] += lax.dot_general(ab, doh_scr[qi], TN, preferred_element_type=F32)
            return c2 + r2[:, Q_TILE:], dq

        def trip(n, c):
            pos, c2, dq = c
            slot = n % 2
            for cp in fetch(stream, n, slot):
                cp.wait()
            more = n + 1 < n_trips
            if ahead:
                nxt = (jnp.where(more, stream, stream + 1), jnp.where(more, n + 1, 0))
                more = more | (stream + 1 < n_streams)
            else:
                nxt = (stream, n + 1)

            @pl.when(more)
            def _():
                for cp in fetch(*nxt, 1 - slot):
                    cp.start()

            for u in range(unroll):
                c2, dq = weigh(pos, a_stage[slot, u], b_stage[slot, u], c2, dq)
                pos = _next_block(pos)
            return pos, c2, dq

        zero = dk_acc[pl.ds(0, Q_TILE), :]
        init = ((jnp.int32(0), jnp.int32(0)), jnp.concatenate([zero[:, :Q_TILE]] * n_heads, axis=0), zero)
        lax.fori_loop(0, n_trips, trip, init)
        dk_ref[...] = dk_acc[...].astype(BF16)
        dv_ref[...] = dv_acc[...].astype(BF16)

    seq = lambda col0: pl.BlockSpec((S, width), lambda b, p: (b, col0 + p))
    return _call_with_exchange(
        body,
        exch,
        name,
        grid=(n_seq, groups),
        in_specs=[seq(0), seq(groups), seq(2 * groups), seq(0), seq(0)] + [pl.BlockSpec(memory_space=pl.ANY)] * 2,
        out_specs=[seq(0)] * 3,
        out_shape=[jax.ShapeDtypeStruct((T, D), BF16)] * 3,
        scratch_shapes=[
            pltpu.VMEM((S, width), F32),
            pltpu.VMEM((S, width), F32),
            pltpu.VMEM((nq, rows, width), BF16),
            pltpu.VMEM((nq, rows, width), BF16),
            pltpu.VMEM((nq, rows, Q_TILE), F32),
            pltpu.VMEM((2, unroll, rows, Q_TILE), BF16),
            pltpu.VMEM((2, unroll, rows, Q_TILE), BF16),
            pltpu.SemaphoreType.DMA((4,)),
        ],
        args=(qkv, qkv, qkv, o, do, a_w, beta),
    )


def _causal_ws(ws_ref, g):
    t = lax.broadcasted_iota(jnp.int32, (SGU_CHUNK, SGU_CHUNK), 0)
    s = lax.broadcasted_iota(jnp.int32, (SGU_CHUNK, SGU_CHUNK), 1)
    return jnp.where(s <= t, ws_ref[g], 0.0)


def _sgu_chunks_per_step(T):
    return next(n for n in (4, 2, 1) if T % (n * SGU_CHUNK) == 0)


def _sgu_fwd(a, gain, ws, bsb, name):
    T, F2 = a.shape
    F = F2 // 2
    gw = F // SGU_GROUPS

    step = SGU_CHUNK * _sgu_chunks_per_step(T)

    def body(a_ref, gain_ref, ws_ref, bsb_ref, y_ref):
        for r0 in range(0, step, SGU_CHUNK):
            rs = slice(r0, r0 + SGU_CHUNK)
            v = _gelu(a_ref[rs, F:].astype(F32))
            vn = (v * _rstd(v) * gain_ref[...]).astype(BF16)
            for g in range(SGU_GROUPS):
                cs = slice(g * gw, (g + 1) * gw)
                w = _causal_ws(ws_ref, g).astype(BF16)
                mixed = jnp.dot(w, vn[:, cs], preferred_element_type=F32) + bsb_ref[g]
                y_ref[rs, cs] = (_gelu(a_ref[rs, cs].astype(F32)) * mixed).astype(BF16)

    return pl.pallas_call(
        body,
        name=name,
        grid=(T // step,),
        in_specs=[
            pl.BlockSpec((step, F2), lambda i: (i, 0)),
            pl.BlockSpec((1, F), lambda i: (0, 0)),
            pl.BlockSpec((SGU_GROUPS, SGU_CHUNK, SGU_CHUNK), lambda i: (0, 0, 0)),
            pl.BlockSpec((SGU_GROUPS, SGU_CHUNK, gw), lambda i: (0, 0, 0)),
        ],
        out_specs=pl.BlockSpec((step, F), lambda i: (i, 0)),
        out_shape=jax.ShapeDtypeStruct((T, F), BF16),
        compiler_params=_params(1),
    )(a, gain.reshape(1, F), ws, bsb)


def _sgu_bwd(a, dy, gain, ws, bsb, name):
    T, F2 = a.shape
    F = F2 // 2
    gw = F // SGU_GROUPS
    step = SGU_CHUNK * _sgu_chunks_per_step(T)

    def body(a_ref, dy_ref, gain_ref, ws_ref, bsb_ref, da_ref, dws_ref, dbs_ref, dgain_ref, dvn_ref):
        @pl.when(pl.program_id(0) == 0)
        def _():
            dws_ref[...] = jnp.zeros_like(dws_ref)
            dbs_ref[...] = jnp.zeros_like(dbs_ref)
            dgain_ref[...] = jnp.zeros_like(dgain_ref)

        gain = gain_ref[...]
        ones = jnp.ones((gw, SGU_CHUNK), BF16)
        for r0 in range(0, step, SGU_CHUNK):
            rs = slice(r0, r0 + SGU_CHUNK)
            v, v_slope = _gelu_and_grad(a_ref[rs, F:].astype(F32))
            rstd = _rstd(v)
            vh = v * rstd
            vn = (vh * gain).astype(BF16)
            for g in range(SGU_GROUPS):
                cs = slice(g * gw, (g + 1) * gw)
                w = _causal_ws(ws_ref, g).astype(BF16)
                mixed = jnp.dot(w, vn[:, cs], preferred_element_type=F32) + bsb_ref[g]
                u, u_slope = _gelu_and_grad(a_ref[rs, cs].astype(F32))
                dyc = dy_ref[rs, cs].astype(F32)
                da_ref[rs, cs] = (dyc * mixed * u_slope).astype(BF16)
                dm = (dyc * u).astype(BF16)
                dbs_ref[g] += jnp.dot(dm, ones, preferred_element_type=F32)
                dws_ref[g] += _causal_mask_f32(lax.dot_general(dm, vn[:, cs], NT, preferred_element_type=F32))
                dvn_ref[:, cs] = lax.dot_general(w, dm, TN, preferred_element_type=F32)
            dvn = dvn_ref[...]
            dgain_ref[...] += jnp.sum(dvn * vh, axis=0, keepdims=True)
            dvh = dvn * gain
            dv = rstd * (dvh - vh * jnp.mean(dvh * vh, axis=-1, keepdims=True))
            da_ref[rs, F:] = (dv * v_slope).astype(BF16)

    acc_spec = pl.BlockSpec((SGU_GROUPS, SGU_CHUNK, SGU_CHUNK), lambda i: (0, 0, 0))
    acc_shape = jax.ShapeDtypeStruct((SGU_GROUPS, SGU_CHUNK, SGU_CHUNK), F32)
    return pl.pallas_call(
        body,
        name=name,
        grid=(T // step,),
        in_specs=[
            pl.BlockSpec((step, F2), lambda i: (i, 0)),
            pl.BlockSpec((step, F), lambda i: (i, 0)),
            pl.BlockSpec((1, F), lambda i: (0, 0)),
            acc_spec,
            pl.BlockSpec((SGU_GROUPS, SGU_CHUNK, gw), lambda i: (0, 0, 0)),
        ],
        out_specs=[
            pl.BlockSpec((step, F2), lambda i: (i, 0)),
            acc_spec,
            acc_spec,
            pl.BlockSpec((1, F), lambda i: (0, 0)),
        ],
        out_shape=[
            jax.ShapeDtypeStruct((T, F2), BF16),
            acc_shape,
            acc_shape,
            jax.ShapeDtypeStruct((1, F), F32),
        ],
        scratch_shapes=[pltpu.VMEM((SGU_CHUNK, F), F32)],
        compiler_params=_params(1),
    )(a, dy, gain.reshape(1, F), ws, bsb)


def _causal_mask_f32(m):
    t = lax.broadcasted_iota(jnp.int32, m.shape, 0)
    s = lax.broadcasted_iota(jnp.int32, m.shape, 1)
    return jnp.where(s <= t, m, 0.0)


def _final_loss(x, gain, target, name):
    T, D = x.shape
    tm = min(T, ROW_TILE // 2)

    def body(x_ref, g_ref, t_ref, sq_ref, dx_ref, dxb_ref, dg_ref):
        xv = x_ref[...]
        gain = g_ref[...]
        err = xv * _rstd(xv) * gain - t_ref[...]
        dx, dg = _norm_bwd(err * (1.0 / D), xv, gain)
        dx_ref[...] = dx
        dxb_ref[...] = dx.astype(BF16)
        sq = jnp.sum(err * err, axis=0, keepdims=True)

        @pl.when(pl.program_id(0) == 0)
        def _():
            sq_ref[...] = sq
            dg_ref[...] = dg

        @pl.when(pl.program_id(0) > 0)
        def _():
            sq_ref[...] += sq
            dg_ref[...] += dg

    row = pl.BlockSpec((tm, D), lambda i: (i, 0))
    vec = pl.BlockSpec((1, D), lambda i: (0, 0))
    return pl.pallas_call(
        body,
        name=name,
        grid=(T // tm,),
        in_specs=[row, vec, row],
        out_specs=[vec, row, row, vec],
        out_shape=[
            jax.ShapeDtypeStruct((1, D), F32),
            jax.ShapeDtypeStruct((T, D), F32),
            jax.ShapeDtypeStruct((T, D), BF16),
            jax.ShapeDtypeStruct((1, D), F32),
        ],
        compiler_params=_params(1),
    )(x, gain.reshape(1, D), target)


def _row_tile(rows, cols, n_arrays):
    budget = VMEM_LIMIT // 2 // (2 * n_arrays * cols * 4)
    tr = rows
    while tr > budget and tr % 16 == 0:
        tr //= 2
    return tr


def _sum_received(own, recv, name):
    R, C = own.shape
    n = recv.shape[0]
    tr = _row_tile(R, C, n + 2)

    def body(own_ref, recv_ref, o_ref):
        s = own_ref[...]
        for k in range(n):
            s = s + recv_ref[k].astype(F32)
        o_ref[...] = s

    return pl.pallas_call(
        body,
        name=name,
        grid=(R // tr,),
        in_specs=[pl.BlockSpec((tr, C), lambda i: (i, 0)), pl.BlockSpec((n, tr, C), lambda i: (0, i, 0))],
        out_specs=pl.BlockSpec((tr, C), lambda i: (i, 0)),
        out_shape=jax.ShapeDtypeStruct((R, C), F32),
        compiler_params=_params(1),
    )(own, recv)


def _sum_chip_shard(g32, recv, chip, name):
    _, L, r, c = g32.shape
    n = recv.shape[0]
    tr = _row_tile(r, c, n + 2)

    def body(chip_ref, own_ref, recv_ref, o_ref):
        s = own_ref[...]
        for k in range(n):
            s = s + recv_ref[k].astype(F32)
        o_ref[...] = s

    return pl.pallas_call(
        body,
        name=name,
        grid_spec=pltpu.PrefetchScalarGridSpec(
            num_scalar_prefetch=1,
            grid=(L, r // tr),
            in_specs=[
                pl.BlockSpec((None, None, tr, c), lambda l, i, chip_ref: (chip_ref[0], l, i, 0)),
                pl.BlockSpec((n, None, tr, c), lambda l, i, chip_ref: (0, l, i, 0)),
            ],
            out_specs=pl.BlockSpec((None, tr, c), lambda l, i, chip_ref: (l, i, 0)),
        ),
        out_shape=jax.ShapeDtypeStruct((L, r, c), F32),
        compiler_params=_params(2),
    )(chip.reshape(1).astype(jnp.int32), g32, recv)


def _adamw(w, m, v, parts, name):
    R, C = w.shape
    n = len(parts)
    tr = _row_tile(R, C, n + 7)

    def body(*refs):
        w_ref, m_ref, v_ref = refs[:3]
        g_ref, d_ref, nm_ref, nv_ref = refs[3 + n :]
        g = refs[3][...]
        for p_ref in refs[4 : 3 + n]:
            g = g + p_ref[...]
        nm = ADAM_B1 * m_ref[...] + (1.0 - ADAM_B1) * g
        nv = ADAM_B2 * v_ref[...] + (1.0 - ADAM_B2) * (g * g)
        m_hat = nm / (1.0 - ADAM_B1**ADAM_STEP)
        v_hat = nv / (1.0 - ADAM_B2**ADAM_STEP)
        g_ref[...] = g
        d_ref[...] = -ADAM_LR * (m_hat / (jnp.sqrt(v_hat) + ADAM_EPS) + ADAM_WD * w_ref[...])
        nm_ref[...] = nm
        nv_ref[...] = nv

    spec = pl.BlockSpec((tr, C), lambda i: (i, 0))
    return pl.pallas_call(
        body,
        name=name,
        grid=(R // tr,),
        in_specs=[spec] * (3 + n),
        out_specs=[spec] * 4,
        out_shape=[jax.ShapeDtypeStruct((R, C), F32)] * 4,
        compiler_params=_params(1),
    )(w, m, v, *parts)


def _swap_with_sibling(parts, name):
    n = len(parts)

    def body(*refs):
        ins, outs = refs[:n], refs[n : 2 * n]
        send_sems, recv_sems = refs[2 * n :]
        sibling = (lax.axis_index("x"), lax.axis_index("y"), 1 - lax.axis_index("c"))
        copies = [
            pltpu.make_async_remote_copy(
                src_ref=ins[a],
                dst_ref=outs[a],
                send_sem=send_sems.at[a],
                recv_sem=recv_sems.at[a],
                device_id=sibling,
                device_id_type=MESH,
            )
            for a in range(n)
        ]
        for cp in copies:
            cp.start()
        for cp in copies:
            cp.wait_recv()
        for cp in copies:
            cp.wait_send()

    any_spec = pl.BlockSpec(memory_space=pl.ANY)
    return pl.pallas_call(
        body,
        name=name,
        in_specs=[any_spec] * n,
        out_specs=[any_spec] * n,
        out_shape=[jax.ShapeDtypeStruct(p.shape, p.dtype) for p in parts],
        scratch_shapes=[pltpu.SemaphoreType.DMA((n,)), pltpu.SemaphoreType.DMA((n,))],
        compiler_params=pltpu.CompilerParams(has_side_effects=True),
    )(*parts)


def _pack(pieces):
    flat = jnp.concatenate([p.reshape(-1) for p in pieces])
    return flat.reshape(-1, LANES)


def _unpack(packed, shapes):
    flat = packed.reshape(-1)
    out, off = [], 0
    for s in shapes:
        size = 1
        for d in s:
            size *= d
        out.append(flat[off : off + size].reshape(s))
        off += size
    return out


def kernel(x, norm_mix, norm_mlp, sb_wqkv, sb_wo, sgu_win, sgu_gain, sgu_ws, sgu_bs, sgu_wout, mlp_w1, mlp_w2, final_norm, loss_target, m_norm_mix, m_norm_mlp, m_sb_wqkv, m_sb_wo, m_sgu_win, m_sgu_gain, m_sgu_ws, m_sgu_bs, m_sgu_wout, m_mlp_w1, m_mlp_w2, m_final_norm, v_norm_mix, v_norm_mlp, v_sb_wqkv, v_sb_wo, v_sgu_win, v_sgu_gain, v_sgu_ws, v_sgu_bs, v_sgu_wout, v_mlp_w1, v_mlp_w2, v_final_norm):
    n_seq, S, D = x.shape
    T = n_seq * S
    depth = norm_mix.shape[0]
    n_sgu = sgu_win.shape[0]
    F = sgu_wout.shape[1] * N_CHIPS
    gw = F // SGU_GROUPS
    chip = 2 * lax.axis_index("x") + lax.axis_index("y")

    QKV, WO, WIN, WOUT, W1, W2, GAIN = range(7)
    big = [sb_wqkv, sb_wo, sgu_win, sgu_wout, mlp_w1, mlp_w2]
    n_sb = sb_wqkv.shape[0]
    shards = [w.astype(BF16) for w in big] + [sgu_gain.reshape(1, -1, LANES)]

    def gather_plan(i):
        j, mlp = i // 2, min(2, depth - i)
        plan = [(WO, j, 1), (W1, i, mlp), (W2, i, mlp)]
        if i + 1 < depth:
            plan += [(WIN, (i + 1) // 2, 1), (WOUT, (i + 1) // 2, 1)]
        if j + 1 < n_sb:
            plan += [(QKV, j + 1, 1)]
        return plan

    wg = _exchange_only(_gather_exchange(shards, None, [(QKV, 0, 1), (GAIN, 0, 1)]), "gather_first_weights")
    gain_full = jnp.transpose(wg[GAIN].reshape(N_CHIPS, n_sgu, F // N_CHIPS), (1, 0, 2)).reshape(n_sgu, F)
    bsb = [jnp.broadcast_to(sgu_bs[j][:, :, None], (SGU_GROUPS, SGU_CHUNK, gw)) for j in range(n_sgu)]

    xs = x.reshape(T, D)
    saved = []
    for i in range(depth):
        j = i // 2
        if i % 2 == 0:
            qkv, h = _norm_matmul(xs, norm_mix[i], wg[QKV], j, f"qkv_fwd_{i}")
            attn, wg = _attn_fwd(qkv, n_seq, S, D, _gather_exchange(shards, wg, gather_plan(i)), f"attn_fwd_{i}")
            wg_qkv, wg_wo, wg_win, wg_wout, wg_w1, wg_w2 = wg[:6]
            x_mid = _act_matmul_res(attn[0], wg_wo, j, xs, None, f"wo_fwd_{i}")
            mix = (qkv, attn)
        else:
            a, h = _norm_matmul(xs, norm_mix[i], wg_win, j, f"win_fwd_{i}")
            yg = _sgu_fwd(a, gain_full[j], sgu_ws[j], bsb[j], f"sgu_fwd_{i}")
            x_mid = _act_matmul_res(yg, wg_wout, j, xs, None, f"wout_fwd_{i}")
            mix = (a, yg)
        a2, h2 = _norm_matmul(x_mid, norm_mlp[i], wg_w1, i, f"w1_fwd_{i}")
        x_out = _act_matmul_res(a2, wg_w2, i, x_mid, "relu2", f"w2_fwd_{i}")
        saved.append((xs, h, mix, x_mid, h2, a2))
        xs = x_out

    sq, dx, dxb, g_final = _final_loss(xs, final_norm, loss_target.reshape(T, D), "loss_head")
    loss = lax.psum(0.5 * jnp.sum(sq) / D, ("x", "y", "c"))

    n_layers = [n_sb, n_sb, n_sgu, n_sgu, depth, depth]
    g32, g16, recv = [None] * 6, [None] * 6, [None] * 6
    done_from, sent_from = list(n_layers), list(n_layers)

    def grad(a, layer, lhs, rhs, shard_lhs, act, name):
        bufs = None if g32[a] is None else (g32[a], g16[a])
        g32[a], g16[a] = _matmul_tn(lhs, rhs, bufs, layer, n_layers[a], shard_lhs, act, name)
        done_from[a] = layer

    def unsent_plan():
        plan = [(a, done_from[a], sent_from[a] - done_from[a]) for a in range(6) if sent_from[a] > done_from[a]]
        for a, l0, _ in plan:
            sent_from[a] = l0
        return plan

    def scatter(plan, small):
        if not plan and small is None:
            return None, lambda moved: None
        arrays = sorted({a for a, _, _ in plan})
        have = [a for a in arrays if recv[a] is not None]
        made = [a for a in arrays if recv[a] is None]
        exch = _scatter_exchange(
            [g16[a] for a in arrays], [recv[a] for a in arrays], [(arrays.index(a), l0, n) for a, l0, n in plan], small
        )

        def take(moved):
            for a, buf in zip(arrays, moved):
                g16[a] = buf
            for a, buf in zip(have + made, moved[len(arrays) :]):
                recv[a] = buf
            return moved[-1]

        return exch, take

    g_mix, g_mlp = [None] * depth, [None] * depth
    g_ws, g_bs, g_gain = [None] * n_sgu, [None] * n_sgu, [None] * n_sgu
    for i in reversed(range(depth)):
        j = i // 2
        x_in, h, mix, x_mid, h2, a2 = saved[i]
        da2 = _matmul_nt(dxb, wg_w2, i, a2, f"w2_bwd_{i}")
        grad(W2, i, a2, dxb, True, "relu2", f"w2_grad_{i}")
        grad(W1, i, h2, da2, False, None, f"w1_grad_{i}")
        (dx, dxb, g_mlp[i]), _ = _matmul_nt_norm_bwd(da2, wg_w1, i, x_mid, norm_mlp[i], dx, None, f"w1_bwd_{i}")
        if i % 2 == 0:
            qkv, attn = mix
            do = _matmul_nt(dxb, wg_wo, j, None, f"wo_bwd_{i}")
            grad(WO, j, attn[0], dxb, True, None, f"wo_grad_{i}")
            sgu_small = _pack([jnp.stack(g_ws), jnp.stack(g_bs), jnp.stack(g_gain)]) if i == 0 and n_sgu else None
            exch, take = scatter(unsent_plan(), sgu_small)
            (dq, dk, dv), moved = _attn_bwd(qkv, attn, do, n_seq, S, D, exch, f"attn_bwd_{i}")
            last = take(moved)
            if sgu_small is not None:
                sgu_small_all = last
            dqkv = jnp.concatenate([dq, dk, dv], axis=1)
            grad(QKV, j, h, dqkv, False, None, f"qkv_grad_{i}")
            exch, take = scatter(unsent_plan(), None)
            (dx, dxb, g_mix[i]), moved = _matmul_nt_norm_bwd(
                dqkv, wg_qkv, j, x_in, norm_mix[i], dx, exch, f"qkv_bwd_{i}"
            )
            take(moved)
        else:
            a, yg = mix
            dyg = _matmul_nt(dxb, wg_wout, j, None, f"wout_bwd_{i}")
            grad(WOUT, j, yg, dxb, True, None, f"wout_grad_{i}")
            da, g_ws[j], dbs, g_gain[j] = _sgu_bwd(a, dyg, gain_full[j], sgu_ws[j], bsb[j], f"sgu_bwd_{i}")
            g_bs[j] = dbs[:, :, 0]
            grad(WIN, j, h, da, False, None, f"win_grad_{i}")
            (dx, dxb, g_mix[i]), _ = _matmul_nt_norm_bwd(da, wg_win, j, x_in, norm_mix[i], dx, None, f"win_bwd_{i}")
    grad_x = dx.reshape(n_seq, S, D)

    names = ["qkv", "wo", "win", "wout", "w1", "w2"]
    exch, take = scatter(unsent_plan(), _pack([jnp.stack(g_mix), jnp.stack(g_mlp), g_final]))
    norm_small_all = take(_exchange_only(exch, "gather_norm_grads"))
    partial = [_sum_chip_shard(g32[a], recv[a], chip, f"sum_{names[a]}") for a in range(6)]
    partial = [p.reshape(-1, p.shape[-1]) for p in partial]
    theirs = _swap_with_sibling(partial, "swap_partial_sums")
    g_small = _unpack(
        _sum_received(norm_small_all[0], norm_small_all[1:], "sum_norm_grads"),
        [norm_mix.shape, norm_mlp.shape, final_norm.shape],
    ) + _unpack(
        _sum_received(sgu_small_all[0], sgu_small_all[1:], "sum_sgu_small_grads"),
        [sgu_ws.shape, sgu_bs.shape, (n_sgu, F)],
    )

    ms = [m_sb_wqkv, m_sb_wo, m_sgu_win, m_sgu_wout, m_mlp_w1, m_mlp_w2]
    vs = [v_sb_wqkv, v_sb_wo, v_sgu_win, v_sgu_wout, v_mlp_w1, v_mlp_w2]
    res = {}
    keys = ["sb_wqkv", "sb_wo", "sgu_win", "sgu_wout", "mlp_w1", "mlp_w2"]
    for key, k, w, m, v, mine, other in zip(keys, names, big, ms, vs, partial, theirs):
        cols = w.shape[-1]
        outs = _adamw(w.reshape(-1, cols), m.reshape(-1, cols), v.reshape(-1, cols), [mine, other], f"adamw_{k}")
        res[key] = [o.reshape(w.shape) for o in outs]

    g_small[5] = lax.dynamic_slice_in_dim(g_small[5], chip * (F // N_CHIPS), F // N_CHIPS, axis=1)
    small_keys = ["norm_mix", "norm_mlp", "final_norm", "sgu_ws", "sgu_bs", "sgu_gain"]
    small_w = [norm_mix, norm_mlp, final_norm, sgu_ws, sgu_bs, sgu_gain]
    small_m = [m_norm_mix, m_norm_mlp, m_final_norm, m_sgu_ws, m_sgu_bs, m_sgu_gain]
    small_v = [v_norm_mix, v_norm_mlp, v_final_norm, v_sgu_ws, v_sgu_bs, v_sgu_gain]
    outs = _adamw(_pack(small_w), _pack(small_m), _pack(small_v), [_pack(g_small)], "adamw_small")
    local_shapes = [w.shape for w in small_w]
    for key, parts in zip(small_keys, zip(*[_unpack(o, local_shapes) for o in outs])):
        res[key] = list(parts)

    order = ["norm_mix", "norm_mlp", "sb_wqkv", "sb_wo", "sgu_win", "sgu_gain", "sgu_ws", "sgu_bs", "sgu_wout", "mlp_w1", "mlp_w2", "final_norm"]
    return (loss, grad_x, *[res[k][0] for k in order], *[res[k][1] for k in order], *[res[k][2] for k in order], *[res[k][3] for k in order])
```

```python
import jax
import jax.numpy as jnp
from jax import lax
from jax.experimental import pallas as pl
from jax.experimental.pallas import tpu as pltpu

F32 = jnp.float32
BF16 = jnp.bfloat16
MESH = pl.DeviceIdType.MESH

EPS = 1e-6
HEAD_DIM = 64
LANES = 128
Q_TILE = 128
SGU_CHUNK = 128
SGU_GROUPS = 8
N_CHIPS = 4
N_DEV = 8
ADAM_LR = 0.001
ADAM_B1 = 0.9
ADAM_B2 = 0.999
ADAM_EPS = 1e-08
ADAM_WD = 0.01
ADAM_STEP = 10
GELU_C0 = 0.7978845608028654
GELU_C1 = 0.044715
VMEM_LIMIT = 48 * 1024 * 1024
ROW_TILE = 1024
NT = (((1,), (1,)), ((), ()))
TN = (((0,), (0,)), ((), ()))


def _params(n_axes):
    return pltpu.CompilerParams(dimension_semantics=("arbitrary",) * n_axes, vmem_limit_bytes=VMEM_LIMIT)


def _rstd(x):
    return lax.rsqrt(jnp.mean(x * x, axis=-1, keepdims=True) + EPS)


def _norm_bwd(dh, x, gain):
    rstd = _rstd(x)
    xh = x * rstd
    dhg = dh * gain
    dx = rstd * (dhg - xh * jnp.mean(dhg * xh, axis=-1, keepdims=True))
    return dx, jnp.sum(dh * xh, axis=0, keepdims=True)


def _gelu(x):
    return (0.5 * x) * (1.0 + jnp.tanh(x * (GELU_C0 + (GELU_C0 * GELU_C1) * (x * x))))


def _gelu_and_grad(x):
    x2 = x * x
    t = jnp.tanh(x * (GELU_C0 + (GELU_C0 * GELU_C1) * x2))
    half_x, p = 0.5 * x, 1.0 + t
    slope = GELU_C0 + (3.0 * GELU_C0 * GELU_C1) * x2
    return half_x * p, 0.5 * p + (half_x * slope) * (1.0 - t * t)


def _act(a, act):
    if act == "relu2":
        r = jnp.maximum(a.astype(F32), 0.0)
        return (r * r).astype(BF16)
    return a.astype(BF16)


def _layer_spec(wg, layer):
    nsh, _, r, c = wg.shape
    return pl.BlockSpec((nsh, None, r, c), lambda i: (0, layer, 0, 0), pipeline_mode=pl.Buffered(1))


def _token_tile(T, wide_row_bytes):
    return min(T, ROW_TILE if ROW_TILE * wide_row_bytes <= 4 * 1024 * 1024 else ROW_TILE // 2)


def _norm_matmul(x, gain, wg, layer, name):
    T, D = x.shape
    nsh, _, _, ns = wg.shape
    tm = min(T, ROW_TILE)

    def body(x_ref, g_ref, w_ref, y_ref, h_ref):
        xv = x_ref[...]
        h = (xv * _rstd(xv) * g_ref[...]).astype(BF16)
        h_ref[...] = h
        for j in range(nsh):
            y_ref[:, j * ns : (j + 1) * ns] = jnp.dot(h, w_ref[j], preferred_element_type=F32).astype(BF16)

    return pl.pallas_call(
        body,
        name=name,
        grid=(T // tm,),
        in_specs=[pl.BlockSpec((tm, D), lambda i: (i, 0)), pl.BlockSpec((1, D), lambda i: (0, 0)), _layer_spec(wg, layer)],
        out_specs=[pl.BlockSpec((tm, nsh * ns), lambda i: (i, 0)), pl.BlockSpec((tm, D), lambda i: (i, 0))],
        out_shape=[jax.ShapeDtypeStruct((T, nsh * ns), BF16), jax.ShapeDtypeStruct((T, D), BF16)],
        compiler_params=_params(1),
    )(x, gain.reshape(1, D), wg)


def _act_matmul_res(a, wg, layer, x_in, act, name):
    T, K = a.shape
    nsh, _, kq, D = wg.shape
    tm = _token_tile(T, K * a.dtype.itemsize)

    def body(a_ref, w_ref, x_ref, o_ref):
        w = w_ref[...].reshape(nsh * kq, D)
        o_ref[...] = x_ref[...] + jnp.dot(_act(a_ref[...], act), w, preferred_element_type=F32)

    return pl.pallas_call(
        body,
        name=name,
        grid=(T // tm,),
        in_specs=[pl.BlockSpec((tm, K), lambda i: (i, 0)), _layer_spec(wg, layer), pl.BlockSpec((tm, D), lambda i: (i, 0))],
        out_specs=pl.BlockSpec((tm, D), lambda i: (i, 0)),
        out_shape=jax.ShapeDtypeStruct((T, D), F32),
        compiler_params=_params(1),
    )(a, wg, x_in)


def _matmul_nt(g, wg, layer, a, name):
    T, D = g.shape
    nsh, _, kq, _ = wg.shape
    tm = _token_tile(T, nsh * kq * jnp.dtype(BF16).itemsize)

    def body(g_ref, w_ref, *rest):
        gv = g_ref[...]
        for k in range(nsh):
            cols = slice(k * kq, (k + 1) * kq)
            r = lax.dot_general(gv, w_ref[k], NT, preferred_element_type=F32)
            if a is not None:
                r = r * (2.0 * jnp.maximum(rest[0][:, cols].astype(F32), 0.0))
            rest[-1][:, cols] = r.astype(BF16)

    row = pl.BlockSpec((tm, nsh * kq), lambda i: (i, 0))
    in_specs = [pl.BlockSpec((tm, D), lambda i: (i, 0)), _layer_spec(wg, layer)]
    args = [g, wg]
    if a is not None:
        in_specs.append(row)
        args.append(a)
    return pl.pallas_call(
        body,
        name=name,
        grid=(T // tm,),
        in_specs=in_specs,
        out_specs=row,
        out_shape=jax.ShapeDtypeStruct((T, nsh * kq), BF16),
        compiler_params=_params(1),
    )(*args)


def _matmul_nt_norm_bwd(da, wg, layer, x, gain, dres, exch, name):
    T, D = x.shape
    nsh, _, _, ns = wg.shape
    tm = min(T, ROW_TILE // 2)

    def body(da_ref, w_ref, x_ref, g_ref, r_ref, dx_ref, dxb_ref, dg_ref):
        dh = lax.dot_general(da_ref[:, :ns], w_ref[0], NT, preferred_element_type=F32)
        for j in range(1, nsh):
            dh = dh + lax.dot_general(da_ref[:, j * ns : (j + 1) * ns], w_ref[j], NT, preferred_element_type=F32)
        dx, dg = _norm_bwd(dh, x_ref[...], g_ref[...])
        dx = dx + r_ref[...]
        dx_ref[...] = dx
        dxb_ref[...] = dx.astype(BF16)

        @pl.when(pl.program_id(0) == 0)
        def _():
            dg_ref[...] = dg

        @pl.when(pl.program_id(0) > 0)
        def _():
            dg_ref[...] += dg

    row = pl.BlockSpec((tm, D), lambda i: (i, 0))
    vec = pl.BlockSpec((1, D), lambda i: (0, 0))
    return _call_with_exchange(
        body,
        exch,
        name,
        grid=(T // tm,),
        in_specs=[pl.BlockSpec((tm, nsh * ns), lambda i: (i, 0)), _layer_spec(wg, layer), row, vec, row],
        out_specs=[row, row, vec],
        out_shape=[
            jax.ShapeDtypeStruct((T, D), F32),
            jax.ShapeDtypeStruct((T, D), BF16),
            jax.ShapeDtypeStruct((1, D), F32),
        ],
        scratch_shapes=[],
        args=(da, wg, x, gain.reshape(1, D), dres),
    )


def _matmul_tn(lhs, rhs, bufs, layer, n_layers, shard_lhs, act, name):
    T = lhs.shape[0]
    rows = lhs.shape[1] // N_CHIPS if shard_lhs else lhs.shape[1]
    cols = rhs.shape[1] if shard_lhs else rhs.shape[1] // N_CHIPS
    tt = min(T, 2 * ROW_TILE)
    n_t = T // tt

    def body(l_ref, r_ref, *rest):
        o32_ref, o16_ref = rest[-2:]
        t = pl.program_id(1)
        upd = lax.dot_general(_act(l_ref[...], act), r_ref[...].astype(BF16), TN, preferred_element_type=F32)

        @pl.when(t == 0)
        def _():
            o32_ref[...] = upd

        @pl.when(t > 0)
        def _():
            o32_ref[...] += upd

        @pl.when(t == n_t - 1)
        def _():
            o16_ref[...] = o32_ref[...].astype(BF16)

    if shard_lhs:
        in_specs = [pl.BlockSpec((tt, rows), lambda s, t: (t, s)), pl.BlockSpec((tt, cols), lambda s, t: (t, 0))]
    else:
        in_specs = [pl.BlockSpec((tt, rows), lambda s, t: (t, 0)), pl.BlockSpec((tt, cols), lambda s, t: (t, s))]
    args = [lhs, rhs]
    aliases = {}
    if bufs is not None:
        in_specs += [pl.BlockSpec(memory_space=pl.ANY)] * 2
        args += list(bufs)
        aliases = {2: 0, 3: 1}
    shape = (N_CHIPS, n_layers, rows, cols)
    return pl.pallas_call(
        body,
        name=name,
        grid=(N_CHIPS, n_t),
        in_specs=in_specs,
        out_specs=[pl.BlockSpec((None, None, rows, cols), lambda s, t: (s, layer, 0, 0))] * 2,
        out_shape=[jax.ShapeDtypeStruct(shape, F32), jax.ShapeDtypeStruct(shape, BF16)],
        input_output_aliases=aliases,
        compiler_params=_params(2),
    )(*args)


def _chip_peers(x, y):
    return [(1 - x, y), (x, 1 - y), (1 - x, 1 - y)]


def _remote(src, dst, sems, s, peer):
    return pltpu.make_async_remote_copy(
        src_ref=src, dst_ref=dst, send_sem=sems[0].at[s], recv_sem=sems[1].at[s], device_id=peer, device_id_type=MESH
    )


class _Exchange:
    def __init__(self, operands, n_alias, new_shapes, n_sems, build):
        self.operands, self.n_alias, self.new_shapes, self.n_sems, self.build = operands, n_alias, new_shapes, n_sems, build

    def out_shapes(self):
        return [jax.ShapeDtypeStruct(a.shape, a.dtype) for a in self.operands[: self.n_alias]] + list(self.new_shapes)

    def scratch(self):
        return [pltpu.SemaphoreType.DMA((n,)) for n in self.n_sems]

    def run(self, ins, outs, sems, first, last):
        starts, recvs, sends, locals_ = self.build(ins, outs, sems)

        def start_all():
            for cp in starts:
                cp.start()

        def wait_all():
            for cp in recvs:
                cp.wait_recv()
            for cp in sends:
                cp.wait_send()
            for cp in locals_:
                cp.wait()

        if first is True:
            start_all()
            return wait_all
        pl.when(first)(start_all)
        return lambda: pl.when(last)(wait_all)


def _gather_exchange(shards, bufs, plan):
    n_arr = len(shards)
    n_cp = len(plan) * (N_CHIPS - 1)

    def build(ins, outs, sems):
        shard_refs = ins[-n_arr:]
        x, y, c = lax.axis_index("x"), lax.axis_index("y"), lax.axis_index("c")
        me = 2 * x + y
        recvs, sends, locals_ = [], [], []
        for p, (a, l0, n) in enumerate(plan):
            src = shard_refs[a].at[pl.ds(l0, n)]
            cp = pltpu.make_async_copy(src, outs[a].at[me, pl.ds(l0, n)], sems[2].at[p])
            locals_.append(cp)
            for k, (px, py) in enumerate(_chip_peers(x, y)):
                s = p * (N_CHIPS - 1) + k
                sends.append(_remote(src, outs[a].at[me, pl.ds(l0, n)], sems, s, (px, py, c)))
                recvs.append(_remote(src, outs[a].at[2 * px + py, pl.ds(l0, n)], sems, s, (px, py, c)))
        return locals_ + sends, recvs, sends, locals_

    if bufs is None:
        new = [jax.ShapeDtypeStruct((N_CHIPS,) + s.shape, s.dtype) for s in shards]
        return _Exchange(list(shards), 0, new, [n_cp, n_cp, len(plan)], build)
    return _Exchange(list(bufs) + list(shards), n_arr, [], [n_cp, n_cp, len(plan)], build)


def _scatter_exchange(g16, recv, plan, small=None):
    n_arr = len(g16)
    have = [r for r in recv if r is not None]
    made = [a for a in range(n_arr) if recv[a] is None]
    n_cp = len(plan) * (N_CHIPS - 1) + (N_DEV - 1 if small is not None else 0)

    def build(ins, outs, sems):
        g_refs = ins[:n_arr]
        recv_refs, it_have, it_made = [], iter(outs[n_arr : n_arr + len(have)]), iter(outs[n_arr + len(have) :])
        for a in range(n_arr):
            recv_refs.append(next(it_made) if recv[a] is None else next(it_have))
        x, y, c = lax.axis_index("x"), lax.axis_index("y"), lax.axis_index("c")
        me = 2 * x + y
        recvs, sends, locals_ = [], [], []
        for p, (a, l0, n) in enumerate(plan):
            for k, (px, py) in enumerate(_chip_peers(x, y)):
                s = p * (N_CHIPS - 1) + k
                dst = recv_refs[a].at[k, pl.ds(l0, n)]
                sends.append(_remote(g_refs[a].at[2 * px + py, pl.ds(l0, n)], dst, sems, s, (px, py, c)))
                recvs.append(_remote(g_refs[a].at[me, pl.ds(l0, n)], dst, sems, s, (px, py, c)))
        if small is not None:
            small_ref, all_ref = ins[-1], outs[-1]
            slot = 4 * x + 2 * y + c
            locals_.append(pltpu.make_async_copy(small_ref, all_ref.at[slot], sems[2].at[0]))
            flips = [(fx, fy, fc) for fx in (0, 1) for fy in (0, 1) for fc in (0, 1)][1:]
            for k, (fx, fy, fc) in enumerate(flips):
                s = len(plan) * (N_CHIPS - 1) + k
                px, py, pc = x ^ fx, y ^ fy, c ^ fc
                sends.append(_remote(small_ref, all_ref.at[slot], sems, s, (px, py, pc)))
                recvs.append(_remote(small_ref, all_ref.at[4 * px + 2 * py + pc], sems, s, (px, py, pc)))
        return locals_ + sends, recvs, sends, locals_

    operands = list(g16) + have + ([small] if small is not None else [])
    new = [jax.ShapeDtypeStruct((N_CHIPS - 1,) + g16[a].shape[1:], BF16) for a in made]
    if small is not None:
        new.append(jax.ShapeDtypeStruct((N_DEV,) + small.shape, F32))
    return _Exchange(operands, n_arr + len(have), new, [n_cp, n_cp, 1], build)


def _call_with_exchange(body, exch, name, grid, in_specs, out_specs, out_shape, scratch_shapes, args):
    n_in, n_out, n_scr = len(in_specs), len(out_shape), len(scratch_shapes)
    if exch is None:
        outs = pl.pallas_call(
            body, name=name, grid=grid, in_specs=in_specs, out_specs=out_specs, out_shape=out_shape,
            scratch_shapes=scratch_shapes, compiler_params=_params(len(grid)),
        )(*args)
        return outs, []
    e_shapes = exch.out_shapes()
    e_in, e_out = len(exch.operands), len(e_shapes)

    def wrapped(*refs):
        ins, refs = refs[:n_in], refs[n_in:]
        e_ins, refs = refs[:e_in], refs[e_in:]
        outs, refs = refs[:n_out], refs[n_out:]
        e_outs, refs = refs[:e_out], refs[e_out:]
        scr, sems = refs[:n_scr], refs[n_scr:]
        first, last = True, True
        for d, g in enumerate(grid):
            first = (pl.program_id(d) == 0) & first
            last = (pl.program_id(d) == g - 1) & last
        finish = exch.run(e_ins, e_outs, sems, first, last)
        body(*ins, *outs, *scr)
        finish()

    any_spec = pl.BlockSpec(memory_space=pl.ANY)
    outs = pl.pallas_call(
        wrapped,
        name=name,
        grid=grid,
        in_specs=list(in_specs) + [any_spec] * e_in,
        out_specs=list(out_specs) + [any_spec] * e_out,
        out_shape=list(out_shape) + e_shapes,
        input_output_aliases={n_in + i: n_out + i for i in range(exch.n_alias)},
        scratch_shapes=list(scratch_shapes) + exch.scratch(),
        compiler_params=pltpu.CompilerParams(
            dimension_semantics=("arbitrary",) * len(grid), vmem_limit_bytes=VMEM_LIMIT, has_side_effects=True
        ),
    )(*args, *exch.operands)
    return outs[:n_out], outs[n_out:]


def _exchange_only(exch, name):
    n_in = len(exch.operands)
    shapes = exch.out_shapes()

    def body(*refs):
        ins, outs, sems = refs[:n_in], refs[n_in : n_in + len(shapes)], refs[n_in + len(shapes) :]
        exch.run(ins, outs, sems, True, True)()

    any_spec = pl.BlockSpec(memory_space=pl.ANY)
    return pl.pallas_call(
        body,
        name=name,
        in_specs=[any_spec] * n_in,
        out_specs=[any_spec] * len(shapes),
        out_shape=shapes,
        input_output_aliases={i: i for i in range(exch.n_alias)},
        scratch_shapes=exch.scratch(),
        compiler_params=pltpu.CompilerParams(has_side_effects=True),
    )(*exch.operands)


ATTN_LANE_TILES = 2
ATTN_FWD_UNROLL = 34
ATTN_BWD_UNROLL = 17


MASKED = -1e30


def _hi_lo(x):
    hi = x.astype(BF16)
    lo = (x - hi.astype(F32)).astype(BF16)
    return jnp.concatenate([hi, lo], axis=1)


def _suffix_matrix(inclusive):
    j = lax.broadcasted_iota(jnp.int32, (2 * Q_TILE, 2 * Q_TILE), 0) & (Q_TILE - 1)
    s = lax.broadcasted_iota(jnp.int32, (2 * Q_TILE, 2 * Q_TILE), 1)
    later = (j >= s) if inclusive else (j > s)
    return jnp.where((s >= Q_TILE) | later, 1.0, 0.0).astype(BF16)


def _log_beta(z):
    return jnp.minimum(z, 0.0) - jnp.log(1.0 + jnp.exp(-jnp.abs(z)))


def _head_masks(width):
    lane = lax.broadcasted_iota(jnp.int32, (1, width), 1)
    return [(lane >= h * HEAD_DIM) & (lane < (h + 1) * HEAD_DIM) for h in range(width // HEAD_DIM)]


def _per_head_rows(x, masks):
    return jnp.concatenate([jnp.where(hm, x, 0) for hm in masks], axis=0)


def _heads_to_lanes(x, n_heads):
    return jnp.concatenate([x[h * Q_TILE : (h + 1) * Q_TILE] for h in range(n_heads)], axis=1)


def _block_start(kb):
    return kb * Q_TILE if isinstance(kb, int) else pl.multiple_of(kb * Q_TILE, Q_TILE)


def _clamp(i, n):
    return jnp.minimum(i, n - 1)


def _next_block(pos):
    qi, kb = pos
    row_done = kb == 0
    nqi = jnp.where(row_done, qi + 1, qi)
    return nqi, jnp.where(row_done, nqi, kb - 1)


def _stream_unroll(n_blocks, wanted):
    return next(u for u in (wanted, 2, 1) if n_blocks % u == 0)


def _past_mask(rows):
    t = lax.broadcasted_iota(jnp.int32, (rows, Q_TILE), 0) & (Q_TILE - 1)
    s = lax.broadcasted_iota(jnp.int32, (rows, Q_TILE), 1)
    return s < t


def _attn_fwd(qkv, n_seq, S, D, exch, name):
    T = n_seq * S
    width = min(D, ATTN_LANE_TILES * LANES)
    n_heads = width // HEAD_DIM
    rows = n_heads * Q_TILE
    nq = S // Q_TILE
    groups = D // width
    n_blocks = nq * (nq + 1) // 2
    unroll = _stream_unroll(n_blocks, ATTN_FWD_UNROLL)
    scale = HEAD_DIM ** -0.5
    n_trips = n_blocks // unroll

    def body(q_ref, k_ref, v_ref, o_ref, a_out, b_out, qh_scr, vh_scr, bias_scr, a_stage, b_stage, sems):
        masks = _head_masks(width)
        sfx = _suffix_matrix(False)
        stream = pl.program_id(0) * groups + pl.program_id(1)

        def per_head_tables(i, c):
            blk = pl.ds(_block_start(i), Q_TILE)
            qh_scr[i] = _per_head_rows(q_ref[blk, :] * scale, masks)
            vh_scr[i] = _per_head_rows(v_ref[blk, :], masks)
            return c

        lax.fori_loop(0, nq, per_head_tables, 0)
        bias_scr[0] = jnp.zeros((rows, Q_TILE), F32)
        bias_scr[1] = jnp.where(_past_mask(rows), 0.0, MASKED)

        def save(n, slot):
            blocks = pl.ds(n * unroll, unroll)
            return [
                pltpu.make_async_copy(a_stage.at[slot], a_out.at[stream, blocks], sems.at[slot]),
                pltpu.make_async_copy(b_stage.at[slot], b_out.at[stream, blocks], sems.at[2 + slot]),
            ]

        def scores(pos):
            qi, kb = pos
            kt = k_ref[pl.ds(_block_start(_clamp(kb, nq)), Q_TILE), :]
            z = lax.dot_general(qh_scr[_clamp(qi, nq)], kt, NT, preferred_element_type=F32)
            z = z + bias_scr[(kb == qi).astype(jnp.int32)]
            lb = _log_beta(z)
            return lb, _hi_lo(lb - z)

        def weigh(pos, st, carry, acc, slot, u):
            qi, kb = pos
            lb, l1 = st
            r = jnp.dot(l1, sfx, preferred_element_type=F32)
            carry = jnp.where(kb == qi, 0.0, carry)
            a = jnp.exp(lb + r[:, :Q_TILE] + carry).astype(BF16)
            a_stage[slot, u] = a
            b_stage[slot, u] = jnp.exp(lb).astype(BF16)
            acc = jnp.where(kb == qi, 0.0, acc) + jnp.dot(
                _heads_to_lanes(a, n_heads), vh_scr[_clamp(kb, nq)], preferred_element_type=F32
            )
            o_ref[pl.ds(_block_start(_clamp(qi, nq)), Q_TILE), :] = acc
            return carry + r[:, Q_TILE:], acc

        def trip(n, c):
            pos, st, carry, acc = c
            slot = n % 2

            @pl.when((n >= 2) | (stream > 0))
            def _():
                for cp in save(0, slot):
                    cp.wait()

            for u in range(unroll):
                nxt = _next_block(pos)
                st_nxt = scores(nxt)
                carry, acc = weigh(pos, st, carry, acc, slot, u)
                pos, st = nxt, st_nxt
            for cp in save(n, slot):
                cp.start()
            return pos, st, carry, acc

        first = (jnp.int32(0), jnp.int32(0))
        zero = bias_scr[0]
        init = (first, scores(first), zero, jnp.concatenate([zero[:Q_TILE]] * (width // Q_TILE), axis=1))
        lax.fori_loop(0, n_trips, trip, init)

        @pl.when(stream == n_seq * groups - 1)
        def _():
            for slot in range(min(2, n_trips)):
                for cp in save(0, slot):
                    cp.wait()

    seq = lambda col0: pl.BlockSpec((S, width), lambda b, p: (b, col0 + p))
    saved = jax.ShapeDtypeStruct((n_seq * groups, n_blocks, rows, Q_TILE), BF16)
    stage = pltpu.VMEM((2, unroll, rows, Q_TILE), BF16)
    (o, a_w, beta), moved = _call_with_exchange(
        body,
        exch,
        name,
        grid=(n_seq, groups),
        in_specs=[seq(0), seq(groups), seq(2 * groups)],
        out_specs=[seq(0), pl.BlockSpec(memory_space=pl.ANY), pl.BlockSpec(memory_space=pl.ANY)],
        out_shape=[jax.ShapeDtypeStruct((T, D), F32), saved, saved],
        scratch_shapes=[
            pltpu.VMEM((nq, rows, width), BF16),
            pltpu.VMEM((nq, rows, width), BF16),
            pltpu.VMEM((2, rows, Q_TILE), F32),
            stage,
            stage,
            pltpu.SemaphoreType.DMA((4,)),
        ],
        args=(qkv, qkv, qkv),
    )
    return (o, a_w, beta), moved


def _attn_bwd(qkv, fwd, do, n_seq, S, D, exch, name):
    o, a_w, beta = fwd
    T = n_seq * S
    width = min(D, ATTN_LANE_TILES * LANES)
    n_heads = width // HEAD_DIM
    rows = n_heads * Q_TILE
    nq = S // Q_TILE
    groups = D // width
    n_blocks = nq * (nq + 1) // 2
    unroll = _stream_unroll(n_blocks, ATTN_BWD_UNROLL)
    scale = HEAD_DIM ** -0.5
    n_trips = n_blocks // unroll

    def body(q_ref, k_ref, v_ref, o_ref, do_ref, a_in, b_in, dq_ref, dk_ref, dv_ref, dk_acc, dv_acc, qh_scr, doh_scr, delta_scr, a_stage, b_stage, sems):
        masks = _head_masks(width)
        sfx_incl = _suffix_matrix(True)
        stream = pl.program_id(0) * groups + pl.program_id(1)

        n_streams = n_seq * groups
        ahead = n_trips % 2 == 0

        def fetch(s, n, slot):
            blocks = pl.ds(n * unroll, unroll)
            return [
                pltpu.make_async_copy(a_in.at[s, blocks], a_stage.at[slot], sems.at[slot]),
                pltpu.make_async_copy(b_in.at[s, blocks], b_stage.at[slot], sems.at[2 + slot]),
            ]

        @pl.when((stream == 0) | (not ahead))
        def _():
            for cp in fetch(stream, 0, 0):
                cp.start()

        dk_acc[...] = jnp.zeros_like(dk_acc)
        dv_acc[...] = jnp.zeros_like(dv_acc)

        lane = lax.broadcasted_iota(jnp.int32, (2 * width, rows), 0) % width
        col = lax.broadcasted_iota(jnp.int32, (2 * width, rows), 1)
        head_sums = jnp.where(lane // HEAD_DIM == col // Q_TILE, 1.0, 0.0).astype(BF16)

        def per_head_tables(i, c):
            blk = pl.ds(_block_start(i), Q_TILE)
            do = do_ref[blk, :]
            qh_scr[i] = _per_head_rows(q_ref[blk, :] * scale, masks)
            doh_scr[i] = _per_head_rows(do, masks)
            prod = do.astype(F32) * o_ref[blk, :]
            d = jnp.dot(_hi_lo(prod), head_sums, preferred_element_type=F32)
            delta_scr[i] = jnp.concatenate([d[:, h * Q_TILE : (h + 1) * Q_TILE] for h in range(n_heads)], axis=0)
            return c

        lax.fori_loop(0, nq, per_head_tables, 0)

        def weigh(pos, ab, beta, c2, dq):
            qi, kb = pos
            first = kb == qi
            blk = pl.ds(_block_start(kb), Q_TILE)
            g = ab.astype(F32) * lax.dot_general(doh_scr[qi], v_ref[blk, :], NT, preferred_element_type=F32)
            r2 = jnp.dot(_hi_lo(g), sfx_incl, preferred_element_type=F32)
            c2 = jnp.where(first, 0.0, c2)
            earlier = delta_scr[qi] - (r2[:, :Q_TILE] + c2)
            beta = beta.astype(F32)
            dzb = (g * (1.0 - beta) - earlier * beta).astype(BF16)
            kh = _per_head_rows(k_ref[blk, :], masks)
            dq = jnp.where(first, 0.0, dq) + jnp.dot(_heads_to_lanes(dzb, n_heads), kh, preferred_element_type=F32)
            dq_ref[pl.ds(_block_start(qi), Q_TILE), :] = (dq * scale).astype(BF16)
            dk_acc[blk, :] += lax.dot_general(dzb, qh_scr[qi], TN, preferred_element_type=F32)
            dv_acc[blk, :] += lax.dot_general(ab, doh_scr[qi], TN, preferred_element_type=F32)
            return c2 + r2[:, Q_TILE:], dq

        def trip(n, c):
            pos, c2, dq = c
            slot = n % 2
            for cp in fetch(stream, n, slot):
                cp.wait()
            more = n + 1 < n_trips
            if ahead:
                nxt = (jnp.where(more, stream, stream + 1), jnp.where(more, n + 1, 0))
                more = more | (stream + 1 < n_streams)
            else:
                nxt = (stream, n + 1)

            @pl.when(more)
            def _():
                for cp in fetch(*nxt, 1 - slot):
                    cp.start()

            for u in range(unroll):
                c2, dq = weigh(pos, a_stage[slot, u], b_stage[slot, u], c2, dq)
                pos = _next_block(pos)
            return pos, c2, dq

        zero = dk_acc[pl.ds(0, Q_TILE), :]
        init = ((jnp.int32(0), jnp.int32(0)), jnp.concatenate([zero[:, :Q_TILE]] * n_heads, axis=0), zero)
        lax.fori_loop(0, n_trips, trip, init)
        dk_ref[...] = dk_acc[...].astype(BF16)
        dv_ref[...] = dv_acc[...].astype(BF16)

    seq = lambda col0: pl.BlockSpec((S, width), lambda b, p: (b, col0 + p))
    return _call_with_exchange(
        body,
        exch,
        name,
        grid=(n_seq, groups),
        in_specs=[seq(0), seq(groups), seq(2 * groups), seq(0), seq(0)] + [pl.BlockSpec(memory_space=pl.ANY)] * 2,
        out_specs=[seq(0)] * 3,
        out_shape=[jax.ShapeDtypeStruct((T, D), BF16)] * 3,
        scratch_shapes=[
            pltpu.VMEM((S, width), F32),
            pltpu.VMEM((S, width), F32),
            pltpu.VMEM((nq, rows, width), BF16),
            pltpu.VMEM((nq, rows, width), BF16),
            pltpu.VMEM((nq, rows, Q_TILE), F32),
            pltpu.VMEM((2, unroll, rows, Q_TILE), BF16),
            pltpu.VMEM((2, unroll, rows, Q_TILE), BF16),
            pltpu.SemaphoreType.DMA((4,)),
        ],
        args=(qkv, qkv, qkv, o, do, a_w, beta),
    )


def _causal_ws(ws_ref, g):
    t = lax.broadcasted_iota(jnp.int32, (SGU_CHUNK, SGU_CHUNK), 0)
    s = lax.broadcasted_iota(jnp.int32, (SGU_CHUNK, SGU_CHUNK), 1)
    return jnp.where(s <= t, ws_ref[g], 0.0)


def _sgu_chunks_per_step(T):
    return 2 if T % (2 * SGU_CHUNK) == 0 else 1


def _sgu_fwd(a, gain, ws, bsb, name):
    T, F2 = a.shape
    F = F2 // 2
    gw = F // SGU_GROUPS

    step = SGU_CHUNK * _sgu_chunks_per_step(T)

    def body(a_ref, gain_ref, ws_ref, bsb_ref, y_ref):
        for r0 in range(0, step, SGU_CHUNK):
            rs = slice(r0, r0 + SGU_CHUNK)
            v = _gelu(a_ref[rs, F:].astype(F32))
            vn = (v * _rstd(v) * gain_ref[...]).astype(BF16)
            for g in range(SGU_GROUPS):
                cs = slice(g * gw, (g + 1) * gw)
                w = _causal_ws(ws_ref, g).astype(BF16)
                mixed = jnp.dot(w, vn[:, cs], preferred_element_type=F32) + bsb_ref[g]
                y_ref[rs, cs] = (_gelu(a_ref[rs, cs].astype(F32)) * mixed).astype(BF16)

    return pl.pallas_call(
        body,
        name=name,
        grid=(T // step,),
        in_specs=[
            pl.BlockSpec((step, F2), lambda i: (i, 0)),
            pl.BlockSpec((1, F), lambda i: (0, 0)),
            pl.BlockSpec((SGU_GROUPS, SGU_CHUNK, SGU_CHUNK), lambda i: (0, 0, 0)),
            pl.BlockSpec((SGU_GROUPS, SGU_CHUNK, gw), lambda i: (0, 0, 0)),
        ],
        out_specs=pl.BlockSpec((step, F), lambda i: (i, 0)),
        out_shape=jax.ShapeDtypeStruct((T, F), BF16),
        compiler_params=_params(1),
    )(a, gain.reshape(1, F), ws, bsb)


def _sgu_bwd(a, dy, gain, ws, bsb, name):
    T, F2 = a.shape
    F = F2 // 2
    gw = F // SGU_GROUPS
    step = SGU_CHUNK * _sgu_chunks_per_step(T)

    def body(a_ref, dy_ref, gain_ref, ws_ref, bsb_ref, da_ref, dws_ref, dbs_ref, dgain_ref, dvn_ref):
        @pl.when(pl.program_id(0) == 0)
        def _():
            dws_ref[...] = jnp.zeros_like(dws_ref)
            dbs_ref[...] = jnp.zeros_like(dbs_ref)
            dgain_ref[...] = jnp.zeros_like(dgain_ref)

        gain = gain_ref[...]
        ones = jnp.ones((gw, SGU_CHUNK), BF16)
        for r0 in range(0, step, SGU_CHUNK):
            rs = slice(r0, r0 + SGU_CHUNK)
            v, v_slope = _gelu_and_grad(a_ref[rs, F:].astype(F32))
            rstd = _rstd(v)
            vh = v * rstd
            vn = (vh * gain).astype(BF16)
            for g in range(SGU_GROUPS):
                cs = slice(g * gw, (g + 1) * gw)
                w = _causal_ws(ws_ref, g).astype(BF16)
                mixed = jnp.dot(w, vn[:, cs], preferred_element_type=F32) + bsb_ref[g]
                u, u_slope = _gelu_and_grad(a_ref[rs, cs].astype(F32))
                dyc = dy_ref[rs, cs].astype(F32)
                da_ref[rs, cs] = (dyc * mixed * u_slope).astype(BF16)
                dm = (dyc * u).astype(BF16)
                dbs_ref[g] += jnp.dot(dm, ones, preferred_element_type=F32)
                dws_ref[g] += _causal_mask_f32(lax.dot_general(dm, vn[:, cs], NT, preferred_element_type=F32))
                dvn_ref[:, cs] = lax.dot_general(w, dm, TN, preferred_element_type=F32)
            dvn = dvn_ref[...]
            dgain_ref[...] += jnp.sum(dvn * vh, axis=0, keepdims=True)
            dvh = dvn * gain
            dv = rstd * (dvh - vh * jnp.mean(dvh * vh, axis=-1, keepdims=True))
            da_ref[rs, F:] = (dv * v_slope).astype(BF16)

    acc_spec = pl.BlockSpec((SGU_GROUPS, SGU_CHUNK, SGU_CHUNK), lambda i: (0, 0, 0))
    acc_shape = jax.ShapeDtypeStruct((SGU_GROUPS, SGU_CHUNK, SGU_CHUNK), F32)
    return pl.pallas_call(
        body,
        name=name,
        grid=(T // step,),
        in_specs=[
            pl.BlockSpec((step, F2), lambda i: (i, 0)),
            pl.BlockSpec((step, F), lambda i: (i, 0)),
            pl.BlockSpec((1, F), lambda i: (0, 0)),
            acc_spec,
            pl.BlockSpec((SGU_GROUPS, SGU_CHUNK, gw), lambda i: (0, 0, 0)),
        ],
        out_specs=[
            pl.BlockSpec((step, F2), lambda i: (i, 0)),
            acc_spec,
            acc_spec,
            pl.BlockSpec((1, F), lambda i: (0, 0)),
        ],
        out_shape=[
            jax.ShapeDtypeStruct((T, F2), BF16),
            acc_shape,
            acc_shape,
            jax.ShapeDtypeStruct((1, F), F32),
        ],
        scratch_shapes=[pltpu.VMEM((SGU_CHUNK, F), F32)],
        compiler_params=_params(1),
    )(a, dy, gain.reshape(1, F), ws, bsb)


def _causal_mask_f32(m):
    t = lax.broadcasted_iota(jnp.int32, m.shape, 0)
    s = lax.broadcasted_iota(jnp.int32, m.shape, 1)
    return jnp.where(s <= t, m, 0.0)


def _final_loss(x, gain, target, name):
    T, D = x.shape
    tm = min(T, ROW_TILE)

    def body(x_ref, g_ref, t_ref, sq_ref, dx_ref, dxb_ref, dg_ref):
        xv = x_ref[...]
        gain = g_ref[...]
        err = xv * _rstd(xv) * gain - t_ref[...]
        dx, dg = _norm_bwd(err * (1.0 / D), xv, gain)
        dx_ref[...] = dx
        dxb_ref[...] = dx.astype(BF16)
        sq = jnp.sum(err * err, axis=0, keepdims=True)

        @pl.when(pl.program_id(0) == 0)
        def _():
            sq_ref[...] = sq
            dg_ref[...] = dg

        @pl.when(pl.program_id(0) > 0)
        def _():
            sq_ref[...] += sq
            dg_ref[...] += dg

    row = pl.BlockSpec((tm, D), lambda i: (i, 0))
    vec = pl.BlockSpec((1, D), lambda i: (0, 0))
    return pl.pallas_call(
        body,
        name=name,
        grid=(T // tm,),
        in_specs=[row, vec, row],
        out_specs=[vec, row, row, vec],
        out_shape=[
            jax.ShapeDtypeStruct((1, D), F32),
            jax.ShapeDtypeStruct((T, D), F32),
            jax.ShapeDtypeStruct((T, D), BF16),
            jax.ShapeDtypeStruct((1, D), F32),
        ],
        compiler_params=_params(1),
    )(x, gain.reshape(1, D), target)


def _row_tile(rows, cols, n_arrays):
    budget = VMEM_LIMIT // 2 // (2 * n_arrays * cols * 4)
    tr = rows
    while tr > budget and tr % 16 == 0:
        tr //= 2
    return tr


def _sum_received(own, recv, name):
    R, C = own.shape
    n = recv.shape[0]
    tr = _row_tile(R, C, n + 2)

    def body(own_ref, recv_ref, o_ref):
        s = own_ref[...]
        for k in range(n):
            s = s + recv_ref[k].astype(F32)
        o_ref[...] = s

    return pl.pallas_call(
        body,
        name=name,
        grid=(R // tr,),
        in_specs=[pl.BlockSpec((tr, C), lambda i: (i, 0)), pl.BlockSpec((n, tr, C), lambda i: (0, i, 0))],
        out_specs=pl.BlockSpec((tr, C), lambda i: (i, 0)),
        out_shape=jax.ShapeDtypeStruct((R, C), F32),
        compiler_params=_params(1),
    )(own, recv)


def _sum_chip_shard(g32, recv, chip, name):
    _, L, r, c = g32.shape
    n = recv.shape[0]
    tr = _row_tile(r, c, n + 2)

    def body(chip_ref, own_ref, recv_ref, o_ref):
        s = own_ref[...]
        for k in range(n):
            s = s + recv_ref[k].astype(F32)
        o_ref[...] = s

    return pl.pallas_call(
        body,
        name=name,
        grid_spec=pltpu.PrefetchScalarGridSpec(
            num_scalar_prefetch=1,
            grid=(L, r // tr),
            in_specs=[
                pl.BlockSpec((None, None, tr, c), lambda l, i, chip_ref: (chip_ref[0], l, i, 0)),
                pl.BlockSpec((n, None, tr, c), lambda l, i, chip_ref: (0, l, i, 0)),
            ],
            out_specs=pl.BlockSpec((None, tr, c), lambda l, i, chip_ref: (l, i, 0)),
        ),
        out_shape=jax.ShapeDtypeStruct((L, r, c), F32),
        compiler_params=_params(2),
    )(chip.reshape(1).astype(jnp.int32), g32, recv)


def _adamw(w, m, v, parts, name):
    R, C = w.shape
    n = len(parts)
    tr = _row_tile(R, C, n + 7)

    def body(*refs):
        w_ref, m_ref, v_ref = refs[:3]
        g_ref, d_ref, nm_ref, nv_ref = refs[3 + n :]
        g = refs[3][...]
        for p_ref in refs[4 : 3 + n]:
            g = g + p_ref[...]
        nm = ADAM_B1 * m_ref[...] + (1.0 - ADAM_B1) * g
        nv = ADAM_B2 * v_ref[...] + (1.0 - ADAM_B2) * (g * g)
        m_hat = nm / (1.0 - ADAM_B1**ADAM_STEP)
        v_hat = nv / (1.0 - ADAM_B2**ADAM_STEP)
        g_ref[...] = g
        d_ref[...] = -ADAM_LR * (m_hat / (jnp.sqrt(v_hat) + ADAM_EPS) + ADAM_WD * w_ref[...])
        nm_ref[...] = nm
        nv_ref[...] = nv

    spec = pl.BlockSpec((tr, C), lambda i: (i, 0))
    return pl.pallas_call(
        body,
        name=name,
        grid=(R // tr,),
        in_specs=[spec] * (3 + n),
        out_specs=[spec] * 4,
        out_shape=[jax.ShapeDtypeStruct((R, C), F32)] * 4,
        compiler_params=_params(1),
    )(w, m, v, *parts)


def _swap_with_sibling(parts, name):
    n = len(parts)

    def body(*refs):
        ins, outs = refs[:n], refs[n : 2 * n]
        send_sems, recv_sems = refs[2 * n :]
        sibling = (lax.axis_index("x"), lax.axis_index("y"), 1 - lax.axis_index("c"))
        copies = [
            pltpu.make_async_remote_copy(
                src_ref=ins[a],
                dst_ref=outs[a],
                send_sem=send_sems.at[a],
                recv_sem=recv_sems.at[a],
                device_id=sibling,
                device_id_type=MESH,
            )
            for a in range(n)
        ]
        for cp in copies:
            cp.start()
        for cp in copies:
            cp.wait_recv()
        for cp in copies:
            cp.wait_send()

    any_spec = pl.BlockSpec(memory_space=pl.ANY)
    return pl.pallas_call(
        body,
        name=name,
        in_specs=[any_spec] * n,
        out_specs=[any_spec] * n,
        out_shape=[jax.ShapeDtypeStruct(p.shape, p.dtype) for p in parts],
        scratch_shapes=[pltpu.SemaphoreType.DMA((n,)), pltpu.SemaphoreType.DMA((n,))],
        compiler_params=pltpu.CompilerParams(has_side_effects=True),
    )(*parts)


def _pack(pieces):
    flat = jnp.concatenate([p.reshape(-1) for p in pieces])
    return flat.reshape(-1, LANES)


def _unpack(packed, shapes):
    flat = packed.reshape(-1)
    out, off = [], 0
    for s in shapes:
        size = 1
        for d in s:
            size *= d
        out.append(flat[off : off + size].reshape(s))
        off += size
    return out


def kernel(x, norm_mix, norm_mlp, sb_wqkv, sb_wo, sgu_win, sgu_gain, sgu_ws, sgu_bs, sgu_wout, mlp_w1, mlp_w2, final_norm, loss_target, m_norm_mix, m_norm_mlp, m_sb_wqkv, m_sb_wo, m_sgu_win, m_sgu_gain, m_sgu_ws, m_sgu_bs, m_sgu_wout, m_mlp_w1, m_mlp_w2, m_final_norm, v_norm_mix, v_norm_mlp, v_sb_wqkv, v_sb_wo, v_sgu_win, v_sgu_gain, v_sgu_ws, v_sgu_bs, v_sgu_wout, v_mlp_w1, v_mlp_w2, v_final_norm):
    n_seq, S, D = x.shape
    T = n_seq * S
    depth = norm_mix.shape[0]
    n_sgu = sgu_win.shape[0]
    F = sgu_wout.shape[1] * N_CHIPS
    gw = F // SGU_GROUPS
    chip = 2 * lax.axis_index("x") + lax.axis_index("y")

    QKV, WO, WIN, WOUT, W1, W2, GAIN = range(7)
    big = [sb_wqkv, sb_wo, sgu_win, sgu_wout, mlp_w1, mlp_w2]
    n_sb = sb_wqkv.shape[0]
    shards = [w.astype(BF16) for w in big] + [sgu_gain.reshape(1, -1, LANES)]

    def gather_plan(i):
        j, mlp = i // 2, min(2, depth - i)
        plan = [(WO, j, 1), (W1, i, mlp), (W2, i, mlp)]
        if i + 1 < depth:
            plan += [(WIN, (i + 1) // 2, 1), (WOUT, (i + 1) // 2, 1)]
        if j + 1 < n_sb:
            plan += [(QKV, j + 1, 1)]
        return plan

    wg = _exchange_only(_gather_exchange(shards, None, [(QKV, 0, 1), (GAIN, 0, 1)]), "gather_first_weights")
    gain_full = jnp.transpose(wg[GAIN].reshape(N_CHIPS, n_sgu, F // N_CHIPS), (1, 0, 2)).reshape(n_sgu, F)
    bsb = [jnp.broadcast_to(sgu_bs[j][:, :, None], (SGU_GROUPS, SGU_CHUNK, gw)) for j in range(n_sgu)]

    xs = x.reshape(T, D)
    saved = []
    for i in range(depth):
        j = i // 2
        if i % 2 == 0:
            qkv, h = _norm_matmul(xs, norm_mix[i], wg[QKV], j, f"qkv_fwd_{i}")
            attn, wg = _attn_fwd(qkv, n_seq, S, D, _gather_exchange(shards, wg, gather_plan(i)), f"attn_fwd_{i}")
            wg_qkv, wg_wo, wg_win, wg_wout, wg_w1, wg_w2 = wg[:6]
            x_mid = _act_matmul_res(attn[0], wg_wo, j, xs, None, f"wo_fwd_{i}")
            mix = (qkv, attn)
        else:
            a, h = _norm_matmul(xs, norm_mix[i], wg_win, j, f"win_fwd_{i}")
            yg = _sgu_fwd(a, gain_full[j], sgu_ws[j], bsb[j], f"sgu_fwd_{i}")
            x_mid = _act_matmul_res(yg, wg_wout, j, xs, None, f"wout_fwd_{i}")
            mix = (a, yg)
        a2, h2 = _norm_matmul(x_mid, norm_mlp[i], wg_w1, i, f"w1_fwd_{i}")
        x_out = _act_matmul_res(a2, wg_w2, i, x_mid, "relu2", f"w2_fwd_{i}")
        saved.append((xs, h, mix, x_mid, h2, a2))
        xs = x_out

    sq, dx, dxb, g_final = _final_loss(xs, final_norm, loss_target.reshape(T, D), "loss_head")
    loss = lax.psum(0.5 * jnp.sum(sq) / D, ("x", "y", "c"))

    n_layers = [n_sb, n_sb, n_sgu, n_sgu, depth, depth]
    g32, g16, recv = [None] * 6, [None] * 6, [None] * 6
    done_from, sent_from = list(n_layers), list(n_layers)

    def grad(a, layer, lhs, rhs, shard_lhs, act, name):
        bufs = None if g32[a] is None else (g32[a], g16[a])
        g32[a], g16[a] = _matmul_tn(lhs, rhs, bufs, layer, n_layers[a], shard_lhs, act, name)
        done_from[a] = layer

    def unsent_plan():
        plan = [(a, done_from[a], sent_from[a] - done_from[a]) for a in range(6) if sent_from[a] > done_from[a]]
        for a, l0, _ in plan:
            sent_from[a] = l0
        return plan

    def scatter(plan, small):
        if not plan and small is None:
            return None, lambda moved: None
        arrays = sorted({a for a, _, _ in plan})
        have = [a for a in arrays if recv[a] is not None]
        made = [a for a in arrays if recv[a] is None]
        exch = _scatter_exchange(
            [g16[a] for a in arrays], [recv[a] for a in arrays], [(arrays.index(a), l0, n) for a, l0, n in plan], small
        )

        def take(moved):
            for a, buf in zip(arrays, moved):
                g16[a] = buf
            for a, buf in zip(have + made, moved[len(arrays) :]):
                recv[a] = buf
            return moved[-1]

        return exch, take

    g_mix, g_mlp = [None] * depth, [None] * depth
    g_ws, g_bs, g_gain = [None] * n_sgu, [None] * n_sgu, [None] * n_sgu
    for i in reversed(range(depth)):
        j = i // 2
        x_in, h, mix, x_mid, h2, a2 = saved[i]
        da2 = _matmul_nt(dxb, wg_w2, i, a2, f"w2_bwd_{i}")
        grad(W2, i, a2, dxb, True, "relu2", f"w2_grad_{i}")
        grad(W1, i, h2, da2, False, None, f"w1_grad_{i}")
        (dx, dxb, g_mlp[i]), _ = _matmul_nt_norm_bwd(da2, wg_w1, i, x_mid, norm_mlp[i], dx, None, f"w1_bwd_{i}")
        if i % 2 == 0:
            qkv, attn = mix
            do = _matmul_nt(dxb, wg_wo, j, None, f"wo_bwd_{i}")
            grad(WO, j, attn[0], dxb, True, None, f"wo_grad_{i}")
            sgu_small = _pack([jnp.stack(g_ws), jnp.stack(g_bs), jnp.stack(g_gain)]) if i == 0 and n_sgu else None
            exch, take = scatter(unsent_plan(), sgu_small)
            (dq, dk, dv), moved = _attn_bwd(qkv, attn, do, n_seq, S, D, exch, f"attn_bwd_{i}")
            last = take(moved)
            if sgu_small is not None:
                sgu_small_all = last
            dqkv = jnp.concatenate([dq, dk, dv], axis=1)
            grad(QKV, j, h, dqkv, False, None, f"qkv_grad_{i}")
            exch, take = scatter(unsent_plan(), None)
            (dx, dxb, g_mix[i]), moved = _matmul_nt_norm_bwd(
                dqkv, wg_qkv, j, x_in, norm_mix[i], dx, exch, f"qkv_bwd_{i}"
            )
            take(moved)
        else:
            a, yg = mix
            dyg = _matmul_nt(dxb, wg_wout, j, None, f"wout_bwd_{i}")
            grad(WOUT, j, yg, dxb, True, None, f"wout_grad_{i}")
            da, g_ws[j], dbs, g_gain[j] = _sgu_bwd(a, dyg, gain_full[j], sgu_ws[j], bsb[j], f"sgu_bwd_{i}")
            g_bs[j] = dbs[:, :, 0]
            grad(WIN, j, h, da, False, None, f"win_grad_{i}")
            (dx, dxb, g_mix[i]), _ = _matmul_nt_norm_bwd(da, wg_win, j, x_in, norm_mix[i], dx, None, f"win_bwd_{i}")
    grad_x = dx.reshape(n_seq, S, D)

    names = ["qkv", "wo", "win", "wout", "w1", "w2"]
    exch, take = scatter(unsent_plan(), _pack([jnp.stack(g_mix), jnp.stack(g_mlp), g_final]))
    norm_small_all = take(_exchange_only(exch, "gather_norm_grads"))
    partial = [_sum_chip_shard(g32[a], recv[a], chip, f"sum_{names[a]}") for a in range(6)]
    partial = [p.reshape(-1, p.shape[-1]) for p in partial]
    theirs = _swap_with_sibling(partial, "swap_partial_sums")
    g_small = _unpack(
        _sum_received(norm_small_all[0], norm_small_all[1:], "sum_norm_grads"),
        [norm_mix.shape, norm_mlp.shape, final_norm.shape],
    ) + _unpack(
        _sum_received(sgu_small_all[0], sgu_small_all[1:], "sum_sgu_small_grads"),
        [sgu_ws.shape, sgu_bs.shape, (n_sgu, F)],
    )

    ms = [m_sb_wqkv, m_sb_wo, m_sgu_win, m_sgu_wout, m_mlp_w1, m_mlp_w2]
    vs = [v_sb_wqkv, v_sb_wo, v_sgu_win, v_sgu_wout, v_mlp_w1, v_mlp_w2]
    res = {}
    keys = ["sb_wqkv", "sb_wo", "sgu_win", "sgu_wout", "mlp_w1", "mlp_w2"]
    for key, k, w, m, v, mine, other in zip(keys, names, big, ms, vs, partial, theirs):
        cols = w.shape[-1]
        outs = _adamw(w.reshape(-1, cols), m.reshape(-1, cols), v.reshape(-1, cols), [mine, other], f"adamw_{k}")
        res[key] = [o.reshape(w.shape) for o in outs]

    g_small[5] = lax.dynamic_slice_in_dim(g_small[5], chip * (F // N_CHIPS), F // N_CHIPS, axis=1)
    small_keys = ["norm_mix", "norm_mlp", "final_norm", "sgu_ws", "sgu_bs", "sgu_gain"]
    small_w = [norm_mix, norm_mlp, final_norm, sgu_ws, sgu_bs, sgu_gain]
    small_m = [m_norm_mix, m_norm_mlp, m_final_norm, m_sgu_ws, m_sgu_bs, m_sgu_gain]
    small_v = [v_norm_mix, v_norm_mlp, v_final_norm, v_sgu_ws, v_sgu_bs, v_sgu_gain]
    outs = _adamw(_pack(small_w), _pack(small_m), _pack(small_v), [_pack(g_small)], "adamw_small")
    local_shapes = [w.shape for w in small_w]
    for key, parts in zip(small_keys, zip(*[_unpack(o, local_shapes) for o in outs])):
        res[key] = list(parts)

    order = ["norm_mix", "norm_mlp", "sb_wqkv", "sb_wo", "sgu_win", "sgu_gain", "sgu_ws", "sgu_bs", "sgu_wout", "mlp_w1", "mlp_w2", "final_norm"]
    return (loss, grad_x, *[res[k][0] for k in order], *[res[k][1] for k in order], *[res[k][2] for k in order], *[res[k][3] for k in order])
```
